```python
import jax, jax.numpy as jnp
from jax import lax
import numpy as np

D_MODEL = 1024
BATCH = 4
SEQ = 8192
DEPTH = 2
DEC_BATCH = 8
DEC_SEQ = 64
PAST_LEN = 4096

CHUNK = 64
Q_BLOCK = 128
PLE_DIM = 256
RMS_EPS = 1e-6
ROPE_THETA = 10000.0
RET_HEADS = 8
RET_DK = 64
RET_DV = 64
RET_QK = RET_HEADS * RET_DK
RET_WIDTH = RET_HEADS * RET_DV
MLA_HEADS = 8
MLA_NOPE = 64
MLA_ROPE = 32
MLA_V = 64
Q_LORA = 256
KV_LORA = 128
MLA_WIDTH = MLA_HEADS * MLA_V
MIX_WIDTH = RET_WIDTH + MLA_WIDTH
OFF_Q = RET_QK
OFF_K = OFF_Q + RET_QK
OFF_V = OFF_K + RET_WIDTH
OFF_G = OFF_V + RET_WIDTH
OFF_CQ = OFF_G + Q_LORA
OFF_CKV = OFF_CQ + KV_LORA
IN_COLS = OFF_CKV + MLA_ROPE
D_FF_DENSE = 2816
N_EXPERTS = 8
TOP_K = 2
D_FF_EXPERT = 3584

kernel_name = 'hybrid_retention_mla_stream_step'


def rms_norm(x, g):
    xf = x.astype(jnp.float32)
    y = xf * lax.rsqrt(jnp.mean(xf * xf, axis=-1, keepdims=True) + RMS_EPS)
    return (y * g.astype(jnp.float32)).astype(x.dtype)


def rope(x, pos):
    half = x.shape[-1] // 2
    inv = ROPE_THETA ** (-jnp.arange(half, dtype=jnp.float32) / half)
    ang = pos.astype(jnp.float32)[:, None] * inv[None, :]
    cos = jnp.cos(ang)[None, :, None, :]
    sin = jnp.sin(ang)[None, :, None, :]
    xf = x.astype(jnp.float32)
    x1, x2 = xf[..., :half], xf[..., half:]
    return jnp.concatenate([x1 * cos - x2 * sin, x2 * cos + x1 * sin], axis=-1).astype(x.dtype)


def ret_log_decay():
    return jnp.log(1.0 - jnp.exp2(-5.0 - jnp.arange(RET_HEADS, dtype=jnp.float32)))


def project(h, pos, attn_norm, w_in, q_norm, w_uq, kv_norm, w_uk):
    B, L, _ = h.shape
    z = rms_norm(h, attn_norm) @ w_in
    q_r, k_r, v_r, g_r, c_q, c_kv, k_rope = jnp.split(
        z, [OFF_Q, OFF_K, OFF_V, OFF_G, OFF_CQ, OFF_CKV], axis=-1)
    q_r = rope(q_r.reshape(B, L, RET_HEADS, RET_DK), pos)
    k_r = rope(k_r.reshape(B, L, RET_HEADS, RET_DK), pos) * (RET_DK ** -0.5)
    v_r = v_r.reshape(B, L, RET_HEADS, RET_DV)
    q = (rms_norm(c_q, q_norm) @ w_uq).reshape(B, L, MLA_HEADS, MLA_NOPE + MLA_ROPE)
    q_nope, q_rope = q[..., :MLA_NOPE], rope(q[..., MLA_NOPE:], pos)
    q_lat = jnp.einsum('blhn,hnc->blhc', q_nope, w_uk)
    c_kv = rms_norm(c_kv, kv_norm)
    k_rope = rope(k_rope[:, :, None, :], pos)[:, :, 0, :]
    return (q_r, k_r, v_r, g_r), (q_lat, q_rope, c_kv, k_rope)


def retention_chunk(state, q, k, v, log_g):
    L = q.shape[1]
    qf, kf, vf = q.astype(jnp.float32), k.astype(jnp.float32), v.astype(jnp.float32)
    idx = jnp.arange(L, dtype=jnp.float32)
    dist = jnp.abs(idx[:, None] - idx[None, :])
    decay = jnp.exp(dist[None] * log_g[:, None, None])
    scores = jnp.einsum('bihd,bjhd->bhij', qf, kf) * decay[None]
    o = jnp.einsum('bhij,bjhe->bihe', scores, vf)
    q_decay = jnp.exp((idx[:, None] + 1.0) * log_g[None, :])
    o = o + jnp.einsum('bihd,bhde->bihe', qf, state) * q_decay[None, :, :, None]
    k_decay = jnp.exp((L - 1.0 - idx)[:, None] * log_g[None, :])
    new_state = (jnp.exp(L * log_g)[None, :, None, None] * state
                 + jnp.einsum('bjhd,bjhe->bhde', kf * k_decay[None, :, :, None], vf))
    return new_state, o


def retention_prompt(q, k, v, log_g):
    B, S = q.shape[0], q.shape[1]
    nc = S // CHUNK
    to_chunks = lambda t: jnp.moveaxis(t.reshape(B, nc, CHUNK, *t.shape[2:]), 1, 0)
    state0 = jnp.zeros((B, RET_HEADS, RET_DK, RET_DV), jnp.float32)
    final, oc = lax.scan(lambda s, xs: retention_chunk(s, xs[0], xs[1], xs[2], log_g),
                         state0, (to_chunks(q), to_chunks(k), to_chunks(v)))
    return final, jnp.moveaxis(oc, 0, 1).reshape(B, S, RET_HEADS, RET_DV)


def retention_out(o, g, ret_norm, dtype):
    B, L = o.shape[0], o.shape[1]
    o = o * lax.rsqrt(jnp.mean(o * o, axis=-1, keepdims=True) + RMS_EPS)
    o = o.reshape(B, L, RET_WIDTH) * ret_norm.astype(jnp.float32)
    return (o * jax.nn.silu(g.astype(jnp.float32))).astype(dtype)


def mla_attend(q_lat, q_rope, c_kv, k_rope, q_pos, k_pos, w_uv):
    scale = (MLA_NOPE + MLA_ROPE) ** -0.5
    ckf = c_kv.astype(jnp.float32)
    s = (jnp.einsum('bqhc,bkc->bhqk', q_lat.astype(jnp.float32), ckf)
         + jnp.einsum('bqhr,bkr->bhqk', q_rope.astype(jnp.float32), k_rope.astype(jnp.float32))) * scale
    mask = (k_pos[None, :] // CHUNK) <= (q_pos[:, None] // CHUNK)
    p = jax.nn.softmax(jnp.where(mask[None, None], s, -jnp.inf), axis=-1)
    o_lat = jnp.einsum('bhqk,bkc->bqhc', p, ckf)
    return jnp.einsum('bqhc,hcv->bqhv', o_lat, w_uv.astype(jnp.float32))


def mla_prompt(q_lat, q_rope, c_kv, k_rope, pos, w_uv):
    B, S = q_lat.shape[0], q_lat.shape[1]
    nb = S // Q_BLOCK
    to_blocks = lambda t: jnp.moveaxis(t.reshape(B, nb, Q_BLOCK, *t.shape[2:]), 1, 0)
    o = lax.map(lambda xs: mla_attend(xs[0], xs[1], c_kv, k_rope, xs[2], pos, w_uv),
                (to_blocks(q_lat), to_blocks(q_rope), pos.reshape(nb, Q_BLOCK)))
    return jnp.moveaxis(o, 0, 1).reshape(B, S, MLA_HEADS, MLA_V)


def merge_heads(h, o_ret, g_r, o_mla, ret_norm, w_o):
    B, L = h.shape[0], h.shape[1]
    heads = jnp.concatenate([retention_out(o_ret, g_r, ret_norm, h.dtype),
                             o_mla.reshape(B, L, MLA_WIDTH).astype(h.dtype)], axis=-1)
    return h + heads @ w_o


def swiglu(x, wg, wu, wd):
    return (jax.nn.silu(x @ wg) * (x @ wu)) @ wd


def moe_swiglu(x, w_router, wg, wu, wd):
    B, L, D = x.shape
    xt = x.reshape(B * L, D)
    logits = (xt @ w_router).astype(jnp.float32)
    top_v, top_i = lax.top_k(logits, TOP_K)
    top_w = jax.nn.softmax(top_v, axis=-1)
    gates = jnp.sum(jax.nn.one_hot(top_i, N_EXPERTS, dtype=jnp.float32) * top_w[..., None], axis=1)
    out = jnp.zeros((B * L, D), jnp.float32)
    for e in range(N_EXPERTS):
        out = out + gates[:, e:e + 1] * swiglu(xt, wg[e], wu[e], wd[e]).astype(jnp.float32)
    return out.astype(x.dtype).reshape(B, L, D)


def channel_and_ple(h, p_l, layer, ffn_norm, w_gate_d, w_up_d, w_down_d, w_router,
                    w_gate_e, w_up_e, w_down_e, ple_norm, w_ple_gate, w_ple_proj):
    u = rms_norm(h, ffn_norm[layer])
    j = layer // 2
    if layer % 2 == 0:
        f = swiglu(u, w_gate_d[j], w_up_d[j], w_down_d[j])
    else:
        f = moe_swiglu(u, w_router[j], w_gate_e[j], w_up_e[j], w_down_e[j])
    h = h + f
    gate = jax.nn.sigmoid(rms_norm(h, ple_norm[layer]) @ w_ple_gate[layer])
    return h + gate * (p_l @ w_ple_proj[layer])


def setup_inputs(seed: int = 0) -> dict:
    key = jax.random.key(seed)
    ks = iter(jax.random.split(key, 40))
    def w(shape, fan_in):
        return jax.random.normal(next(ks), shape, jnp.float32) * (fan_in ** -0.5)
    def gain(shape):
        return 1.0 + 0.05 * jax.random.normal(next(ks), shape, jnp.float32)
    def rnd(shape, s=1.0):
        return s * jax.random.normal(next(ks), shape, jnp.float32)
    nd, nm = (DEPTH + 1) // 2, DEPTH // 2
    return {
        'x_prompt': rnd((BATCH, SEQ, D_MODEL)),
        'x_sample': rnd((DEC_BATCH, DEC_SEQ, D_MODEL)),
        'p_prompt': rnd((DEPTH, BATCH, SEQ, PLE_DIM)),
        'p_sample': rnd((DEPTH, DEC_BATCH, DEC_SEQ, PLE_DIM)),
        'cache_ckv': rnd((DEPTH, DEC_BATCH, PAST_LEN, KV_LORA)),
        'cache_krope': rnd((DEPTH, DEC_BATCH, PAST_LEN, MLA_ROPE)),
        'state_ret': rnd((DEPTH, DEC_BATCH, RET_HEADS, RET_DK, RET_DV)),
        'attn_norm': gain((DEPTH, D_MODEL)),
        'w_in': w((DEPTH, D_MODEL, IN_COLS), D_MODEL),
        'q_norm': gain((DEPTH, Q_LORA)),
        'w_uq': w((DEPTH, Q_LORA, MLA_HEADS * (MLA_NOPE + MLA_ROPE)), Q_LORA),
        'kv_norm': gain((DEPTH, KV_LORA)),
        'w_uk': w((DEPTH, MLA_HEADS, MLA_NOPE, KV_LORA), KV_LORA),
        'w_uv': w((DEPTH, MLA_HEADS, KV_LORA, MLA_V), KV_LORA),
        'ret_norm': gain((DEPTH, RET_WIDTH)),
        'w_o': w((DEPTH, MIX_WIDTH, D_MODEL), MIX_WIDTH),
        'ffn_norm': gain((DEPTH, D_MODEL)),
        'w_gate_d': w((nd, D_MODEL, D_FF_DENSE), D_MODEL),
        'w_up_d': w((nd, D_MODEL, D_FF_DENSE), D_MODEL),
        'w_down_d': w((nd, D_FF_DENSE, D_MODEL), D_FF_DENSE),
        'w_router': w((nm, D_MODEL, N_EXPERTS), D_MODEL),
        'w_gate_e': w((nm, N_EXPERTS, D_MODEL, D_FF_EXPERT), D_MODEL),
        'w_up_e': w((nm, N_EXPERTS, D_MODEL, D_FF_EXPERT), D_MODEL),
        'w_down_e': w((nm, N_EXPERTS, D_FF_EXPERT, D_MODEL), D_FF_EXPERT),
        'ple_norm': gain((DEPTH, D_MODEL)),
        'w_ple_gate': w((DEPTH, D_MODEL, D_MODEL), D_MODEL),
        'w_ple_proj': w((DEPTH, PLE_DIM, D_MODEL), PLE_DIM),
        'final_norm': gain((D_MODEL,)),
    }


def reference(x_prompt, x_sample, p_prompt, p_sample, cache_ckv, cache_krope, state_ret,
              attn_norm, w_in, q_norm, w_uq, kv_norm, w_uk, w_uv, ret_norm, w_o,
              ffn_norm, w_gate_d, w_up_d, w_down_d, w_router, w_gate_e, w_up_e, w_down_e,
              ple_norm, w_ple_gate, w_ple_proj, final_norm):
    S = x_prompt.shape[1]
    L = x_sample.shape[1]
    P = cache_ckv.shape[2]
    pos_p = jnp.arange(S, dtype=jnp.int32)
    pos_s = P + jnp.arange(L, dtype=jnp.int32)
    kpos_s = jnp.arange(P + L, dtype=jnp.int32)
    log_g = ret_log_decay()
    hp, hs = x_prompt, x_sample
    ckv_p, kro_p, ret_p, ckv_s, kro_s, ret_s = [], [], [], [], [], []
    for l in range(DEPTH):
        proj_w = (attn_norm[l], w_in[l], q_norm[l], w_uq[l], kv_norm[l], w_uk[l])
        tail_w = (ffn_norm, w_gate_d, w_up_d, w_down_d, w_router, w_gate_e, w_up_e, w_down_e,
                  ple_norm, w_ple_gate, w_ple_proj)
        (q_r, k_r, v_r, g_r), (q_lat, q_rope, c_kv, k_rope) = project(hp, pos_p, *proj_w)
        st, o_ret = retention_prompt(q_r, k_r, v_r, log_g)
        o_mla = mla_prompt(q_lat, q_rope, c_kv, k_rope, pos_p, w_uv[l])
        hp = merge_heads(hp, o_ret, g_r, o_mla, ret_norm[l], w_o[l])
        hp = channel_and_ple(hp, p_prompt[l], l, *tail_w)
        ckv_p.append(c_kv)
        kro_p.append(k_rope)
        ret_p.append(st.astype(x_prompt.dtype))
        (q_r, k_r, v_r, g_r), (q_lat, q_rope, c_kv, k_rope) = project(hs, pos_s, *proj_w)
        st, o_ret = retention_chunk(state_ret[l].astype(jnp.float32), q_r, k_r, v_r, log_g)
        ckv_all = jnp.concatenate([cache_ckv[l].astype(c_kv.dtype), c_kv], axis=1)
        kro_all = jnp.concatenate([cache_krope[l].astype(k_rope.dtype), k_rope], axis=1)
        o_mla = mla_attend(q_lat, q_rope, ckv_all, kro_all, pos_s, kpos_s, w_uv[l])
        hs = merge_heads(hs, o_ret, g_r, o_mla, ret_norm[l], w_o[l])
        hs = channel_and_ple(hs, p_sample[l], l, *tail_w)
        ckv_s.append(c_kv)
        kro_s.append(k_rope)
        ret_s.append(st.astype(x_sample.dtype))
    y_prompt = rms_norm(hp, final_norm)
    y_sample = rms_norm(hs, final_norm)
    return (y_prompt, y_sample, jnp.stack(ckv_p), jnp.stack(kro_p), jnp.stack(ret_p),
            jnp.stack(ckv_s), jnp.stack(kro_s), jnp.stack(ret_s))
```

```python
import functools

import numpy as np
import jax
import jax.numpy as jnp
from jax import lax
from jax.experimental import pallas as pl
from jax.experimental.pallas import tpu as pltpu

F32 = jnp.float32
BF16 = jnp.bfloat16

CHUNK = 64
CHUNK_SHIFT = 6
RMS_EPS = 1e-6
ROPE_THETA = 10000.0
RET_HEADS = 8
RET_DK = 64
RET_DV = 64
RET_W = RET_HEADS * RET_DK
MLA_HEADS = 8
MLA_NOPE = 64
MLA_ROPE = 32
MLA_V = 64
Q_LORA = 256
KV_LORA = 128
MLA_W = MLA_HEADS * MLA_V
N_PAIRS = RET_HEADS // 2
LANES = 128
QK_PAD = 256
OFF_Q = RET_W
OFF_K = OFF_Q + RET_W
OFF_V = OFF_K + RET_W
OFF_G = OFF_V + RET_W
OFF_CQ = OFF_G + Q_LORA
OFF_CKV = OFF_CQ + KV_LORA
IN_COLS = OFF_CKV + MLA_ROPE
IN_COLS_PAD = OFF_CKV + LANES
NEG = -1e30
VMEM_LIMIT = 56 * 1024 * 1024


def _pick(n, cands):
    for c in cands:
        if n % c == 0:
            return c
    return n


def _cparams(sem):
    return pltpu.CompilerParams(dimension_semantics=sem, vmem_limit_bytes=VMEM_LIMIT)


def _rms(x, g):
    return x * lax.rsqrt(jnp.mean(x * x, axis=-1, keepdims=True) + RMS_EPS) * g


def _dot(a, b):
    return jnp.dot(a, b, preferred_element_type=F32)


def _dot_nt(a, b):
    return lax.dot_general(a, b, (((1,), (1,)), ((), ())), preferred_element_type=F32)


def _dot_tn(a, b):
    return lax.dot_general(a, b, (((0,), (0,)), ((), ())), preferred_element_type=F32)


def _proj_kernel(h_ref, an_ref, win_ref, qn_ref, wuq_ref, kvn_ref, wbig_ref,
                 c64_ref, sm64_ref, sp64_ref, c16_ref, s16_ref, ck_ref, smk_ref, spk_ref,
                 qr_ref, kr_ref, vr_ref, gr_ref, qx_ref, ckv_ref, kro_ref, kx_ref):
    xn = _rms(h_ref[...], an_ref[...]).astype(BF16)
    c64, sm64, sp64 = c64_ref[...], sm64_ref[...], sp64_ref[...]

    def rope64(z):
        return (z * c64 + pltpu.roll(z, LANES - 32, 1) * sm64 + pltpu.roll(z, 32, 1) * sp64)

    for j in range(RET_W // LANES):
        lo = j * LANES
        zq = _dot(xn, win_ref[:, lo:lo + LANES])
        qr_ref[:, lo:lo + LANES] = rope64(zq)
        zk = _dot(xn, win_ref[:, OFF_Q + lo:OFF_Q + lo + LANES])
        kr_ref[:, lo:lo + LANES] = rope64(zk) * (RET_DK ** -0.5)
    vr_ref[...] = _dot(xn, win_ref[:, OFF_K:OFF_V]).astype(BF16)
    gr_ref[...] = _dot(xn, win_ref[:, OFF_V:OFF_G])

    cq = _dot(xn, win_ref[:, OFF_G:OFF_CQ])
    q = _dot(_rms(cq, qn_ref[...]).astype(BF16), wuq_ref[...])
    nq = MLA_HEADS * MLA_NOPE
    x1, x2 = q[:, nq:nq + LANES], q[:, nq + LANES:nq + 2 * LANES]
    c16, s16 = c16_ref[...], s16_ref[...]
    scale = (MLA_NOPE + MLA_ROPE) ** -0.5
    qcat = jnp.concatenate([q[:, :nq], x1 * c16 - x2 * s16, x2 * c16 + x1 * s16], axis=-1)
    qx_ref[...] = _dot((qcat * scale).astype(BF16), wbig_ref[...]).astype(BF16)

    ckv = _rms(_dot(xn, win_ref[:, OFF_CQ:OFF_CKV]), kvn_ref[...])
    ckv_ref[...] = ckv
    zk = _dot(xn, win_ref[:, OFF_CKV:IN_COLS_PAD])
    kro = (zk * ck_ref[...] + pltpu.roll(zk, LANES - 16, 1) * smk_ref[...]
           + pltpu.roll(zk, 16, 1) * spk_ref[...])
    kro_ref[...] = kro[:, :MLA_ROPE]
    kx_ref[:, :KV_LORA] = ckv.astype(BF16)
    kx_ref[:, KV_LORA:] = kro.astype(BF16)


def _proj(h, an, win, qn, wuq, kvn, wbig, tabs, tm):
    T, D = h.shape
    row = lambda w: pl.BlockSpec((tm, w), lambda i: (i, 0))
    full = lambda a: pl.BlockSpec(a.shape, lambda i: (0,) * a.ndim)
    out_shapes = (
        jax.ShapeDtypeStruct((T, RET_W), F32), jax.ShapeDtypeStruct((T, RET_W), F32),
        jax.ShapeDtypeStruct((T, RET_W), BF16), jax.ShapeDtypeStruct((T, RET_W), F32),
        jax.ShapeDtypeStruct((T, MLA_HEADS * QK_PAD), BF16),
        jax.ShapeDtypeStruct((T, KV_LORA), F32), jax.ShapeDtypeStruct((T, MLA_ROPE), F32),
        jax.ShapeDtypeStruct((T, QK_PAD), BF16))
    return pl.pallas_call(
        _proj_kernel, grid=(T // tm,),
        in_specs=[row(D), full(an), full(win), full(qn), full(wuq), full(kvn), full(wbig)]
                 + [row(LANES)] * 8,
        out_specs=(row(RET_W), row(RET_W), row(RET_W), row(RET_W), row(MLA_HEADS * QK_PAD),
                   row(KV_LORA), row(MLA_ROPE), row(QK_PAD)),
        out_shape=out_shapes, compiler_params=_cparams(("parallel",)), name="proj",
    )(h, an, win, qn, wuq, kvn, wbig, *tabs)


def _ret_kernel(q_ref, k_ref, v_ref, g_ref, s0_ref, d_ref, qd_ref, kd_ref, gs_ref, bm_ref, rn_ref,
                o_ref, so_ref, st_ref, *, n_chunks):
    i = pl.program_id(1)

    @pl.when(i == 0)
    def _():
        st_ref[...] = s0_ref[0]

    lane = lax.broadcasted_iota(jnp.int32, (CHUNK, LANES), 1)
    first = lane < RET_DK

    def chunk(c, carry):
        rows = pl.ds(pl.multiple_of(c * CHUNK, CHUNK), CHUNK)
        for p in range(N_PAIRS):
            cols = slice(p * LANES, (p + 1) * LANES)
            qp, kp, vp = q_ref[rows, cols], k_ref[rows, cols], v_ref[rows, cols]
            kb = kp.astype(BF16)
            s_a = _dot_nt(jnp.where(first, qp, 0.0).astype(BF16), kb) * d_ref[2 * p]
            s_b = _dot_nt(jnp.where(first, 0.0, qp).astype(BF16), kb) * d_ref[2 * p + 1]
            o = jnp.where(first, _dot(s_a.astype(BF16), vp), _dot(s_b.astype(BF16), vp))
            st = st_ref[p]
            o = o + _dot(qp.astype(BF16), st.astype(BF16)) * qd_ref[p]
            upd = _dot_tn((kp * kd_ref[p]).astype(BF16), vp)
            st_ref[p] = st * gs_ref[p] + upd * bm_ref[...]
            oo = o * o
            ss_a = jnp.sum(jnp.where(first, oo, 0.0), axis=-1, keepdims=True)
            ss_b = jnp.sum(jnp.where(first, 0.0, oo), axis=-1, keepdims=True)
            rs = jnp.where(first, lax.rsqrt(ss_a * (1.0 / RET_DV) + RMS_EPS),
                           lax.rsqrt(ss_b * (1.0 / RET_DV) + RMS_EPS))
            g = g_ref[rows, cols]
            o_ref[rows, cols] = (o * rs * rn_ref[:, cols] * (g * jax.nn.sigmoid(g))).astype(BF16)
        return carry

    lax.fori_loop(0, n_chunks, chunk, 0)

    @pl.when(i == pl.num_programs(1) - 1)
    def _():
        so_ref[0] = st_ref[...]


def _retention(qr, kr, vr, gr, s0, tabs, rn, nb, seq, row0, rb):
    nblk = seq // rb
    off = row0 // rb
    tok = pl.BlockSpec((rb, RET_W), lambda b, i: (off + b * nblk + i, 0))
    full = lambda a: pl.BlockSpec(a.shape, lambda b, i: (0,) * a.ndim)
    st_spec = pl.BlockSpec((1, N_PAIRS, LANES, LANES), lambda b, i: (b, 0, 0, 0))
    d, qd, kd, gs, bm = tabs
    return pl.pallas_call(
        functools.partial(_ret_kernel, n_chunks=rb // CHUNK), grid=(nb, nblk),
        in_specs=[tok, tok, tok, tok, st_spec, full(d), full(qd), full(kd), full(gs), full(bm), full(rn)],
        out_specs=(pl.BlockSpec((rb, RET_W), lambda b, i: (b * nblk + i, 0)), st_spec),
        out_shape=(jax.ShapeDtypeStruct((nb * seq, RET_W), BF16),
                   jax.ShapeDtypeStruct((nb, N_PAIRS, LANES, LANES), F32)),
        scratch_shapes=[pltpu.VMEM((N_PAIRS, LANES, LANES), F32)],
        compiler_params=_cparams(("parallel", "arbitrary")), name="retention",
    )(qr, kr, vr, gr, s0, d, qd, kd, gs, bm, rn)


def _attn_kernel(q_ref, k_ref, wuv_ref, o_ref, q_s, m_s, l_s, acc_s, *, tq, kb, q_pos0, sk_valid):
    i = pl.program_id(1)
    R = MLA_HEADS * tq
    for h in range(MLA_HEADS):
        q_s[h * tq:(h + 1) * tq, :] = q_ref[:, h * QK_PAD:(h + 1) * QK_PAD]
    m_s[...] = jnp.full(m_s.shape, NEG, F32)
    l_s[...] = jnp.zeros(l_s.shape, F32)
    acc_s[...] = jnp.zeros(acc_s.shape, F32)

    chunk_end = lambda t: ((t >> CHUNK_SHIFT) + 1) << CHUNK_SHIFT
    kb_shift = kb.bit_length() - 1
    qstart = q_pos0 + i * tq
    lim_first = jnp.minimum(chunk_end(qstart), sk_valid)
    lim_last = jnp.minimum(chunk_end(qstart + tq - 1), sk_valid)
    n_full = lim_first >> kb_shift
    n_blk = (lim_last + kb - 1) >> kb_shift

    def step(j, masked):
        kblk = k_ref[pl.ds(pl.multiple_of(j * kb, kb), kb), :]
        s = _dot_nt(q_s[...], kblk)
        if masked:
            tok = lax.broadcasted_iota(jnp.int32, (R, 1), 0) & (tq - 1)
            row_lim = jnp.minimum(chunk_end(qstart + tok), sk_valid)
            kidx = j * kb + lax.broadcasted_iota(jnp.int32, (1, kb), 1)
            s = jnp.where(kidx < row_lim, s, NEG)
        m_prev = m_s[...]
        m_next = jnp.maximum(m_prev, jnp.max(s, axis=-1, keepdims=True))
        p = jnp.exp(s - jnp.tile(m_next, (1, kb // LANES)))
        alpha = jnp.exp(m_prev - m_next)
        l_s[...] = alpha * l_s[...] + jnp.sum(p, axis=-1, keepdims=True)
        acc_s[...] = alpha * acc_s[...] + _dot(p.astype(BF16), kblk[:, :KV_LORA])
        m_s[...] = m_next

    def full_body(j, c):
        step(j, False)
        return c

    def mask_body(j, c):
        step(j, True)
        return c

    lax.fori_loop(0, n_full, full_body, 0)
    lax.fori_loop(n_full, n_blk, mask_body, 0)

    o_lat = (acc_s[...] / l_s[...]).astype(BF16)
    out = _dot(o_lat[0:tq], wuv_ref[0])
    for h in range(1, MLA_HEADS):
        out = out + _dot(o_lat[h * tq:(h + 1) * tq], wuv_ref[h])
    o_ref[...] = out.astype(BF16)


def _attention(qx, kx, wuv, nb, seq_q, row0, tq, kb, q_pos0, sk, sk_valid):
    nq = seq_q // tq
    off = row0 // tq
    R = MLA_HEADS * tq
    return pl.pallas_call(
        functools.partial(_attn_kernel, tq=tq, kb=kb, q_pos0=q_pos0, sk_valid=sk_valid),
        grid=(nb, nq),
        in_specs=[pl.BlockSpec((tq, MLA_HEADS * QK_PAD), lambda b, i: (off + b * nq + i, 0)),
                  pl.BlockSpec((sk, QK_PAD), lambda b, i: (b, 0)),
                  pl.BlockSpec(wuv.shape, lambda b, i: (0, 0, 0))],
        out_specs=pl.BlockSpec((tq, MLA_W), lambda b, i: (b * nq + i, 0)),
        out_shape=jax.ShapeDtypeStruct((nb * seq_q, MLA_W), BF16),
        scratch_shapes=[pltpu.VMEM((R, QK_PAD), BF16), pltpu.VMEM((R, LANES), F32),
                        pltpu.VMEM((R, LANES), F32), pltpu.VMEM((R, KV_LORA), F32)],
        compiler_params=_cparams(("parallel", "arbitrary")), name="attention",
    )(qx, kx, wuv)


def _merge_kernel(h_ref, a_ref, b_ref, wo_ref, o_ref):
    o_ref[...] = (h_ref[...] + _dot(a_ref[...], wo_ref[:RET_W, :])
                  + _dot(b_ref[...], wo_ref[RET_W:, :]))


def _merge(h, a, b, wo, tm):
    T, D = h.shape
    row = lambda w: pl.BlockSpec((tm, w), lambda i: (i, 0))
    return pl.pallas_call(
        _merge_kernel, grid=(T // tm,),
        in_specs=[row(D), row(RET_W), row(MLA_W), pl.BlockSpec(wo.shape, lambda i: (0, 0))],
        out_specs=row(D), out_shape=jax.ShapeDtypeStruct((T, D), F32),
        compiler_params=_cparams(("parallel",)), name="merge",
    )(h, a, b, wo)


def _router_kernel(h_ref, fn_ref, wr_ref, g_ref, *, n_experts):
    u = _rms(h_ref[...], fn_ref[...])
    logits = jnp.dot(u, wr_ref[...], precision=lax.Precision.HIGHEST, preferred_element_type=F32)
    lane = lax.broadcasted_iota(jnp.int32, logits.shape, 1).astype(F32)
    lg = jnp.where(lane < n_experts, logits, NEG)
    m1 = jnp.max(lg, axis=-1, keepdims=True)
    i1 = jnp.min(jnp.where(lg == m1, lane, float(LANES)), axis=-1, keepdims=True)
    lg2 = jnp.where(lane == i1, NEG, lg)
    m2 = jnp.max(lg2, axis=-1, keepdims=True)
    i2 = jnp.min(jnp.where(lg2 == m2, lane, float(LANES)), axis=-1, keepdims=True)
    e2 = jnp.exp(m2 - m1)
    den = 1.0 + e2
    g_ref[...] = jnp.where(lane == i1, 1.0 / den, 0.0) + jnp.where(lane == i2, e2 / den, 0.0)


def _router(h, fn, wr, n_experts, tm):
    T, D = h.shape
    return pl.pallas_call(
        functools.partial(_router_kernel, n_experts=n_experts), grid=(T // tm,),
        in_specs=[pl.BlockSpec((tm, D), lambda i: (i, 0)), pl.BlockSpec(fn.shape, lambda i: (0, 0)),
                  pl.BlockSpec(wr.shape, lambda i: (0, 0))],
        out_specs=pl.BlockSpec((tm, LANES), lambda i: (i, 0)),
        out_shape=jax.ShapeDtypeStruct((T, LANES), F32),
        compiler_params=_cparams(("parallel",)), name="router",
    )(h, fn, wr)


def _ffn_kernel(h_ref, fn_ref, gt_ref, wg_ref, wu_ref, wd_ref, o_ref, u_s, *, gated):
    e, f = pl.program_id(1), pl.program_id(2)

    @pl.when((e == 0) & (f == 0))
    def _():
        hh = h_ref[...]
        u_s[...] = _rms(hh, fn_ref[...]).astype(BF16)
        o_ref[...] = hh

    u = u_s[...]
    a = _dot(u, wg_ref[0])
    mid = (a * jax.nn.sigmoid(a) * _dot(u, wu_ref[0])).astype(BF16)
    y = _dot(mid, wd_ref[0])
    if gated:
        gt = gt_ref[...]
        lane = lax.broadcasted_iota(jnp.int32, gt.shape, 1)
        y = y * jnp.sum(jnp.where(lane == e, gt, 0.0), axis=-1, keepdims=True)
    o_ref[...] += y


def _ffn(h, fn, gates, wg, wu, wd, tm, tf, gated):
    T, D = h.shape
    E, _, F = wg.shape
    return pl.pallas_call(
        functools.partial(_ffn_kernel, gated=gated), grid=(T // tm, E, F // tf),
        in_specs=[pl.BlockSpec((tm, D), lambda i, e, f: (i, 0)),
                  pl.BlockSpec(fn.shape, lambda i, e, f: (0, 0)),
                  pl.BlockSpec((tm, LANES), lambda i, e, f: (i, 0)),
                  pl.BlockSpec((1, D, tf), lambda i, e, f: (e, 0, f)),
                  pl.BlockSpec((1, D, tf), lambda i, e, f: (e, 0, f)),
                  pl.BlockSpec((1, tf, D), lambda i, e, f: (e, f, 0))],
        out_specs=pl.BlockSpec((tm, D), lambda i, e, f: (i, 0)),
        out_shape=jax.ShapeDtypeStruct((T, D), F32),
        scratch_shapes=[pltpu.VMEM((tm, D), BF16)],
        compiler_params=_cparams(("parallel", "arbitrary", "arbitrary")), name="ffn",
    )(h, fn, gates, wg, wu, wd)


def _ple_kernel(h_ref, pn_ref, wg_ref, p_ref, wp_ref, fin_ref, o_ref, *, final):
    hh = h_ref[...]
    gate = jax.nn.sigmoid(_dot(_rms(hh, pn_ref[...]).astype(BF16), wg_ref[...]))
    out = hh + gate * _dot(p_ref[...].astype(BF16), wp_ref[...])
    if final:
        out = _rms(out, fin_ref[...])
    o_ref[...] = out


def _ple(h, pn, wg, p, wp, fin, tm, final):
    T, D = h.shape
    full = lambda a: pl.BlockSpec(a.shape, lambda i: (0,) * a.ndim)
    return pl.pallas_call(
        functools.partial(_ple_kernel, final=final), grid=(T // tm,),
        in_specs=[pl.BlockSpec((tm, D), lambda i: (i, 0)), full(pn), full(wg),
                  pl.BlockSpec((tm, p.shape[1]), lambda i: (i, 0)), full(wp), full(fin)],
        out_specs=pl.BlockSpec((tm, D), lambda i: (i, 0)),
        out_shape=jax.ShapeDtypeStruct((T, D), F32),
        compiler_params=_cparams(("parallel",)), name="ple",
    )(h, pn, wg, p, wp, fin)


def _rope_tables(pos):
    pos = pos.astype(F32)[:, None]
    lane = np.arange(LANES)
    inv32 = ROPE_THETA ** (-jnp.arange(32, dtype=F32) / 32)
    inv16 = ROPE_THETA ** (-jnp.arange(16, dtype=F32) / 16)
    a64 = (pos * inv32[None, :])[:, lane % 32]
    a16 = (pos * inv16[None, :])[:, lane % 16]
    lo64 = jnp.asarray((lane % 64) < 32)
    c64, s64 = jnp.cos(a64), jnp.sin(a64)
    c16, s16 = jnp.cos(a16), jnp.sin(a16)
    in_k = jnp.asarray(lane < MLA_ROPE)
    lo_k = jnp.asarray(lane < 16)
    hi_k = jnp.asarray((lane >= 16) & (lane < MLA_ROPE))
    return (c64, jnp.where(lo64, -s64, 0.0), jnp.where(lo64, 0.0, s64), c16, s16,
            jnp.where(in_k, c16, 0.0), jnp.where(lo_k, -s16, 0.0), jnp.where(hi_k, s16, 0.0))


def _ret_tables(L):
    log_g = jnp.log(1.0 - jnp.exp2(-5.0 - jnp.arange(RET_HEADS, dtype=F32)))
    idx = jnp.arange(L, dtype=F32)
    dist = jnp.abs(idx[:, None] - idx[None, :])
    d = jnp.exp(dist[None] * log_g[:, None, None])
    qdec = jnp.exp((idx[:, None] + 1.0) * log_g[None, :])
    kdec = jnp.exp((L - 1.0 - idx)[:, None] * log_g[None, :])
    sdec = jnp.exp(L * log_g)
    wide = lambda t: jnp.repeat(t, RET_DK, axis=1).reshape(L, N_PAIRS, LANES).transpose(1, 0, 2)
    bm = jnp.asarray(np.kron(np.eye(2, dtype=np.float32), np.ones((RET_DK, RET_DV), np.float32)))
    gs = jnp.repeat(sdec, RET_DK).reshape(N_PAIRS, LANES, 1) * bm[None]
    return d, wide(qdec), wide(kdec), gs, bm


def _state_to_pairs(s):
    B = s.shape[0]
    s = s.reshape(B, N_PAIRS, 2, RET_DK, RET_DV)
    eye = jnp.eye(2, dtype=s.dtype)
    out = s[:, :, :, :, None, :] * eye[None, None, :, None, :, None]
    return out.reshape(B, N_PAIRS, LANES, LANES)


def _pairs_to_state(sp):
    B = sp.shape[0]
    s = sp.reshape(B, N_PAIRS, 2, RET_DK, 2, RET_DV)
    return jnp.stack([s[:, :, 0, :, 0, :], s[:, :, 1, :, 1, :]], axis=2).reshape(B, RET_HEADS, RET_DK, RET_DV)


def _uq_perm():
    per = MLA_NOPE + MLA_ROPE
    half = MLA_ROPE // 2
    nope = [h * per + d for h in range(MLA_HEADS) for d in range(MLA_NOPE)]
    x1 = [h * per + MLA_NOPE + f for h in range(MLA_HEADS) for f in range(half)]
    x2 = [h * per + MLA_NOPE + half + f for h in range(MLA_HEADS) for f in range(half)]
    return np.array(nope + x1 + x2, np.int32)


def _big_query_weight(w_uk):
    H, half = MLA_HEADS, MLA_ROPE // 2
    eye = jnp.eye(H, dtype=w_uk.dtype)
    wpad = jnp.pad(w_uk, ((0, 0), (0, 0), (0, QK_PAD - KV_LORA)))
    top = (eye[:, None, :, None] * wpad[:, :, None, :]).reshape(H * MLA_NOPE, H * QK_PAD)
    sel = np.zeros((2 * H * half, H * QK_PAD), np.float32)
    for h in range(H):
        for f in range(half):
            sel[h * half + f, h * QK_PAD + KV_LORA + f] = 1.0
            sel[H * half + h * half + f, h * QK_PAD + KV_LORA + half + f] = 1.0
    return jnp.concatenate([top, jnp.asarray(sel, w_uk.dtype)], axis=0)


def kernel(x_prompt, x_sample, p_prompt, p_sample, cache_ckv, cache_krope, state_ret, attn_norm, w_in, q_norm, w_uq, kv_norm, w_uk, w_uv, ret_norm, w_o, ffn_norm, w_gate_d, w_up_d, w_down_d, w_router, w_gate_e, w_up_e, w_down_e, ple_norm, w_ple_gate, w_ple_proj, final_norm):
    Bp, S, D = x_prompt.shape
    Bs, L, _ = x_sample.shape
    depth = w_in.shape[0]
    P = cache_ckv.shape[2]
    Tp, Ts = Bp * S, Bs * L
    T = Tp + Ts
    assert S % CHUNK == 0 and P % CHUNK == 0 and L == CHUNK and Tp % CHUNK == 0
    n_experts = w_router.shape[-1]

    tm_proj = _pick(T, (640, 512, 256, 128, 64))
    tm_ffn = _pick(T, (1280, 640, 512, 256, 128, 64))
    rb = _pick(S, (512, 256, 128, 64))
    tq = _pick(S, (128, 64))
    kb = _pick(S, (512, 256, 128))
    kb_s = 512
    sk_s = P + L
    sk_pad = -(-sk_s // kb_s) * kb_s

    pos = jnp.concatenate([jnp.tile(jnp.arange(S, dtype=jnp.int32), Bp),
                           P + jnp.tile(jnp.arange(L, dtype=jnp.int32), Bs)])
    rope_tabs = _rope_tables(pos)
    ret_tabs_p = _ret_tables(CHUNK)
    ret_tabs_s = _ret_tables(L)
    perm = _uq_perm()
    row2 = lambda v: v.reshape(1, -1)

    h = jnp.concatenate([x_prompt.reshape(Tp, D), x_sample.reshape(Ts, D)], axis=0)
    outs = {k: [] for k in ("ckv_p", "kro_p", "ret_p", "ckv_s", "kro_s", "ret_s")}
    for l in range(depth):
        win = jnp.pad(w_in[l], ((0, 0), (0, IN_COLS_PAD - IN_COLS))).astype(BF16)
        wuq = w_uq[l][:, perm].astype(BF16)
        wbig = _big_query_weight(w_uk[l]).astype(BF16)
        qr, kr, vr, gr, qx, ckv, kro, kx = _proj(
            h, row2(attn_norm[l]), win, row2(q_norm[l]), wuq, row2(kv_norm[l]), wbig, rope_tabs, tm_proj)

        rn = row2(ret_norm[l])
        zero_state = jnp.zeros((Bp, N_PAIRS, LANES, LANES), F32)
        o_ret_p, st_p = _retention(qr, kr, vr, gr, zero_state, ret_tabs_p, rn, Bp, S, 0, rb)
        o_ret_s, st_s = _retention(qr, kr, vr, gr, _state_to_pairs(state_ret[l].astype(F32)),
                                   ret_tabs_s, rn, Bs, L, Tp, L)

        wuv = w_uv[l].astype(BF16)
        wuv_big = (jnp.eye(MLA_HEADS, dtype=BF16)[:, None, :, None] * wuv[:, :, None, :]
                   ).reshape(MLA_HEADS, KV_LORA, MLA_W)
        o_mla_p = _attention(qx, kx, wuv_big, Bp, S, 0, tq, kb, 0, S, S)
        cache_kx = jnp.concatenate(
            [cache_ckv[l], cache_krope[l], jnp.zeros((Bs, P, QK_PAD - KV_LORA - MLA_ROPE), F32)],
            axis=-1).astype(BF16)
        kx_s = jnp.concatenate([cache_kx, kx[Tp:].reshape(Bs, L, QK_PAD),
                                jnp.zeros((Bs, sk_pad - sk_s, QK_PAD), BF16)], axis=1)
        o_mla_s = _attention(qx, kx_s.reshape(Bs * sk_pad, QK_PAD), wuv_big, Bs, L, Tp, L, kb_s, P,
                             sk_pad, sk_s)

        h = _merge(h, jnp.concatenate([o_ret_p, o_ret_s], axis=0),
                   jnp.concatenate([o_mla_p, o_mla_s], axis=0), w_o[l].astype(BF16), tm_proj)

        fn = row2(ffn_norm[l])
        j = l // 2
        if l % 2 == 0:
            tf = _pick(w_gate_d.shape[-1], (256, 128))
            h = _ffn(h, fn, jnp.ones((T, LANES), F32), w_gate_d[j].astype(BF16)[None],
                     w_up_d[j].astype(BF16)[None], w_down_d[j].astype(BF16)[None], tm_ffn, tf, False)
        else:
            tf = _pick(w_gate_e.shape[-1], (512, 256, 128))
            wr = jnp.pad(w_router[j], ((0, 0), (0, LANES - n_experts)))
            gates = _router(h, fn, wr, n_experts, tm_proj)
            h = _ffn(h, fn, gates, w_gate_e[j].astype(BF16), w_up_e[j].astype(BF16),
                     w_down_e[j].astype(BF16), tm_ffn, tf, True)

        p_l = jnp.concatenate([p_prompt[l].reshape(Tp, -1), p_sample[l].reshape(Ts, -1)], axis=0)
        h = _ple(h, row2(ple_norm[l]), w_ple_gate[l].astype(BF16), p_l, w_ple_proj[l].astype(BF16),
                 row2(final_norm), tm_proj, l == depth - 1)

        outs["ckv_p"].append(ckv[:Tp].reshape(Bp, S, KV_LORA))
        outs["kro_p"].append(kro[:Tp].reshape(Bp, S, MLA_ROPE))
        outs["ret_p"].append(_pairs_to_state(st_p))
        outs["ckv_s"].append(ckv[Tp:].reshape(Bs, L, KV_LORA))
        outs["kro_s"].append(kro[Tp:].reshape(Bs, L, MLA_ROPE))
        outs["ret_s"].append(_pairs_to_state(st_s))

    return (h[:Tp].reshape(Bp, S, D), h[Tp:].reshape(Bs, L, D),
            jnp.stack(outs["ckv_p"]), jnp.stack(outs["kro_p"]), jnp.stack(outs["ret_p"]),
            jnp.stack(outs["ckv_s"]), jnp.stack(outs["kro_s"]), jnp.stack(outs["ret_s"]))
```

```python
import functools

import numpy as np
import jax
import jax.numpy as jnp
from jax import lax
from jax.experimental import pallas as pl
from jax.experimental.pallas import tpu as pltpu

F32 = jnp.float32
BF16 = jnp.bfloat16

CHUNK = 64
CHUNK_SHIFT = 6
RMS_EPS = 1e-6
ROPE_THETA = 10000.0
RET_HEADS = 8
RET_DK = 64
RET_DV = 64
RET_W = RET_HEADS * RET_DK
MLA_HEADS = 8
MLA_NOPE = 64
MLA_ROPE = 32
MLA_V = 64
Q_LORA = 256
KV_LORA = 128
MLA_W = MLA_HEADS * MLA_V
TOP_K = 2
N_PAIRS = RET_HEADS // 2
LANES = 128
SUBLANES = 8
QK_PAD = 256
OFF_Q = RET_W
OFF_K = OFF_Q + RET_W
OFF_V = OFF_K + RET_W
OFF_G = OFF_V + RET_W
OFF_CQ = OFF_G + Q_LORA
OFF_CKV = OFF_CQ + KV_LORA
IN_COLS = OFF_CKV + MLA_ROPE
IN_COLS_PAD = OFF_CKV + LANES
NEG = -1e30
VMEM_LIMIT = 56 * 1024 * 1024


def _pick(n, cands):
    for c in cands:
        if n % c == 0:
            return c
    return n


def _cparams(sem):
    return pltpu.CompilerParams(dimension_semantics=sem, vmem_limit_bytes=VMEM_LIMIT)


def _rms(x, g):
    return x * lax.rsqrt(jnp.mean(x * x, axis=-1, keepdims=True) + RMS_EPS) * g


def _dot(a, b):
    return jnp.dot(a, b, preferred_element_type=F32)


def _dot_nt(a, b):
    return lax.dot_general(a, b, (((1,), (1,)), ((), ())), preferred_element_type=F32)


def _dot_tn(a, b):
    return lax.dot_general(a, b, (((0,), (0,)), ((), ())), preferred_element_type=F32)


def _proj_kernel(h_ref, an_ref, win_ref, qn_ref, wuq_ref, kvn_ref, wbig_ref,
                 c64_ref, sm64_ref, sp64_ref, c16_ref, s16_ref, ck_ref, smk_ref, spk_ref,
                 qr_ref, kr_ref, vr_ref, gr_ref, qx_ref, ckv_ref, kro_ref, kx_ref):
    xn = _rms(h_ref[...], an_ref[...]).astype(BF16)
    c64, sm64, sp64 = c64_ref[...], sm64_ref[...], sp64_ref[...]

    def rope64(z):
        return (z * c64 + pltpu.roll(z, LANES - 32, 1) * sm64 + pltpu.roll(z, 32, 1) * sp64)

    for j in range(RET_W // LANES):
        lo = j * LANES
        zq = _dot(xn, win_ref[:, lo:lo + LANES])
        qr_ref[:, lo:lo + LANES] = rope64(zq)
        zk = _dot(xn, win_ref[:, OFF_Q + lo:OFF_Q + lo + LANES])
        kr_ref[:, lo:lo + LANES] = rope64(zk) * (RET_DK ** -0.5)
    vr_ref[...] = _dot(xn, win_ref[:, OFF_K:OFF_V]).astype(BF16)
    gr_ref[...] = _dot(xn, win_ref[:, OFF_V:OFF_G])

    cq = _dot(xn, win_ref[:, OFF_G:OFF_CQ])
    q = _dot(_rms(cq, qn_ref[...]).astype(BF16), wuq_ref[...])
    nq = MLA_HEADS * MLA_NOPE
    x1, x2 = q[:, nq:nq + LANES], q[:, nq + LANES:nq + 2 * LANES]
    c16, s16 = c16_ref[...], s16_ref[...]
    scale = (MLA_NOPE + MLA_ROPE) ** -0.5
    qcat = jnp.concatenate([q[:, :nq], x1 * c16 - x2 * s16, x2 * c16 + x1 * s16], axis=-1)
    qx_ref[...] = _dot((qcat * scale).astype(BF16), wbig_ref[...]).astype(BF16)

    ckv = _rms(_dot(xn, win_ref[:, OFF_CQ:OFF_CKV]), kvn_ref[...])
    ckv_ref[...] = ckv
    zk = _dot(xn, win_ref[:, OFF_CKV:IN_COLS_PAD])
    kro = (zk * ck_ref[...] + pltpu.roll(zk, LANES - 16, 1) * smk_ref[...]
           + pltpu.roll(zk, 16, 1) * spk_ref[...])
    kro_ref[...] = kro[:, :MLA_ROPE]
    kx_ref[:, :KV_LORA] = ckv.astype(BF16)
    kx_ref[:, KV_LORA:] = kro.astype(BF16)


def _proj(h, an, win, qn, wuq, kvn, wbig, tabs, tm):
    T, D = h.shape
    row = lambda w: pl.BlockSpec((tm, w), lambda i: (i, 0))
    full = lambda a: pl.BlockSpec(a.shape, lambda i: (0,) * a.ndim)
    out_shapes = (
        jax.ShapeDtypeStruct((T, RET_W), F32), jax.ShapeDtypeStruct((T, RET_W), F32),
        jax.ShapeDtypeStruct((T, RET_W), BF16), jax.ShapeDtypeStruct((T, RET_W), F32),
        jax.ShapeDtypeStruct((T, MLA_HEADS * QK_PAD), BF16),
        jax.ShapeDtypeStruct((T, KV_LORA), F32), jax.ShapeDtypeStruct((T, MLA_ROPE), F32),
        jax.ShapeDtypeStruct((T, QK_PAD), BF16))
    return pl.pallas_call(
        _proj_kernel, grid=(T // tm,),
        in_specs=[row(D), full(an), full(win), full(qn), full(wuq), full(kvn), full(wbig)]
                 + [row(LANES)] * 8,
        out_specs=(row(RET_W), row(RET_W), row(RET_W), row(RET_W), row(MLA_HEADS * QK_PAD),
                   row(KV_LORA), row(MLA_ROPE), row(QK_PAD)),
        out_shape=out_shapes, compiler_params=_cparams(("parallel",)), name="proj",
    )(h, an, win, qn, wuq, kvn, wbig, *tabs)


def _ret_kernel(q_ref, k_ref, v_ref, g_ref, s0_ref, d_ref, qd_ref, kd_ref, gs_ref, bm_ref, rn_ref,
                o_ref, so_ref, st_ref, *, n_chunks):
    i = pl.program_id(1)

    @pl.when(i == 0)
    def _():
        st_ref[...] = s0_ref[0]

    lane = lax.broadcasted_iota(jnp.int32, (CHUNK, LANES), 1)
    first = lane < RET_DK

    def chunk(c, carry):
        rows = pl.ds(pl.multiple_of(c * CHUNK, CHUNK), CHUNK)
        for p in range(N_PAIRS):
            cols = slice(p * LANES, (p + 1) * LANES)
            qp, kp, vp = q_ref[rows, cols], k_ref[rows, cols], v_ref[rows, cols]
            kb = kp.astype(BF16)
            s_a = _dot_nt(jnp.where(first, qp, 0.0).astype(BF16), kb) * d_ref[2 * p]
            s_b = _dot_nt(jnp.where(first, 0.0, qp).astype(BF16), kb) * d_ref[2 * p + 1]
            o = jnp.where(first, _dot(s_a.astype(BF16), vp), _dot(s_b.astype(BF16), vp))
            st = st_ref[p]
            o = o + _dot(qp.astype(BF16), st.astype(BF16)) * qd_ref[p]
            upd = _dot_tn((kp * kd_ref[p]).astype(BF16), vp)
            st_ref[p] = st * gs_ref[p] + upd * bm_ref[...]
            oo = o * o
            ss_a = jnp.sum(jnp.where(first, oo, 0.0), axis=-1, keepdims=True)
            ss_b = jnp.sum(jnp.where(first, 0.0, oo), axis=-1, keepdims=True)
            rs = jnp.where(first, lax.rsqrt(ss_a * (1.0 / RET_DV) + RMS_EPS),
                           lax.rsqrt(ss_b * (1.0 / RET_DV) + RMS_EPS))
            g = g_ref[rows, cols]
            o_ref[rows, cols] = (o * rs * rn_ref[:, cols] * (g * jax.nn.sigmoid(g))).astype(BF16)
        return carry

    lax.fori_loop(0, n_chunks, chunk, 0)

    @pl.when(i == pl.num_programs(1) - 1)
    def _():
        so_ref[0] = st_ref[...]


def _retention(qr, kr, vr, gr, s0, tabs, rn, nb, seq, row0, rb):
    nblk = seq // rb
    off = row0 // rb
    tok = pl.BlockSpec((rb, RET_W), lambda b, i: (off + b * nblk + i, 0))
    full = lambda a: pl.BlockSpec(a.shape, lambda b, i: (0,) * a.ndim)
    st_spec = pl.BlockSpec((1, N_PAIRS, LANES, LANES), lambda b, i: (b, 0, 0, 0))
    d, qd, kd, gs, bm = tabs
    return pl.pallas_call(
        functools.partial(_ret_kernel, n_chunks=rb // CHUNK), grid=(nb, nblk),
        in_specs=[tok, tok, tok, tok, st_spec, full(d), full(qd), full(kd), full(gs), full(bm), full(rn)],
        out_specs=(pl.BlockSpec((rb, RET_W), lambda b, i: (b * nblk + i, 0)), st_spec),
        out_shape=(jax.ShapeDtypeStruct((nb * seq, RET_W), BF16),
                   jax.ShapeDtypeStruct((nb, N_PAIRS, LANES, LANES), F32)),
        scratch_shapes=[pltpu.VMEM((N_PAIRS, LANES, LANES), F32)],
        compiler_params=_cparams(("parallel", "arbitrary")), name="retention",
    )(qr, kr, vr, gr, s0, d, qd, kd, gs, bm, rn)


def _attn_kernel(q_ref, k_ref, wuv_ref, o_ref, q_s, m_s, l_s, acc_s, *, tq, kb, q_pos0, sk_valid):
    i = pl.program_id(1)
    R = MLA_HEADS * tq
    for h in range(MLA_HEADS):
        q_s[h * tq:(h + 1) * tq, :] = q_ref[:, h * QK_PAD:(h + 1) * QK_PAD]
    m_s[...] = jnp.full(m_s.shape, NEG, F32)
    l_s[...] = jnp.zeros(l_s.shape, F32)
    acc_s[...] = jnp.zeros(acc_s.shape, F32)

    chunk_end = lambda t: ((t >> CHUNK_SHIFT) + 1) << CHUNK_SHIFT
    kb_shift = kb.bit_length() - 1
    qstart = q_pos0 + i * tq
    lim_first = jnp.minimum(chunk_end(qstart), sk_valid)
    lim_last = jnp.minimum(chunk_end(qstart + tq - 1), sk_valid)
    n_full = lim_first >> kb_shift
    n_blk = (lim_last + kb - 1) >> kb_shift

    def step(j, masked):
        kblk = k_ref[pl.ds(pl.multiple_of(j * kb, kb), kb), :]
        s = _dot_nt(q_s[...], kblk)
        if masked:
            tok = lax.broadcasted_iota(jnp.int32, (R, 1), 0) & (tq - 1)
            row_lim = jnp.minimum(chunk_end(qstart + tok), sk_valid)
            kidx = j * kb + lax.broadcasted_iota(jnp.int32, (1, kb), 1)
            s = jnp.where(kidx < row_lim, s, NEG)
        m_prev = m_s[...]
        m_next = jnp.maximum(m_prev, jnp.max(s, axis=-1, keepdims=True))
        p = jnp.exp(s - jnp.tile(m_next, (1, kb // LANES)))
        alpha = jnp.exp(m_prev - m_next)
        l_s[...] = alpha * l_s[...] + jnp.sum(p, axis=-1, keepdims=True)
        acc_s[...] = alpha * acc_s[...] + _dot(p.astype(BF16), kblk[:, :KV_LORA])
        m_s[...] = m_next

    def full_body(j, c):
        step(j, False)
        return c

    def mask_body(j, c):
        step(j, True)
        return c

    lax.fori_loop(0, n_full, full_body, 0)
    lax.fori_loop(n_full, n_blk, mask_body, 0)

    o_lat = (acc_s[...] / l_s[...]).astype(BF16)
    out = _dot(o_lat[0:tq], wuv_ref[0])
    for h in range(1, MLA_HEADS):
        out = out + _dot(o_lat[h * tq:(h + 1) * tq], wuv_ref[h])
    o_ref[...] = out.astype(BF16)


def _attention(qx, kx, wuv, nb, seq_q, row0, tq, kb, q_pos0, sk, sk_valid):
    nq = seq_q // tq
    off = row0 // tq
    R = MLA_HEADS * tq
    return pl.pallas_call(
        functools.partial(_attn_kernel, tq=tq, kb=kb, q_pos0=q_pos0, sk_valid=sk_valid),
        grid=(nb, nq),
        in_specs=[pl.BlockSpec((tq, MLA_HEADS * QK_PAD), lambda b, i: (off + b * nq + i, 0)),
                  pl.BlockSpec((sk, QK_PAD), lambda b, i: (b, 0)),
                  pl.BlockSpec(wuv.shape, lambda b, i: (0, 0, 0))],
        out_specs=pl.BlockSpec((tq, MLA_W), lambda b, i: (b * nq + i, 0)),
        out_shape=jax.ShapeDtypeStruct((nb * seq_q, MLA_W), BF16),
        scratch_shapes=[pltpu.VMEM((R, QK_PAD), BF16), pltpu.VMEM((R, LANES), F32),
                        pltpu.VMEM((R, LANES), F32), pltpu.VMEM((R, KV_LORA), F32)],
        compiler_params=_cparams(("parallel", "arbitrary")), name="attention",
    )(qx, kx, wuv)


def _merge_kernel(h_ref, a_ref, b_ref, wo_ref, o_ref):
    o_ref[...] = (h_ref[...] + _dot(a_ref[...], wo_ref[:RET_W, :])
                  + _dot(b_ref[...], wo_ref[RET_W:, :]))


def _merge(h, a, b, wo, tm):
    T, D = h.shape
    row = lambda w: pl.BlockSpec((tm, w), lambda i: (i, 0))
    return pl.pallas_call(
        _merge_kernel, grid=(T // tm,),
        in_specs=[row(D), row(RET_W), row(MLA_W), pl.BlockSpec(wo.shape, lambda i: (0, 0))],
        out_specs=row(D), out_shape=jax.ShapeDtypeStruct((T, D), F32),
        compiler_params=_cparams(("parallel",)), name="merge",
    )(h, a, b, wo)


def _ffn_kernel(h_ref, fn_ref, wg_ref, wu_ref, wd_ref, o_ref, u_s):
    @pl.when(pl.program_id(1) == 0)
    def _():
        hh = h_ref[...]
        u_s[...] = _rms(hh, fn_ref[...]).astype(BF16)
        o_ref[...] = hh

    u = u_s[...]
    a = _dot(u, wg_ref[...])
    mid = (a * jax.nn.sigmoid(a) * _dot(u, wu_ref[...])).astype(BF16)
    o_ref[...] += _dot(mid, wd_ref[...])


def _ffn(h, fn, wg, wu, wd, tm, tf):
    T, D = h.shape
    F = wg.shape[1]
    return pl.pallas_call(
        _ffn_kernel, grid=(T // tm, F // tf),
        in_specs=[pl.BlockSpec((tm, D), lambda i, f: (i, 0)),
                  pl.BlockSpec(fn.shape, lambda i, f: (0, 0)),
                  pl.BlockSpec((D, tf), lambda i, f: (0, f)),
                  pl.BlockSpec((D, tf), lambda i, f: (0, f)),
                  pl.BlockSpec((tf, D), lambda i, f: (f, 0))],
        out_specs=pl.BlockSpec((tm, D), lambda i, f: (i, 0)),
        out_shape=jax.ShapeDtypeStruct((T, D), F32),
        scratch_shapes=[pltpu.VMEM((tm, D), BF16)],
        compiler_params=_cparams(("parallel", "arbitrary")), name="ffn",
    )(h, fn, wg, wu, wd)


L_E1, L_E2, L_W1, L_W2, L_R1, L_R2 = range(6)


def _router_kernel(h_ref, fn_ref, wr_ref, meta_ref, cnt_ref, run_s, *, n_experts):
    tm = h_ref.shape[0]

    @pl.when(pl.program_id(0) == 0)
    def _():
        run_s[...] = jnp.zeros(run_s.shape, F32)

    u = _rms(h_ref[...], fn_ref[...])
    logits = jnp.dot(u, wr_ref[...], precision=lax.Precision.HIGHEST, preferred_element_type=F32)
    lane = lax.broadcasted_iota(jnp.int32, logits.shape, 1).astype(F32)
    lg = jnp.where(lane < n_experts, logits, NEG)
    m1 = jnp.max(lg, axis=-1, keepdims=True)
    i1 = jnp.min(jnp.where(lg == m1, lane, float(LANES)), axis=-1, keepdims=True)
    lg2 = jnp.where(lane == i1, NEG, lg)
    m2 = jnp.max(lg2, axis=-1, keepdims=True)
    i2 = jnp.min(jnp.where(lg2 == m2, lane, float(LANES)), axis=-1, keepdims=True)
    e2 = jnp.exp(m2 - m1)
    den = 1.0 + e2

    hit = jnp.where((lane == i1) | (lane == i2), 1.0, 0.0)
    r_i = lax.broadcasted_iota(jnp.int32, (tm, tm), 0)
    c_i = lax.broadcasted_iota(jnp.int32, (tm, tm), 1)
    before = jnp.where(c_i < r_i, 1.0, 0.0).astype(BF16)
    prefix = _dot(before, hit.astype(BF16)) + run_s[0:1, :]
    r1 = jnp.sum(jnp.where(lane == i1, prefix, 0.0), axis=-1, keepdims=True)
    r2 = jnp.sum(jnp.where(lane == i2, prefix, 0.0), axis=-1, keepdims=True)
    run_s[...] = run_s[...] + jnp.sum(hit, axis=0, keepdims=True)
    cnt_ref[...] = run_s[...]

    meta = jnp.where(lane == L_E1, i1, 0.0)
    for ln, val in ((L_E2, i2), (L_W1, 1.0 / den), (L_W2, e2 / den), (L_R1, r1), (L_R2, r2)):
        meta = jnp.where(lane == ln, val, meta)
    meta_ref[...] = meta


def _router(h, fn, wr, n_experts, tm):
    T, D = h.shape
    return pl.pallas_call(
        functools.partial(_router_kernel, n_experts=n_experts), grid=(T // tm,),
        in_specs=[pl.BlockSpec((tm, D), lambda i: (i, 0)), pl.BlockSpec(fn.shape, lambda i: (0, 0)),
                  pl.BlockSpec(wr.shape, lambda i: (0, 0))],
        out_specs=(pl.BlockSpec((tm, LANES), lambda i: (i, 0)),
                   pl.BlockSpec((SUBLANES, LANES), lambda i: (0, 0))),
        out_shape=(jax.ShapeDtypeStruct((T, LANES), F32), jax.ShapeDtypeStruct((SUBLANES, LANES), F32)),
        scratch_shapes=[pltpu.VMEM((SUBLANES, LANES), F32)],
        compiler_params=_cparams(("arbitrary",)), name="router",
    )(h, fn, wr)


def _tile_index_copy(d_hbm, idx_s, sem, tile, slot):
    return pltpu.make_async_copy(d_hbm.at[tile], idx_s.at[slot], sem.at[slot])


def _dispatch_kernel(d_hbm, h_ref, xs_in, xs_out, idx_s, isem, rsem):
    del xs_in
    tm = h_ref.shape[0]
    i, n = pl.program_id(0), pl.num_programs(0)
    slot = i % 2

    @pl.when(i == 0)
    def _():
        _tile_index_copy(d_hbm, idx_s, isem, 0, 0).start()

    @pl.when(i + 1 < n)
    def _():
        _tile_index_copy(d_hbm, idx_s, isem, i + 1, 1 - slot).start()

    _tile_index_copy(d_hbm, idx_s, isem, i, slot).wait()

    def row_copy(r, dst):
        return pltpu.make_async_copy(h_ref.at[pl.ds(r, 1)], xs_out.at[pl.ds(dst, 1)], rsem)

    def body(r, c):
        for k in range(TOP_K):
            row_copy(r, idx_s[slot, k * tm + r]).start()
        return c

    lax.fori_loop(0, tm, body, 0)
    for k in range(TOP_K):
        pltpu.make_async_copy(h_ref, xs_out.at[pl.ds(0, tm)], rsem).wait()


def _dispatch(d_tiles, h, xs_init, tm):
    T, D = h.shape
    return pl.pallas_call(
        _dispatch_kernel, grid=(T // tm,),
        in_specs=[pl.BlockSpec(memory_space=pl.ANY), pl.BlockSpec((tm, D), lambda i: (i, 0)),
                  pl.BlockSpec(memory_space=pl.ANY)],
        out_specs=pl.BlockSpec(memory_space=pl.ANY),
        out_shape=jax.ShapeDtypeStruct(xs_init.shape, F32),
        scratch_shapes=[pltpu.SMEM((2, TOP_K * tm), jnp.int32), pltpu.SemaphoreType.DMA((2,)),
                        pltpu.SemaphoreType.DMA(())],
        input_output_aliases={2: 0},
        compiler_params=_cparams(("arbitrary",)), name="dispatch",
    )(d_tiles, h, xs_init)


def _gmm_kernel(te_ref, nu_ref, x_ref, fn_ref, wg_ref, wu_ref, wd_ref, o_ref, u_s):
    i, f = pl.program_id(0), pl.program_id(1)
    used = i < nu_ref[0]

    @pl.when(f == 0)
    def _():
        o_ref[...] = jnp.zeros(o_ref.shape, F32)

    @pl.when(used & (f == 0))
    def _():
        u_s[...] = _rms(x_ref[...], fn_ref[...]).astype(BF16)

    @pl.when(used)
    def _():
        u = u_s[...]
        a = _dot(u, wg_ref[0])
        mid = (a * jax.nn.sigmoid(a) * _dot(u, wu_ref[0])).astype(BF16)
        o_ref[...] += _dot(mid, wd_ref[0])


def _gmm(tile_expert, n_used, xs, fn, wg, wu, wd, tmg, tf):
    R, D = xs.shape
    F = wg.shape[-1]
    nf = F // tf
    f_eff = lambda i, f, nu: jnp.where(i < nu[0], f, nf - 1)
    grid_spec = pltpu.PrefetchScalarGridSpec(
        num_scalar_prefetch=2, grid=(R // tmg, nf),
        in_specs=[pl.BlockSpec((tmg, D), lambda i, f, te, nu: (i, 0)),
                  pl.BlockSpec(fn.shape, lambda i, f, te, nu: (0, 0)),
                  pl.BlockSpec((1, D, tf), lambda i, f, te, nu: (te[i], 0, f_eff(i, f, nu))),
                  pl.BlockSpec((1, D, tf), lambda i, f, te, nu: (te[i], 0, f_eff(i, f, nu))),
                  pl.BlockSpec((1, tf, D), lambda i, f, te, nu: (te[i], f_eff(i, f, nu), 0))],
        out_specs=pl.BlockSpec((tmg, D), lambda i, f, te, nu: (i, 0)),
        scratch_shapes=[pltpu.VMEM((tmg, D), BF16)])
    return pl.pallas_call(
        _gmm_kernel, grid_spec=grid_spec, out_shape=jax.ShapeDtypeStruct((R, D), F32),
        compiler_params=_cparams(("arbitrary", "arbitrary")), name="experts",
    )(tile_expert, n_used, xs, fn, wg, wu, wd)


def _combine_kernel(d_hbm, h_ref, meta_ref, ys_hbm, o_ref, idx_s, ya_s, yb_s, isem, rsem):
    tm = h_ref.shape[0]
    i, n = pl.program_id(0), pl.num_programs(0)
    slot = i % 2

    def fetch(tile, s):
        cp = _tile_index_copy(d_hbm, idx_s, isem, tile, s)
        cp.start()
        cp.wait()

        def body(r, c):
            pltpu.make_async_copy(ys_hbm.at[pl.ds(idx_s[s, r], 1)], ya_s.at[s, pl.ds(r, 1)],
                                  rsem.at[s]).start()
            pltpu.make_async_copy(ys_hbm.at[pl.ds(idx_s[s, tm + r], 1)], yb_s.at[s, pl.ds(r, 1)],
                                  rsem.at[s]).start()
            return c

        lax.fori_loop(0, tm, body, 0)

    @pl.when(i == 0)
    def _():
        fetch(0, 0)

    @pl.when(i + 1 < n)
    def _():
        fetch(i + 1, 1 - slot)

    pltpu.make_async_copy(ys_hbm.at[pl.ds(0, tm)], ya_s.at[slot], rsem.at[slot]).wait()
    pltpu.make_async_copy(ys_hbm.at[pl.ds(0, tm)], yb_s.at[slot], rsem.at[slot]).wait()
    meta = meta_ref[...]
    o_ref[...] = (h_ref[...] + meta[:, L_W1:L_W1 + 1] * ya_s[slot]
                  + meta[:, L_W2:L_W2 + 1] * yb_s[slot])


def _combine(d_tiles, h, meta, ys, tm):
    T, D = h.shape
    return pl.pallas_call(
        _combine_kernel, grid=(T // tm,),
        in_specs=[pl.BlockSpec(memory_space=pl.ANY), pl.BlockSpec((tm, D), lambda i: (i, 0)),
                  pl.BlockSpec((tm, LANES), lambda i: (i, 0)), pl.BlockSpec(memory_space=pl.ANY)],
        out_specs=pl.BlockSpec((tm, D), lambda i: (i, 0)),
        out_shape=jax.ShapeDtypeStruct((T, D), F32),
        scratch_shapes=[pltpu.SMEM((2, TOP_K * tm), jnp.int32), pltpu.VMEM((2, tm, D), F32),
                        pltpu.VMEM((2, tm, D), F32), pltpu.SemaphoreType.DMA((2,)),
                        pltpu.SemaphoreType.DMA((2,))],
        compiler_params=_cparams(("arbitrary",)), name="combine",
    )(d_tiles, h, meta, ys)


def _moe(h, fn, w_router, wg, wu, wd, tm, tmg, tf):
    T, D = h.shape
    E = w_router.shape[-1]
    meta, counts = _router(h, fn, jnp.pad(w_router, ((0, 0), (0, LANES - E))), E, tm)
    cnt = counts[0, :E].astype(jnp.int32)
    padded = ((cnt + tmg - 1) // tmg) * tmg
    ends = jnp.cumsum(padded)
    off = ends - padded
    n_tiles = -(-TOP_K * T // tmg) + E
    tile_expert = jnp.minimum(
        jnp.sum(ends[None, :] <= (jnp.arange(n_tiles, dtype=jnp.int32) * tmg)[:, None], axis=1), E - 1
    ).astype(jnp.int32)
    n_used = (ends[-1] // tmg).astype(jnp.int32).reshape(1)
    e12 = meta[:, L_E1:L_E2 + 1].astype(jnp.int32)
    dest = off[e12] + meta[:, L_R1:L_R2 + 1].astype(jnp.int32)
    d_tiles = dest.reshape(T // tm, tm, TOP_K).transpose(0, 2, 1).reshape(T // tm, TOP_K * tm)

    xs = _dispatch(d_tiles, h, jnp.zeros((n_tiles * tmg, D), F32), tm)
    ys = _gmm(tile_expert, n_used, xs, fn, wg, wu, wd, tmg, tf)
    return _combine(d_tiles, h, meta, ys, tm)


def _ple_kernel(h_ref, pn_ref, wg_ref, p_ref, wp_ref, fin_ref, o_ref, *, final):
    hh = h_ref[...]
    gate = jax.nn.sigmoid(_dot(_rms(hh, pn_ref[...]).astype(BF16), wg_ref[...]))
    out = hh + gate * _dot(p_ref[...].astype(BF16), wp_ref[...])
    if final:
        out = _rms(out, fin_ref[...])
    o_ref[...] = out


def _ple(h, pn, wg, p, wp, fin, tm, final):
    T, D = h.shape
    full = lambda a: pl.BlockSpec(a.shape, lambda i: (0,) * a.ndim)
    return pl.pallas_call(
        functools.partial(_ple_kernel, final=final), grid=(T // tm,),
        in_specs=[pl.BlockSpec((tm, D), lambda i: (i, 0)), full(pn), full(wg),
                  pl.BlockSpec((tm, p.shape[1]), lambda i: (i, 0)), full(wp), full(fin)],
        out_specs=pl.BlockSpec((tm, D), lambda i: (i, 0)),
        out_shape=jax.ShapeDtypeStruct((T, D), F32),
        compiler_params=_cparams(("parallel",)), name="ple",
    )(h, pn, wg, p, wp, fin)


def _rope_tables(pos):
    pos = pos.astype(F32)[:, None]
    lane = np.arange(LANES)
    inv32 = ROPE_THETA ** (-jnp.arange(32, dtype=F32) / 32)
    inv16 = ROPE_THETA ** (-jnp.arange(16, dtype=F32) / 16)
    a64 = (pos * inv32[None, :])[:, lane % 32]
    a16 = (pos * inv16[None, :])[:, lane % 16]
    lo64 = jnp.asarray((lane % 64) < 32)
    c64, s64 = jnp.cos(a64), jnp.sin(a64)
    c16, s16 = jnp.cos(a16), jnp.sin(a16)
    in_k = jnp.asarray(lane < MLA_ROPE)
    lo_k = jnp.asarray(lane < 16)
    hi_k = jnp.asarray((lane >= 16) & (lane < MLA_ROPE))
    return (c64, jnp.where(lo64, -s64, 0.0), jnp.where(lo64, 0.0, s64), c16, s16,
            jnp.where(in_k, c16, 0.0), jnp.where(lo_k, -s16, 0.0), jnp.where(hi_k, s16, 0.0))


def _ret_tables(L):
    log_g = jnp.log(1.0 - jnp.exp2(-5.0 - jnp.arange(RET_HEADS, dtype=F32)))
    idx = jnp.arange(L, dtype=F32)
    dist = jnp.abs(idx[:, None] - idx[None, :])
    d = jnp.exp(dist[None] * log_g[:, None, None])
    qdec = jnp.exp((idx[:, None] + 1.0) * log_g[None, :])
    kdec = jnp.exp((L - 1.0 - idx)[:, None] * log_g[None, :])
    sdec = jnp.exp(L * log_g)
    wide = lambda t: jnp.repeat(t, RET_DK, axis=1).reshape(L, N_PAIRS, LANES).transpose(1, 0, 2)
    bm = jnp.asarray(np.kron(np.eye(2, dtype=np.float32), np.ones((RET_DK, RET_DV), np.float32)))
    gs = jnp.repeat(sdec, RET_DK).reshape(N_PAIRS, LANES, 1) * bm[None]
    return d, wide(qdec), wide(kdec), gs, bm


def _state_to_pairs(s):
    B = s.shape[0]
    s = s.reshape(B, N_PAIRS, 2, RET_DK, RET_DV)
    eye = jnp.eye(2, dtype=s.dtype)
    out = s[:, :, :, :, None, :] * eye[None, None, :, None, :, None]
    return out.reshape(B, N_PAIRS, LANES, LANES)


def _pairs_to_state(sp):
    B = sp.shape[0]
    s = sp.reshape(B, N_PAIRS, 2, RET_DK, 2, RET_DV)
    return jnp.stack([s[:, :, 0, :, 0, :], s[:, :, 1, :, 1, :]], axis=2).reshape(B, RET_HEADS, RET_DK, RET_DV)


def _uq_perm():
    per = MLA_NOPE + MLA_ROPE
    half = MLA_ROPE // 2
    nope = [h * per + d for h in range(MLA_HEADS) for d in range(MLA_NOPE)]
    x1 = [h * per + MLA_NOPE + f for h in range(MLA_HEADS) for f in range(half)]
    x2 = [h * per + MLA_NOPE + half + f for h in range(MLA_HEADS) for f in range(half)]
    return np.array(nope + x1 + x2, np.int32)


def _big_query_weight(w_uk):
    H, half = MLA_HEADS, MLA_ROPE // 2
    eye = jnp.eye(H, dtype=w_uk.dtype)
    wpad = jnp.pad(w_uk, ((0, 0), (0, 0), (0, QK_PAD - KV_LORA)))
    top = (eye[:, None, :, None] * wpad[:, :, None, :]).reshape(H * MLA_NOPE, H * QK_PAD)
    sel = np.zeros((2 * H * half, H * QK_PAD), np.float32)
    for h in range(H):
        for f in range(half):
            sel[h * half + f, h * QK_PAD + KV_LORA + f] = 1.0
            sel[H * half + h * half + f, h * QK_PAD + KV_LORA + half + f] = 1.0
    return jnp.concatenate([top, jnp.asarray(sel, w_uk.dtype)], axis=0)


def kernel(x_prompt, x_sample, p_prompt, p_sample, cache_ckv, cache_krope, state_ret, attn_norm, w_in, q_norm, w_uq, kv_norm, w_uk, w_uv, ret_norm, w_o, ffn_norm, w_gate_d, w_up_d, w_down_d, w_router, w_gate_e, w_up_e, w_down_e, ple_norm, w_ple_gate, w_ple_proj, final_norm):
    Bp, S, D = x_prompt.shape
    Bs, L, _ = x_sample.shape
    depth = w_in.shape[0]
    P = cache_ckv.shape[2]
    Tp, Ts = Bp * S, Bs * L
    T = Tp + Ts
    assert S % CHUNK == 0 and P % CHUNK == 0 and L == CHUNK and Tp % CHUNK == 0
    assert w_router.shape[-1] >= TOP_K

    tm_proj = _pick(T, (640, 512, 256, 128, 64))
    tm_ffn = _pick(T, (1280, 640, 512, 256, 128, 64))
    tm_gmm = 1024
    rb = _pick(S, (512, 256, 128, 64))
    tq = _pick(S, (128, 64))
    kb = _pick(S, (512, 256, 128))
    kb_s = 512
    sk_s = P + L
    sk_pad = -(-sk_s // kb_s) * kb_s

    pos = jnp.concatenate([jnp.tile(jnp.arange(S, dtype=jnp.int32), Bp),
                           P + jnp.tile(jnp.arange(L, dtype=jnp.int32), Bs)])
    rope_tabs = _rope_tables(pos)
    ret_tabs_p = _ret_tables(CHUNK)
    ret_tabs_s = _ret_tables(L)
    perm = _uq_perm()
    row2 = lambda v: v.reshape(1, -1)

    h = jnp.concatenate([x_prompt.reshape(Tp, D), x_sample.reshape(Ts, D)], axis=0)
    outs = {k: [] for k in ("ckv_p", "kro_p", "ret_p", "ckv_s", "kro_s", "ret_s")}
    for l in range(depth):
        win = jnp.pad(w_in[l], ((0, 0), (0, IN_COLS_PAD - IN_COLS))).astype(BF16)
        wuq = w_uq[l][:, perm].astype(BF16)
        wbig = _big_query_weight(w_uk[l]).astype(BF16)
        qr, kr, vr, gr, qx, ckv, kro, kx = _proj(
            h, row2(attn_norm[l]), win, row2(q_norm[l]), wuq, row2(kv_norm[l]), wbig, rope_tabs, tm_proj)

        rn = row2(ret_norm[l])
        zero_state = jnp.zeros((Bp, N_PAIRS, LANES, LANES), F32)
        o_ret_p, st_p = _retention(qr, kr, vr, gr, zero_state, ret_tabs_p, rn, Bp, S, 0, rb)
        o_ret_s, st_s = _retention(qr, kr, vr, gr, _state_to_pairs(state_ret[l].astype(F32)),
                                   ret_tabs_s, rn, Bs, L, Tp, L)

        wuv = w_uv[l].astype(BF16)
        wuv_big = (jnp.eye(MLA_HEADS, dtype=BF16)[:, None, :, None] * wuv[:, :, None, :]
                   ).reshape(MLA_HEADS, KV_LORA, MLA_W)
        o_mla_p = _attention(qx, kx, wuv_big, Bp, S, 0, tq, kb, 0, S, S)
        cache_kx = jnp.concatenate(
            [cache_ckv[l], cache_krope[l], jnp.zeros((Bs, P, QK_PAD - KV_LORA - MLA_ROPE), F32)],
            axis=-1).astype(BF16)
        kx_s = jnp.concatenate([cache_kx, kx[Tp:].reshape(Bs, L, QK_PAD),
                                jnp.zeros((Bs, sk_pad - sk_s, QK_PAD), BF16)], axis=1)
        o_mla_s = _attention(qx, kx_s.reshape(Bs * sk_pad, QK_PAD), wuv_big, Bs, L, Tp, L, kb_s, P,
                             sk_pad, sk_s)

        h = _merge(h, jnp.concatenate([o_ret_p, o_ret_s], axis=0),
                   jnp.concatenate([o_mla_p, o_mla_s], axis=0), w_o[l].astype(BF16), tm_proj)

        fn = row2(ffn_norm[l])
        j = l // 2
        if l % 2 == 0:
            tf = _pick(w_gate_d.shape[-1], (256, 128))
            h = _ffn(h, fn, w_gate_d[j].astype(BF16), w_up_d[j].astype(BF16),
                     w_down_d[j].astype(BF16), tm_ffn, tf)
        else:
            tf = _pick(w_gate_e.shape[-1], (512, 256, 128))
            h = _moe(h, fn, w_router[j], w_gate_e[j].astype(BF16), w_up_e[j].astype(BF16),
                     w_down_e[j].astype(BF16), tm_proj, tm_gmm, tf)

        p_l = jnp.concatenate([p_prompt[l].reshape(Tp, -1), p_sample[l].reshape(Ts, -1)], axis=0)
        h = _ple(h, row2(ple_norm[l]), w_ple_gate[l].astype(BF16), p_l, w_ple_proj[l].astype(BF16),
                 row2(final_norm), tm_proj, l == depth - 1)

        outs["ckv_p"].append(ckv[:Tp].reshape(Bp, S, KV_LORA))
        outs["kro_p"].append(kro[:Tp].reshape(Bp, S, MLA_ROPE))
        outs["ret_p"].append(_pairs_to_state(st_p))
        outs["ckv_s"].append(ckv[Tp:].reshape(Bs, L, KV_LORA))
        outs["kro_s"].append(kro[Tp:].reshape(Bs, L, MLA_ROPE))
        outs["ret_s"].append(_pairs_to_state(st_s))

    return (h[:Tp].reshape(Bp, S, D), h[Tp:].reshape(Bs, L, D),
            jnp.stack(outs["ckv_p"]), jnp.stack(outs["kro_p"]), jnp.stack(outs["ret_p"]),
            jnp.stack(outs["ckv_s"]), jnp.stack(outs["kro_s"]), jnp.stack(outs["ret_s"]))
```

```python
import functools

import numpy as np
import jax
import jax.numpy as jnp
from jax import lax
from jax.experimental import pallas as pl
from jax.experimental.pallas import tpu as pltpu

F32 = jnp.float32
BF16 = jnp.bfloat16

CHUNK = 64
CHUNK_SHIFT = 6
RMS_EPS = 1e-6
ROPE_THETA = 10000.0
RET_HEADS = 8
RET_DK = 64
RET_DV = 64
RET_W = RET_HEADS * RET_DK
MLA_HEADS = 8
MLA_NOPE = 64
MLA_ROPE = 32
MLA_V = 64
Q_LORA = 256
KV_LORA = 128
MLA_W = MLA_HEADS * MLA_V
TOP_K = 2
N_PAIRS = RET_HEADS // 2
LANES = 128
SUBLANES = 8
QK_PAD = 256
OFF_Q = RET_W
OFF_K = OFF_Q + RET_W
OFF_V = OFF_K + RET_W
OFF_G = OFF_V + RET_W
OFF_CQ = OFF_G + Q_LORA
OFF_CKV = OFF_CQ + KV_LORA
IN_COLS = OFF_CKV + MLA_ROPE
IN_COLS_PAD = OFF_CKV + LANES
NEG = -1e30
VMEM_LIMIT = 56 * 1024 * 1024


def _pick(n, cands):
    for c in cands:
        if n % c == 0:
            return c
    return n


def _cparams(sem):
    return pltpu.CompilerParams(dimension_semantics=sem, vmem_limit_bytes=VMEM_LIMIT)


def _rms(x, g):
    return x * lax.rsqrt(jnp.mean(x * x, axis=-1, keepdims=True) + RMS_EPS) * g


def _dot(a, b):
    return jnp.dot(a, b, preferred_element_type=F32)


def _dot_nt(a, b):
    return lax.dot_general(a, b, (((1,), (1,)), ((), ())), preferred_element_type=F32)


def _dot_tn(a, b):
    return lax.dot_general(a, b, (((0,), (0,)), ((), ())), preferred_element_type=F32)


def _proj_kernel(h_ref, an_ref, win_ref, qn_ref, wuq_ref, kvn_ref, wbig_ref,
                 c64_ref, sm64_ref, sp64_ref, c16_ref, s16_ref, ck_ref, smk_ref, spk_ref,
                 qr_ref, kr_ref, vr_ref, gr_ref, qx_ref, ckv_ref, kro_ref, kx_ref):
    xn = _rms(h_ref[...], an_ref[...]).astype(BF16)
    c64, sm64, sp64 = c64_ref[...], sm64_ref[...], sp64_ref[...]

    def rope64(z):
        return (z * c64 + pltpu.roll(z, LANES - 32, 1) * sm64 + pltpu.roll(z, 32, 1) * sp64)

    for j in range(RET_W // LANES):
        lo = j * LANES
        zq = _dot(xn, win_ref[:, lo:lo + LANES])
        qr_ref[:, lo:lo + LANES] = rope64(zq)
        zk = _dot(xn, win_ref[:, OFF_Q + lo:OFF_Q + lo + LANES])
        kr_ref[:, lo:lo + LANES] = rope64(zk) * (RET_DK ** -0.5)
    vr_ref[...] = _dot(xn, win_ref[:, OFF_K:OFF_V]).astype(BF16)
    gr_ref[...] = _dot(xn, win_ref[:, OFF_V:OFF_G])

    cq = _dot(xn, win_ref[:, OFF_G:OFF_CQ])
    q = _dot(_rms(cq, qn_ref[...]).astype(BF16), wuq_ref[...])
    nq = MLA_HEADS * MLA_NOPE
    x1, x2 = q[:, nq:nq + LANES], q[:, nq + LANES:nq + 2 * LANES]
    c16, s16 = c16_ref[...], s16_ref[...]
    scale = (MLA_NOPE + MLA_ROPE) ** -0.5 * float(np.log2(np.e))
    qcat = jnp.concatenate([q[:, :nq], x1 * c16 - x2 * s16, x2 * c16 + x1 * s16], axis=-1)
    qx_ref[...] = _dot((qcat * scale).astype(BF16), wbig_ref[...]).astype(BF16)

    ckv = _rms(_dot(xn, win_ref[:, OFF_CQ:OFF_CKV]), kvn_ref[...])
    ckv_ref[...] = ckv
    zk = _dot(xn, win_ref[:, OFF_CKV:IN_COLS_PAD])
    kro = (zk * ck_ref[...] + pltpu.roll(zk, LANES - 16, 1) * smk_ref[...]
           + pltpu.roll(zk, 16, 1) * spk_ref[...])
    kro_ref[...] = kro[:, :MLA_ROPE]
    kx_ref[:, :KV_LORA] = ckv.astype(BF16)
    kx_ref[:, KV_LORA:] = kro.astype(BF16)


def _proj(h, an, win, qn, wuq, kvn, wbig, tabs, tab_tile, tm):
    T, D = h.shape
    row = lambda w: pl.BlockSpec((tm, w), lambda i: (i, 0))
    tab = pl.BlockSpec((tm, LANES), lambda i: (tab_tile(i), 0))
    full = lambda a: pl.BlockSpec(a.shape, lambda i: (0,) * a.ndim)
    out_shapes = (
        jax.ShapeDtypeStruct((T, RET_W), F32), jax.ShapeDtypeStruct((T, RET_W), F32),
        jax.ShapeDtypeStruct((T, RET_W), BF16), jax.ShapeDtypeStruct((T, RET_W), F32),
        jax.ShapeDtypeStruct((T, MLA_HEADS * QK_PAD), BF16),
        jax.ShapeDtypeStruct((T, KV_LORA), F32), jax.ShapeDtypeStruct((T, MLA_ROPE), F32),
        jax.ShapeDtypeStruct((T, QK_PAD), BF16))
    return pl.pallas_call(
        _proj_kernel, grid=(T // tm,),
        in_specs=[row(D), full(an), full(win), full(qn), full(wuq), full(kvn), full(wbig)]
                 + [tab] * 8,
        out_specs=(row(RET_W), row(RET_W), row(RET_W), row(RET_W), row(MLA_HEADS * QK_PAD),
                   row(KV_LORA), row(MLA_ROPE), row(QK_PAD)),
        out_shape=out_shapes, compiler_params=_cparams(("parallel",)), name="proj",
    )(h, an, win, qn, wuq, kvn, wbig, *tabs)


def _ret_kernel(q_ref, k_ref, v_ref, g_ref, s0_ref, d_ref, qd_ref, kd_ref, gs_ref, bm_ref, rn_ref,
                o_ref, so_ref, st_ref, *, n_chunks):
    i = pl.program_id(1)

    @pl.when(i == 0)
    def _():
        st_ref[...] = s0_ref[0]

    lane = lax.broadcasted_iota(jnp.int32, (CHUNK, LANES), 1)
    first = lane < RET_DK

    def chunk(c, carry):
        rows = pl.ds(pl.multiple_of(c * CHUNK, CHUNK), CHUNK)
        for p in range(N_PAIRS):
            cols = slice(p * LANES, (p + 1) * LANES)
            qp, kp, vp = q_ref[rows, cols], k_ref[rows, cols], v_ref[rows, cols]
            kb = kp.astype(BF16)
            s_a = _dot_nt(jnp.where(first, qp, 0.0).astype(BF16), kb) * d_ref[2 * p]
            s_b = _dot_nt(jnp.where(first, 0.0, qp).astype(BF16), kb) * d_ref[2 * p + 1]
            o = jnp.where(first, _dot(s_a.astype(BF16), vp), _dot(s_b.astype(BF16), vp))
            st = st_ref[p]
            o = o + _dot(qp.astype(BF16), st.astype(BF16)) * qd_ref[p]
            upd = _dot_tn((kp * kd_ref[p]).astype(BF16), vp)
            st_ref[p] = st * gs_ref[p] + upd * bm_ref[...]
            oo = o * o
            ss_a = jnp.sum(jnp.where(first, oo, 0.0), axis=-1, keepdims=True)
            ss_b = jnp.sum(jnp.where(first, 0.0, oo), axis=-1, keepdims=True)
            rs = jnp.where(first, lax.rsqrt(ss_a * (1.0 / RET_DV) + RMS_EPS),
                           lax.rsqrt(ss_b * (1.0 / RET_DV) + RMS_EPS))
            g = g_ref[rows, cols]
            o_ref[rows, cols] = (o * rs * rn_ref[:, cols] * (g * jax.nn.sigmoid(g))).astype(BF16)
        return carry

    lax.fori_loop(0, n_chunks, chunk, 0)

    @pl.when(i == pl.num_programs(1) - 1)
    def _():
        so_ref[0] = st_ref[...]


def _retention(qr, kr, vr, gr, s0, tabs, rn, nb, seq, row0, rb):
    nblk = seq // rb
    off = row0 // rb
    tok = pl.BlockSpec((rb, RET_W), lambda b, i: (off + b * nblk + i, 0))
    full = lambda a: pl.BlockSpec(a.shape, lambda b, i: (0,) * a.ndim)
    st_spec = pl.BlockSpec((1, N_PAIRS, LANES, LANES), lambda b, i: (b, 0, 0, 0))
    d, qd, kd, gs, bm = tabs
    return pl.pallas_call(
        functools.partial(_ret_kernel, n_chunks=rb // CHUNK), grid=(nb, nblk),
        in_specs=[tok, tok, tok, tok, st_spec, full(d), full(qd), full(kd), full(gs), full(bm), full(rn)],
        out_specs=(pl.BlockSpec((rb, RET_W), lambda b, i: (b * nblk + i, 0)), st_spec),
        out_shape=(jax.ShapeDtypeStruct((nb * seq, RET_W), BF16),
                   jax.ShapeDtypeStruct((nb, N_PAIRS, LANES, LANES), F32)),
        scratch_shapes=[pltpu.VMEM((N_PAIRS, LANES, LANES), F32)],
        compiler_params=_cparams(("parallel", "arbitrary")), name="retention",
    )(qr, kr, vr, gr, s0, d, qd, kd, gs, bm, rn)


def _attn_kernel(q_ref, k_ref, wuv_ref, o_ref, q_s, s0_s, s1_s, m_s, l_s, acc_s,
                 *, tq, kb, q_pos0, sk_valid):
    i = pl.program_id(1)
    R = MLA_HEADS * tq
    for h in range(MLA_HEADS):
        q_s[h * tq:(h + 1) * tq, :] = q_ref[:, h * QK_PAD:(h + 1) * QK_PAD]
    m_s[...] = jnp.full(m_s.shape, NEG, F32)
    l_s[...] = jnp.zeros(l_s.shape, F32)
    acc_s[...] = jnp.zeros(acc_s.shape, F32)

    chunk_end = lambda t: ((t >> CHUNK_SHIFT) + 1) << CHUNK_SHIFT
    kb_shift = kb.bit_length() - 1
    qstart = q_pos0 + i * tq
    lim_first = jnp.minimum(chunk_end(qstart), sk_valid)
    lim_last = jnp.minimum(chunk_end(qstart + tq - 1), sk_valid)
    n_full = lim_first >> kb_shift
    n_blk = (lim_last + kb - 1) >> kb_shift

    last = n_blk - 1

    def key_block(j):
        return k_ref[pl.ds(pl.multiple_of(j * kb, kb), kb), :]

    def scores(j, s_ref):
        s_ref[...] = _dot_nt(q_s[...], key_block(j))

    def update(j, s_ref, masked):
        s = s_ref[...]
        if masked:
            tok = lax.broadcasted_iota(jnp.int32, (R, 1), 0) & (tq - 1)
            row_lim = jnp.minimum(chunk_end(qstart + tok), sk_valid)
            kidx = j * kb + lax.broadcasted_iota(jnp.int32, (1, kb), 1)
            s = jnp.where(kidx < row_lim, s, NEG)
        m_prev = m_s[...]
        m_next = jnp.maximum(m_prev, jnp.max(s, axis=-1, keepdims=True))
        p = jnp.exp2(s - jnp.tile(m_next, (1, kb // LANES)))
        alpha = jnp.exp2(m_prev - m_next)
        l_s[...] = alpha * l_s[...] + jnp.sum(p, axis=-1, keepdims=True)
        acc_s[...] = alpha * acc_s[...] + _dot(p.astype(BF16), key_block(j)[:, :KV_LORA])
        m_s[...] = m_next

    n_pipe = jnp.minimum(n_full, last)
    odd = n_pipe & 1

    @pl.when(odd == 1)
    def _():
        scores(0, s0_s)
        update(0, s0_s, False)

    scores(odd, s0_s)

    def pair(jj, c):
        j = odd + 2 * jj
        scores(j + 1, s1_s)
        update(j, s0_s, False)
        scores(j + 2, s0_s)
        update(j + 1, s1_s, False)
        return c

    lax.fori_loop(0, n_pipe >> 1, pair, 0)
    update(n_pipe, s0_s, True)

    def tail(j, c):
        scores(j, s0_s)
        update(j, s0_s, True)
        return c

    lax.fori_loop(n_pipe + 1, n_blk, tail, 0)

    o_lat = (acc_s[...] / l_s[...]).astype(BF16)
    out = _dot(o_lat[0:tq], wuv_ref[0])
    for h in range(1, MLA_HEADS):
        out = out + _dot(o_lat[h * tq:(h + 1) * tq], wuv_ref[h])
    o_ref[...] = out.astype(BF16)


def _attention(qx, kx, wuv, nb, seq_q, row0, tq, kb, q_pos0, sk, sk_valid):
    nq = seq_q // tq
    off = row0 // tq
    R = MLA_HEADS * tq
    return pl.pallas_call(
        functools.partial(_attn_kernel, tq=tq, kb=kb, q_pos0=q_pos0, sk_valid=sk_valid),
        grid=(nb, nq),
        in_specs=[pl.BlockSpec((tq, MLA_HEADS * QK_PAD), lambda b, i: (off + b * nq + i, 0)),
                  pl.BlockSpec((sk, QK_PAD), lambda b, i: (b, 0)),
                  pl.BlockSpec(wuv.shape, lambda b, i: (0, 0, 0))],
        out_specs=pl.BlockSpec((tq, MLA_W), lambda b, i: (b * nq + i, 0)),
        out_shape=jax.ShapeDtypeStruct((nb * seq_q, MLA_W), BF16),
        scratch_shapes=[pltpu.VMEM((R, QK_PAD), BF16), pltpu.VMEM((R, kb), F32),
                        pltpu.VMEM((R, kb), F32), pltpu.VMEM((R, LANES), F32),
                        pltpu.VMEM((R, LANES), F32), pltpu.VMEM((R, KV_LORA), F32)],
        compiler_params=_cparams(("parallel", "arbitrary")), name="attention",
    )(qx, kx, wuv)


def _merge_kernel(h_ref, ap_ref, bp_ref, as_ref, bs_ref, wo_ref, o_ref, *, n_prompt_tiles):
    def project(a_ref, b_ref):
        o_ref[...] = (h_ref[...] + _dot(a_ref[...], wo_ref[:RET_W, :])
                      + _dot(b_ref[...], wo_ref[RET_W:, :]))

    @pl.when(pl.program_id(0) < n_prompt_tiles)
    def _():
        project(ap_ref, bp_ref)

    @pl.when(pl.program_id(0) >= n_prompt_tiles)
    def _():
        project(as_ref, bs_ref)


def _merge(h, a_p, b_p, a_s, b_s, wo, tm):
    T, D = h.shape
    npt = a_p.shape[0] // tm
    row = lambda w: pl.BlockSpec((tm, w), lambda i: (i, 0))
    prm = lambda w: pl.BlockSpec((tm, w), lambda i: (jnp.minimum(i, npt - 1), 0))
    smp = lambda w: pl.BlockSpec((tm, w), lambda i: (jnp.maximum(i - npt, 0), 0))
    return pl.pallas_call(
        functools.partial(_merge_kernel, n_prompt_tiles=npt), grid=(T // tm,),
        in_specs=[row(D), prm(RET_W), prm(MLA_W), smp(RET_W), smp(MLA_W),
                  pl.BlockSpec(wo.shape, lambda i: (0, 0))],
        out_specs=row(D), out_shape=jax.ShapeDtypeStruct((T, D), F32),
        compiler_params=_cparams(("parallel",)), name="merge",
    )(h, a_p, b_p, a_s, b_s, wo)


def _ffn_kernel(h_ref, fn_ref, wg_ref, wu_ref, wd_ref, o_ref, u_s):
    @pl.when(pl.program_id(1) == 0)
    def _():
        hh = h_ref[...]
        u_s[...] = _rms(hh, fn_ref[...]).astype(BF16)
        o_ref[...] = hh

    u = u_s[...]
    a = _dot(u, wg_ref[...])
    mid = (a * jax.nn.sigmoid(a) * _dot(u, wu_ref[...])).astype(BF16)
    o_ref[...] += _dot(mid, wd_ref[...])


def _ffn(h, fn, wg, wu, wd, tm, tf):
    T, D = h.shape
    F = wg.shape[1]
    return pl.pallas_call(
        _ffn_kernel, grid=(T // tm, F // tf),
        in_specs=[pl.BlockSpec((tm, D), lambda i, f: (i, 0)),
                  pl.BlockSpec(fn.shape, lambda i, f: (0, 0)),
                  pl.BlockSpec((D, tf), lambda i, f: (0, f)),
                  pl.BlockSpec((D, tf), lambda i, f: (0, f)),
                  pl.BlockSpec((tf, D), lambda i, f: (f, 0))],
        out_specs=pl.BlockSpec((tm, D), lambda i, f: (i, 0)),
        out_shape=jax.ShapeDtypeStruct((T, D), F32),
        scratch_shapes=[pltpu.VMEM((tm, D), BF16)],
        compiler_params=_cparams(("parallel", "arbitrary")), name="ffn",
    )(h, fn, wg, wu, wd)


L_E1, L_E2, L_W1, L_W2, L_R1, L_R2 = range(6)


def _router_kernel(h_ref, fn_ref, wr_ref, meta_ref, cnt_ref, run_s, *, n_experts):
    tm = h_ref.shape[0]

    @pl.when(pl.program_id(0) == 0)
    def _():
        run_s[...] = jnp.zeros(run_s.shape, F32)

    u = _rms(h_ref[...], fn_ref[...])
    logits = jnp.dot(u, wr_ref[...], precision=lax.Precision.HIGHEST, preferred_element_type=F32)
    lane = lax.broadcasted_iota(jnp.int32, logits.shape, 1).astype(F32)
    lg = jnp.where(lane < n_experts, logits, NEG)
    m1 = jnp.max(lg, axis=-1, keepdims=True)
    i1 = jnp.min(jnp.where(lg == m1, lane, float(LANES)), axis=-1, keepdims=True)
    lg2 = jnp.where(lane == i1, NEG, lg)
    m2 = jnp.max(lg2, axis=-1, keepdims=True)
    i2 = jnp.min(jnp.where(lg2 == m2, lane, float(LANES)), axis=-1, keepdims=True)
    e2 = jnp.exp(m2 - m1)
    den = 1.0 + e2

    hit = jnp.where((lane == i1) | (lane == i2), 1.0, 0.0)
    r_i = lax.broadcasted_iota(jnp.int32, (tm, tm), 0)
    c_i = lax.broadcasted_iota(jnp.int32, (tm, tm), 1)
    before = jnp.where(c_i < r_i, 1.0, 0.0).astype(BF16)
    prefix = _dot(before, hit.astype(BF16)) + run_s[0:1, :]
    r1 = jnp.sum(jnp.where(lane == i1, prefix, 0.0), axis=-1, keepdims=True)
    r2 = jnp.sum(jnp.where(lane == i2, prefix, 0.0), axis=-1, keepdims=True)
    run_s[...] = run_s[...] + jnp.sum(hit, axis=0, keepdims=True)
    cnt_ref[...] = run_s[...]

    meta = jnp.where(lane == L_E1, i1, 0.0)
    for ln, val in ((L_E2, i2), (L_W1, 1.0 / den), (L_W2, e2 / den), (L_R1, r1), (L_R2, r2)):
        meta = jnp.where(lane == ln, val, meta)
    meta_ref[...] = meta


def _router(h, fn, wr, n_experts, tm):
    T, D = h.shape
    return pl.pallas_call(
        functools.partial(_router_kernel, n_experts=n_experts), grid=(T // tm,),
        in_specs=[pl.BlockSpec((tm, D), lambda i: (i, 0)), pl.BlockSpec(fn.shape, lambda i: (0, 0)),
                  pl.BlockSpec(wr.shape, lambda i: (0, 0))],
        out_specs=(pl.BlockSpec((tm, LANES), lambda i: (i, 0)),
                   pl.BlockSpec((SUBLANES, LANES), lambda i: (0, 0))),
        out_shape=(jax.ShapeDtypeStruct((T, LANES), F32), jax.ShapeDtypeStruct((SUBLANES, LANES), F32)),
        scratch_shapes=[pltpu.VMEM((SUBLANES, LANES), F32)],
        compiler_params=_cparams(("arbitrary",)), name="router",
    )(h, fn, wr)


def _tile_index_copy(d_hbm, idx_s, sem, tile, slot):
    return pltpu.make_async_copy(d_hbm.at[tile], idx_s.at[slot], sem.at[slot])


def _dispatch_kernel(d_hbm, h_ref, xs_in, xs_out, idx_s, isem, rsem):
    del xs_in
    tm = h_ref.shape[0]
    i, n = pl.program_id(0), pl.num_programs(0)
    slot = i % 2

    @pl.when(i == 0)
    def _():
        _tile_index_copy(d_hbm, idx_s, isem, 0, 0).start()

    @pl.when(i + 1 < n)
    def _():
        _tile_index_copy(d_hbm, idx_s, isem, i + 1, 1 - slot).start()

    _tile_index_copy(d_hbm, idx_s, isem, i, slot).wait()

    def row_copy(r, dst):
        return pltpu.make_async_copy(h_ref.at[pl.ds(r, 1)], xs_out.at[pl.ds(dst, 1)], rsem)

    def body(r, c):
        for k in range(TOP_K):
            row_copy(r, idx_s[slot, k * tm + r]).start()
        return c

    lax.fori_loop(0, tm, body, 0)
    for k in range(TOP_K):
        pltpu.make_async_copy(h_ref, xs_out.at[pl.ds(0, tm)], rsem).wait()


def _dispatch(d_tiles, h, xs_init, tm):
    T, D = h.shape
    return pl.pallas_call(
        _dispatch_kernel, grid=(T // tm,),
        in_specs=[pl.BlockSpec(memory_space=pl.ANY), pl.BlockSpec((tm, D), lambda i: (i, 0)),
                  pl.BlockSpec(memory_space=pl.ANY)],
        out_specs=pl.BlockSpec(memory_space=pl.ANY),
        out_shape=jax.ShapeDtypeStruct(xs_init.shape, F32),
        scratch_shapes=[pltpu.SMEM((2, TOP_K * tm), jnp.int32), pltpu.SemaphoreType.DMA((2,)),
                        pltpu.SemaphoreType.DMA(())],
        input_output_aliases={2: 0},
        compiler_params=_cparams(("arbitrary",)), name="dispatch",
    )(d_tiles, h, xs_init)


def _gmm_kernel(te_ref, nu_ref, x_ref, fn_ref, wg_ref, wu_ref, wd_ref, o_ref, u_s):
    i, f = pl.program_id(0), pl.program_id(1)
    used = i < nu_ref[0]

    @pl.when(f == 0)
    def _():
        o_ref[...] = jnp.zeros(o_ref.shape, F32)

    @pl.when(used & (f == 0))
    def _():
        u_s[...] = _rms(x_ref[...], fn_ref[...]).astype(BF16)

    @pl.when(used)
    def _():
        u = u_s[...]
        a = _dot(u, wg_ref[0])
        mid = (a * jax.nn.sigmoid(a) * _dot(u, wu_ref[0])).astype(BF16)
        o_ref[...] += _dot(mid, wd_ref[0])


def _gmm(tile_expert, n_used, xs, fn, wg, wu, wd, tmg, tf):
    R, D = xs.shape
    F = wg.shape[-1]
    nf = F // tf
    f_eff = lambda i, f, nu: jnp.where(i < nu[0], f, nf - 1)
    grid_spec = pltpu.PrefetchScalarGridSpec(
        num_scalar_prefetch=2, grid=(R // tmg, nf),
        in_specs=[pl.BlockSpec((tmg, D), lambda i, f, te, nu: (i, 0)),
                  pl.BlockSpec(fn.shape, lambda i, f, te, nu: (0, 0)),
                  pl.BlockSpec((1, D, tf), lambda i, f, te, nu: (te[i], 0, f_eff(i, f, nu))),
                  pl.BlockSpec((1, D, tf), lambda i, f, te, nu: (te[i], 0, f_eff(i, f, nu))),
                  pl.BlockSpec((1, tf, D), lambda i, f, te, nu: (te[i], f_eff(i, f, nu), 0))],
        out_specs=pl.BlockSpec((tmg, D), lambda i, f, te, nu: (i, 0)),
        scratch_shapes=[pltpu.VMEM((tmg, D), BF16)])
    return pl.pallas_call(
        _gmm_kernel, grid_spec=grid_spec, out_shape=jax.ShapeDtypeStruct((R, D), F32),
        compiler_params=_cparams(("arbitrary", "arbitrary")), name="experts",
    )(tile_expert, n_used, xs, fn, wg, wu, wd)


def _combine_kernel(d_hbm, h_ref, meta_ref, ys_hbm, o_ref, idx_s, ya_s, yb_s, isem, rsem):
    tm = h_ref.shape[0]
    i, n = pl.program_id(0), pl.num_programs(0)
    slot = i % 2

    def fetch(tile, s):
        cp = _tile_index_copy(d_hbm, idx_s, isem, tile, s)
        cp.start()
        cp.wait()

        def body(r, c):
            pltpu.make_async_copy(ys_hbm.at[pl.ds(idx_s[s, r], 1)], ya_s.at[s, pl.ds(r, 1)],
                                  rsem.at[s]).start()
            pltpu.make_async_copy(ys_hbm.at[pl.ds(idx_s[s, tm + r], 1)], yb_s.at[s, pl.ds(r, 1)],
                                  rsem.at[s]).start()
            return c

        lax.fori_loop(0, tm, body, 0)

    @pl.when(i == 0)
    def _():
        fetch(0, 0)

    @pl.when(i + 1 < n)
    def _():
        fetch(i + 1, 1 - slot)

    pltpu.make_async_copy(ys_hbm.at[pl.ds(0, tm)], ya_s.at[slot], rsem.at[slot]).wait()
    pltpu.make_async_copy(ys_hbm.at[pl.ds(0, tm)], yb_s.at[slot], rsem.at[slot]).wait()
    meta = meta_ref[...]
    o_ref[...] = (h_ref[...] + meta[:, L_W1:L_W1 + 1] * ya_s[slot]
                  + meta[:, L_W2:L_W2 + 1] * yb_s[slot])


def _combine(d_tiles, h, meta, ys, tm):
    T, D = h.shape
    return pl.pallas_call(
        _combine_kernel, grid=(T // tm,),
        in_specs=[pl.BlockSpec(memory_space=pl.ANY), pl.BlockSpec((tm, D), lambda i: (i, 0)),
                  pl.BlockSpec((tm, LANES), lambda i: (i, 0)), pl.BlockSpec(memory_space=pl.ANY)],
        out_specs=pl.BlockSpec((tm, D), lambda i: (i, 0)),
        out_shape=jax.ShapeDtypeStruct((T, D), F32),
        scratch_shapes=[pltpu.SMEM((2, TOP_K * tm), jnp.int32), pltpu.VMEM((2, tm, D), F32),
                        pltpu.VMEM((2, tm, D), F32), pltpu.SemaphoreType.DMA((2,)),
                        pltpu.SemaphoreType.DMA((2,))],
        compiler_params=_cparams(("arbitrary",)), name="combine",
    )(d_tiles, h, meta, ys)


def _moe(h, fn, w_router, wg, wu, wd, tm, tmg, tf):
    T, D = h.shape
    E = w_router.shape[-1]
    meta, counts = _router(h, fn, jnp.pad(w_router, ((0, 0), (0, LANES - E))), E, tm)
    cnt = counts[0, :E].astype(jnp.int32)
    padded = ((cnt + tmg - 1) // tmg) * tmg
    ends = jnp.cumsum(padded)
    off = ends - padded
    n_tiles = -(-TOP_K * T // tmg) + E
    tile_expert = jnp.minimum(
        jnp.sum(ends[None, :] <= (jnp.arange(n_tiles, dtype=jnp.int32) * tmg)[:, None], axis=1), E - 1
    ).astype(jnp.int32)
    n_used = (ends[-1] // tmg).astype(jnp.int32).reshape(1)
    e12 = meta[:, L_E1:L_E2 + 1].astype(jnp.int32)
    dest = off[e12] + meta[:, L_R1:L_R2 + 1].astype(jnp.int32)
    d_tiles = dest.reshape(T // tm, tm, TOP_K).transpose(0, 2, 1).reshape(T // tm, TOP_K * tm)

    xs = _dispatch(d_tiles, h, jnp.zeros((n_tiles * tmg, D), F32), tm)
    ys = _gmm(tile_expert, n_used, xs, fn, wg, wu, wd, tmg, tf)
    return _combine(d_tiles, h, meta, ys, tm)


def _ple_kernel(h_ref, pn_ref, wg_ref, p_ref, wp_ref, fin_ref, o_ref, *, final):
    hh = h_ref[...]
    gate = jax.nn.sigmoid(_dot(_rms(hh, pn_ref[...]).astype(BF16), wg_ref[...]))
    out = hh + gate * _dot(p_ref[...].astype(BF16), wp_ref[...])
    if final:
        out = _rms(out, fin_ref[...])
    o_ref[...] = out


def _ple(h, pn, wg, p, wp, fin, tm, final, row0=0, rows=None):
    D = h.shape[1]
    rows = h.shape[0] if rows is None else rows
    off = row0 // tm
    full = lambda a: pl.BlockSpec(a.shape, lambda i: (0,) * a.ndim)
    return pl.pallas_call(
        functools.partial(_ple_kernel, final=final), grid=(rows // tm,),
        in_specs=[pl.BlockSpec((tm, D), lambda i: (off + i, 0)), full(pn), full(wg),
                  pl.BlockSpec((tm, p.shape[1]), lambda i: (i, 0)), full(wp), full(fin)],
        out_specs=pl.BlockSpec((tm, D), lambda i: (i, 0)),
        out_shape=jax.ShapeDtypeStruct((rows, D), F32),
        compiler_params=_cparams(("parallel",)), name="ple",
    )(h, pn, wg, p, wp, fin)


def _rope_tables(pos):
    pos = pos.astype(F32)[:, None]
    lane = np.arange(LANES)
    inv32 = ROPE_THETA ** (-jnp.arange(32, dtype=F32) / 32)
    inv16 = ROPE_THETA ** (-jnp.arange(16, dtype=F32) / 16)
    a64 = (pos * inv32[None, :])[:, lane % 32]
    a16 = (pos * inv16[None, :])[:, lane % 16]
    lo64 = jnp.asarray((lane % 64) < 32)
    c64, s64 = jnp.cos(a64), jnp.sin(a64)
    c16, s16 = jnp.cos(a16), jnp.sin(a16)
    in_k = jnp.asarray(lane < MLA_ROPE)
    lo_k = jnp.asarray(lane < 16)
    hi_k = jnp.asarray((lane >= 16) & (lane < MLA_ROPE))
    return (c64, jnp.where(lo64, -s64, 0.0), jnp.where(lo64, 0.0, s64), c16, s16,
            jnp.where(in_k, c16, 0.0), jnp.where(lo_k, -s16, 0.0), jnp.where(hi_k, s16, 0.0))


def _ret_tables(L):
    log_g = jnp.log(1.0 - jnp.exp2(-5.0 - jnp.arange(RET_HEADS, dtype=F32)))
    idx = jnp.arange(L, dtype=F32)
    dist = jnp.abs(idx[:, None] - idx[None, :])
    d = jnp.exp(dist[None] * log_g[:, None, None])
    qdec = jnp.exp((idx[:, None] + 1.0) * log_g[None, :])
    kdec = jnp.exp((L - 1.0 - idx)[:, None] * log_g[None, :])
    sdec = jnp.exp(L * log_g)
    wide = lambda t: jnp.repeat(t, RET_DK, axis=1).reshape(L, N_PAIRS, LANES).transpose(1, 0, 2)
    bm = jnp.asarray(np.kron(np.eye(2, dtype=np.float32), np.ones((RET_DK, RET_DV), np.float32)))
    gs = jnp.repeat(sdec, RET_DK).reshape(N_PAIRS, LANES, 1) * bm[None]
    return d, wide(qdec), wide(kdec), gs, bm


def _state_to_pairs(s):
    B = s.shape[0]
    s = s.reshape(B, N_PAIRS, 2, RET_DK, RET_DV)
    eye = jnp.eye(2, dtype=s.dtype)
    out = s[:, :, :, :, None, :] * eye[None, None, :, None, :, None]
    return out.reshape(B, N_PAIRS, LANES, LANES)


def _pairs_to_state(sp):
    B = sp.shape[0]
    s = sp.reshape(B, N_PAIRS, 2, RET_DK, 2, RET_DV)
    return jnp.stack([s[:, :, 0, :, 0, :], s[:, :, 1, :, 1, :]], axis=2).reshape(B, RET_HEADS, RET_DK, RET_DV)


def _uq_perm():
    per = MLA_NOPE + MLA_ROPE
    half = MLA_ROPE // 2
    nope = [h * per + d for h in range(MLA_HEADS) for d in range(MLA_NOPE)]
    x1 = [h * per + MLA_NOPE + f for h in range(MLA_HEADS) for f in range(half)]
    x2 = [h * per + MLA_NOPE + half + f for h in range(MLA_HEADS) for f in range(half)]
    return np.array(nope + x1 + x2, np.int32)


def _big_query_weight(w_uk):
    H, half = MLA_HEADS, MLA_ROPE // 2
    eye = jnp.eye(H, dtype=w_uk.dtype)
    wpad = jnp.pad(w_uk, ((0, 0), (0, 0), (0, QK_PAD - KV_LORA)))
    top = (eye[:, None, :, None] * wpad[:, :, None, :]).reshape(H * MLA_NOPE, H * QK_PAD)
    sel = np.zeros((2 * H * half, H * QK_PAD), np.float32)
    for h in range(H):
        for f in range(half):
            sel[h * half + f, h * QK_PAD + KV_LORA + f] = 1.0
            sel[H * half + h * half + f, h * QK_PAD + KV_LORA + half + f] = 1.0
    return jnp.concatenate([top, jnp.asarray(sel, w_uk.dtype)], axis=0)


def kernel(x_prompt, x_sample, p_prompt, p_sample, cache_ckv, cache_krope, state_ret, attn_norm, w_in, q_norm, w_uq, kv_norm, w_uk, w_uv, ret_norm, w_o, ffn_norm, w_gate_d, w_up_d, w_down_d, w_router, w_gate_e, w_up_e, w_down_e, ple_norm, w_ple_gate, w_ple_proj, final_norm):
    Bp, S, D = x_prompt.shape
    Bs, L, _ = x_sample.shape
    depth = w_in.shape[0]
    P = cache_ckv.shape[2]
    Tp, Ts = Bp * S, Bs * L
    T = Tp + Ts
    assert S % CHUNK == 0 and P % CHUNK == 0 and L == CHUNK and Tp % CHUNK == 0
    assert w_router.shape[-1] >= TOP_K

    tm_proj = _pick(int(np.gcd(S, Ts)), (512, 256, 128, 64))
    tm_ffn = _pick(T, (1280, 640, 512, 256, 128, 64))
    tm_gmm = 1024
    rb = _pick(S, (512, 256, 128, 64))
    tq = _pick(S, (128, 64))
    kb = _pick(S, (512, 256, 128))
    kb_s = 512
    sk_s = P + L
    sk_pad = -(-sk_s // kb_s) * kb_s

    pos = jnp.concatenate([jnp.arange(S, dtype=jnp.int32),
                           P + jnp.tile(jnp.arange(L, dtype=jnp.int32), Bs)])
    rope_tabs = _rope_tables(pos)
    n_pt, pt_per_seq = Tp // tm_proj, S // tm_proj
    tab_tile = lambda i: jnp.where(i < n_pt, i % pt_per_seq, pt_per_seq + i - n_pt)
    ret_tabs_p = _ret_tables(CHUNK)
    ret_tabs_s = _ret_tables(L)
    perm = _uq_perm()
    row2 = lambda v: v.reshape(1, -1)

    h = jnp.concatenate([x_prompt.reshape(Tp, D), x_sample.reshape(Ts, D)], axis=0)
    outs = {k: [] for k in ("ckv_p", "kro_p", "ret_p", "ckv_s", "kro_s", "ret_s")}
    for l in range(depth):
        win = jnp.pad(w_in[l], ((0, 0), (0, IN_COLS_PAD - IN_COLS))).astype(BF16)
        wuq = w_uq[l][:, perm].astype(BF16)
        wbig = _big_query_weight(w_uk[l]).astype(BF16)
        qr, kr, vr, gr, qx, ckv, kro, kx = _proj(
            h, row2(attn_norm[l]), win, row2(q_norm[l]), wuq, row2(kv_norm[l]), wbig, rope_tabs,
            tab_tile, tm_proj)

        rn = row2(ret_norm[l])
        zero_state = jnp.zeros((Bp, N_PAIRS, LANES, LANES), F32)
        o_ret_p, st_p = _retention(qr, kr, vr, gr, zero_state, ret_tabs_p, rn, Bp, S, 0, rb)
        o_ret_s, st_s = _retention(qr, kr, vr, gr, _state_to_pairs(state_ret[l].astype(F32)),
                                   ret_tabs_s, rn, Bs, L, Tp, L)

        wuv = w_uv[l].astype(BF16)
        wuv_big = (jnp.eye(MLA_HEADS, dtype=BF16)[:, None, :, None] * wuv[:, :, None, :]
                   ).reshape(MLA_HEADS, KV_LORA, MLA_W)
        o_mla_p = _attention(qx, kx, wuv_big, Bp, S, 0, tq, kb, 0, S, S)
        cache_kx = jnp.concatenate(
            [cache_ckv[l], cache_krope[l], jnp.zeros((Bs, P, QK_PAD - KV_LORA - MLA_ROPE), F32)],
            axis=-1).astype(BF16)
        kx_s = jnp.concatenate([cache_kx, kx[Tp:].reshape(Bs, L, QK_PAD),
                                jnp.zeros((Bs, sk_pad - sk_s, QK_PAD), BF16)], axis=1)
        o_mla_s = _attention(qx, kx_s.reshape(Bs * sk_pad, QK_PAD), wuv_big, Bs, L, Tp, L, kb_s, P,
                             sk_pad, sk_s)

        h = _merge(h, o_ret_p, o_mla_p, o_ret_s, o_mla_s, w_o[l].astype(BF16), tm_proj)

        fn = row2(ffn_norm[l])
        j = l // 2
        if l % 2 == 0:
            tf = _pick(w_gate_d.shape[-1], (256, 128))
            h = _ffn(h, fn, w_gate_d[j].astype(BF16), w_up_d[j].astype(BF16),
                     w_down_d[j].astype(BF16), tm_ffn, tf)
        else:
            tf = _pick(w_gate_e.shape[-1], (512, 256, 128))
            h = _moe(h, fn, w_router[j], w_gate_e[j].astype(BF16), w_up_e[j].astype(BF16),
                     w_down_e[j].astype(BF16), tm_proj, tm_gmm, tf)

        ple_w = (row2(ple_norm[l]), w_ple_gate[l].astype(BF16))
        wp = w_ple_proj[l].astype(BF16)
        if l < depth - 1:
            p_l = jnp.concatenate([p_prompt[l].reshape(Tp, -1), p_sample[l].reshape(Ts, -1)], axis=0)
            h = _ple(h, *ple_w, p_l, wp, row2(final_norm), tm_proj, False)
        else:
            y_p = _ple(h, *ple_w, p_prompt[l].reshape(Tp, -1), wp, row2(final_norm), tm_proj, True, 0, Tp)
            y_s = _ple(h, *ple_w, p_sample[l].reshape(Ts, -1), wp, row2(final_norm), tm_proj, True, Tp, Ts)

        outs["ckv_p"].append(ckv[:Tp].reshape(Bp, S, KV_LORA))
        outs["kro_p"].append(kro[:Tp].reshape(Bp, S, MLA_ROPE))
        outs["ret_p"].append(_pairs_to_state(st_p))
        outs["ckv_s"].append(ckv[Tp:].reshape(Bs, L, KV_LORA))
        outs["kro_s"].append(kro[Tp:].reshape(Bs, L, MLA_ROPE))
        outs["ret_s"].append(_pairs_to_state(st_s))

    return (y_p.reshape(Bp, S, D), y_s.reshape(Bs, L, D),
            jnp.stack(outs["ckv_p"]), jnp.stack(outs["kro_p"]), jnp.stack(outs["ret_p"]),
            jnp.stack(outs["ckv_s"]), jnp.stack(outs["kro_s"]), jnp.stack(outs["ret_s"]))
```

```python
import functools

import numpy as np
import jax
import jax.numpy as jnp
from jax import lax
from jax.experimental import pallas as pl
from jax.experimental.pallas import tpu as pltpu

F32 = jnp.float32
BF16 = jnp.bfloat16

CHUNK = 64
CHUNK_SHIFT = 6
RMS_EPS = 1e-6
ROPE_THETA = 10000.0
RET_HEADS = 8
RET_DK = 64
RET_DV = 64
RET_W = RET_HEADS * RET_DK
MLA_HEADS = 8
MLA_NOPE = 64
MLA_ROPE = 32
MLA_V = 64
Q_LORA = 256
KV_LORA = 128
MLA_W = MLA_HEADS * MLA_V
TOP_K = 2
N_PAIRS = RET_HEADS // 2
LANES = 128
SUBLANES = 8
QK_PAD = 256
OFF_Q = RET_W
OFF_K = OFF_Q + RET_W
OFF_V = OFF_K + RET_W
OFF_G = OFF_V + RET_W
OFF_CQ = OFF_G + Q_LORA
OFF_CKV = OFF_CQ + KV_LORA
IN_COLS = OFF_CKV + MLA_ROPE
IN_COLS_PAD = OFF_CKV + LANES
DMA_UNROLL = 8
NEG = -1e30
VMEM_LIMIT = 56 * 1024 * 1024


def _pick(n, cands):
    for c in cands:
        if n % c == 0:
            return c
    return n


def _cparams(sem):
    return pltpu.CompilerParams(dimension_semantics=sem, vmem_limit_bytes=VMEM_LIMIT)


def _rms(x, g):
    return x * lax.rsqrt(jnp.mean(x * x, axis=-1, keepdims=True) + RMS_EPS) * g


def _dot(a, b):
    return jnp.dot(a, b, preferred_element_type=F32)


def _dot_nt(a, b):
    return lax.dot_general(a, b, (((1,), (1,)), ((), ())), preferred_element_type=F32)


def _dot_tn(a, b):
    return lax.dot_general(a, b, (((0,), (0,)), ((), ())), preferred_element_type=F32)


def _proj_kernel(h_ref, an_ref, win_ref, qn_ref, wuq_ref, kvn_ref, wbig_ref,
                 c64_ref, sm64_ref, sp64_ref, c16_ref, s16_ref, ck_ref, smk_ref, spk_ref,
                 qr_ref, kr_ref, vr_ref, gr_ref, qx_ref, ckv_ref, kro_ref, kx_ref):
    xn = _rms(h_ref[...], an_ref[...]).astype(BF16)
    c64, sm64, sp64 = c64_ref[...], sm64_ref[...], sp64_ref[...]

    def rope64(z):
        return (z * c64 + pltpu.roll(z, LANES - 32, 1) * sm64 + pltpu.roll(z, 32, 1) * sp64)

    for j in range(RET_W // LANES):
        lo = j * LANES
        zq = _dot(xn, win_ref[:, lo:lo + LANES])
        qr_ref[:, lo:lo + LANES] = rope64(zq)
        zk = _dot(xn, win_ref[:, OFF_Q + lo:OFF_Q + lo + LANES])
        kr_ref[:, lo:lo + LANES] = rope64(zk) * (RET_DK ** -0.5)
    vr_ref[...] = _dot(xn, win_ref[:, OFF_K:OFF_V]).astype(BF16)
    gr_ref[...] = _dot(xn, win_ref[:, OFF_V:OFF_G])

    cq = _dot(xn, win_ref[:, OFF_G:OFF_CQ])
    q = _dot(_rms(cq, qn_ref[...]).astype(BF16), wuq_ref[...])
    nq = MLA_HEADS * MLA_NOPE
    x1, x2 = q[:, nq:nq + LANES], q[:, nq + LANES:nq + 2 * LANES]
    c16, s16 = c16_ref[...], s16_ref[...]
    scale = (MLA_NOPE + MLA_ROPE) ** -0.5 * float(np.log2(np.e))
    qcat = jnp.concatenate([q[:, :nq], x1 * c16 - x2 * s16, x2 * c16 + x1 * s16], axis=-1)
    qx_ref[...] = _dot((qcat * scale).astype(BF16), wbig_ref[...]).astype(BF16)

    ckv = _rms(_dot(xn, win_ref[:, OFF_CQ:OFF_CKV]), kvn_ref[...])
    ckv_ref[...] = ckv
    zk = _dot(xn, win_ref[:, OFF_CKV:IN_COLS_PAD])
    kro = (zk * ck_ref[...] + pltpu.roll(zk, LANES - 16, 1) * smk_ref[...]
           + pltpu.roll(zk, 16, 1) * spk_ref[...])
    kro_ref[...] = kro[:, :MLA_ROPE]
    kx_ref[:, :KV_LORA] = ckv.astype(BF16)
    kx_ref[:, KV_LORA:] = kro.astype(BF16)


def _proj(h, an, win, qn, wuq, kvn, wbig, tabs, tab_tile, tm):
    T, D = h.shape
    row = lambda w: pl.BlockSpec((tm, w), lambda i: (i, 0))
    tab = pl.BlockSpec((tm, LANES), lambda i: (tab_tile(i), 0))
    full = lambda a: pl.BlockSpec(a.shape, lambda i: (0,) * a.ndim)
    out_shapes = (
        jax.ShapeDtypeStruct((T, RET_W), F32), jax.ShapeDtypeStruct((T, RET_W), F32),
        jax.ShapeDtypeStruct((T, RET_W), BF16), jax.ShapeDtypeStruct((T, RET_W), F32),
        jax.ShapeDtypeStruct((T, MLA_HEADS * QK_PAD), BF16),
        jax.ShapeDtypeStruct((T, KV_LORA), F32), jax.ShapeDtypeStruct((T, MLA_ROPE), F32),
        jax.ShapeDtypeStruct((T, QK_PAD), BF16))
    return pl.pallas_call(
        _proj_kernel, grid=(T // tm,),
        in_specs=[row(D), full(an), full(win), full(qn), full(wuq), full(kvn), full(wbig)]
                 + [tab] * 8,
        out_specs=(row(RET_W), row(RET_W), row(RET_W), row(RET_W), row(MLA_HEADS * QK_PAD),
                   row(KV_LORA), row(MLA_ROPE), row(QK_PAD)),
        out_shape=out_shapes, compiler_params=_cparams(("parallel",)), name="proj",
    )(h, an, win, qn, wuq, kvn, wbig, *tabs)


def _ret_kernel(q_ref, k_ref, v_ref, g_ref, s0_ref, d_ref, qd_ref, kd_ref, gs_ref, bm_ref, rn_ref,
                o_ref, so_ref, st_ref, *, n_chunks):
    i = pl.program_id(1)

    @pl.when(i == 0)
    def _():
        st_ref[...] = s0_ref[0]

    lane = lax.broadcasted_iota(jnp.int32, (CHUNK, LANES), 1)
    first = lane < RET_DK

    def chunk(c, carry):
        rows = pl.ds(pl.multiple_of(c * CHUNK, CHUNK), CHUNK)
        for p in range(N_PAIRS):
            cols = slice(p * LANES, (p + 1) * LANES)
            qp, kp, vp = q_ref[rows, cols], k_ref[rows, cols], v_ref[rows, cols]
            q2 = jnp.concatenate([jnp.where(first, qp, 0.0), jnp.where(first, 0.0, qp)], axis=0)
            sd = (_dot_nt(q2.astype(BF16), kp.astype(BF16)) * d_ref[p]).astype(BF16)
            qq = (qp * qd_ref[p]).astype(BF16)
            st = st_ref[p]
            lhs = jnp.concatenate([jnp.concatenate([qq, qq], axis=0), sd], axis=1)
            rhs = jnp.concatenate([st.astype(BF16), vp], axis=0)
            o2 = _dot(lhs, rhs)
            o = jnp.where(first, o2[:CHUNK], o2[CHUNK:])
            upd = _dot_tn((kp * kd_ref[p]).astype(BF16), vp)
            st_ref[p] = st * gs_ref[p] + upd * bm_ref[...]
            oo = o * o
            ss_a = jnp.sum(jnp.where(first, oo, 0.0), axis=-1, keepdims=True)
            ss_b = jnp.sum(jnp.where(first, 0.0, oo), axis=-1, keepdims=True)
            rs = jnp.where(first, lax.rsqrt(ss_a * (1.0 / RET_DV) + RMS_EPS),
                           lax.rsqrt(ss_b * (1.0 / RET_DV) + RMS_EPS))
            g = g_ref[rows, cols]
            o_ref[rows, cols] = (o * rs * rn_ref[:, cols] * (g * jax.nn.sigmoid(g))).astype(BF16)
        return carry

    lax.fori_loop(0, n_chunks, chunk, 0, unroll=min(n_chunks, 4))

    @pl.when(i == pl.num_programs(1) - 1)
    def _():
        so_ref[0] = st_ref[...]


def _retention(qr, kr, vr, gr, s0, tabs, rn, nb, seq, row0, rb):
    nblk = seq // rb
    off = row0 // rb
    tok = pl.BlockSpec((rb, RET_W), lambda b, i: (off + b * nblk + i, 0))
    full = lambda a: pl.BlockSpec(a.shape, lambda b, i: (0,) * a.ndim)
    st_spec = pl.BlockSpec((1, N_PAIRS, LANES, LANES), lambda b, i: (b, 0, 0, 0))
    d, qd, kd, gs, bm = tabs
    return pl.pallas_call(
        functools.partial(_ret_kernel, n_chunks=rb // CHUNK), grid=(nb, nblk),
        in_specs=[tok, tok, tok, tok, st_spec, full(d), full(qd), full(kd), full(gs), full(bm), full(rn)],
        out_specs=(pl.BlockSpec((rb, RET_W), lambda b, i: (b * nblk + i, 0)), st_spec),
        out_shape=(jax.ShapeDtypeStruct((nb * seq, RET_W), BF16),
                   jax.ShapeDtypeStruct((nb, N_PAIRS, LANES, LANES), F32)),
        scratch_shapes=[pltpu.VMEM((N_PAIRS, LANES, LANES), F32)],
        compiler_params=_cparams(("parallel", "arbitrary")), name="retention",
    )(qr, kr, vr, gr, s0, d, qd, kd, gs, bm, rn)


def _attn_kernel(q_ref, k_ref, wuv_ref, o_ref, q_s, s0_s, s1_s, m_s, l_s, acc_s,
                 *, tq, kb, q_pos0, sk_valid):
    i = pl.program_id(1)
    R = MLA_HEADS * tq
    for h in range(MLA_HEADS):
        q_s[h * tq:(h + 1) * tq, :] = q_ref[:, h * QK_PAD:(h + 1) * QK_PAD]
    m_s[...] = jnp.full(m_s.shape, NEG, F32)
    l_s[...] = jnp.zeros(l_s.shape, F32)
    acc_s[...] = jnp.zeros(acc_s.shape, F32)

    chunk_end = lambda t: ((t >> CHUNK_SHIFT) + 1) << CHUNK_SHIFT
    kb_shift = kb.bit_length() - 1
    qstart = q_pos0 + i * tq
    lim_first = jnp.minimum(chunk_end(qstart), sk_valid)
    lim_last = jnp.minimum(chunk_end(qstart + tq - 1), sk_valid)
    n_full = lim_first >> kb_shift
    n_blk = (lim_last + kb - 1) >> kb_shift

    last = n_blk - 1

    def key_block(j):
        return k_ref[pl.ds(pl.multiple_of(j * kb, kb), kb), :]

    def scores(j, s_ref):
        s_ref[...] = _dot_nt(q_s[...], key_block(j))

    def update(j, s_ref, masked):
        s = s_ref[...]
        if masked:
            tok = lax.broadcasted_iota(jnp.int32, (R, 1), 0) & (tq - 1)
            row_lim = jnp.minimum(chunk_end(qstart + tok), sk_valid)
            kidx = j * kb + lax.broadcasted_iota(jnp.int32, (1, kb), 1)
            s = jnp.where(kidx < row_lim, s, NEG)
        m_prev = m_s[...]
        m_next = jnp.maximum(m_prev, jnp.max(s, axis=-1, keepdims=True))
        p = jnp.exp2(s - jnp.tile(m_next, (1, kb // LANES)))
        alpha = jnp.exp2(m_prev - m_next)
        l_s[...] = alpha * l_s[...] + jnp.sum(p, axis=-1, keepdims=True)
        acc_s[...] = alpha * acc_s[...] + _dot(p.astype(BF16), key_block(j)[:, :KV_LORA])
        m_s[...] = m_next

    n_pipe = jnp.minimum(n_full, last)
    odd = n_pipe & 1

    @pl.when(odd == 1)
    def _():
        scores(0, s0_s)
        update(0, s0_s, False)

    scores(odd, s0_s)

    def pair(jj, c):
        j = odd + 2 * jj
        scores(j + 1, s1_s)
        update(j, s0_s, False)
        scores(j + 2, s0_s)
        update(j + 1, s1_s, False)
        return c

    lax.fori_loop(0, n_pipe >> 1, pair, 0)
    update(n_pipe, s0_s, True)

    def tail(j, c):
        scores(j, s0_s)
        update(j, s0_s, True)
        return c

    lax.fori_loop(n_pipe + 1, n_blk, tail, 0)

    o_lat = (acc_s[...] / l_s[...]).astype(BF16)
    out = _dot(o_lat[0:tq], wuv_ref[0])
    for h in range(1, MLA_HEADS):
        out = out + _dot(o_lat[h * tq:(h + 1) * tq], wuv_ref[h])
    o_ref[...] = out.astype(BF16)


def _attention(qx, kx, wuv, nb, seq_q, row0, tq, kb, q_pos0, sk, sk_valid):
    nq = seq_q // tq
    off = row0 // tq
    R = MLA_HEADS * tq
    return pl.pallas_call(
        functools.partial(_attn_kernel, tq=tq, kb=kb, q_pos0=q_pos0, sk_valid=sk_valid),
        grid=(nb, nq),
        in_specs=[pl.BlockSpec((tq, MLA_HEADS * QK_PAD), lambda b, i: (off + b * nq + i, 0)),
                  pl.BlockSpec((sk, QK_PAD), lambda b, i: (b, 0)),
                  pl.BlockSpec(wuv.shape, lambda b, i: (0, 0, 0))],
        out_specs=pl.BlockSpec((tq, MLA_W), lambda b, i: (b * nq + i, 0)),
        out_shape=jax.ShapeDtypeStruct((nb * seq_q, MLA_W), BF16),
        scratch_shapes=[pltpu.VMEM((R, QK_PAD), BF16), pltpu.VMEM((R, kb), F32),
                        pltpu.VMEM((R, kb), F32), pltpu.VMEM((R, LANES), F32),
                        pltpu.VMEM((R, LANES), F32), pltpu.VMEM((R, KV_LORA), F32)],
        compiler_params=_cparams(("parallel", "arbitrary")), name="attention",
    )(qx, kx, wuv)


def _merge_kernel(h_ref, ap_ref, bp_ref, as_ref, bs_ref, wo_ref, o_ref, *, n_prompt_tiles):
    def project(a_ref, b_ref):
        o_ref[...] = (h_ref[...] + _dot(a_ref[...], wo_ref[:RET_W, :])
                      + _dot(b_ref[...], wo_ref[RET_W:, :]))

    @pl.when(pl.program_id(0) < n_prompt_tiles)
    def _():
        project(ap_ref, bp_ref)

    @pl.when(pl.program_id(0) >= n_prompt_tiles)
    def _():
        project(as_ref, bs_ref)


def _merge(h, a_p, b_p, a_s, b_s, wo, tm):
    T, D = h.shape
    npt = a_p.shape[0] // tm
    row = lambda w: pl.BlockSpec((tm, w), lambda i: (i, 0))
    prm = lambda w: pl.BlockSpec((tm, w), lambda i: (jnp.minimum(i, npt - 1), 0))
    smp = lambda w: pl.BlockSpec((tm, w), lambda i: (jnp.maximum(i - npt, 0), 0))
    return pl.pallas_call(
        functools.partial(_merge_kernel, n_prompt_tiles=npt), grid=(T // tm,),
        in_specs=[row(D), prm(RET_W), prm(MLA_W), smp(RET_W), smp(MLA_W),
                  pl.BlockSpec(wo.shape, lambda i: (0, 0))],
        out_specs=row(D), out_shape=jax.ShapeDtypeStruct((T, D), F32),
        compiler_params=_cparams(("parallel",)), name="merge",
    )(h, a_p, b_p, a_s, b_s, wo)


def _ffn_kernel(h_ref, fn_ref, wg_ref, wu_ref, wd_ref, o_ref, u_s):
    @pl.when(pl.program_id(1) == 0)
    def _():
        hh = h_ref[...]
        u_s[...] = _rms(hh, fn_ref[...]).astype(BF16)
        o_ref[...] = hh

    u = u_s[...]
    a = _dot(u, wg_ref[...])
    mid = (a * jax.nn.sigmoid(a) * _dot(u, wu_ref[...])).astype(BF16)
    o_ref[...] += _dot(mid, wd_ref[...])


def _ffn(h, fn, wg, wu, wd, tm, tf):
    T, D = h.shape
    F = wg.shape[1]
    return pl.pallas_call(
        _ffn_kernel, grid=(T // tm, F // tf),
        in_specs=[pl.BlockSpec((tm, D), lambda i, f: (i, 0)),
                  pl.BlockSpec(fn.shape, lambda i, f: (0, 0)),
                  pl.BlockSpec((D, tf), lambda i, f: (0, f)),
                  pl.BlockSpec((D, tf), lambda i, f: (0, f)),
                  pl.BlockSpec((tf, D), lambda i, f: (f, 0))],
        out_specs=pl.BlockSpec((tm, D), lambda i, f: (i, 0)),
        out_shape=jax.ShapeDtypeStruct((T, D), F32),
        scratch_shapes=[pltpu.VMEM((tm, D), BF16)],
        compiler_params=_cparams(("parallel", "arbitrary")), name="ffn",
    )(h, fn, wg, wu, wd)


L_E1, L_E2, L_W1, L_W2, L_R1, L_R2 = range(6)


def _router_kernel(h_ref, fn_ref, wr_ref, meta_ref, cnt_ref, run_s, *, n_experts):
    tm = h_ref.shape[0]

    @pl.when(pl.program_id(0) == 0)
    def _():
        run_s[...] = jnp.zeros(run_s.shape, F32)

    u = _rms(h_ref[...], fn_ref[...])
    logits = jnp.dot(u, wr_ref[...], precision=lax.Precision.HIGHEST, preferred_element_type=F32)
    lane = lax.broadcasted_iota(jnp.int32, logits.shape, 1).astype(F32)
    lg = jnp.where(lane < n_experts, logits, NEG)
    m1 = jnp.max(lg, axis=-1, keepdims=True)
    i1 = jnp.min(jnp.where(lg == m1, lane, float(LANES)), axis=-1, keepdims=True)
    lg2 = jnp.where(lane == i1, NEG, lg)
    m2 = jnp.max(lg2, axis=-1, keepdims=True)
    i2 = jnp.min(jnp.where(lg2 == m2, lane, float(LANES)), axis=-1, keepdims=True)
    e2 = jnp.exp(m2 - m1)
    den = 1.0 + e2

    hit = jnp.where((lane == i1) | (lane == i2), 1.0, 0.0)
    r_i = lax.broadcasted_iota(jnp.int32, (tm, tm), 0)
    c_i = lax.broadcasted_iota(jnp.int32, (tm, tm), 1)
    before = jnp.where(c_i < r_i, 1.0, 0.0).astype(BF16)
    prefix = _dot(before, hit.astype(BF16)) + run_s[0:1, :]
    r1 = jnp.sum(jnp.where(lane == i1, prefix, 0.0), axis=-1, keepdims=True)
    r2 = jnp.sum(jnp.where(lane == i2, prefix, 0.0), axis=-1, keepdims=True)
    run_s[...] = run_s[...] + jnp.sum(hit, axis=0, keepdims=True)
    cnt_ref[...] = run_s[...]

    meta = jnp.where(lane == L_E1, i1, 0.0)
    for ln, val in ((L_E2, i2), (L_W1, 1.0 / den), (L_W2, e2 / den), (L_R1, r1), (L_R2, r2)):
        meta = jnp.where(lane == ln, val, meta)
    meta_ref[...] = meta


def _router(h, fn, wr, n_experts, tm):
    T, D = h.shape
    return pl.pallas_call(
        functools.partial(_router_kernel, n_experts=n_experts), grid=(T // tm,),
        in_specs=[pl.BlockSpec((tm, D), lambda i: (i, 0)), pl.BlockSpec(fn.shape, lambda i: (0, 0)),
                  pl.BlockSpec(wr.shape, lambda i: (0, 0))],
        out_specs=(pl.BlockSpec((tm, LANES), lambda i: (i, 0)),
                   pl.BlockSpec((SUBLANES, LANES), lambda i: (0, 0))),
        out_shape=(jax.ShapeDtypeStruct((T, LANES), F32), jax.ShapeDtypeStruct((SUBLANES, LANES), F32)),
        scratch_shapes=[pltpu.VMEM((SUBLANES, LANES), F32)],
        compiler_params=_cparams(("arbitrary",)), name="router",
    )(h, fn, wr)


def _tile_index_copy(d_hbm, idx_s, sem, tile, slot):
    return pltpu.make_async_copy(d_hbm.at[tile], idx_s.at[slot], sem.at[slot])


def _dispatch_kernel(d_hbm, h_ref, xs_in, xs_out, idx_s, isem, rsem):
    del xs_in
    tm = h_ref.shape[0]
    i, n = pl.program_id(0), pl.num_programs(0)
    slot = i % 2

    @pl.when(i == 0)
    def _():
        _tile_index_copy(d_hbm, idx_s, isem, 0, 0).start()

    @pl.when(i + 1 < n)
    def _():
        _tile_index_copy(d_hbm, idx_s, isem, i + 1, 1 - slot).start()

    _tile_index_copy(d_hbm, idx_s, isem, i, slot).wait()

    def row_copy(r, dst):
        return pltpu.make_async_copy(h_ref.at[pl.ds(r, 1)], xs_out.at[pl.ds(dst, 1)], rsem)

    def body(r, c):
        for k in range(TOP_K):
            row_copy(r, idx_s[slot, k * tm + r]).start(priority=k % 2)
        return c

    lax.fori_loop(0, tm, body, 0, unroll=DMA_UNROLL)
    for k in range(TOP_K):
        pltpu.make_async_copy(h_ref, xs_out.at[pl.ds(0, tm)], rsem).wait()


def _dispatch(d_tiles, h, xs_init, tm):
    T, D = h.shape
    return pl.pallas_call(
        _dispatch_kernel, grid=(T // tm,),
        in_specs=[pl.BlockSpec(memory_space=pl.ANY), pl.BlockSpec((tm, D), lambda i: (i, 0)),
                  pl.BlockSpec(memory_space=pl.ANY)],
        out_specs=pl.BlockSpec(memory_space=pl.ANY),
        out_shape=jax.ShapeDtypeStruct(xs_init.shape, F32),
        scratch_shapes=[pltpu.SMEM((2, TOP_K * tm), jnp.int32), pltpu.SemaphoreType.DMA((2,)),
                        pltpu.SemaphoreType.DMA(())],
        input_output_aliases={2: 0},
        compiler_params=_cparams(("arbitrary",)), name="dispatch",
    )(d_tiles, h, xs_init)


def _gmm_kernel(te_ref, nu_ref, x_ref, fn_ref, wg_ref, wu_ref, wd_ref, o_ref, u_s):
    i, f = pl.program_id(0), pl.program_id(1)
    used = i < nu_ref[0]

    @pl.when(f == 0)
    def _():
        o_ref[...] = jnp.zeros(o_ref.shape, F32)

    @pl.when(used & (f == 0))
    def _():
        u_s[...] = _rms(x_ref[...], fn_ref[...]).astype(BF16)

    @pl.when(used)
    def _():
        u = u_s[...]
        a = _dot(u, wg_ref[0])
        mid = (a * jax.nn.sigmoid(a) * _dot(u, wu_ref[0])).astype(BF16)
        o_ref[...] += _dot(mid, wd_ref[0])


def _gmm(tile_expert, n_used, xs, fn, wg, wu, wd, tmg, tf):
    R, D = xs.shape
    F = wg.shape[-1]
    nf = F // tf
    f_eff = lambda i, f, nu: jnp.where(i < nu[0], f, nf - 1)
    grid_spec = pltpu.PrefetchScalarGridSpec(
        num_scalar_prefetch=2, grid=(R // tmg, nf),
        in_specs=[pl.BlockSpec((tmg, D), lambda i, f, te, nu: (i, 0)),
                  pl.BlockSpec(fn.shape, lambda i, f, te, nu: (0, 0)),
                  pl.BlockSpec((1, D, tf), lambda i, f, te, nu: (te[i], 0, f_eff(i, f, nu))),
                  pl.BlockSpec((1, D, tf), lambda i, f, te, nu: (te[i], 0, f_eff(i, f, nu))),
                  pl.BlockSpec((1, tf, D), lambda i, f, te, nu: (te[i], f_eff(i, f, nu), 0))],
        out_specs=pl.BlockSpec((tmg, D), lambda i, f, te, nu: (i, 0)),
        scratch_shapes=[pltpu.VMEM((tmg, D), BF16)])
    return pl.pallas_call(
        _gmm_kernel, grid_spec=grid_spec, out_shape=jax.ShapeDtypeStruct((R, D), F32),
        compiler_params=_cparams(("arbitrary", "arbitrary")), name="experts",
    )(tile_expert, n_used, xs, fn, wg, wu, wd)


def _combine_kernel(d_hbm, h_ref, meta_ref, ys_hbm, o_ref, idx_s, ya_s, yb_s, isem, rsem):
    tm = h_ref.shape[0]
    i, n = pl.program_id(0), pl.num_programs(0)
    slot = i % 2

    def fetch(tile, s):
        cp = _tile_index_copy(d_hbm, idx_s, isem, tile, s)
        cp.start()
        cp.wait()

        def body(r, c):
            pltpu.make_async_copy(ys_hbm.at[pl.ds(idx_s[s, r], 1)], ya_s.at[s, pl.ds(r, 1)],
                                  rsem.at[s]).start(priority=0)
            pltpu.make_async_copy(ys_hbm.at[pl.ds(idx_s[s, tm + r], 1)], yb_s.at[s, pl.ds(r, 1)],
                                  rsem.at[s]).start(priority=1)
            return c

        lax.fori_loop(0, tm, body, 0, unroll=DMA_UNROLL)

    @pl.when(i == 0)
    def _():
        fetch(0, 0)

    @pl.when(i + 1 < n)
    def _():
        fetch(i + 1, 1 - slot)

    pltpu.make_async_copy(ys_hbm.at[pl.ds(0, tm)], ya_s.at[slot], rsem.at[slot]).wait()
    pltpu.make_async_copy(ys_hbm.at[pl.ds(0, tm)], yb_s.at[slot], rsem.at[slot]).wait()
    meta = meta_ref[...]
    o_ref[...] = (h_ref[...] + meta[:, L_W1:L_W1 + 1] * ya_s[slot]
                  + meta[:, L_W2:L_W2 + 1] * yb_s[slot])


def _combine(d_tiles, h, meta, ys, tm):
    T, D = h.shape
    return pl.pallas_call(
        _combine_kernel, grid=(T // tm,),
        in_specs=[pl.BlockSpec(memory_space=pl.ANY), pl.BlockSpec((tm, D), lambda i: (i, 0)),
                  pl.BlockSpec((tm, LANES), lambda i: (i, 0)), pl.BlockSpec(memory_space=pl.ANY)],
        out_specs=pl.BlockSpec((tm, D), lambda i: (i, 0)),
        out_shape=jax.ShapeDtypeStruct((T, D), F32),
        scratch_shapes=[pltpu.SMEM((2, TOP_K * tm), jnp.int32), pltpu.VMEM((2, tm, D), F32),
                        pltpu.VMEM((2, tm, D), F32), pltpu.SemaphoreType.DMA((2,)),
                        pltpu.SemaphoreType.DMA((2,))],
        compiler_params=_cparams(("arbitrary",)), name="combine",
    )(d_tiles, h, meta, ys)


def _moe(h, fn, w_router, wg, wu, wd, tm, tmg, tf):
    T, D = h.shape
    E = w_router.shape[-1]
    meta, counts = _router(h, fn, jnp.pad(w_router, ((0, 0), (0, LANES - E))), E, tm)
    cnt = counts[0, :E].astype(jnp.int32)
    padded = ((cnt + tmg - 1) // tmg) * tmg
    ends = jnp.cumsum(padded)
    off = ends - padded
    n_tiles = -(-TOP_K * T // tmg) + E
    tile_expert = jnp.minimum(
        jnp.sum(ends[None, :] <= (jnp.arange(n_tiles, dtype=jnp.int32) * tmg)[:, None], axis=1), E - 1
    ).astype(jnp.int32)
    n_used = (ends[-1] // tmg).astype(jnp.int32).reshape(1)
    e12 = meta[:, L_E1:L_E2 + 1].astype(jnp.int32)
    dest = off[e12] + meta[:, L_R1:L_R2 + 1].astype(jnp.int32)
    d_tiles = dest.reshape(T // tm, tm, TOP_K).transpose(0, 2, 1).reshape(T // tm, TOP_K * tm)

    xs = _dispatch(d_tiles, h, jnp.zeros((n_tiles * tmg, D), F32), tm)
    ys = _gmm(tile_expert, n_used, xs, fn, wg, wu, wd, tmg, tf)
    return _combine(d_tiles, h, meta, ys, tm)


def _ple_kernel(h_ref, pn_ref, wg_ref, p_ref, wp_ref, fin_ref, o_ref, *, final):
    hh = h_ref[...]
    gate = jax.nn.sigmoid(_dot(_rms(hh, pn_ref[...]).astype(BF16), wg_ref[...]))
    out = hh + gate * _dot(p_ref[...].astype(BF16), wp_ref[...])
    if final:
        out = _rms(out, fin_ref[...])
    o_ref[...] = out


def _ple(h, pn, wg, p, wp, fin, tm, final, row0=0, rows=None):
    D = h.shape[1]
    rows = h.shape[0] if rows is None else rows
    off = row0 // tm
    full = lambda a: pl.BlockSpec(a.shape, lambda i: (0,) * a.ndim)
    return pl.pallas_call(
        functools.partial(_ple_kernel, final=final), grid=(rows // tm,),
        in_specs=[pl.BlockSpec((tm, D), lambda i: (off + i, 0)), full(pn), full(wg),
                  pl.BlockSpec((tm, p.shape[1]), lambda i: (i, 0)), full(wp), full(fin)],
        out_specs=pl.BlockSpec((tm, D), lambda i: (i, 0)),
        out_shape=jax.ShapeDtypeStruct((rows, D), F32),
        compiler_params=_cparams(("parallel",)), name="ple",
    )(h, pn, wg, p, wp, fin)


def _rope_tables(pos):
    pos = pos.astype(F32)[:, None]
    lane = np.arange(LANES)
    inv32 = ROPE_THETA ** (-jnp.arange(32, dtype=F32) / 32)
    inv16 = ROPE_THETA ** (-jnp.arange(16, dtype=F32) / 16)
    a64 = (pos * inv32[None, :])[:, lane % 32]
    a16 = (pos * inv16[None, :])[:, lane % 16]
    lo64 = jnp.asarray((lane % 64) < 32)
    c64, s64 = jnp.cos(a64), jnp.sin(a64)
    c16, s16 = jnp.cos(a16), jnp.sin(a16)
    in_k = jnp.asarray(lane < MLA_ROPE)
    lo_k = jnp.asarray(lane < 16)
    hi_k = jnp.asarray((lane >= 16) & (lane < MLA_ROPE))
    return (c64, jnp.where(lo64, -s64, 0.0), jnp.where(lo64, 0.0, s64), c16, s16,
            jnp.where(in_k, c16, 0.0), jnp.where(lo_k, -s16, 0.0), jnp.where(hi_k, s16, 0.0))


def _ret_tables(L):
    log_g = jnp.log(1.0 - jnp.exp2(-5.0 - jnp.arange(RET_HEADS, dtype=F32)))
    idx = jnp.arange(L, dtype=F32)
    dist = jnp.abs(idx[:, None] - idx[None, :])
    d = jnp.exp(dist[None] * log_g[:, None, None])
    qdec = jnp.exp((idx[:, None] + 1.0) * log_g[None, :])
    kdec = jnp.exp((L - 1.0 - idx)[:, None] * log_g[None, :])
    sdec = jnp.exp(L * log_g)
    wide = lambda t: jnp.repeat(t, RET_DK, axis=1).reshape(L, N_PAIRS, LANES).transpose(1, 0, 2)
    bm = jnp.asarray(np.kron(np.eye(2, dtype=np.float32), np.ones((RET_DK, RET_DV), np.float32)))
    gs = jnp.repeat(sdec, RET_DK).reshape(N_PAIRS, LANES, 1) * bm[None]
    return d.reshape(N_PAIRS, 2 * L, L), wide(qdec), wide(kdec), gs, bm


def _state_to_pairs(s):
    B = s.shape[0]
    s = s.reshape(B, N_PAIRS, 2, RET_DK, RET_DV)
    eye = jnp.eye(2, dtype=s.dtype)
    out = s[:, :, :, :, None, :] * eye[None, None, :, None, :, None]
    return out.reshape(B, N_PAIRS, LANES, LANES)


def _pairs_to_state(sp):
    B = sp.shape[0]
    s = sp.reshape(B, N_PAIRS, 2, RET_DK, 2, RET_DV)
    return jnp.stack([s[:, :, 0, :, 0, :], s[:, :, 1, :, 1, :]], axis=2).reshape(B, RET_HEADS, RET_DK, RET_DV)


def _uq_perm():
    per = MLA_NOPE + MLA_ROPE
    half = MLA_ROPE // 2
    nope = [h * per + d for h in range(MLA_HEADS) for d in range(MLA_NOPE)]
    x1 = [h * per + MLA_NOPE + f for h in range(MLA_HEADS) for f in range(half)]
    x2 = [h * per + MLA_NOPE + half + f for h in range(MLA_HEADS) for f in range(half)]
    return np.array(nope + x1 + x2, np.int32)


def _big_query_weight(w_uk):
    H, half = MLA_HEADS, MLA_ROPE // 2
    eye = jnp.eye(H, dtype=w_uk.dtype)
    wpad = jnp.pad(w_uk, ((0, 0), (0, 0), (0, QK_PAD - KV_LORA)))
    top = (eye[:, None, :, None] * wpad[:, :, None, :]).reshape(H * MLA_NOPE, H * QK_PAD)
    sel = np.zeros((2 * H * half, H * QK_PAD), np.float32)
    for h in range(H):
        for f in range(half):
            sel[h * half + f, h * QK_PAD + KV_LORA + f] = 1.0
            sel[H * half + h * half + f, h * QK_PAD + KV_LORA + half + f] = 1.0
    return jnp.concatenate([top, jnp.asarray(sel, w_uk.dtype)], axis=0)


def kernel(x_prompt, x_sample, p_prompt, p_sample, cache_ckv, cache_krope, state_ret, attn_norm, w_in, q_norm, w_uq, kv_norm, w_uk, w_uv, ret_norm, w_o, ffn_norm, w_gate_d, w_up_d, w_down_d, w_router, w_gate_e, w_up_e, w_down_e, ple_norm, w_ple_gate, w_ple_proj, final_norm):
    Bp, S, D = x_prompt.shape
    Bs, L, _ = x_sample.shape
    depth = w_in.shape[0]
    P = cache_ckv.shape[2]
    Tp, Ts = Bp * S, Bs * L
    T = Tp + Ts
    assert S % CHUNK == 0 and P % CHUNK == 0 and L == CHUNK and Tp % CHUNK == 0
    assert w_router.shape[-1] >= TOP_K

    tm_proj = _pick(int(np.gcd(S, Ts)), (512, 256, 128, 64))
    tm_ffn = _pick(T, (1280, 640, 512, 256, 128, 64))
    tm_gmm = 1024
    rb = _pick(S, (512, 256, 128, 64))
    tq = _pick(S, (256, 128, 64))
    kb = _pick(S, (512, 256, 128))
    kb_s = 512
    sk_s = P + L
    sk_pad = -(-sk_s // kb_s) * kb_s

    pos = jnp.concatenate([jnp.arange(S, dtype=jnp.int32),
                           P + jnp.tile(jnp.arange(L, dtype=jnp.int32), Bs)])
    rope_tabs = _rope_tables(pos)
    n_pt, pt_per_seq = Tp // tm_proj, S // tm_proj
    tab_tile = lambda i: jnp.where(i < n_pt, i % pt_per_seq, pt_per_seq + i - n_pt)
    ret_tabs_p = _ret_tables(CHUNK)
    ret_tabs_s = _ret_tables(L)
    perm = _uq_perm()
    row2 = lambda v: v.reshape(1, -1)

    h = jnp.concatenate([x_prompt.reshape(Tp, D), x_sample.reshape(Ts, D)], axis=0)
    outs = {k: [] for k in ("ckv_p", "kro_p", "ret_p", "ckv_s", "kro_s", "ret_s")}
    for l in range(depth):
        win = jnp.pad(w_in[l], ((0, 0), (0, IN_COLS_PAD - IN_COLS))).astype(BF16)
        wuq = w_uq[l][:, perm].astype(BF16)
        wbig = _big_query_weight(w_uk[l]).astype(BF16)
        qr, kr, vr, gr, qx, ckv, kro, kx = _proj(
            h, row2(attn_norm[l]), win, row2(q_norm[l]), wuq, row2(kv_norm[l]), wbig, rope_tabs,
            tab_tile, tm_proj)

        rn = row2(ret_norm[l])
        zero_state = jnp.zeros((Bp, N_PAIRS, LANES, LANES), F32)
        o_ret_p, st_p = _retention(qr, kr, vr, gr, zero_state, ret_tabs_p, rn, Bp, S, 0, rb)
        o_ret_s, st_s = _retention(qr, kr, vr, gr, _state_to_pairs(state_ret[l].astype(F32)),
                                   ret_tabs_s, rn, Bs, L, Tp, L)

        wuv = w_uv[l].astype(BF16)
        wuv_big = (jnp.eye(MLA_HEADS, dtype=BF16)[:, None, :, None] * wuv[:, :, None, :]
                   ).reshape(MLA_HEADS, KV_LORA, MLA_W)
        o_mla_p = _attention(qx, kx, wuv_big, Bp, S, 0, tq, kb, 0, S, S)
        cache_kx = jnp.concatenate(
            [cache_ckv[l], cache_krope[l], jnp.zeros((Bs, P, QK_PAD - KV_LORA - MLA_ROPE), F32)],
            axis=-1).astype(BF16)
        kx_s = jnp.concatenate([cache_kx, kx[Tp:].reshape(Bs, L, QK_PAD),
                                jnp.zeros((Bs, sk_pad - sk_s, QK_PAD), BF16)], axis=1)
        o_mla_s = _attention(qx, kx_s.reshape(Bs * sk_pad, QK_PAD), wuv_big, Bs, L, Tp, L, kb_s, P,
                             sk_pad, sk_s)

        h = _merge(h, o_ret_p, o_mla_p, o_ret_s, o_mla_s, w_o[l].astype(BF16), tm_proj)

        fn = row2(ffn_norm[l])
        j = l // 2
        if l % 2 == 0:
            tf = _pick(w_gate_d.shape[-1], (256, 128))
            h = _ffn(h, fn, w_gate_d[j].astype(BF16), w_up_d[j].astype(BF16),
                     w_down_d[j].astype(BF16), tm_ffn, tf)
        else:
            tf = _pick(w_gate_e.shape[-1], (512, 256, 128))
            h = _moe(h, fn, w_router[j], w_gate_e[j].astype(BF16), w_up_e[j].astype(BF16),
                     w_down_e[j].astype(BF16), tm_proj, tm_gmm, tf)

        ple_w = (row2(ple_norm[l]), w_ple_gate[l].astype(BF16))
        wp = w_ple_proj[l].astype(BF16)
        if l < depth - 1:
            p_l = jnp.concatenate([p_prompt[l].reshape(Tp, -1), p_sample[l].reshape(Ts, -1)], axis=0)
            h = _ple(h, *ple_w, p_l, wp, row2(final_norm), tm_proj, False)
        else:
            y_p = _ple(h, *ple_w, p_prompt[l].reshape(Tp, -1), wp, row2(final_norm), tm_proj, True, 0, Tp)
            y_s = _ple(h, *ple_w, p_sample[l].reshape(Ts, -1), wp, row2(final_norm), tm_proj, True, Tp, Ts)

        outs["ckv_p"].append(ckv[:Tp].reshape(Bp, S, KV_LORA))
        outs["kro_p"].append(kro[:Tp].reshape(Bp, S, MLA_ROPE))
        outs["ret_p"].append(_pairs_to_state(st_p))
        outs["ckv_s"].append(ckv[Tp:].reshape(Bs, L, KV_LORA))
        outs["kro_s"].append(kro[Tp:].reshape(Bs, L, MLA_ROPE))
        outs["ret_s"].append(_pairs_to_state(st_s))

    return (y_p.reshape(Bp, S, D), y_s.reshape(Bs, L, D),
            jnp.stack(outs["ckv_p"]), jnp.stack(outs["kro_p"]), jnp.stack(outs["ret_p"]),
            jnp.stack(outs["ckv_s"]), jnp.stack(outs["kro_s"]), jnp.stack(outs["ret_s"]))
```

```python
import functools

import numpy as np
import jax
import jax.numpy as jnp
from jax import lax
from jax.experimental import pallas as pl
from jax.experimental.pallas import tpu as pltpu

F32 = jnp.float32
BF16 = jnp.bfloat16

CHUNK = 64
CHUNK_SHIFT = 6
RMS_EPS = 1e-6
ROPE_THETA = 10000.0
RET_HEADS = 8
RET_DK = 64
RET_DV = 64
RET_W = RET_HEADS * RET_DK
MLA_HEADS = 8
MLA_NOPE = 64
MLA_ROPE = 32
MLA_V = 64
Q_LORA = 256
KV_LORA = 128
MLA_W = MLA_HEADS * MLA_V
TOP_K = 2
N_PAIRS = RET_HEADS // 2
LANES = 128
SUBLANES = 8
QK_PAD = 256
OFF_Q = RET_W
OFF_K = OFF_Q + RET_W
OFF_V = OFF_K + RET_W
OFF_G = OFF_V + RET_W
OFF_CQ = OFF_G + Q_LORA
OFF_CKV = OFF_CQ + KV_LORA
IN_COLS = OFF_CKV + MLA_ROPE
IN_COLS_PAD = OFF_CKV + LANES
DMA_UNROLL = 8
NEG = -1e30
VMEM_LIMIT = 56 * 1024 * 1024


def _pick(n, cands):
    for c in cands:
        if n % c == 0:
            return c
    return n


def _cparams(sem):
    return pltpu.CompilerParams(dimension_semantics=sem, vmem_limit_bytes=VMEM_LIMIT)


def _rms(x, g):
    return x * lax.rsqrt(jnp.mean(x * x, axis=-1, keepdims=True) + RMS_EPS) * g


def _dot(a, b):
    return jnp.dot(a, b, preferred_element_type=F32)


def _dot_nt(a, b):
    return lax.dot_general(a, b, (((1,), (1,)), ((), ())), preferred_element_type=F32)


def _dot_tn(a, b):
    return lax.dot_general(a, b, (((0,), (0,)), ((), ())), preferred_element_type=F32)


def _proj_kernel(h_ref, an_ref, win_ref, qn_ref, wuq_ref, kvn_ref, wbig_ref,
                 c64_ref, sm64_ref, sp64_ref, c16_ref, s16_ref, ck_ref, smk_ref, spk_ref,
                 qr_ref, kr_ref, vr_ref, gr_ref, qx_ref, ckv_ref, kro_ref, kx_ref):
    xn = _rms(h_ref[...], an_ref[...]).astype(BF16)
    c64, sm64, sp64 = c64_ref[...], sm64_ref[...], sp64_ref[...]

    def rope64(z):
        return (z * c64 + pltpu.roll(z, LANES - 32, 1) * sm64 + pltpu.roll(z, 32, 1) * sp64)

    for lo in range(0, RET_W, QK_PAD):
        zq = _dot(xn, win_ref[:, lo:lo + QK_PAD])
        zk = _dot(xn, win_ref[:, OFF_Q + lo:OFF_Q + lo + QK_PAD])
        for half in range(0, QK_PAD, LANES):
            dst = slice(lo + half, lo + half + LANES)
            qr_ref[:, dst] = rope64(zq[:, half:half + LANES])
            kr_ref[:, dst] = rope64(zk[:, half:half + LANES]) * (RET_DK ** -0.5)

    zkv = _dot(xn, win_ref[:, OFF_CQ:IN_COLS_PAD])
    ckv = _rms(zkv[:, :KV_LORA], kvn_ref[...])
    ckv_ref[...] = ckv
    zk = zkv[:, KV_LORA:]
    kro = (zk * ck_ref[...] + pltpu.roll(zk, LANES - 16, 1) * smk_ref[...]
           + pltpu.roll(zk, 16, 1) * spk_ref[...])
    kro_ref[...] = kro[:, :MLA_ROPE]
    kx_ref[:, :KV_LORA] = ckv.astype(BF16)
    kx_ref[:, KV_LORA:] = kro.astype(BF16)

    cq = _dot(xn, win_ref[:, OFF_G:OFF_CQ])
    q = _dot(_rms(cq, qn_ref[...]).astype(BF16), wuq_ref[...])
    nq = MLA_HEADS * MLA_NOPE
    x1, x2 = q[:, nq:nq + LANES], q[:, nq + LANES:nq + 2 * LANES]
    c16, s16 = c16_ref[...], s16_ref[...]
    scale = (MLA_NOPE + MLA_ROPE) ** -0.5 * float(np.log2(np.e))
    qcat = jnp.concatenate([q[:, :nq], x1 * c16 - x2 * s16, x2 * c16 + x1 * s16], axis=-1)
    qcat = (qcat * scale).astype(BF16)
    half_w = (MLA_HEADS // 2) * QK_PAD
    for g in range(2):
        rows = slice(g * (nq // 2), (g + 1) * (nq // 2))
        cols = slice(g * half_w, (g + 1) * half_w)
        qx_ref[:, cols] = (_dot(qcat[:, rows], wbig_ref[rows, cols])
                           + _dot(qcat[:, nq:], wbig_ref[nq:, cols])).astype(BF16)

    vr_ref[...] = _dot(xn, win_ref[:, OFF_K:OFF_V]).astype(BF16)
    gr_ref[...] = _dot(xn, win_ref[:, OFF_V:OFF_G])


def _proj(h, an, win, qn, wuq, kvn, wbig, tabs, tab_tile, tm):
    T, D = h.shape
    row = lambda w: pl.BlockSpec((tm, w), lambda i: (i, 0))
    tab = pl.BlockSpec((tm, LANES), lambda i: (tab_tile(i), 0))
    full = lambda a: pl.BlockSpec(a.shape, lambda i: (0,) * a.ndim)
    out_shapes = (
        jax.ShapeDtypeStruct((T, RET_W), F32), jax.ShapeDtypeStruct((T, RET_W), F32),
        jax.ShapeDtypeStruct((T, RET_W), BF16), jax.ShapeDtypeStruct((T, RET_W), F32),
        jax.ShapeDtypeStruct((T, MLA_HEADS * QK_PAD), BF16),
        jax.ShapeDtypeStruct((T, KV_LORA), F32), jax.ShapeDtypeStruct((T, MLA_ROPE), F32),
        jax.ShapeDtypeStruct((T, QK_PAD), BF16))
    return pl.pallas_call(
        _proj_kernel, grid=(T // tm,),
        in_specs=[row(D), full(an), full(win), full(qn), full(wuq), full(kvn), full(wbig)]
                 + [tab] * 8,
        out_specs=(row(RET_W), row(RET_W), row(RET_W), row(RET_W), row(MLA_HEADS * QK_PAD),
                   row(KV_LORA), row(MLA_ROPE), row(QK_PAD)),
        out_shape=out_shapes, compiler_params=_cparams(("parallel",)), name="proj",
    )(h, an, win, qn, wuq, kvn, wbig, *tabs)


def _ret_kernel(q_ref, k_ref, v_ref, g_ref, s0_ref, d_ref, qd_ref, kd_ref, gs_ref, bm_ref, rn_ref,
                o_ref, so_ref, st_ref, *, n_chunks):
    i = pl.program_id(1)

    @pl.when(i == 0)
    def _():
        st_ref[...] = s0_ref[0]

    lane = lax.broadcasted_iota(jnp.int32, (CHUNK, LANES), 1)
    first = lane < RET_DK

    def chunk(c, carry):
        rows = pl.ds(pl.multiple_of(c * CHUNK, CHUNK), CHUNK)
        for p in range(N_PAIRS):
            cols = slice(p * LANES, (p + 1) * LANES)
            qp, kp, vp = q_ref[rows, cols], k_ref[rows, cols], v_ref[rows, cols]
            q2 = jnp.concatenate([jnp.where(first, qp, 0.0), jnp.where(first, 0.0, qp)], axis=0)
            sd = (_dot_nt(q2.astype(BF16), kp.astype(BF16)) * d_ref[p]).astype(BF16)
            qq = (qp * qd_ref[p]).astype(BF16)
            st = st_ref[p]
            lhs = jnp.concatenate([jnp.concatenate([qq, qq], axis=0), sd], axis=1)
            rhs = jnp.concatenate([st.astype(BF16), vp], axis=0)
            o2 = _dot(lhs, rhs)
            o = jnp.where(first, o2[:CHUNK], o2[CHUNK:])
            upd = _dot_tn((kp * kd_ref[p]).astype(BF16), vp)
            st_ref[p] = st * gs_ref[p] + upd * bm_ref[...]
            oo = o * o
            ss_a = jnp.sum(jnp.where(first, oo, 0.0), axis=-1, keepdims=True)
            ss_b = jnp.sum(jnp.where(first, 0.0, oo), axis=-1, keepdims=True)
            rs = jnp.where(first, lax.rsqrt(ss_a * (1.0 / RET_DV) + RMS_EPS),
                           lax.rsqrt(ss_b * (1.0 / RET_DV) + RMS_EPS))
            g = g_ref[rows, cols]
            o_ref[rows, cols] = (o * rs * rn_ref[:, cols] * (g * jax.nn.sigmoid(g))).astype(BF16)
        return carry

    lax.fori_loop(0, n_chunks, chunk, 0, unroll=min(n_chunks, 4))

    @pl.when(i == pl.num_programs(1) - 1)
    def _():
        so_ref[0] = st_ref[...]


def _retention(qr, kr, vr, gr, s0, tabs, rn, nb, seq, row0, rb):
    nblk = seq // rb
    off = row0 // rb
    tok = pl.BlockSpec((rb, RET_W), lambda b, i: (off + b * nblk + i, 0))
    full = lambda a: pl.BlockSpec(a.shape, lambda b, i: (0,) * a.ndim)
    st_spec = pl.BlockSpec((1, N_PAIRS, LANES, LANES), lambda b, i: (b, 0, 0, 0))
    d, qd, kd, gs, bm = tabs
    return pl.pallas_call(
        functools.partial(_ret_kernel, n_chunks=rb // CHUNK), grid=(nb, nblk),
        in_specs=[tok, tok, tok, tok, st_spec, full(d), full(qd), full(kd), full(gs), full(bm), full(rn)],
        out_specs=(pl.BlockSpec((rb, RET_W), lambda b, i: (b * nblk + i, 0)), st_spec),
        out_shape=(jax.ShapeDtypeStruct((nb * seq, RET_W), BF16),
                   jax.ShapeDtypeStruct((nb, N_PAIRS, LANES, LANES), F32)),
        scratch_shapes=[pltpu.VMEM((N_PAIRS, LANES, LANES), F32)],
        compiler_params=_cparams(("parallel", "arbitrary")), name="retention",
    )(qr, kr, vr, gr, s0, d, qd, kd, gs, bm, rn)


def _attn_kernel(q_ref, k_ref, wuv_ref, o_ref, q_s, s0_s, s1_s, m_s, l_s, acc_s,
                 *, tq, kb, q_pos0, sk_valid):
    i = pl.program_id(1)
    R = MLA_HEADS * tq
    for h in range(MLA_HEADS):
        q_s[h * tq:(h + 1) * tq, :] = q_ref[:, h * QK_PAD:(h + 1) * QK_PAD]
    m_s[...] = jnp.full(m_s.shape, NEG, F32)
    l_s[...] = jnp.zeros(l_s.shape, F32)
    acc_s[...] = jnp.zeros(acc_s.shape, F32)

    chunk_end = lambda t: ((t >> CHUNK_SHIFT) + 1) << CHUNK_SHIFT
    kb_shift = kb.bit_length() - 1
    qstart = q_pos0 + i * tq
    lim_first = jnp.minimum(chunk_end(qstart), sk_valid)
    lim_last = jnp.minimum(chunk_end(qstart + tq - 1), sk_valid)
    n_full = lim_first >> kb_shift
    n_blk = (lim_last + kb - 1) >> kb_shift

    last = n_blk - 1

    def key_block(j):
        return k_ref[pl.ds(pl.multiple_of(j * kb, kb), kb), :]

    def scores(j, s_ref):
        s_ref[...] = _dot_nt(q_s[...], key_block(j))

    def update(j, s_ref, masked):
        s = s_ref[...]
        if masked:
            tok = lax.broadcasted_iota(jnp.int32, (R, 1), 0) & (tq - 1)
            row_lim = jnp.minimum(chunk_end(qstart + tok), sk_valid)
            kidx = j * kb + lax.broadcasted_iota(jnp.int32, (1, kb), 1)
            s = jnp.where(kidx < row_lim, s, NEG)
        m_prev = m_s[...]
        m_next = jnp.maximum(m_prev, jnp.max(s, axis=-1, keepdims=True))
        p = jnp.exp2(s - jnp.tile(m_next, (1, kb // LANES)))
        alpha = jnp.exp2(m_prev - m_next)
        l_s[...] = alpha * l_s[...] + jnp.sum(p, axis=-1, keepdims=True)
        acc_s[...] = alpha * acc_s[...] + _dot(p.astype(BF16), key_block(j)[:, :KV_LORA])
        m_s[...] = m_next

    n_pipe = jnp.minimum(n_full, last)
    odd = n_pipe & 1

    @pl.when(odd == 1)
    def _():
        scores(0, s0_s)
        update(0, s0_s, False)

    scores(odd, s0_s)

    def pair(jj, c):
        j = odd + 2 * jj
        scores(j + 1, s1_s)
        update(j, s0_s, False)
        scores(j + 2, s0_s)
        update(j + 1, s1_s, False)
        return c

    lax.fori_loop(0, n_pipe >> 1, pair, 0)
    update(n_pipe, s0_s, True)

    def tail(j, c):
        scores(j, s0_s)
        update(j, s0_s, True)
        return c

    lax.fori_loop(n_pipe + 1, n_blk, tail, 0)

    o_lat = (acc_s[...] / l_s[...]).astype(BF16)
    out = _dot(o_lat[0:tq], wuv_ref[0])
    for h in range(1, MLA_HEADS):
        out = out + _dot(o_lat[h * tq:(h + 1) * tq], wuv_ref[h])
    o_ref[...] = out.astype(BF16)


def _attention(qx, kx, wuv, nb, seq_q, row0, tq, kb, q_pos0, sk, sk_valid):
    nq = seq_q // tq
    off = row0 // tq
    R = MLA_HEADS * tq
    return pl.pallas_call(
        functools.partial(_attn_kernel, tq=tq, kb=kb, q_pos0=q_pos0, sk_valid=sk_valid),
        grid=(nb, nq),
        in_specs=[pl.BlockSpec((tq, MLA_HEADS * QK_PAD), lambda b, i: (off + b * nq + i, 0)),
                  pl.BlockSpec((sk, QK_PAD), lambda b, i: (b, 0)),
                  pl.BlockSpec(wuv.shape, lambda b, i: (0, 0, 0))],
        out_specs=pl.BlockSpec((tq, MLA_W), lambda b, i: (b * nq + i, 0)),
        out_shape=jax.ShapeDtypeStruct((nb * seq_q, MLA_W), BF16),
        scratch_shapes=[pltpu.VMEM((R, QK_PAD), BF16), pltpu.VMEM((R, kb), F32),
                        pltpu.VMEM((R, kb), F32), pltpu.VMEM((R, LANES), F32),
                        pltpu.VMEM((R, LANES), F32), pltpu.VMEM((R, KV_LORA), F32)],
        compiler_params=_cparams(("parallel", "arbitrary")), name="attention",
    )(qx, kx, wuv)


def _merge_kernel(h_ref, ap_ref, bp_ref, as_ref, bs_ref, wo_ref, o_ref, *, n_prompt_tiles):
    def project(a_ref, b_ref):
        o_ref[...] = (h_ref[...] + _dot(a_ref[...], wo_ref[:RET_W, :])
                      + _dot(b_ref[...], wo_ref[RET_W:, :]))

    @pl.when(pl.program_id(0) < n_prompt_tiles)
    def _():
        project(ap_ref, bp_ref)

    @pl.when(pl.program_id(0) >= n_prompt_tiles)
    def _():
        project(as_ref, bs_ref)


def _merge(h, a_p, b_p, a_s, b_s, wo, tm):
    T, D = h.shape
    npt = a_p.shape[0] // tm
    row = lambda w: pl.BlockSpec((tm, w), lambda i: (i, 0))
    prm = lambda w: pl.BlockSpec((tm, w), lambda i: (jnp.minimum(i, npt - 1), 0))
    smp = lambda w: pl.BlockSpec((tm, w), lambda i: (jnp.maximum(i - npt, 0), 0))
    return pl.pallas_call(
        functools.partial(_merge_kernel, n_prompt_tiles=npt), grid=(T // tm,),
        in_specs=[row(D), prm(RET_W), prm(MLA_W), smp(RET_W), smp(MLA_W),
                  pl.BlockSpec(wo.shape, lambda i: (0, 0))],
        out_specs=row(D), out_shape=jax.ShapeDtypeStruct((T, D), F32),
        compiler_params=_cparams(("parallel",)), name="merge",
    )(h, a_p, b_p, a_s, b_s, wo)


def _ffn_kernel(h_ref, fn_ref, wg_ref, wu_ref, wd_ref, o_ref, u_s):
    @pl.when(pl.program_id(1) == 0)
    def _():
        hh = h_ref[...]
        u_s[...] = _rms(hh, fn_ref[...]).astype(BF16)
        o_ref[...] = hh

    u = u_s[...]
    a = _dot(u, wg_ref[...])
    mid = (a * jax.nn.sigmoid(a) * _dot(u, wu_ref[...])).astype(BF16)
    o_ref[...] += _dot(mid, wd_ref[...])


def _ffn(h, fn, wg, wu, wd, tm, tf):
    T, D = h.shape
    F = wg.shape[1]
    return pl.pallas_call(
        _ffn_kernel, grid=(T // tm, F // tf),
        in_specs=[pl.BlockSpec((tm, D), lambda i, f: (i, 0)),
                  pl.BlockSpec(fn.shape, lambda i, f: (0, 0)),
                  pl.BlockSpec((D, tf), lambda i, f: (0, f)),
                  pl.BlockSpec((D, tf), lambda i, f: (0, f)),
                  pl.BlockSpec((tf, D), lambda i, f: (f, 0))],
        out_specs=pl.BlockSpec((tm, D), lambda i, f: (i, 0)),
        out_shape=jax.ShapeDtypeStruct((T, D), F32),
        scratch_shapes=[pltpu.VMEM((tm, D), BF16)],
        compiler_params=_cparams(("parallel", "arbitrary")), name="ffn",
    )(h, fn, wg, wu, wd)


L_E1, L_E2, L_W1, L_W2, L_R1, L_R2 = range(6)


def _router_kernel(h_ref, fn_ref, wr_ref, meta_ref, cnt_ref, run_s, *, n_experts):
    tm = h_ref.shape[0]

    @pl.when(pl.program_id(0) == 0)
    def _():
        run_s[...] = jnp.zeros(run_s.shape, F32)

    u = _rms(h_ref[...], fn_ref[...])
    logits = jnp.dot(u, wr_ref[...], precision=lax.Precision.HIGHEST, preferred_element_type=F32)
    lane = lax.broadcasted_iota(jnp.int32, logits.shape, 1).astype(F32)
    lg = jnp.where(lane < n_experts, logits, NEG)
    m1 = jnp.max(lg, axis=-1, keepdims=True)
    i1 = jnp.min(jnp.where(lg == m1, lane, float(LANES)), axis=-1, keepdims=True)
    lg2 = jnp.where(lane == i1, NEG, lg)
    m2 = jnp.max(lg2, axis=-1, keepdims=True)
    i2 = jnp.min(jnp.where(lg2 == m2, lane, float(LANES)), axis=-1, keepdims=True)
    e2 = jnp.exp(m2 - m1)
    den = 1.0 + e2

    hit = jnp.where((lane == i1) | (lane == i2), 1.0, 0.0)
    r_i = lax.broadcasted_iota(jnp.int32, (tm, tm), 0)
    c_i = lax.broadcasted_iota(jnp.int32, (tm, tm), 1)
    before = jnp.where(c_i < r_i, 1.0, 0.0).astype(BF16)
    prefix = _dot(before, hit.astype(BF16)) + run_s[0:1, :]
    r1 = jnp.sum(jnp.where(lane == i1, prefix, 0.0), axis=-1, keepdims=True)
    r2 = jnp.sum(jnp.where(lane == i2, prefix, 0.0), axis=-1, keepdims=True)
    run_s[...] = run_s[...] + jnp.sum(hit, axis=0, keepdims=True)
    cnt_ref[...] = run_s[...]

    meta = jnp.where(lane == L_E1, i1, 0.0)
    for ln, val in ((L_E2, i2), (L_W1, 1.0 / den), (L_W2, e2 / den), (L_R1, r1), (L_R2, r2)):
        meta = jnp.where(lane == ln, val, meta)
    meta_ref[...] = meta


def _router(h, fn, wr, n_experts, tm):
    T, D = h.shape
    return pl.pallas_call(
        functools.partial(_router_kernel, n_experts=n_experts), grid=(T // tm,),
        in_specs=[pl.BlockSpec((tm, D), lambda i: (i, 0)), pl.BlockSpec(fn.shape, lambda i: (0, 0)),
                  pl.BlockSpec(wr.shape, lambda i: (0, 0))],
        out_specs=(pl.BlockSpec((tm, LANES), lambda i: (i, 0)),
                   pl.BlockSpec((SUBLANES, LANES), lambda i: (0, 0))),
        out_shape=(jax.ShapeDtypeStruct((T, LANES), F32), jax.ShapeDtypeStruct((SUBLANES, LANES), F32)),
        scratch_shapes=[pltpu.VMEM((SUBLANES, LANES), F32)],
        compiler_params=_cparams(("arbitrary",)), name="router",
    )(h, fn, wr)


def _tile_index_copy(d_hbm, idx_s, sem, tile, slot):
    return pltpu.make_async_copy(d_hbm.at[tile], idx_s.at[slot], sem.at[slot])


def _dispatch_kernel(fill_ref, d_hbm, h_ref, xs_out, idx_s, zero_s, isem, rsem, fsem):
    tm = h_ref.shape[0]
    tmg = zero_s.shape[0]
    i, n = pl.program_id(0), pl.num_programs(0)
    slot = i % 2

    def fill_copy(k):
        start = pl.multiple_of(fill_ref[k], tmg)
        return pltpu.make_async_copy(zero_s, xs_out.at[pl.ds(start, tmg)], fsem)

    @pl.when(i == 0)
    def _():
        _tile_index_copy(d_hbm, idx_s, isem, 0, 0).start()
        zero_s[...] = jnp.zeros(zero_s.shape, F32)
        for k in range(fill_ref.shape[0]):
            pl.when(fill_ref[k] >= 0)(lambda k=k: fill_copy(k).start(priority=k % 2))
        for k in range(fill_ref.shape[0]):
            pl.when(fill_ref[k] >= 0)(lambda k=k: fill_copy(k).wait())

    @pl.when(i + 1 < n)
    def _():
        _tile_index_copy(d_hbm, idx_s, isem, i + 1, 1 - slot).start()

    _tile_index_copy(d_hbm, idx_s, isem, i, slot).wait()

    def row_copy(r, dst):
        return pltpu.make_async_copy(h_ref.at[pl.ds(r, 1)], xs_out.at[pl.ds(dst, 1)], rsem)

    def body(r, c):
        for k in range(TOP_K):
            row_copy(r, idx_s[slot, k * tm + r]).start(priority=k % 2)
        return c

    lax.fori_loop(0, tm, body, 0, unroll=DMA_UNROLL)
    for k in range(TOP_K):
        pltpu.make_async_copy(h_ref, xs_out.at[pl.ds(0, tm)], rsem).wait()


def _dispatch(fill_starts, d_tiles, h, n_rows, tm, tmg):
    T, D = h.shape
    grid_spec = pltpu.PrefetchScalarGridSpec(
        num_scalar_prefetch=1, grid=(T // tm,),
        in_specs=[pl.BlockSpec(memory_space=pl.ANY), pl.BlockSpec((tm, D), lambda i, fs: (i, 0))],
        out_specs=pl.BlockSpec(memory_space=pl.ANY),
        scratch_shapes=[pltpu.SMEM((2, TOP_K * tm), jnp.int32), pltpu.VMEM((tmg, D), F32),
                        pltpu.SemaphoreType.DMA((2,)), pltpu.SemaphoreType.DMA(()),
                        pltpu.SemaphoreType.DMA(())])
    return pl.pallas_call(
        _dispatch_kernel, grid_spec=grid_spec, out_shape=jax.ShapeDtypeStruct((n_rows, D), F32),
        compiler_params=_cparams(("arbitrary",)), name="dispatch",
    )(fill_starts, d_tiles, h)


def _gmm_kernel(te_ref, nu_ref, x_ref, fn_ref, wg_ref, wu_ref, wd_ref, o_ref, u_s):
    i, f = pl.program_id(0), pl.program_id(1)
    used = i < nu_ref[0]

    @pl.when(f == 0)
    def _():
        o_ref[...] = jnp.zeros(o_ref.shape, F32)

    @pl.when(used & (f == 0))
    def _():
        u_s[...] = _rms(x_ref[...], fn_ref[...]).astype(BF16)

    @pl.when(used)
    def _():
        u = u_s[...]
        a = _dot(u, wg_ref[0].astype(BF16))
        mid = (a * jax.nn.sigmoid(a) * _dot(u, wu_ref[0].astype(BF16))).astype(BF16)
        o_ref[...] += _dot(mid, wd_ref[0].astype(BF16))


def _gmm(tile_expert, n_used, xs, fn, wg, wu, wd, tmg, tf):
    R, D = xs.shape
    F = wg.shape[-1]
    nf = F // tf
    f_eff = lambda i, f, nu: jnp.where(i < nu[0], f, nf - 1)
    grid_spec = pltpu.PrefetchScalarGridSpec(
        num_scalar_prefetch=2, grid=(R // tmg, nf),
        in_specs=[pl.BlockSpec((tmg, D), lambda i, f, te, nu: (i, 0)),
                  pl.BlockSpec(fn.shape, lambda i, f, te, nu: (0, 0)),
                  pl.BlockSpec((1, D, tf), lambda i, f, te, nu: (te[i], 0, f_eff(i, f, nu))),
                  pl.BlockSpec((1, D, tf), lambda i, f, te, nu: (te[i], 0, f_eff(i, f, nu))),
                  pl.BlockSpec((1, tf, D), lambda i, f, te, nu: (te[i], f_eff(i, f, nu), 0))],
        out_specs=pl.BlockSpec((tmg, D), lambda i, f, te, nu: (i, 0)),
        scratch_shapes=[pltpu.VMEM((tmg, D), BF16)])
    return pl.pallas_call(
        _gmm_kernel, grid_spec=grid_spec, out_shape=jax.ShapeDtypeStruct((R, D), F32),
        compiler_params=_cparams(("arbitrary", "arbitrary")), name="experts",
    )(tile_expert, n_used, xs, fn, wg, wu, wd)


def _combine_kernel(d_hbm, h_ref, meta_ref, ys_hbm, o_ref, idx_s, ya_s, yb_s, isem, rsem):
    tm = h_ref.shape[0]
    i, n = pl.program_id(0), pl.num_programs(0)
    slot = i % 2

    def fetch(tile, s):
        cp = _tile_index_copy(d_hbm, idx_s, isem, tile, s)
        cp.start()
        cp.wait()

        def body(r, c):
            pltpu.make_async_copy(ys_hbm.at[pl.ds(idx_s[s, r], 1)], ya_s.at[s, pl.ds(r, 1)],
                                  rsem.at[s]).start(priority=0)
            pltpu.make_async_copy(ys_hbm.at[pl.ds(idx_s[s, tm + r], 1)], yb_s.at[s, pl.ds(r, 1)],
                                  rsem.at[s]).start(priority=1)
            return c

        lax.fori_loop(0, tm, body, 0, unroll=DMA_UNROLL)

    @pl.when(i == 0)
    def _():
        fetch(0, 0)

    @pl.when(i + 1 < n)
    def _():
        fetch(i + 1, 1 - slot)

    pltpu.make_async_copy(ys_hbm.at[pl.ds(0, tm)], ya_s.at[slot], rsem.at[slot]).wait()
    pltpu.make_async_copy(ys_hbm.at[pl.ds(0, tm)], yb_s.at[slot], rsem.at[slot]).wait()
    meta = meta_ref[...]
    o_ref[...] = (h_ref[...] + meta[:, L_W1:L_W1 + 1] * ya_s[slot]
                  + meta[:, L_W2:L_W2 + 1] * yb_s[slot])


def _combine(d_tiles, h, meta, ys, tm):
    T, D = h.shape
    return pl.pallas_call(
        _combine_kernel, grid=(T // tm,),
        in_specs=[pl.BlockSpec(memory_space=pl.ANY), pl.BlockSpec((tm, D), lambda i: (i, 0)),
                  pl.BlockSpec((tm, LANES), lambda i: (i, 0)), pl.BlockSpec(memory_space=pl.ANY)],
        out_specs=pl.BlockSpec((tm, D), lambda i: (i, 0)),
        out_shape=jax.ShapeDtypeStruct((T, D), F32),
        scratch_shapes=[pltpu.SMEM((2, TOP_K * tm), jnp.int32), pltpu.VMEM((2, tm, D), F32),
                        pltpu.VMEM((2, tm, D), F32), pltpu.SemaphoreType.DMA((2,)),
                        pltpu.SemaphoreType.DMA((2,))],
        compiler_params=_cparams(("arbitrary",)), name="combine",
    )(d_tiles, h, meta, ys)


def _moe(h, fn, w_router, wg, wu, wd, tm, tmg, tf):
    T, D = h.shape
    E = w_router.shape[-1]
    meta, counts = _router(h, fn, jnp.pad(w_router, ((0, 0), (0, LANES - E))), E, tm)
    cnt = counts[0, :E].astype(jnp.int32)
    padded = ((cnt + tmg - 1) // tmg) * tmg
    ends = jnp.cumsum(padded)
    off = ends - padded
    n_tiles = -(-TOP_K * T // tmg) + E
    tile_expert = jnp.minimum(
        jnp.sum(ends[None, :] <= (jnp.arange(n_tiles, dtype=jnp.int32) * tmg)[:, None], axis=1), E - 1
    ).astype(jnp.int32)
    n_used = (ends[-1] // tmg).astype(jnp.int32).reshape(1)
    e12 = meta[:, L_E1:L_E2 + 1].astype(jnp.int32)
    dest = off[e12] + meta[:, L_R1:L_R2 + 1].astype(jnp.int32)
    d_tiles = dest.reshape(T // tm, tm, TOP_K).transpose(0, 2, 1).reshape(T // tm, TOP_K * tm)

    trailing = ends[-1] + jnp.arange(E + 1, dtype=jnp.int32) * tmg
    fill_starts = jnp.concatenate([
        jnp.where(padded > 0, ends - tmg, -1),
        jnp.where(trailing < n_tiles * tmg, trailing, -1)]).astype(jnp.int32)
    xs = _dispatch(fill_starts, d_tiles, h, n_tiles * tmg, tm, tmg)
    ys = _gmm(tile_expert, n_used, xs, fn, wg, wu, wd, tmg, tf)
    return _combine(d_tiles, h, meta, ys, tm)


def _ple_kernel(h_ref, pn_ref, wg_ref, p_ref, wp_ref, fin_ref, o_ref, *, final):
    hh = h_ref[...]
    gate = jax.nn.sigmoid(_dot(_rms(hh, pn_ref[...]).astype(BF16), wg_ref[...]))
    out = hh + gate * _dot(p_ref[...].astype(BF16), wp_ref[...])
    if final:
        out = _rms(out, fin_ref[...])
    o_ref[...] = out


def _ple(h, pn, wg, p, wp, fin, tm, final, row0=0, rows=None):
    D = h.shape[1]
    rows = h.shape[0] if rows is None else rows
    off = row0 // tm
    full = lambda a: pl.BlockSpec(a.shape, lambda i: (0,) * a.ndim)
    return pl.pallas_call(
        functools.partial(_ple_kernel, final=final), grid=(rows // tm,),
        in_specs=[pl.BlockSpec((tm, D), lambda i: (off + i, 0)), full(pn), full(wg),
                  pl.BlockSpec((tm, p.shape[1]), lambda i: (i, 0)), full(wp), full(fin)],
        out_specs=pl.BlockSpec((tm, D), lambda i: (i, 0)),
        out_shape=jax.ShapeDtypeStruct((rows, D), F32),
        compiler_params=_cparams(("parallel",)), name="ple",
    )(h, pn, wg, p, wp, fin)


def _rope_tables(pos):
    pos = pos.astype(F32)[:, None]
    lane = np.arange(LANES)
    inv32 = ROPE_THETA ** (-jnp.arange(32, dtype=F32) / 32)
    inv16 = ROPE_THETA ** (-jnp.arange(16, dtype=F32) / 16)
    a64 = (pos * inv32[None, :])[:, lane % 32]
    a16 = (pos * inv16[None, :])[:, lane % 16]
    lo64 = jnp.asarray((lane % 64) < 32)
    c64, s64 = jnp.cos(a64), jnp.sin(a64)
    c16, s16 = jnp.cos(a16), jnp.sin(a16)
    in_k = jnp.asarray(lane < MLA_ROPE)
    lo_k = jnp.asarray(lane < 16)
    hi_k = jnp.asarray((lane >= 16) & (lane < MLA_ROPE))
    return (c64, jnp.where(lo64, -s64, 0.0), jnp.where(lo64, 0.0, s64), c16, s16,
            jnp.where(in_k, c16, 0.0), jnp.where(lo_k, -s16, 0.0), jnp.where(hi_k, s16, 0.0))


def _ret_tables(L):
    log_g = jnp.log(1.0 - jnp.exp2(-5.0 - jnp.arange(RET_HEADS, dtype=F32)))
    idx = jnp.arange(L, dtype=F32)
    dist = jnp.abs(idx[:, None] - idx[None, :])
    d = jnp.exp(dist[None] * log_g[:, None, None])
    qdec = jnp.exp((idx[:, None] + 1.0) * log_g[None, :])
    kdec = jnp.exp((L - 1.0 - idx)[:, None] * log_g[None, :])
    sdec = jnp.exp(L * log_g)
    wide = lambda t: jnp.repeat(t, RET_DK, axis=1).reshape(L, N_PAIRS, LANES).transpose(1, 0, 2)
    bm = jnp.asarray(np.kron(np.eye(2, dtype=np.float32), np.ones((RET_DK, RET_DV), np.float32)))
    gs = jnp.repeat(sdec, RET_DK).reshape(N_PAIRS, LANES, 1) * bm[None]
    return d.reshape(N_PAIRS, 2 * L, L), wide(qdec), wide(kdec), gs, bm


def _state_to_pairs(s):
    B = s.shape[0]
    s = s.reshape(B, N_PAIRS, 2, RET_DK, RET_DV)
    eye = jnp.eye(2, dtype=s.dtype)
    out = s[:, :, :, :, None, :] * eye[None, None, :, None, :, None]
    return out.reshape(B, N_PAIRS, LANES, LANES)


def _pairs_to_state(sp):
    B = sp.shape[0]
    s = sp.reshape(B, N_PAIRS, 2, RET_DK, 2, RET_DV)
    return jnp.stack([s[:, :, 0, :, 0, :], s[:, :, 1, :, 1, :]], axis=2).reshape(B, RET_HEADS, RET_DK, RET_DV)


def _uq_perm():
    per = MLA_NOPE + MLA_ROPE
    half = MLA_ROPE // 2
    nope = [h * per + d for h in range(MLA_HEADS) for d in range(MLA_NOPE)]
    x1 = [h * per + MLA_NOPE + f for h in range(MLA_HEADS) for f in range(half)]
    x2 = [h * per + MLA_NOPE + half + f for h in range(MLA_HEADS) for f in range(half)]
    return np.array(nope + x1 + x2, np.int32)


def _big_query_weight(w_uk):
    H, half = MLA_HEADS, MLA_ROPE // 2
    eye = jnp.eye(H, dtype=w_uk.dtype)
    wpad = jnp.pad(w_uk, ((0, 0), (0, 0), (0, QK_PAD - KV_LORA)))
    top = (eye[:, None, :, None] * wpad[:, :, None, :]).reshape(H * MLA_NOPE, H * QK_PAD)
    sel = np.zeros((2 * H * half, H * QK_PAD), np.float32)
    for h in range(H):
        for f in range(half):
            sel[h * half + f, h * QK_PAD + KV_LORA + f] = 1.0
            sel[H * half + h * half + f, h * QK_PAD + KV_LORA + half + f] = 1.0
    return jnp.concatenate([top, jnp.asarray(sel, w_uk.dtype)], axis=0)


def kernel(x_prompt, x_sample, p_prompt, p_sample, cache_ckv, cache_krope, state_ret, attn_norm, w_in, q_norm, w_uq, kv_norm, w_uk, w_uv, ret_norm, w_o, ffn_norm, w_gate_d, w_up_d, w_down_d, w_router, w_gate_e, w_up_e, w_down_e, ple_norm, w_ple_gate, w_ple_proj, final_norm):
    Bp, S, D = x_prompt.shape
    Bs, L, _ = x_sample.shape
    depth = w_in.shape[0]
    P = cache_ckv.shape[2]
    Tp, Ts = Bp * S, Bs * L
    T = Tp + Ts
    assert S % CHUNK == 0 and P % CHUNK == 0 and L == CHUNK and Tp % CHUNK == 0
    assert w_router.shape[-1] >= TOP_K

    tm_proj = _pick(int(np.gcd(S, Ts)), (512, 256, 128, 64))
    tm_ffn = _pick(T, (1280, 640, 512, 256, 128, 64))
    tm_gmm = 1024
    rb = _pick(S, (512, 256, 128, 64))
    tq = _pick(S, (256, 128, 64))
    kb = _pick(S, (512, 256, 128))
    kb_s = 512
    sk_s = P + L
    sk_pad = -(-sk_s // kb_s) * kb_s

    pos = jnp.concatenate([jnp.arange(S, dtype=jnp.int32),
                           P + jnp.tile(jnp.arange(L, dtype=jnp.int32), Bs)])
    rope_tabs = _rope_tables(pos)
    n_pt, pt_per_seq = Tp // tm_proj, S // tm_proj
    tab_tile = lambda i: jnp.where(i < n_pt, i % pt_per_seq, pt_per_seq + i - n_pt)
    ret_tabs_p = _ret_tables(CHUNK)
    ret_tabs_s = _ret_tables(L)
    perm = _uq_perm()
    row2 = lambda v: v.reshape(1, -1)

    h = jnp.concatenate([x_prompt.reshape(Tp, D), x_sample.reshape(Ts, D)], axis=0)
    outs = {k: [] for k in ("ckv_p", "kro_p", "ret_p", "ckv_s", "kro_s", "ret_s")}
    for l in range(depth):
        win = jnp.pad(w_in[l], ((0, 0), (0, IN_COLS_PAD - IN_COLS))).astype(BF16)
        wuq = w_uq[l][:, perm].astype(BF16)
        wbig = _big_query_weight(w_uk[l]).astype(BF16)
        qr, kr, vr, gr, qx, ckv, kro, kx = _proj(
            h, row2(attn_norm[l]), win, row2(q_norm[l]), wuq, row2(kv_norm[l]), wbig, rope_tabs,
            tab_tile, tm_proj)

        rn = row2(ret_norm[l])
        zero_state = jnp.zeros((Bp, N_PAIRS, LANES, LANES), F32)
        o_ret_p, st_p = _retention(qr, kr, vr, gr, zero_state, ret_tabs_p, rn, Bp, S, 0, rb)
        o_ret_s, st_s = _retention(qr, kr, vr, gr, _state_to_pairs(state_ret[l].astype(F32)),
                                   ret_tabs_s, rn, Bs, L, Tp, L)

        wuv = w_uv[l].astype(BF16)
        wuv_big = (jnp.eye(MLA_HEADS, dtype=BF16)[:, None, :, None] * wuv[:, :, None, :]
                   ).reshape(MLA_HEADS, KV_LORA, MLA_W)
        o_mla_p = _attention(qx, kx, wuv_big, Bp, S, 0, tq, kb, 0, S, S)
        cache_kx = jnp.concatenate(
            [cache_ckv[l], cache_krope[l], jnp.zeros((Bs, P, QK_PAD - KV_LORA - MLA_ROPE), F32)],
            axis=-1).astype(BF16)
        kx_s = jnp.concatenate([cache_kx, kx[Tp:].reshape(Bs, L, QK_PAD),
                                jnp.zeros((Bs, sk_pad - sk_s, QK_PAD), BF16)], axis=1)
        o_mla_s = _attention(qx, kx_s.reshape(Bs * sk_pad, QK_PAD), wuv_big, Bs, L, Tp, L, kb_s, P,
                             sk_pad, sk_s)

        h = _merge(h, o_ret_p, o_mla_p, o_ret_s, o_mla_s, w_o[l].astype(BF16), tm_proj)

        fn = row2(ffn_norm[l])
        j = l // 2
        if l % 2 == 0:
            tf = _pick(w_gate_d.shape[-1], (256, 128))
            h = _ffn(h, fn, w_gate_d[j].astype(BF16), w_up_d[j].astype(BF16),
                     w_down_d[j].astype(BF16), tm_ffn, tf)
        else:
            tf = _pick(w_gate_e.shape[-1], (512, 256, 128))
            h = _moe(h, fn, w_router[j], w_gate_e[j], w_up_e[j], w_down_e[j], tm_proj, tm_gmm, tf)

        ple_w = (row2(ple_norm[l]), w_ple_gate[l].astype(BF16))
        wp = w_ple_proj[l].astype(BF16)
        if l < depth - 1:
            p_l = jnp.concatenate([p_prompt[l].reshape(Tp, -1), p_sample[l].reshape(Ts, -1)], axis=0)
            h = _ple(h, *ple_w, p_l, wp, row2(final_norm), tm_proj, False)
        else:
            y_p = _ple(h, *ple_w, p_prompt[l].reshape(Tp, -1), wp, row2(final_norm), tm_proj, True, 0, Tp)
            y_s = _ple(h, *ple_w, p_sample[l].reshape(Ts, -1), wp, row2(final_norm), tm_proj, True, Tp, Ts)

        outs["ckv_p"].append(ckv[:Tp].reshape(Bp, S, KV_LORA))
        outs["kro_p"].append(kro[:Tp].reshape(Bp, S, MLA_ROPE))
        outs["ret_p"].append(_pairs_to_state(st_p))
        outs["ckv_s"].append(ckv[Tp:].reshape(Bs, L, KV_LORA))
        outs["kro_s"].append(kro[Tp:].reshape(Bs, L, MLA_ROPE))
        outs["ret_s"].append(_pairs_to_state(st_s))

    return (y_p.reshape(Bp, S, D), y_s.reshape(Bs, L, D),
            jnp.stack(outs["ckv_p"]), jnp.stack(outs["kro_p"]), jnp.stack(outs["ret_p"]),
            jnp.stack(outs["ckv_s"]), jnp.stack(outs["kro_s"]), jnp.stack(outs["ret_s"]))
```

```python
import functools

import numpy as np
import jax
import jax.numpy as jnp
from jax import lax
from jax.experimental import pallas as pl
from jax.experimental.pallas import tpu as pltpu

F32 = jnp.float32
BF16 = jnp.bfloat16

CHUNK = 64
CHUNK_SHIFT = 6
RMS_EPS = 1e-6
ROPE_THETA = 10000.0
RET_HEADS = 8
RET_DK = 64
RET_DV = 64
RET_W = RET_HEADS * RET_DK
MLA_HEADS = 8
MLA_NOPE = 64
MLA_ROPE = 32
MLA_V = 64
Q_LORA = 256
KV_LORA = 128
MLA_W = MLA_HEADS * MLA_V
TOP_K = 2
N_PAIRS = RET_HEADS // 2
LANES = 128
SUBLANES = 8
QK_PAD = 256
OFF_Q = RET_W
OFF_K = OFF_Q + RET_W
OFF_V = OFF_K + RET_W
OFF_G = OFF_V + RET_W
OFF_CQ = OFF_G + Q_LORA
OFF_CKV = OFF_CQ + KV_LORA
IN_COLS = OFF_CKV + MLA_ROPE
IN_COLS_PAD = OFF_CKV + LANES
NEG = -1e30
VMEM_LIMIT = 56 * 1024 * 1024


def _pick(n, cands):
    for c in cands:
        if n % c == 0:
            return c
    return n


def _cparams(sem):
    return pltpu.CompilerParams(dimension_semantics=sem, vmem_limit_bytes=VMEM_LIMIT)


def _rms(x, g):
    return x * lax.rsqrt(jnp.mean(x * x, axis=-1, keepdims=True) + RMS_EPS) * g


def _dot(a, b):
    return jnp.dot(a, b, preferred_element_type=F32)


def _dot_nt(a, b):
    return lax.dot_general(a, b, (((1,), (1,)), ((), ())), preferred_element_type=F32)


def _dot_tn(a, b):
    return lax.dot_general(a, b, (((0,), (0,)), ((), ())), preferred_element_type=F32)


def _proj_kernel(h_ref, an_ref, win_ref, qn_ref, wuq_ref, kvn_ref, wbig_ref,
                 c64_ref, sm64_ref, sp64_ref, c16_ref, s16_ref, ck_ref, smk_ref, spk_ref,
                 qr_ref, kr_ref, vr_ref, gr_ref, qx_ref, ckv_ref, kro_ref, kx_ref):
    xn = _rms(h_ref[...], an_ref[...]).astype(BF16)
    c64, sm64, sp64 = c64_ref[...], sm64_ref[...], sp64_ref[...]

    def rope64(z):
        return (z * c64 + pltpu.roll(z, LANES - 32, 1) * sm64 + pltpu.roll(z, 32, 1) * sp64)

    for lo in range(0, RET_W, QK_PAD):
        zq = _dot(xn, win_ref[:, lo:lo + QK_PAD])
        zk = _dot(xn, win_ref[:, OFF_Q + lo:OFF_Q + lo + QK_PAD])
        for half in range(0, QK_PAD, LANES):
            dst = slice(lo + half, lo + half + LANES)
            qr_ref[:, dst] = rope64(zq[:, half:half + LANES])
            kr_ref[:, dst] = rope64(zk[:, half:half + LANES]) * (RET_DK ** -0.5)

    zkv = _dot(xn, win_ref[:, OFF_CQ:IN_COLS_PAD])
    ckv = _rms(zkv[:, :KV_LORA], kvn_ref[...])
    ckv_ref[...] = ckv
    zk = zkv[:, KV_LORA:]
    kro = (zk * ck_ref[...] + pltpu.roll(zk, LANES - 16, 1) * smk_ref[...]
           + pltpu.roll(zk, 16, 1) * spk_ref[...])
    kro_ref[...] = kro[:, :MLA_ROPE]
    kx_ref[:, :KV_LORA] = ckv.astype(BF16)
    kx_ref[:, KV_LORA:] = kro.astype(BF16)

    cq = _dot(xn, win_ref[:, OFF_G:OFF_CQ])
    q = _dot(_rms(cq, qn_ref[...]).astype(BF16), wuq_ref[...])
    nq = MLA_HEADS * MLA_NOPE
    x1, x2 = q[:, nq:nq + LANES], q[:, nq + LANES:nq + 2 * LANES]
    c16, s16 = c16_ref[...], s16_ref[...]
    scale = (MLA_NOPE + MLA_ROPE) ** -0.5 * float(np.log2(np.e))
    qcat = jnp.concatenate([q[:, :nq], x1 * c16 - x2 * s16, x2 * c16 + x1 * s16], axis=-1)
    qcat = (qcat * scale).astype(BF16)
    half_w = (MLA_HEADS // 2) * QK_PAD
    for g in range(2):
        rows = slice(g * (nq // 2), (g + 1) * (nq // 2))
        cols = slice(g * half_w, (g + 1) * half_w)
        qx_ref[:, cols] = (_dot(qcat[:, rows], wbig_ref[rows, cols])
                           + _dot(qcat[:, nq:], wbig_ref[nq:, cols])).astype(BF16)

    vr_ref[...] = _dot(xn, win_ref[:, OFF_K:OFF_V]).astype(BF16)
    gr_ref[...] = _dot(xn, win_ref[:, OFF_V:OFF_G])


def _proj(h, an, win, qn, wuq, kvn, wbig, tabs, tab_tile, tm):
    T, D = h.shape
    row = lambda w: pl.BlockSpec((tm, w), lambda i: (i, 0))
    tab = pl.BlockSpec((tm, LANES), lambda i: (tab_tile(i), 0))
    full = lambda a: pl.BlockSpec(a.shape, lambda i: (0,) * a.ndim)
    out_shapes = (
        jax.ShapeDtypeStruct((T, RET_W), F32), jax.ShapeDtypeStruct((T, RET_W), F32),
        jax.ShapeDtypeStruct((T, RET_W), BF16), jax.ShapeDtypeStruct((T, RET_W), F32),
        jax.ShapeDtypeStruct((T, MLA_HEADS * QK_PAD), BF16),
        jax.ShapeDtypeStruct((T, KV_LORA), F32), jax.ShapeDtypeStruct((T, MLA_ROPE), F32),
        jax.ShapeDtypeStruct((T, QK_PAD), BF16))
    return pl.pallas_call(
        _proj_kernel, grid=(T // tm,),
        in_specs=[row(D), full(an), full(win), full(qn), full(wuq), full(kvn), full(wbig)]
                 + [tab] * 8,
        out_specs=(row(RET_W), row(RET_W), row(RET_W), row(RET_W), row(MLA_HEADS * QK_PAD),
                   row(KV_LORA), row(MLA_ROPE), row(QK_PAD)),
        out_shape=out_shapes, compiler_params=_cparams(("parallel",)), name="proj",
    )(h, an, win, qn, wuq, kvn, wbig, *tabs)


def _ret_kernel(q_ref, k_ref, v_ref, g_ref, s0_ref, d_ref, qd_ref, kd_ref, gs_ref, bm_ref, rn_ref,
                o_ref, so_ref, st_ref, *, n_chunks):
    i = pl.program_id(1)

    @pl.when(i == 0)
    def _():
        st_ref[...] = s0_ref[0]

    lane = lax.broadcasted_iota(jnp.int32, (CHUNK, LANES), 1)
    first = lane < RET_DK

    def chunk(c, carry):
        rows = pl.ds(pl.multiple_of(c * CHUNK, CHUNK), CHUNK)
        for p in range(N_PAIRS):
            cols = slice(p * LANES, (p + 1) * LANES)
            qp, kp, vp = q_ref[rows, cols], k_ref[rows, cols], v_ref[rows, cols]
            q2 = jnp.concatenate([jnp.where(first, qp, 0.0), jnp.where(first, 0.0, qp)], axis=0)
            sd = (_dot_nt(q2.astype(BF16), kp.astype(BF16)) * d_ref[p]).astype(BF16)
            qq = (qp * qd_ref[p]).astype(BF16)
            st = st_ref[p]
            lhs = jnp.concatenate([jnp.concatenate([qq, qq], axis=0), sd], axis=1)
            rhs = jnp.concatenate([st.astype(BF16), vp], axis=0)
            o2 = _dot(lhs, rhs)
            o = jnp.where(first, o2[:CHUNK], o2[CHUNK:])
            upd = _dot_tn((kp * kd_ref[p]).astype(BF16), vp)
            st_ref[p] = st * gs_ref[p] + upd * bm_ref[...]
            oo = o * o
            ss_a = jnp.sum(jnp.where(first, oo, 0.0), axis=-1, keepdims=True)
            ss_b = jnp.sum(jnp.where(first, 0.0, oo), axis=-1, keepdims=True)
            rs = jnp.where(first, lax.rsqrt(ss_a * (1.0 / RET_DV) + RMS_EPS),
                           lax.rsqrt(ss_b * (1.0 / RET_DV) + RMS_EPS))
            g = g_ref[rows, cols]
            o_ref[rows, cols] = (o * rs * rn_ref[:, cols] * (g * jax.nn.sigmoid(g))).astype(BF16)
        return carry

    lax.fori_loop(0, n_chunks, chunk, 0, unroll=min(n_chunks, 4))

    @pl.when(i == pl.num_programs(1) - 1)
    def _():
        so_ref[0] = st_ref[...]


def _retention(qr, kr, vr, gr, s0, tabs, rn, nb, seq, row0, rb):
    nblk = seq // rb
    off = row0 // rb
    tok = pl.BlockSpec((rb, RET_W), lambda b, i: (off + b * nblk + i, 0))
    full = lambda a: pl.BlockSpec(a.shape, lambda b, i: (0,) * a.ndim)
    st_spec = pl.BlockSpec((1, N_PAIRS, LANES, LANES), lambda b, i: (b, 0, 0, 0))
    d, qd, kd, gs, bm = tabs
    return pl.pallas_call(
        functools.partial(_ret_kernel, n_chunks=rb // CHUNK), grid=(nb, nblk),
        in_specs=[tok, tok, tok, tok, st_spec, full(d), full(qd), full(kd), full(gs), full(bm), full(rn)],
        out_specs=(pl.BlockSpec((rb, RET_W), lambda b, i: (b * nblk + i, 0)), st_spec),
        out_shape=(jax.ShapeDtypeStruct((nb * seq, RET_W), BF16),
                   jax.ShapeDtypeStruct((nb, N_PAIRS, LANES, LANES), F32)),
        scratch_shapes=[pltpu.VMEM((N_PAIRS, LANES, LANES), F32)],
        compiler_params=_cparams(("parallel", "arbitrary")), name="retention",
    )(qr, kr, vr, gr, s0, d, qd, kd, gs, bm, rn)


def _attn_kernel(q_ref, k_ref, wuv_ref, o_ref, q_s, s0_s, s1_s, m_s, l_s, acc_s,
                 *, tq, kb, q_pos0, sk_valid):
    i = pl.program_id(1)
    R = MLA_HEADS * tq
    for h in range(MLA_HEADS):
        q_s[h * tq:(h + 1) * tq, :] = q_ref[:, h * QK_PAD:(h + 1) * QK_PAD]
    m_s[...] = jnp.full(m_s.shape, NEG, F32)
    l_s[...] = jnp.zeros(l_s.shape, F32)
    acc_s[...] = jnp.zeros(acc_s.shape, F32)

    chunk_end = lambda t: ((t >> CHUNK_SHIFT) + 1) << CHUNK_SHIFT
    kb_shift = kb.bit_length() - 1
    qstart = q_pos0 + i * tq
    lim_first = jnp.minimum(chunk_end(qstart), sk_valid)
    lim_last = jnp.minimum(chunk_end(qstart + tq - 1), sk_valid)
    n_full = lim_first >> kb_shift
    n_blk = (lim_last + kb - 1) >> kb_shift

    last = n_blk - 1

    def key_block(j):
        return k_ref[pl.ds(pl.multiple_of(j * kb, kb), kb), :]

    def scores(j, s_ref):
        s_ref[...] = _dot_nt(q_s[...], key_block(j))

    def update(j, s_ref, masked):
        s = s_ref[...]
        if masked:
            tok = lax.broadcasted_iota(jnp.int32, (R, 1), 0) & (tq - 1)
            row_lim = jnp.minimum(chunk_end(qstart + tok), sk_valid)
            kidx = j * kb + lax.broadcasted_iota(jnp.int32, (1, kb), 1)
            s = jnp.where(kidx < row_lim, s, NEG)
        m_prev = m_s[...]
        m_next = jnp.maximum(m_prev, jnp.max(s, axis=-1, keepdims=True))
        p = jnp.exp2(s - jnp.tile(m_next, (1, kb // LANES)))
        alpha = jnp.exp2(m_prev - m_next)
        l_s[...] = alpha * l_s[...] + jnp.sum(p, axis=-1, keepdims=True)
        acc_s[...] = alpha * acc_s[...] + _dot(p.astype(BF16), key_block(j)[:, :KV_LORA])
        m_s[...] = m_next

    n_pipe = jnp.minimum(n_full, last)
    odd = n_pipe & 1

    @pl.when(odd == 1)
    def _():
        scores(0, s0_s)
        update(0, s0_s, False)

    scores(odd, s0_s)

    def pair(jj, c):
        j = odd + 2 * jj
        scores(j + 1, s1_s)
        update(j, s0_s, False)
        scores(j + 2, s0_s)
        update(j + 1, s1_s, False)
        return c

    lax.fori_loop(0, n_pipe >> 1, pair, 0)
    update(n_pipe, s0_s, True)

    def tail(j, c):
        scores(j, s0_s)
        update(j, s0_s, True)
        return c

    lax.fori_loop(n_pipe + 1, n_blk, tail, 0)

    o_lat = (acc_s[...] / l_s[...]).astype(BF16)
    out = _dot(o_lat[0:tq], wuv_ref[0])
    for h in range(1, MLA_HEADS):
        out = out + _dot(o_lat[h * tq:(h + 1) * tq], wuv_ref[h])
    o_ref[...] = out.astype(BF16)


def _attention(qx, kx, wuv, nb, seq_q, row0, tq, kb, q_pos0, sk, sk_valid):
    nq = seq_q // tq
    off = row0 // tq
    R = MLA_HEADS * tq
    return pl.pallas_call(
        functools.partial(_attn_kernel, tq=tq, kb=kb, q_pos0=q_pos0, sk_valid=sk_valid),
        grid=(nb, nq),
        in_specs=[pl.BlockSpec((tq, MLA_HEADS * QK_PAD), lambda b, i: (off + b * nq + i, 0)),
                  pl.BlockSpec((sk, QK_PAD), lambda b, i: (b, 0)),
                  pl.BlockSpec(wuv.shape, lambda b, i: (0, 0, 0))],
        out_specs=pl.BlockSpec((tq, MLA_W), lambda b, i: (b * nq + i, 0)),
        out_shape=jax.ShapeDtypeStruct((nb * seq_q, MLA_W), BF16),
        scratch_shapes=[pltpu.VMEM((R, QK_PAD), BF16), pltpu.VMEM((R, kb), F32),
                        pltpu.VMEM((R, kb), F32), pltpu.VMEM((R, LANES), F32),
                        pltpu.VMEM((R, LANES), F32), pltpu.VMEM((R, KV_LORA), F32)],
        compiler_params=_cparams(("parallel", "arbitrary")), name="attention",
    )(qx, kx, wuv)


def _merge_kernel(h_ref, ap_ref, bp_ref, as_ref, bs_ref, wo_ref, o_ref, *, n_prompt_tiles):
    def project(a_ref, b_ref):
        o_ref[...] = (h_ref[...] + _dot(a_ref[...], wo_ref[:RET_W, :])
                      + _dot(b_ref[...], wo_ref[RET_W:, :]))

    @pl.when(pl.program_id(0) < n_prompt_tiles)
    def _():
        project(ap_ref, bp_ref)

    @pl.when(pl.program_id(0) >= n_prompt_tiles)
    def _():
        project(as_ref, bs_ref)


def _merge(h, a_p, b_p, a_s, b_s, wo, tm):
    T, D = h.shape
    npt = a_p.shape[0] // tm
    row = lambda w: pl.BlockSpec((tm, w), lambda i: (i, 0))
    prm = lambda w: pl.BlockSpec((tm, w), lambda i: (jnp.minimum(i, npt - 1), 0))
    smp = lambda w: pl.BlockSpec((tm, w), lambda i: (jnp.maximum(i - npt, 0), 0))
    return pl.pallas_call(
        functools.partial(_merge_kernel, n_prompt_tiles=npt), grid=(T // tm,),
        in_specs=[row(D), prm(RET_W), prm(MLA_W), smp(RET_W), smp(MLA_W),
                  pl.BlockSpec(wo.shape, lambda i: (0, 0))],
        out_specs=row(D), out_shape=jax.ShapeDtypeStruct((T, D), F32),
        compiler_params=_cparams(("parallel",)), name="merge",
    )(h, a_p, b_p, a_s, b_s, wo)


def _ffn_kernel(h_ref, fn_ref, wg_ref, wu_ref, wd_ref, o_ref):
    hh = h_ref[...]
    u = _rms(hh, fn_ref[...]).astype(BF16)
    a = _dot(u, wg_ref[...])
    mid = (a * jax.nn.sigmoid(a) * _dot(u, wu_ref[...])).astype(BF16)
    o_ref[...] = hh + _dot(mid, wd_ref[...])


def _ffn(h, fn, wg, wu, wd, tm):
    T, D = h.shape
    resident = lambda a: pl.BlockSpec(a.shape, lambda i: (0, 0), pipeline_mode=pl.Buffered(1))
    return pl.pallas_call(
        _ffn_kernel, grid=(T // tm,),
        in_specs=[pl.BlockSpec((tm, D), lambda i: (i, 0)), pl.BlockSpec(fn.shape, lambda i: (0, 0)),
                  resident(wg), resident(wu), resident(wd)],
        out_specs=pl.BlockSpec((tm, D), lambda i: (i, 0)),
        out_shape=jax.ShapeDtypeStruct((T, D), F32),
        compiler_params=_cparams(("parallel",)), name="ffn",
    )(h, fn, wg, wu, wd)


L_E1, L_E2, L_W1, L_W2, L_R1, L_R2 = range(6)


def _router_kernel(h_ref, fn_ref, wr_ref, meta_ref, cnt_ref, run_s, *, n_experts):
    tm = h_ref.shape[0]

    @pl.when(pl.program_id(0) == 0)
    def _():
        run_s[...] = jnp.zeros(run_s.shape, F32)

    u = _rms(h_ref[...], fn_ref[...])
    logits = jnp.dot(u, wr_ref[...], precision=lax.Precision.HIGHEST, preferred_element_type=F32)
    lane = lax.broadcasted_iota(jnp.int32, logits.shape, 1).astype(F32)
    lg = jnp.where(lane < n_experts, logits, NEG)
    m1 = jnp.max(lg, axis=-1, keepdims=True)
    i1 = jnp.min(jnp.where(lg == m1, lane, float(LANES)), axis=-1, keepdims=True)
    lg2 = jnp.where(lane == i1, NEG, lg)
    m2 = jnp.max(lg2, axis=-1, keepdims=True)
    i2 = jnp.min(jnp.where(lg2 == m2, lane, float(LANES)), axis=-1, keepdims=True)
    e2 = jnp.exp(m2 - m1)
    den = 1.0 + e2

    hit = jnp.where((lane == i1) | (lane == i2), 1.0, 0.0)
    r_i = lax.broadcasted_iota(jnp.int32, (tm, tm), 0)
    c_i = lax.broadcasted_iota(jnp.int32, (tm, tm), 1)
    before = jnp.where(c_i < r_i, 1.0, 0.0).astype(BF16)
    prefix = _dot(before, hit.astype(BF16)) + run_s[0:1, :]
    r1 = jnp.sum(jnp.where(lane == i1, prefix, 0.0), axis=-1, keepdims=True)
    r2 = jnp.sum(jnp.where(lane == i2, prefix, 0.0), axis=-1, keepdims=True)
    run_s[...] = run_s[...] + jnp.sum(hit, axis=0, keepdims=True)
    cnt_ref[...] = run_s[...]

    meta = jnp.where(lane == L_E1, i1, 0.0)
    for ln, val in ((L_E2, i2), (L_W1, 1.0 / den), (L_W2, e2 / den), (L_R1, r1), (L_R2, r2)):
        meta = jnp.where(lane == ln, val, meta)
    meta_ref[...] = meta


def _router(h, fn, wr, n_experts, tm):
    T, D = h.shape
    return pl.pallas_call(
        functools.partial(_router_kernel, n_experts=n_experts), grid=(T // tm,),
        in_specs=[pl.BlockSpec((tm, D), lambda i: (i, 0)), pl.BlockSpec(fn.shape, lambda i: (0, 0)),
                  pl.BlockSpec(wr.shape, lambda i: (0, 0))],
        out_specs=(pl.BlockSpec((tm, LANES), lambda i: (i, 0)),
                   pl.BlockSpec((SUBLANES, LANES), lambda i: (0, 0))),
        out_shape=(jax.ShapeDtypeStruct((T, LANES), F32), jax.ShapeDtypeStruct((SUBLANES, LANES), F32)),
        scratch_shapes=[pltpu.VMEM((SUBLANES, LANES), F32)],
        compiler_params=_cparams(("arbitrary",)), name="router",
    )(h, fn, wr)


def _tile_index_copy(d_hbm, idx_s, sem, tile, slot):
    n = d_hbm.shape[1]
    return pltpu.make_async_copy(d_hbm.at[tile], idx_s.at[pl.ds(pl.multiple_of(slot * n, n), n)],
                                 sem.at[slot])


def _dispatch_kernel(fill_ref, d_hbm, h_ref, xs_out, idx_s, zero_s, isem, rsem, fsem):
    tm = h_ref.shape[0]
    tmg = zero_s.shape[0]
    i, n = pl.program_id(0), pl.num_programs(0)
    slot = i % 2

    def fill_copy(k):
        start = pl.multiple_of(fill_ref[k], tmg)
        return pltpu.make_async_copy(zero_s, xs_out.at[pl.ds(start, tmg)], fsem)

    @pl.when(i == 0)
    def _():
        _tile_index_copy(d_hbm, idx_s, isem, 0, 0).start()
        zero_s[...] = jnp.zeros(zero_s.shape, F32)
        for k in range(fill_ref.shape[0]):
            pl.when(fill_ref[k] >= 0)(lambda k=k: fill_copy(k).start(priority=k % 2))
        for k in range(fill_ref.shape[0]):
            pl.when(fill_ref[k] >= 0)(lambda k=k: fill_copy(k).wait())

    @pl.when(i + 1 < n)
    def _():
        _tile_index_copy(d_hbm, idx_s, isem, i + 1, 1 - slot).start()

    _tile_index_copy(d_hbm, idx_s, isem, i, slot).wait()

    def row_copy(r, dst):
        return pltpu.make_async_copy(h_ref.at[pl.ds(r, 1)], xs_out.at[pl.ds(dst, 1)], rsem)

    def body(g, c):
        r0 = pl.multiple_of(g * SUBLANES, SUBLANES)
        for u in range(SUBLANES):
            for k in range(TOP_K):
                row_copy(r0 + u, idx_s[slot * (TOP_K * tm) + k * tm + r0 + u]).start(priority=k % 2)
        return c

    lax.fori_loop(0, tm // SUBLANES, body, 0)
    for k in range(TOP_K):
        pltpu.make_async_copy(h_ref, xs_out.at[pl.ds(0, tm)], rsem).wait()


def _dispatch(fill_starts, d_tiles, h, n_rows, tm, tmg):
    T, D = h.shape
    grid_spec = pltpu.PrefetchScalarGridSpec(
        num_scalar_prefetch=1, grid=(T // tm,),
        in_specs=[pl.BlockSpec(memory_space=pl.ANY), pl.BlockSpec((tm, D), lambda i, fs: (i, 0))],
        out_specs=pl.BlockSpec(memory_space=pl.ANY),
        scratch_shapes=[pltpu.SMEM((2 * TOP_K * tm,), jnp.int32), pltpu.VMEM((tmg, D), F32),
                        pltpu.SemaphoreType.DMA((2,)), pltpu.SemaphoreType.DMA(()),
                        pltpu.SemaphoreType.DMA(())])
    return pl.pallas_call(
        _dispatch_kernel, grid_spec=grid_spec, out_shape=jax.ShapeDtypeStruct((n_rows, D), F32),
        compiler_params=_cparams(("arbitrary",)), name="dispatch",
    )(fill_starts, d_tiles, h)


def _gmm_kernel(te_ref, nu_ref, x_ref, fn_ref, wg_ref, wu_ref, wd_ref, o_ref, u_s):
    i, f = pl.program_id(0), pl.program_id(1)
    used = i < nu_ref[0]

    @pl.when(f == 0)
    def _():
        o_ref[...] = jnp.zeros(o_ref.shape, F32)

    @pl.when(used & (f == 0))
    def _():
        u_s[...] = _rms(x_ref[...], fn_ref[...]).astype(BF16)

    @pl.when(used)
    def _():
        u = u_s[...]
        a = _dot(u, wg_ref[0].astype(BF16))
        mid = (a * jax.nn.sigmoid(a) * _dot(u, wu_ref[0].astype(BF16))).astype(BF16)
        o_ref[...] += _dot(mid, wd_ref[0].astype(BF16))


def _gmm(tile_expert, n_used, xs, fn, wg, wu, wd, tmg, tf):
    R, D = xs.shape
    F = wg.shape[-1]
    nf = F // tf
    f_eff = lambda i, f, nu: jnp.where(i < nu[0], f, nf - 1)
    grid_spec = pltpu.PrefetchScalarGridSpec(
        num_scalar_prefetch=2, grid=(R // tmg, nf),
        in_specs=[pl.BlockSpec((tmg, D), lambda i, f, te, nu: (i, 0)),
                  pl.BlockSpec(fn.shape, lambda i, f, te, nu: (0, 0)),
                  pl.BlockSpec((1, D, tf), lambda i, f, te, nu: (te[i], 0, f_eff(i, f, nu))),
                  pl.BlockSpec((1, D, tf), lambda i, f, te, nu: (te[i], 0, f_eff(i, f, nu))),
                  pl.BlockSpec((1, tf, D), lambda i, f, te, nu: (te[i], f_eff(i, f, nu), 0))],
        out_specs=pl.BlockSpec((tmg, D), lambda i, f, te, nu: (i, 0)),
        scratch_shapes=[pltpu.VMEM((tmg, D), BF16)])
    return pl.pallas_call(
        _gmm_kernel, grid_spec=grid_spec, out_shape=jax.ShapeDtypeStruct((R, D), F32),
        compiler_params=_cparams(("arbitrary", "arbitrary")), name="experts",
    )(tile_expert, n_used, xs, fn, wg, wu, wd)


def _combine_kernel(d_hbm, h_ref, meta_ref, ys_hbm, o_ref, idx_s, ya_s, yb_s, isem, rsem):
    tm = h_ref.shape[0]
    i, n = pl.program_id(0), pl.num_programs(0)
    slot = i % 2

    def fetch(tile, s):
        cp = _tile_index_copy(d_hbm, idx_s, isem, tile, s)
        cp.start()
        cp.wait()

        def body(g, c):
            r0 = pl.multiple_of(g * SUBLANES, SUBLANES)
            base = s * (TOP_K * tm) + r0
            for u in range(SUBLANES):
                pltpu.make_async_copy(ys_hbm.at[pl.ds(idx_s[base + u], 1)],
                                      ya_s.at[s, pl.ds(r0 + u, 1)], rsem.at[s]).start(priority=0)
                pltpu.make_async_copy(ys_hbm.at[pl.ds(idx_s[base + tm + u], 1)],
                                      yb_s.at[s, pl.ds(r0 + u, 1)], rsem.at[s]).start(priority=1)
            return c

        lax.fori_loop(0, tm // SUBLANES, body, 0)

    @pl.when(i == 0)
    def _():
        fetch(0, 0)

    @pl.when(i + 1 < n)
    def _():
        fetch(i + 1, 1 - slot)

    pltpu.make_async_copy(ys_hbm.at[pl.ds(0, tm)], ya_s.at[slot], rsem.at[slot]).wait()
    pltpu.make_async_copy(ys_hbm.at[pl.ds(0, tm)], yb_s.at[slot], rsem.at[slot]).wait()
    meta = meta_ref[...]
    o_ref[...] = (h_ref[...] + meta[:, L_W1:L_W1 + 1] * ya_s[slot]
                  + meta[:, L_W2:L_W2 + 1] * yb_s[slot])


def _combine(d_tiles, h, meta, ys, tm):
    T, D = h.shape
    return pl.pallas_call(
        _combine_kernel, grid=(T // tm,),
        in_specs=[pl.BlockSpec(memory_space=pl.ANY), pl.BlockSpec((tm, D), lambda i: (i, 0)),
                  pl.BlockSpec((tm, LANES), lambda i: (i, 0)), pl.BlockSpec(memory_space=pl.ANY)],
        out_specs=pl.BlockSpec((tm, D), lambda i: (i, 0)),
        out_shape=jax.ShapeDtypeStruct((T, D), F32),
        scratch_shapes=[pltpu.SMEM((2 * TOP_K * tm,), jnp.int32), pltpu.VMEM((2, tm, D), F32),
                        pltpu.VMEM((2, tm, D), F32), pltpu.SemaphoreType.DMA((2,)),
                        pltpu.SemaphoreType.DMA((2,))],
        compiler_params=_cparams(("arbitrary",)), name="combine",
    )(d_tiles, h, meta, ys)


def _moe(h, fn, w_router, wg, wu, wd, tm, tmg, tf):
    T, D = h.shape
    E = w_router.shape[-1]
    meta, counts = _router(h, fn, jnp.pad(w_router, ((0, 0), (0, LANES - E))), E, tm)
    cnt = counts[0, :E].astype(jnp.int32)
    padded = ((cnt + tmg - 1) // tmg) * tmg
    ends = jnp.cumsum(padded)
    off = ends - padded
    n_tiles = -(-TOP_K * T // tmg) + E
    tile_expert = jnp.minimum(
        jnp.sum(ends[None, :] <= (jnp.arange(n_tiles, dtype=jnp.int32) * tmg)[:, None], axis=1), E - 1
    ).astype(jnp.int32)
    n_used = (ends[-1] // tmg).astype(jnp.int32).reshape(1)
    e12 = meta[:, L_E1:L_E2 + 1].astype(jnp.int32)
    dest = off[e12] + meta[:, L_R1:L_R2 + 1].astype(jnp.int32)
    d_tiles = dest.reshape(T // tm, tm, TOP_K).transpose(0, 2, 1).reshape(T // tm, TOP_K * tm)

    trailing = ends[-1] + jnp.arange(E + 1, dtype=jnp.int32) * tmg
    fill_starts = jnp.concatenate([
        jnp.where(padded > 0, ends - tmg, -1),
        jnp.where(trailing < n_tiles * tmg, trailing, -1)]).astype(jnp.int32)
    xs = _dispatch(fill_starts, d_tiles, h, n_tiles * tmg, tm, tmg)
    ys = _gmm(tile_expert, n_used, xs, fn, wg, wu, wd, tmg, tf)
    return _combine(d_tiles, h, meta, ys, tm)


def _ple_kernel(h_ref, pn_ref, wg_ref, p_ref, wp_ref, fin_ref, o_ref, *, final):
    hh = h_ref[...]
    gate = jax.nn.sigmoid(_dot(_rms(hh, pn_ref[...]).astype(BF16), wg_ref[...]))
    out = hh + gate * _dot(p_ref[...].astype(BF16), wp_ref[...])
    if final:
        out = _rms(out, fin_ref[...])
    o_ref[...] = out


def _ple(h, pn, wg, p, wp, fin, tm, final, row0=0, rows=None):
    D = h.shape[1]
    rows = h.shape[0] if rows is None else rows
    off = row0 // tm
    full = lambda a: pl.BlockSpec(a.shape, lambda i: (0,) * a.ndim)
    return pl.pallas_call(
        functools.partial(_ple_kernel, final=final), grid=(rows // tm,),
        in_specs=[pl.BlockSpec((tm, D), lambda i: (off + i, 0)), full(pn), full(wg),
                  pl.BlockSpec((tm, p.shape[1]), lambda i: (i, 0)), full(wp), full(fin)],
        out_specs=pl.BlockSpec((tm, D), lambda i: (i, 0)),
        out_shape=jax.ShapeDtypeStruct((rows, D), F32),
        compiler_params=_cparams(("parallel",)), name="ple",
    )(h, pn, wg, p, wp, fin)


def _rope_tables(pos):
    pos = pos.astype(F32)[:, None]
    lane = np.arange(LANES)
    inv32 = ROPE_THETA ** (-jnp.arange(32, dtype=F32) / 32)
    inv16 = ROPE_THETA ** (-jnp.arange(16, dtype=F32) / 16)
    a64 = (pos * inv32[None, :])[:, lane % 32]
    a16 = (pos * inv16[None, :])[:, lane % 16]
    lo64 = jnp.asarray((lane % 64) < 32)
    c64, s64 = jnp.cos(a64), jnp.sin(a64)
    c16, s16 = jnp.cos(a16), jnp.sin(a16)
    in_k = jnp.asarray(lane < MLA_ROPE)
    lo_k = jnp.asarray(lane < 16)
    hi_k = jnp.asarray((lane >= 16) & (lane < MLA_ROPE))
    return (c64, jnp.where(lo64, -s64, 0.0), jnp.where(lo64, 0.0, s64), c16, s16,
            jnp.where(in_k, c16, 0.0), jnp.where(lo_k, -s16, 0.0), jnp.where(hi_k, s16, 0.0))


def _ret_tables(L):
    log_g = jnp.log(1.0 - jnp.exp2(-5.0 - jnp.arange(RET_HEADS, dtype=F32)))
    idx = jnp.arange(L, dtype=F32)
    dist = jnp.abs(idx[:, None] - idx[None, :])
    d = jnp.exp(dist[None] * log_g[:, None, None])
    qdec = jnp.exp((idx[:, None] + 1.0) * log_g[None, :])
    kdec = jnp.exp((L - 1.0 - idx)[:, None] * log_g[None, :])
    sdec = jnp.exp(L * log_g)
    wide = lambda t: jnp.repeat(t, RET_DK, axis=1).reshape(L, N_PAIRS, LANES).transpose(1, 0, 2)
    bm = jnp.asarray(np.kron(np.eye(2, dtype=np.float32), np.ones((RET_DK, RET_DV), np.float32)))
    gs = jnp.repeat(sdec, RET_DK).reshape(N_PAIRS, LANES, 1) * bm[None]
    return d.reshape(N_PAIRS, 2 * L, L), wide(qdec), wide(kdec), gs, bm


def _state_to_pairs(s):
    B = s.shape[0]
    s = s.reshape(B, N_PAIRS, 2, RET_DK, RET_DV)
    eye = jnp.eye(2, dtype=s.dtype)
    out = s[:, :, :, :, None, :] * eye[None, None, :, None, :, None]
    return out.reshape(B, N_PAIRS, LANES, LANES)


def _pairs_to_state(sp):
    B = sp.shape[0]
    s = sp.reshape(B, N_PAIRS, 2, RET_DK, 2, RET_DV)
    return jnp.stack([s[:, :, 0, :, 0, :], s[:, :, 1, :, 1, :]], axis=2).reshape(B, RET_HEADS, RET_DK, RET_DV)


def _uq_perm():
    per = MLA_NOPE + MLA_ROPE
    half = MLA_ROPE // 2
    nope = [h * per + d for h in range(MLA_HEADS) for d in range(MLA_NOPE)]
    x1 = [h * per + MLA_NOPE + f for h in range(MLA_HEADS) for f in range(half)]
    x2 = [h * per + MLA_NOPE + half + f for h in range(MLA_HEADS) for f in range(half)]
    return np.array(nope + x1 + x2, np.int32)


def _big_query_weight(w_uk):
    H, half = MLA_HEADS, MLA_ROPE // 2
    eye = jnp.eye(H, dtype=w_uk.dtype)
    wpad = jnp.pad(w_uk, ((0, 0), (0, 0), (0, QK_PAD - KV_LORA)))
    top = (eye[:, None, :, None] * wpad[:, :, None, :]).reshape(H * MLA_NOPE, H * QK_PAD)
    sel = np.zeros((2 * H * half, H * QK_PAD), np.float32)
    for h in range(H):
        for f in range(half):
            sel[h * half + f, h * QK_PAD + KV_LORA + f] = 1.0
            sel[H * half + h * half + f, h * QK_PAD + KV_LORA + half + f] = 1.0
    return jnp.concatenate([top, jnp.asarray(sel, w_uk.dtype)], axis=0)


def kernel(x_prompt, x_sample, p_prompt, p_sample, cache_ckv, cache_krope, state_ret, attn_norm, w_in, q_norm, w_uq, kv_norm, w_uk, w_uv, ret_norm, w_o, ffn_norm, w_gate_d, w_up_d, w_down_d, w_router, w_gate_e, w_up_e, w_down_e, ple_norm, w_ple_gate, w_ple_proj, final_norm):
    Bp, S, D = x_prompt.shape
    Bs, L, _ = x_sample.shape
    depth = w_in.shape[0]
    P = cache_ckv.shape[2]
    Tp, Ts = Bp * S, Bs * L
    T = Tp + Ts
    assert S % CHUNK == 0 and P % CHUNK == 0 and L == CHUNK and Tp % CHUNK == 0
    assert w_router.shape[-1] >= TOP_K

    tm_proj = _pick(int(np.gcd(S, Ts)), (512, 256, 128, 64))
    tm_gmm = 1024
    rb = _pick(S, (512, 256, 128, 64))
    tq = _pick(S, (256, 128, 64))
    kb = _pick(S, (512, 256, 128))
    kb_s = 512
    sk_s = P + L
    sk_pad = -(-sk_s // kb_s) * kb_s

    pos = jnp.concatenate([jnp.arange(S, dtype=jnp.int32),
                           P + jnp.tile(jnp.arange(L, dtype=jnp.int32), Bs)])
    rope_tabs = _rope_tables(pos)
    n_pt, pt_per_seq = Tp // tm_proj, S // tm_proj
    tab_tile = lambda i: jnp.where(i < n_pt, i % pt_per_seq, pt_per_seq + i - n_pt)
    ret_tabs_p = _ret_tables(CHUNK)
    ret_tabs_s = _ret_tables(L)
    perm = _uq_perm()
    row2 = lambda v: v.reshape(1, -1)

    h = jnp.concatenate([x_prompt.reshape(Tp, D), x_sample.reshape(Ts, D)], axis=0)
    outs = {k: [] for k in ("ckv_p", "kro_p", "ret_p", "ckv_s", "kro_s", "ret_s")}
    for l in range(depth):
        win = jnp.pad(w_in[l], ((0, 0), (0, IN_COLS_PAD - IN_COLS))).astype(BF16)
        wuq = w_uq[l][:, perm].astype(BF16)
        wbig = _big_query_weight(w_uk[l]).astype(BF16)
        qr, kr, vr, gr, qx, ckv, kro, kx = _proj(
            h, row2(attn_norm[l]), win, row2(q_norm[l]), wuq, row2(kv_norm[l]), wbig, rope_tabs,
            tab_tile, tm_proj)

        rn = row2(ret_norm[l])
        zero_state = jnp.zeros((Bp, N_PAIRS, LANES, LANES), F32)
        o_ret_p, st_p = _retention(qr, kr, vr, gr, zero_state, ret_tabs_p, rn, Bp, S, 0, rb)
        o_ret_s, st_s = _retention(qr, kr, vr, gr, _state_to_pairs(state_ret[l].astype(F32)),
                                   ret_tabs_s, rn, Bs, L, Tp, L)

        wuv = w_uv[l].astype(BF16)
        wuv_big = (jnp.eye(MLA_HEADS, dtype=BF16)[:, None, :, None] * wuv[:, :, None, :]
                   ).reshape(MLA_HEADS, KV_LORA, MLA_W)
        o_mla_p = _attention(qx, kx, wuv_big, Bp, S, 0, tq, kb, 0, S, S)
        cache_kx = jnp.concatenate(
            [cache_ckv[l], cache_krope[l], jnp.zeros((Bs, P, QK_PAD - KV_LORA - MLA_ROPE), F32)],
            axis=-1).astype(BF16)
        kx_s = jnp.concatenate([cache_kx, kx[Tp:].reshape(Bs, L, QK_PAD),
                                jnp.zeros((Bs, sk_pad - sk_s, QK_PAD), BF16)], axis=1)
        o_mla_s = _attention(qx, kx_s.reshape(Bs * sk_pad, QK_PAD), wuv_big, Bs, L, Tp, L, kb_s, P,
                             sk_pad, sk_s)

        h = _merge(h, o_ret_p, o_mla_p, o_ret_s, o_mla_s, w_o[l].astype(BF16), tm_proj)

        fn = row2(ffn_norm[l])
        j = l // 2
        if l % 2 == 0:
            h = _ffn(h, fn, w_gate_d[j].astype(BF16), w_up_d[j].astype(BF16),
                     w_down_d[j].astype(BF16), tm_proj)
        else:
            tf = _pick(w_gate_e.shape[-1], (512, 256, 128))
            h = _moe(h, fn, w_router[j], w_gate_e[j], w_up_e[j], w_down_e[j], tm_proj, tm_gmm, tf)

        ple_w = (row2(ple_norm[l]), w_ple_gate[l].astype(BF16))
        wp = w_ple_proj[l].astype(BF16)
        if l < depth - 1:
            p_l = jnp.concatenate([p_prompt[l].reshape(Tp, -1), p_sample[l].reshape(Ts, -1)], axis=0)
            h = _ple(h, *ple_w, p_l, wp, row2(final_norm), tm_proj, False)
        else:
            y_p = _ple(h, *ple_w, p_prompt[l].reshape(Tp, -1), wp, row2(final_norm), tm_proj, True, 0, Tp)
            y_s = _ple(h, *ple_w, p_sample[l].reshape(Ts, -1), wp, row2(final_norm), tm_proj, True, Tp, Ts)

        outs["ckv_p"].append(ckv[:Tp].reshape(Bp, S, KV_LORA))
        outs["kro_p"].append(kro[:Tp].reshape(Bp, S, MLA_ROPE))
        outs["ret_p"].append(_pairs_to_state(st_p))
        outs["ckv_s"].append(ckv[Tp:].reshape(Bs, L, KV_LORA))
        outs["kro_s"].append(kro[Tp:].reshape(Bs, L, MLA_ROPE))
        outs["ret_s"].append(_pairs_to_state(st_s))

    return (y_p.reshape(Bp, S, D), y_s.reshape(Bs, L, D),
            jnp.stack(outs["ckv_p"]), jnp.stack(outs["kro_p"]), jnp.stack(outs["ret_p"]),
            jnp.stack(outs["ckv_s"]), jnp.stack(outs["kro_s"]), jnp.stack(outs["ret_s"]))
```

```python
import functools

import numpy as np
import jax
import jax.numpy as jnp
from jax import lax
from jax.experimental import pallas as pl
from jax.experimental.pallas import tpu as pltpu

F32 = jnp.float32
BF16 = jnp.bfloat16

CHUNK = 64
CHUNK_SHIFT = 6
RMS_EPS = 1e-6
ROPE_THETA = 10000.0
RET_HEADS = 8
RET_DK = 64
RET_DV = 64
RET_W = RET_HEADS * RET_DK
MLA_HEADS = 8
MLA_NOPE = 64
MLA_ROPE = 32
MLA_V = 64
Q_LORA = 256
KV_LORA = 128
MLA_W = MLA_HEADS * MLA_V
TOP_K = 2
N_PAIRS = RET_HEADS // 2
LANES = 128
SUBLANES = 8
QK_PAD = 256
OFF_Q = RET_W
OFF_K = OFF_Q + RET_W
OFF_V = OFF_K + RET_W
OFF_G = OFF_V + RET_W
OFF_CQ = OFF_G + Q_LORA
OFF_CKV = OFF_CQ + KV_LORA
IN_COLS = OFF_CKV + MLA_ROPE
IN_COLS_PAD = OFF_CKV + LANES
NEG = -1e30
VMEM_LIMIT = 56 * 1024 * 1024


def _pick(n, cands):
    for c in cands:
        if n % c == 0:
            return c
    return n


def _cparams(sem, flags=None):
    return pltpu.CompilerParams(dimension_semantics=sem, vmem_limit_bytes=VMEM_LIMIT, flags=flags)


def _rms(x, g):
    return x * lax.rsqrt(jnp.mean(x * x, axis=-1, keepdims=True) + RMS_EPS) * g


def _dot(a, b):
    return jnp.dot(a, b, preferred_element_type=F32)


def _dot_nt(a, b):
    return lax.dot_general(a, b, (((1,), (1,)), ((), ())), preferred_element_type=F32)


def _dot_tn(a, b):
    return lax.dot_general(a, b, (((0,), (0,)), ((), ())), preferred_element_type=F32)


def _proj_kernel(h_ref, *refs):
    _proj_body(h_ref[...], *refs)


def _proj_streams_kernel(hp_ref, hs_ref, *refs, n_prompt_tiles):
    x = jnp.where(pl.program_id(0) < n_prompt_tiles, hp_ref[...], hs_ref[...])
    refs[-1][...] = x
    _proj_body(x, *refs[:-1])


def _proj_body(x, an_ref, win_ref, qn_ref, wuq_ref, kvn_ref, wbig_ref,
               c64_ref, sm64_ref, sp64_ref, c16_ref, s16_ref, ck_ref, smk_ref, spk_ref,
               qr_ref, kr_ref, vr_ref, gr_ref, qx_ref, ckv_ref, kro_ref, kx_ref):
    xn = _rms(x, an_ref[...]).astype(BF16)
    c64, sm64, sp64 = c64_ref[...], sm64_ref[...], sp64_ref[...]

    def rope64(z):
        return (z * c64 + pltpu.roll(z, LANES - 32, 1) * sm64 + pltpu.roll(z, 32, 1) * sp64)

    for lo in range(0, RET_W, QK_PAD):
        zq = _dot(xn, win_ref[:, lo:lo + QK_PAD])
        zk = _dot(xn, win_ref[:, OFF_Q + lo:OFF_Q + lo + QK_PAD])
        for half in range(0, QK_PAD, LANES):
            dst = slice(lo + half, lo + half + LANES)
            qr_ref[:, dst] = rope64(zq[:, half:half + LANES])
            kr_ref[:, dst] = rope64(zk[:, half:half + LANES]) * (RET_DK ** -0.5)

    zkv = _dot(xn, win_ref[:, OFF_CQ:IN_COLS_PAD])
    ckv = _rms(zkv[:, :KV_LORA], kvn_ref[...])
    ckv_ref[...] = ckv
    zk = zkv[:, KV_LORA:]
    kro = (zk * ck_ref[...] + pltpu.roll(zk, LANES - 16, 1) * smk_ref[...]
           + pltpu.roll(zk, 16, 1) * spk_ref[...])
    kro_ref[...] = kro[:, :MLA_ROPE]
    kx_ref[:, :KV_LORA] = ckv.astype(BF16)
    kx_ref[:, KV_LORA:] = kro.astype(BF16)

    cq = _dot(xn, win_ref[:, OFF_G:OFF_CQ])
    q = _dot(_rms(cq, qn_ref[...]).astype(BF16), wuq_ref[...])
    nq = MLA_HEADS * MLA_NOPE
    x1, x2 = q[:, nq:nq + LANES], q[:, nq + LANES:nq + 2 * LANES]
    c16, s16 = c16_ref[...], s16_ref[...]
    scale = (MLA_NOPE + MLA_ROPE) ** -0.5 * float(np.log2(np.e))
    qcat = jnp.concatenate([q[:, :nq], x1 * c16 - x2 * s16, x2 * c16 + x1 * s16], axis=-1)
    qcat = (qcat * scale).astype(BF16)
    half_w = (MLA_HEADS // 2) * QK_PAD
    for g in range(2):
        rows = slice(g * (nq // 2), (g + 1) * (nq // 2))
        cols = slice(g * half_w, (g + 1) * half_w)
        qx_ref[:, cols] = (_dot(qcat[:, rows], wbig_ref[rows, cols])
                           + _dot(qcat[:, nq:], wbig_ref[nq:, cols])).astype(BF16)

    vr_ref[...] = _dot(xn, win_ref[:, OFF_K:OFF_V]).astype(BF16)
    gr_ref[...] = _dot(xn, win_ref[:, OFF_V:OFF_G])


def _stream_specs(tm, width, n_prompt_tiles):
    return (pl.BlockSpec((tm, width), lambda i: (jnp.minimum(i, n_prompt_tiles - 1), 0)),
            pl.BlockSpec((tm, width), lambda i: (jnp.maximum(i - n_prompt_tiles, 0), 0)))


def _proj(h, an, win, qn, wuq, kvn, wbig, tabs, tab_tile, tm):
    streams = isinstance(h, tuple)
    T = sum(a.shape[0] for a in h) if streams else h.shape[0]
    D = h[0].shape[1] if streams else h.shape[1]
    row = lambda w: pl.BlockSpec((tm, w), lambda i: (i, 0))
    tab = pl.BlockSpec((tm, LANES), lambda i: (tab_tile(i), 0))
    full = lambda a: pl.BlockSpec(a.shape, lambda i: (0,) * a.ndim)
    out_shapes = (
        jax.ShapeDtypeStruct((T, RET_W), F32), jax.ShapeDtypeStruct((T, RET_W), F32),
        jax.ShapeDtypeStruct((T, RET_W), BF16), jax.ShapeDtypeStruct((T, RET_W), F32),
        jax.ShapeDtypeStruct((T, MLA_HEADS * QK_PAD), BF16),
        jax.ShapeDtypeStruct((T, KV_LORA), F32), jax.ShapeDtypeStruct((T, MLA_ROPE), F32),
        jax.ShapeDtypeStruct((T, QK_PAD), BF16))
    out_specs = (row(RET_W), row(RET_W), row(RET_W), row(RET_W), row(MLA_HEADS * QK_PAD),
                 row(KV_LORA), row(MLA_ROPE), row(QK_PAD))
    if streams:
        npt = h[0].shape[0] // tm
        kern = functools.partial(_proj_streams_kernel, n_prompt_tiles=npt)
        h_specs, h_args = list(_stream_specs(tm, D, npt)), h
        out_specs += (row(D),)
        out_shapes += (jax.ShapeDtypeStruct((T, D), F32),)
    else:
        kern, h_specs, h_args = _proj_kernel, [row(D)], (h,)
    return pl.pallas_call(
        kern, grid=(T // tm,),
        in_specs=h_specs + [full(an), full(win), full(qn), full(wuq), full(kvn), full(wbig)]
                 + [tab] * 8,
        out_specs=out_specs,
        out_shape=out_shapes, compiler_params=_cparams(("parallel",)), name="proj",
    )(*h_args, an, win, qn, wuq, kvn, wbig, *tabs)


def _ret_kernel(q_ref, k_ref, v_ref, g_ref, s0_ref, d_ref, qd_ref, kd_ref, gs_ref, bm_ref, rn_ref,
                o_ref, so_ref, st_ref, *, n_chunks):
    i = pl.program_id(1)

    @pl.when(i == 0)
    def _():
        st_ref[...] = s0_ref[0]

    lane = lax.broadcasted_iota(jnp.int32, (CHUNK, LANES), 1)
    first = lane < RET_DK

    def chunk(c, carry):
        rows = pl.ds(pl.multiple_of(c * CHUNK, CHUNK), CHUNK)
        for p in range(N_PAIRS):
            cols = slice(p * LANES, (p + 1) * LANES)
            qp, kp, vp = q_ref[rows, cols], k_ref[rows, cols], v_ref[rows, cols]
            q2 = jnp.concatenate([jnp.where(first, qp, 0.0), jnp.where(first, 0.0, qp)], axis=0)
            sd = (_dot_nt(q2.astype(BF16), kp.astype(BF16)) * d_ref[p]).astype(BF16)
            qq = (qp * qd_ref[p]).astype(BF16)
            st = st_ref[p]
            lhs = jnp.concatenate([jnp.concatenate([qq, qq], axis=0), sd], axis=1)
            rhs = jnp.concatenate([st.astype(BF16), vp], axis=0)
            o2 = _dot(lhs, rhs)
            o = jnp.where(first, o2[:CHUNK], o2[CHUNK:])
            upd = _dot_tn((kp * kd_ref[p]).astype(BF16), vp)
            st_ref[p] = st * gs_ref[p] + upd * bm_ref[...]
            oo = o * o
            ss_a = jnp.sum(jnp.where(first, oo, 0.0), axis=-1, keepdims=True)
            ss_b = jnp.sum(jnp.where(first, 0.0, oo), axis=-1, keepdims=True)
            rs = jnp.where(first, lax.rsqrt(ss_a * (1.0 / RET_DV) + RMS_EPS),
                           lax.rsqrt(ss_b * (1.0 / RET_DV) + RMS_EPS))
            g = g_ref[rows, cols]
            o_ref[rows, cols] = (o * rs * rn_ref[:, cols] * (g * jax.nn.sigmoid(g))).astype(BF16)
        return carry

    lax.fori_loop(0, n_chunks, chunk, 0, unroll=min(n_chunks, 4))

    @pl.when(i == pl.num_programs(1) - 1)
    def _():
        so_ref[0] = st_ref[...]


def _retention(qr, kr, vr, gr, s0, tabs, rn, nb, seq, row0, rb):
    nblk = seq // rb
    off = row0 // rb
    tok = pl.BlockSpec((rb, RET_W), lambda b, i: (off + b * nblk + i, 0))
    full = lambda a: pl.BlockSpec(a.shape, lambda b, i: (0,) * a.ndim)
    st_spec = pl.BlockSpec((1, N_PAIRS, LANES, LANES), lambda b, i: (b, 0, 0, 0))
    d, qd, kd, gs, bm = tabs
    return pl.pallas_call(
        functools.partial(_ret_kernel, n_chunks=rb // CHUNK), grid=(nb, nblk),
        in_specs=[tok, tok, tok, tok, st_spec, full(d), full(qd), full(kd), full(gs), full(bm), full(rn)],
        out_specs=(pl.BlockSpec((rb, RET_W), lambda b, i: (b * nblk + i, 0)), st_spec),
        out_shape=(jax.ShapeDtypeStruct((nb * seq, RET_W), BF16),
                   jax.ShapeDtypeStruct((nb, N_PAIRS, LANES, LANES), F32)),
        scratch_shapes=[pltpu.VMEM((N_PAIRS, LANES, LANES), F32)],
        compiler_params=_cparams(("parallel", "arbitrary")), name="retention",
    )(qr, kr, vr, gr, s0, d, qd, kd, gs, bm, rn)


def _attn_kernel(q_ref, k_ref, wuv_ref, o_ref, q_s, s0_s, s1_s, m_s, l_s, acc_s,
                 *, tq, kb, q_pos0, sk_valid):
    i = pl.program_id(1)
    R = MLA_HEADS * tq
    for h in range(MLA_HEADS):
        q_s[h * tq:(h + 1) * tq, :] = q_ref[:, h * QK_PAD:(h + 1) * QK_PAD]
    m_s[...] = jnp.full(m_s.shape, NEG, F32)
    l_s[...] = jnp.zeros(l_s.shape, F32)
    acc_s[...] = jnp.zeros(acc_s.shape, F32)

    chunk_end = lambda t: ((t >> CHUNK_SHIFT) + 1) << CHUNK_SHIFT
    kb_shift = kb.bit_length() - 1
    qstart = q_pos0 + i * tq
    lim_first = jnp.minimum(chunk_end(qstart), sk_valid)
    lim_last = jnp.minimum(chunk_end(qstart + tq - 1), sk_valid)
    n_full = lim_first >> kb_shift
    n_blk = (lim_last + kb - 1) >> kb_shift

    last = n_blk - 1

    def key_block(j):
        return k_ref[pl.ds(pl.multiple_of(j * kb, kb), kb), :]

    def scores(j, s_ref):
        s_ref[...] = _dot_nt(q_s[...], key_block(j))

    def update(j, s_ref, masked):
        s = s_ref[...]
        if masked:
            tok = lax.broadcasted_iota(jnp.int32, (R, 1), 0) & (tq - 1)
            row_lim = jnp.minimum(chunk_end(qstart + tok), sk_valid)
            kidx = j * kb + lax.broadcasted_iota(jnp.int32, (1, kb), 1)
            s = jnp.where(kidx < row_lim, s, NEG)
        m_prev = m_s[...]
        m_next = jnp.maximum(m_prev, jnp.max(s, axis=-1, keepdims=True))
        p = jnp.exp2(s - jnp.tile(m_next, (1, kb // LANES)))
        alpha = jnp.exp2(m_prev - m_next)
        p_lanes = p[:, :LANES]
        for c in range(LANES, kb, LANES):
            p_lanes = p_lanes + p[:, c:c + LANES]
        l_s[...] = alpha * l_s[...] + p_lanes
        acc_s[...] = alpha * acc_s[...] + _dot(p.astype(BF16), key_block(j)[:, :KV_LORA])
        m_s[...] = m_next

    n_pipe = jnp.minimum(n_full, last)
    odd = n_pipe & 1

    @pl.when(odd == 1)
    def _():
        scores(0, s0_s)
        update(0, s0_s, False)

    scores(odd, s0_s)

    def pair(jj, c):
        j = odd + 2 * jj
        scores(j + 1, s1_s)
        update(j, s0_s, False)
        scores(j + 2, s0_s)
        update(j + 1, s1_s, False)
        return c

    lax.fori_loop(0, n_pipe >> 1, pair, 0)
    update(n_pipe, s0_s, True)

    def tail(j, c):
        scores(j, s0_s)
        update(j, s0_s, True)
        return c

    lax.fori_loop(n_pipe + 1, n_blk, tail, 0)

    o_lat = (acc_s[...] / jnp.sum(l_s[...], axis=-1, keepdims=True)).astype(BF16)
    out = None
    for p in range(MLA_HEADS // 2):
        pair = jnp.concatenate([o_lat[(2 * p) * tq:(2 * p + 1) * tq],
                                o_lat[(2 * p + 1) * tq:(2 * p + 2) * tq]], axis=1)
        term = _dot(pair, wuv_ref[p])
        out = term if out is None else out + term
    o_ref[...] = out.astype(BF16)


def _attention(qx, kx, wuv, nb, seq_q, row0, tq, kb, q_pos0, sk, sk_valid):
    nq = seq_q // tq
    off = row0 // tq
    R = MLA_HEADS * tq
    return pl.pallas_call(
        functools.partial(_attn_kernel, tq=tq, kb=kb, q_pos0=q_pos0, sk_valid=sk_valid),
        grid=(nb, nq),
        in_specs=[pl.BlockSpec((tq, MLA_HEADS * QK_PAD), lambda b, i: (off + b * nq + i, 0)),
                  pl.BlockSpec((sk, QK_PAD), lambda b, i: (b, 0)),
                  pl.BlockSpec(wuv.shape, lambda b, i: (0, 0, 0))],
        out_specs=pl.BlockSpec((tq, MLA_W), lambda b, i: (b * nq + i, 0)),
        out_shape=jax.ShapeDtypeStruct((nb * seq_q, MLA_W), BF16),
        scratch_shapes=[pltpu.VMEM((R, QK_PAD), BF16), pltpu.VMEM((R, kb), F32),
                        pltpu.VMEM((R, kb), F32), pltpu.VMEM((R, LANES), F32),
                        pltpu.VMEM((R, LANES), F32), pltpu.VMEM((R, KV_LORA), F32)],
        compiler_params=_cparams(("parallel", "arbitrary")),
        name="attention",
    )(qx, kx, wuv)


def _merge_kernel(h_ref, ap_ref, bp_ref, as_ref, bs_ref, wo_ref, o_ref, *, n_prompt_tiles):
    def project(a_ref, b_ref):
        o_ref[...] = (h_ref[...] + _dot(a_ref[...], wo_ref[:RET_W, :])
                      + _dot(b_ref[...], wo_ref[RET_W:, :]))

    @pl.when(pl.program_id(0) < n_prompt_tiles)
    def _():
        project(ap_ref, bp_ref)

    @pl.when(pl.program_id(0) >= n_prompt_tiles)
    def _():
        project(as_ref, bs_ref)


def _merge(h, a_p, b_p, a_s, b_s, wo, tm):
    T, D = h.shape
    npt = a_p.shape[0] // tm
    row = lambda w: pl.BlockSpec((tm, w), lambda i: (i, 0))
    (a_prm, a_smp), (b_prm, b_smp) = _stream_specs(tm, RET_W, npt), _stream_specs(tm, MLA_W, npt)
    return pl.pallas_call(
        functools.partial(_merge_kernel, n_prompt_tiles=npt), grid=(T // tm,),
        in_specs=[row(D), a_prm, b_prm, a_smp, b_smp, pl.BlockSpec(wo.shape, lambda i: (0, 0))],
        out_specs=row(D), out_shape=jax.ShapeDtypeStruct((T, D), F32),
        compiler_params=_cparams(("parallel",)), name="merge",
    )(h, a_p, b_p, a_s, b_s, wo)


def _ffn_kernel(h_ref, fn_ref, wg_ref, wu_ref, wd_ref, o_ref):
    hh = h_ref[...]
    u = _rms(hh, fn_ref[...]).astype(BF16)
    a = _dot(u, wg_ref[...])
    mid = (a * jax.nn.sigmoid(a) * _dot(u, wu_ref[...])).astype(BF16)
    o_ref[...] = hh + _dot(mid, wd_ref[...])


def _ffn(h, fn, wg, wu, wd, tm):
    T, D = h.shape
    resident = lambda a: pl.BlockSpec(a.shape, lambda i: (0, 0), pipeline_mode=pl.Buffered(1))
    return pl.pallas_call(
        _ffn_kernel, grid=(T // tm,),
        in_specs=[pl.BlockSpec((tm, D), lambda i: (i, 0)), pl.BlockSpec(fn.shape, lambda i: (0, 0)),
                  resident(wg), resident(wu), resident(wd)],
        out_specs=pl.BlockSpec((tm, D), lambda i: (i, 0)),
        out_shape=jax.ShapeDtypeStruct((T, D), F32),
        compiler_params=_cparams(("parallel",)), name="ffn",
    )(h, fn, wg, wu, wd)


L_E1, L_E2, L_W1, L_W2, L_R1, L_R2 = range(6)


def _router_kernel(h_ref, fn_ref, wr_ref, meta_ref, cnt_ref, run_s, *, n_experts):
    tm = h_ref.shape[0]

    @pl.when(pl.program_id(0) == 0)
    def _():
        run_s[...] = jnp.zeros(run_s.shape, F32)

    u = _rms(h_ref[...], fn_ref[...])
    logits = jnp.dot(u, wr_ref[...], precision=lax.Precision.HIGHEST, preferred_element_type=F32)
    lane = lax.broadcasted_iota(jnp.int32, logits.shape, 1).astype(F32)
    lg = jnp.where(lane < n_experts, logits, NEG)
    m1 = jnp.max(lg, axis=-1, keepdims=True)
    i1 = jnp.min(jnp.where(lg == m1, lane, float(LANES)), axis=-1, keepdims=True)
    lg2 = jnp.where(lane == i1, NEG, lg)
    m2 = jnp.max(lg2, axis=-1, keepdims=True)
    i2 = jnp.min(jnp.where(lg2 == m2, lane, float(LANES)), axis=-1, keepdims=True)
    e2 = jnp.exp(m2 - m1)
    den = 1.0 + e2

    hit = jnp.where((lane == i1) | (lane == i2), 1.0, 0.0)
    r_i = lax.broadcasted_iota(jnp.int32, (tm, tm), 0)
    c_i = lax.broadcasted_iota(jnp.int32, (tm, tm), 1)
    before = jnp.where(c_i < r_i, 1.0, 0.0).astype(BF16)
    prefix = _dot(before, hit.astype(BF16)) + run_s[0:1, :]
    r1 = jnp.sum(jnp.where(lane == i1, prefix, 0.0), axis=-1, keepdims=True)
    r2 = jnp.sum(jnp.where(lane == i2, prefix, 0.0), axis=-1, keepdims=True)
    run_s[...] = run_s[...] + jnp.sum(hit, axis=0, keepdims=True)
    cnt_ref[...] = run_s[...]

    meta = jnp.where(lane == L_E1, i1, 0.0)
    for ln, val in ((L_E2, i2), (L_W1, 1.0 / den), (L_W2, e2 / den), (L_R1, r1), (L_R2, r2)):
        meta = jnp.where(lane == ln, val, meta)
    meta_ref[...] = meta


def _router(h, fn, wr, n_experts, tm):
    T, D = h.shape
    return pl.pallas_call(
        functools.partial(_router_kernel, n_experts=n_experts), grid=(T // tm,),
        in_specs=[pl.BlockSpec((tm, D), lambda i: (i, 0)), pl.BlockSpec(fn.shape, lambda i: (0, 0)),
                  pl.BlockSpec(wr.shape, lambda i: (0, 0))],
        out_specs=(pl.BlockSpec((tm, LANES), lambda i: (i, 0)),
                   pl.BlockSpec((SUBLANES, LANES), lambda i: (0, 0))),
        out_shape=(jax.ShapeDtypeStruct((T, LANES), F32), jax.ShapeDtypeStruct((SUBLANES, LANES), F32)),
        scratch_shapes=[pltpu.VMEM((SUBLANES, LANES), F32)],
        compiler_params=_cparams(("arbitrary",)), name="router",
    )(h, fn, wr)


def _tile_index_copy(d_hbm, idx_s, sem, tile, slot):
    n = d_hbm.shape[1]
    return pltpu.make_async_copy(d_hbm.at[tile], idx_s.at[pl.ds(pl.multiple_of(slot * n, n), n)],
                                 sem.at[slot])


def _dispatch_kernel(fill_ref, d_hbm, h_ref, xs_out, idx_s, zero_s, isem, rsem, fsem):
    tm = h_ref.shape[0]
    tmg = zero_s.shape[0]
    i, n = pl.program_id(0), pl.num_programs(0)
    slot = i % 2

    def fill_copy(k):
        start = pl.multiple_of(fill_ref[k], tmg)
        return pltpu.make_async_copy(zero_s, xs_out.at[pl.ds(start, tmg)], fsem)

    @pl.when(i == 0)
    def _():
        _tile_index_copy(d_hbm, idx_s, isem, 0, 0).start()
        zero_s[...] = jnp.zeros(zero_s.shape, F32)
        for k in range(fill_ref.shape[0]):
            pl.when(fill_ref[k] >= 0)(lambda k=k: fill_copy(k).start(priority=k % 2))
        for k in range(fill_ref.shape[0]):
            pl.when(fill_ref[k] >= 0)(lambda k=k: fill_copy(k).wait())

    @pl.when(i + 1 < n)
    def _():
        _tile_index_copy(d_hbm, idx_s, isem, i + 1, 1 - slot).start()

    _tile_index_copy(d_hbm, idx_s, isem, i, slot).wait()

    def row_copy(r, dst):
        return pltpu.make_async_copy(h_ref.at[pl.ds(r, 1)], xs_out.at[pl.ds(dst, 1)], rsem)

    def body(g, c):
        r0 = pl.multiple_of(g * SUBLANES, SUBLANES)
        for u in range(SUBLANES):
            for k in range(TOP_K):
                row_copy(r0 + u, idx_s[slot * (TOP_K * tm) + k * tm + r0 + u]).start(priority=k % 2)
        return c

    lax.fori_loop(0, tm // SUBLANES, body, 0)
    for k in range(TOP_K):
        pltpu.make_async_copy(h_ref, xs_out.at[pl.ds(0, tm)], rsem).wait()


def _dispatch(fill_starts, d_tiles, h, n_rows, tm, tmg):
    T, D = h.shape
    grid_spec = pltpu.PrefetchScalarGridSpec(
        num_scalar_prefetch=1, grid=(T // tm,),
        in_specs=[pl.BlockSpec(memory_space=pl.ANY), pl.BlockSpec((tm, D), lambda i, fs: (i, 0))],
        out_specs=pl.BlockSpec(memory_space=pl.ANY),
        scratch_shapes=[pltpu.SMEM((2 * TOP_K * tm,), jnp.int32), pltpu.VMEM((tmg, D), F32),
                        pltpu.SemaphoreType.DMA((2,)), pltpu.SemaphoreType.DMA(()),
                        pltpu.SemaphoreType.DMA(())])
    return pl.pallas_call(
        _dispatch_kernel, grid_spec=grid_spec, out_shape=jax.ShapeDtypeStruct((n_rows, D), F32),
        compiler_params=_cparams(("arbitrary",)), name="dispatch",
    )(fill_starts, d_tiles, h)


def _gmm_kernel(te_ref, nu_ref, x_ref, fn_ref, wg_ref, wu_ref, wd_ref, o_ref, u_s):
    i, f = pl.program_id(0), pl.program_id(1)
    used = i < nu_ref[0]

    @pl.when(f == 0)
    def _():
        o_ref[...] = jnp.zeros(o_ref.shape, F32)

    @pl.when(used & (f == 0))
    def _():
        u_s[...] = _rms(x_ref[...], fn_ref[...]).astype(BF16)

    @pl.when(used)
    def _():
        u = u_s[...]
        a = _dot(u, wg_ref[0].astype(BF16))
        mid = (a * jax.nn.sigmoid(a) * _dot(u, wu_ref[0].astype(BF16))).astype(BF16)
        o_ref[...] += _dot(mid, wd_ref[0].astype(BF16))


def _gmm(tile_expert, n_used, xs, fn, wg, wu, wd, tmg, tf):
    R, D = xs.shape
    F = wg.shape[-1]
    nf = F // tf
    f_eff = lambda i, f, nu: jnp.where(i < nu[0], f, nf - 1)
    grid_spec = pltpu.PrefetchScalarGridSpec(
        num_scalar_prefetch=2, grid=(R // tmg, nf),
        in_specs=[pl.BlockSpec((tmg, D), lambda i, f, te, nu: (i, 0)),
                  pl.BlockSpec(fn.shape, lambda i, f, te, nu: (0, 0)),
                  pl.BlockSpec((1, D, tf), lambda i, f, te, nu: (te[i], 0, f_eff(i, f, nu))),
                  pl.BlockSpec((1, D, tf), lambda i, f, te, nu: (te[i], 0, f_eff(i, f, nu))),
                  pl.BlockSpec((1, tf, D), lambda i, f, te, nu: (te[i], f_eff(i, f, nu), 0))],
        out_specs=pl.BlockSpec((tmg, D), lambda i, f, te, nu: (i, 0)),
        scratch_shapes=[pltpu.VMEM((tmg, D), BF16)])
    return pl.pallas_call(
        _gmm_kernel, grid_spec=grid_spec, out_shape=jax.ShapeDtypeStruct((R, D), F32),
        compiler_params=_cparams(("arbitrary", "arbitrary")), name="experts",
    )(tile_expert, n_used, xs, fn, wg, wu, wd)


def _combine_kernel(d_hbm, h_ref, meta_ref, ys_hbm, o_ref, idx_s, ya_s, yb_s, isem, rsem):
    tm = h_ref.shape[0]
    i, n = pl.program_id(0), pl.num_programs(0)
    slot = i % 2

    def fetch(tile, s):
        cp = _tile_index_copy(d_hbm, idx_s, isem, tile, s)
        cp.start()
        cp.wait()

        def body(g, c):
            r0 = pl.multiple_of(g * SUBLANES, SUBLANES)
            base = s * (TOP_K * tm) + r0
            for u in range(SUBLANES):
                pltpu.make_async_copy(ys_hbm.at[pl.ds(idx_s[base + u], 1)],
                                      ya_s.at[s, pl.ds(r0 + u, 1)], rsem.at[s]).start(priority=0)
                pltpu.make_async_copy(ys_hbm.at[pl.ds(idx_s[base + tm + u], 1)],
                                      yb_s.at[s, pl.ds(r0 + u, 1)], rsem.at[s]).start(priority=1)
            return c

        lax.fori_loop(0, tm // SUBLANES, body, 0)

    @pl.when(i == 0)
    def _():
        fetch(0, 0)

    @pl.when(i + 1 < n)
    def _():
        fetch(i + 1, 1 - slot)

    pltpu.make_async_copy(ys_hbm.at[pl.ds(0, tm)], ya_s.at[slot], rsem.at[slot]).wait()
    pltpu.make_async_copy(ys_hbm.at[pl.ds(0, tm)], yb_s.at[slot], rsem.at[slot]).wait()
    meta = meta_ref[...]
    o_ref[...] = (h_ref[...] + meta[:, L_W1:L_W1 + 1] * ya_s[slot]
                  + meta[:, L_W2:L_W2 + 1] * yb_s[slot])


def _combine(d_tiles, h, meta, ys, tm):
    T, D = h.shape
    return pl.pallas_call(
        _combine_kernel, grid=(T // tm,),
        in_specs=[pl.BlockSpec(memory_space=pl.ANY), pl.BlockSpec((tm, D), lambda i: (i, 0)),
                  pl.BlockSpec((tm, LANES), lambda i: (i, 0)), pl.BlockSpec(memory_space=pl.ANY)],
        out_specs=pl.BlockSpec((tm, D), lambda i: (i, 0)),
        out_shape=jax.ShapeDtypeStruct((T, D), F32),
        scratch_shapes=[pltpu.SMEM((2 * TOP_K * tm,), jnp.int32), pltpu.VMEM((2, tm, D), F32),
                        pltpu.VMEM((2, tm, D), F32), pltpu.SemaphoreType.DMA((2,)),
                        pltpu.SemaphoreType.DMA((2,))],
        compiler_params=_cparams(("arbitrary",)), name="combine",
    )(d_tiles, h, meta, ys)


def _moe(h, fn, w_router, wg, wu, wd, tm, tmg, tf):
    T, D = h.shape
    E = w_router.shape[-1]
    meta, counts = _router(h, fn, jnp.pad(w_router, ((0, 0), (0, LANES - E))), E, tm)
    cnt = counts[0, :E].astype(jnp.int32)
    padded = ((cnt + tmg - 1) // tmg) * tmg
    ends = jnp.cumsum(padded)
    off = ends - padded
    n_tiles = -(-TOP_K * T // tmg) + E
    tile_expert = jnp.minimum(
        jnp.sum(ends[None, :] <= (jnp.arange(n_tiles, dtype=jnp.int32) * tmg)[:, None], axis=1), E - 1
    ).astype(jnp.int32)
    n_used = (ends[-1] // tmg).astype(jnp.int32).reshape(1)
    e12 = meta[:, L_E1:L_E2 + 1].astype(jnp.int32)
    dest = off[e12] + meta[:, L_R1:L_R2 + 1].astype(jnp.int32)
    d_tiles = dest.reshape(T // tm, tm, TOP_K).transpose(0, 2, 1).reshape(T // tm, TOP_K * tm)

    trailing = ends[-1] + jnp.arange(E + 1, dtype=jnp.int32) * tmg
    fill_starts = jnp.concatenate([
        jnp.where(padded > 0, ends - tmg, -1),
        jnp.where(trailing < n_tiles * tmg, trailing, -1)]).astype(jnp.int32)
    xs = _dispatch(fill_starts, d_tiles, h, n_tiles * tmg, tm, tmg)
    ys = _gmm(tile_expert, n_used, xs, fn, wg, wu, wd, tmg, tf)
    return _combine(d_tiles, h, meta, ys, tm)


def _ple_kernel(h_ref, pn_ref, wg_ref, wp_ref, fin_ref, *refs, final, n_prompt_tiles):
    *p_refs, o_ref = refs
    if len(p_refs) == 2:
        p = jnp.where(pl.program_id(0) < n_prompt_tiles, p_refs[0][...], p_refs[1][...])
    else:
        p = p_refs[0][...]
    hh = h_ref[...]
    gate = jax.nn.sigmoid(_dot(_rms(hh, pn_ref[...]).astype(BF16), wg_ref[...]))
    out = hh + gate * _dot(p.astype(BF16), wp_ref[...])
    if final:
        out = _rms(out, fin_ref[...])
    o_ref[...] = out


def _ple(h, pn, wg, p, wp, fin, tm, final, row0=0, rows=None):
    D = h.shape[1]
    rows = h.shape[0] if rows is None else rows
    off = row0 // tm
    full = lambda a: pl.BlockSpec(a.shape, lambda i: (0,) * a.ndim)
    if isinstance(p, tuple):
        npt = p[0].shape[0] // tm
        p_specs = list(_stream_specs(tm, p[0].shape[1], npt))
    else:
        npt, p_specs, p = 0, [pl.BlockSpec((tm, p.shape[1]), lambda i: (i, 0))], (p,)
    return pl.pallas_call(
        functools.partial(_ple_kernel, final=final, n_prompt_tiles=npt), grid=(rows // tm,),
        in_specs=[pl.BlockSpec((tm, D), lambda i: (off + i, 0)), full(pn), full(wg), full(wp),
                  full(fin)] + p_specs,
        out_specs=pl.BlockSpec((tm, D), lambda i: (i, 0)),
        out_shape=jax.ShapeDtypeStruct((rows, D), F32),
        compiler_params=_cparams(("parallel",)), name="ple",
    )(h, pn, wg, wp, fin, *p)


def _rope_tables(pos):
    pos = pos.astype(F32)[:, None]
    lane = np.arange(LANES)
    inv32 = ROPE_THETA ** (-jnp.arange(32, dtype=F32) / 32)
    inv16 = ROPE_THETA ** (-jnp.arange(16, dtype=F32) / 16)
    a64 = (pos * inv32[None, :])[:, lane % 32]
    a16 = (pos * inv16[None, :])[:, lane % 16]
    lo64 = jnp.asarray((lane % 64) < 32)
    c64, s64 = jnp.cos(a64), jnp.sin(a64)
    c16, s16 = jnp.cos(a16), jnp.sin(a16)
    in_k = jnp.asarray(lane < MLA_ROPE)
    lo_k = jnp.asarray(lane < 16)
    hi_k = jnp.asarray((lane >= 16) & (lane < MLA_ROPE))
    return (c64, jnp.where(lo64, -s64, 0.0), jnp.where(lo64, 0.0, s64), c16, s16,
            jnp.where(in_k, c16, 0.0), jnp.where(lo_k, -s16, 0.0), jnp.where(hi_k, s16, 0.0))


def _ret_tables(L):
    log_g = jnp.log(1.0 - jnp.exp2(-5.0 - jnp.arange(RET_HEADS, dtype=F32)))
    idx = jnp.arange(L, dtype=F32)
    dist = jnp.abs(idx[:, None] - idx[None, :])
    d = jnp.exp(dist[None] * log_g[:, None, None])
    qdec = jnp.exp((idx[:, None] + 1.0) * log_g[None, :])
    kdec = jnp.exp((L - 1.0 - idx)[:, None] * log_g[None, :])
    sdec = jnp.exp(L * log_g)
    wide = lambda t: jnp.repeat(t, RET_DK, axis=1).reshape(L, N_PAIRS, LANES).transpose(1, 0, 2)
    bm = jnp.asarray(np.kron(np.eye(2, dtype=np.float32), np.ones((RET_DK, RET_DV), np.float32)))
    gs = jnp.repeat(sdec, RET_DK).reshape(N_PAIRS, LANES, 1) * bm[None]
    return d.reshape(N_PAIRS, 2 * L, L), wide(qdec), wide(kdec), gs, bm


def _state_to_pairs(s):
    B = s.shape[0]
    s = s.reshape(B, N_PAIRS, 2, RET_DK, RET_DV)
    eye = jnp.eye(2, dtype=s.dtype)
    out = s[:, :, :, :, None, :] * eye[None, None, :, None, :, None]
    return out.reshape(B, N_PAIRS, LANES, LANES)


def _pairs_to_state(sp):
    B = sp.shape[0]
    s = sp.reshape(B, N_PAIRS, 2, RET_DK, 2, RET_DV)
    return jnp.stack([s[:, :, 0, :, 0, :], s[:, :, 1, :, 1, :]], axis=2).reshape(B, RET_HEADS, RET_DK, RET_DV)


def _uq_perm():
    per = MLA_NOPE + MLA_ROPE
    half = MLA_ROPE // 2
    nope = [h * per + d for h in range(MLA_HEADS) for d in range(MLA_NOPE)]
    x1 = [h * per + MLA_NOPE + f for h in range(MLA_HEADS) for f in range(half)]
    x2 = [h * per + MLA_NOPE + half + f for h in range(MLA_HEADS) for f in range(half)]
    return np.array(nope + x1 + x2, np.int32)


def _big_query_weight(w_uk):
    H, half = MLA_HEADS, MLA_ROPE // 2
    eye = jnp.eye(H, dtype=w_uk.dtype)
    wpad = jnp.pad(w_uk, ((0, 0), (0, 0), (0, QK_PAD - KV_LORA)))
    top = (eye[:, None, :, None] * wpad[:, :, None, :]).reshape(H * MLA_NOPE, H * QK_PAD)
    sel = np.zeros((2 * H * half, H * QK_PAD), np.float32)
    for h in range(H):
        for f in range(half):
            sel[h * half + f, h * QK_PAD + KV_LORA + f] = 1.0
            sel[H * half + h * half + f, h * QK_PAD + KV_LORA + half + f] = 1.0
    return jnp.concatenate([top, jnp.asarray(sel, w_uk.dtype)], axis=0)


def kernel(x_prompt, x_sample, p_prompt, p_sample, cache_ckv, cache_krope, state_ret, attn_norm, w_in, q_norm, w_uq, kv_norm, w_uk, w_uv, ret_norm, w_o, ffn_norm, w_gate_d, w_up_d, w_down_d, w_router, w_gate_e, w_up_e, w_down_e, ple_norm, w_ple_gate, w_ple_proj, final_norm):
    Bp, S, D = x_prompt.shape
    Bs, L, _ = x_sample.shape
    depth = w_in.shape[0]
    P = cache_ckv.shape[2]
    Tp, Ts = Bp * S, Bs * L
    T = Tp + Ts
    assert S % CHUNK == 0 and P % CHUNK == 0 and L == CHUNK and Tp % CHUNK == 0
    assert w_router.shape[-1] >= TOP_K

    tm_proj = _pick(int(np.gcd(S, Ts)), (512, 256, 128, 64))
    tm_gmm = 1024
    rb = _pick(S, (512, 256, 128, 64))
    tq = _pick(S, (256, 128, 64))
    kb = _pick(S, (512, 256, 128))
    kb_s = 512
    sk_s = P + L
    sk_pad = -(-sk_s // kb_s) * kb_s

    pos = jnp.concatenate([jnp.arange(S, dtype=jnp.int32),
                           P + jnp.tile(jnp.arange(L, dtype=jnp.int32), Bs)])
    rope_tabs = _rope_tables(pos)
    n_pt, pt_per_seq = Tp // tm_proj, S // tm_proj
    tab_tile = lambda i: jnp.where(i < n_pt, i % pt_per_seq, pt_per_seq + i - n_pt)
    ret_tabs_p = _ret_tables(CHUNK)
    ret_tabs_s = _ret_tables(L)
    perm = _uq_perm()
    row2 = lambda v: v.reshape(1, -1)

    h = (x_prompt.reshape(Tp, D), x_sample.reshape(Ts, D))
    outs = {k: [] for k in ("ckv_p", "kro_p", "ret_p", "ckv_s", "kro_s", "ret_s")}
    for l in range(depth):
        win = jnp.pad(w_in[l], ((0, 0), (0, IN_COLS_PAD - IN_COLS))).astype(BF16)
        wuq = w_uq[l][:, perm].astype(BF16)
        wbig = _big_query_weight(w_uk[l]).astype(BF16)
        qr, kr, vr, gr, qx, ckv, kro, kx, *h_cat = _proj(
            h, row2(attn_norm[l]), win, row2(q_norm[l]), wuq, row2(kv_norm[l]), wbig, rope_tabs,
            tab_tile, tm_proj)
        if h_cat:
            h, = h_cat

        rn = row2(ret_norm[l])
        zero_state = jnp.zeros((Bp, N_PAIRS, LANES, LANES), F32)
        o_ret_p, st_p = _retention(qr, kr, vr, gr, zero_state, ret_tabs_p, rn, Bp, S, 0, rb)
        o_ret_s, st_s = _retention(qr, kr, vr, gr, _state_to_pairs(state_ret[l].astype(F32)),
                                   ret_tabs_s, rn, Bs, L, Tp, L)

        wuv = w_uv[l].astype(BF16)
        wuv_big = (jnp.eye(MLA_HEADS, dtype=BF16)[:, None, :, None] * wuv[:, :, None, :]
                   ).reshape(MLA_HEADS // 2, 2 * KV_LORA, MLA_W)
        o_mla_p = _attention(qx, kx, wuv_big, Bp, S, 0, tq, kb, 0, S, S)
        cache_kx = jnp.concatenate(
            [cache_ckv[l], cache_krope[l], jnp.zeros((Bs, P, QK_PAD - KV_LORA - MLA_ROPE), F32)],
            axis=-1).astype(BF16)
        kx_s = jnp.concatenate([cache_kx, kx[Tp:].reshape(Bs, L, QK_PAD),
                                jnp.zeros((Bs, sk_pad - sk_s, QK_PAD), BF16)], axis=1)
        o_mla_s = _attention(qx, kx_s.reshape(Bs * sk_pad, QK_PAD), wuv_big, Bs, L, Tp, L, kb_s, P,
                             sk_pad, sk_s)

        h = _merge(h, o_ret_p, o_mla_p, o_ret_s, o_mla_s, w_o[l].astype(BF16), tm_proj)

        fn = row2(ffn_norm[l])
        j = l // 2
        if l % 2 == 0:
            h = _ffn(h, fn, w_gate_d[j].astype(BF16), w_up_d[j].astype(BF16),
                     w_down_d[j].astype(BF16), tm_proj)
        else:
            tf = _pick(w_gate_e.shape[-1], (512, 256, 128))
            h = _moe(h, fn, w_router[j], w_gate_e[j], w_up_e[j], w_down_e[j], tm_proj, tm_gmm, tf)

        ple_w = (row2(ple_norm[l]), w_ple_gate[l].astype(BF16))
        wp = w_ple_proj[l].astype(BF16)
        if l < depth - 1:
            p_l = (p_prompt[l].reshape(Tp, -1), p_sample[l].reshape(Ts, -1))
            h = _ple(h, *ple_w, p_l, wp, row2(final_norm), tm_proj, False)
        else:
            y_p = _ple(h, *ple_w, p_prompt[l].reshape(Tp, -1), wp, row2(final_norm), tm_proj, True, 0, Tp)
            y_s = _ple(h, *ple_w, p_sample[l].reshape(Ts, -1), wp, row2(final_norm), tm_proj, True, Tp, Ts)

        outs["ckv_p"].append(ckv[:Tp].reshape(Bp, S, KV_LORA))
        outs["kro_p"].append(kro[:Tp].reshape(Bp, S, MLA_ROPE))
        outs["ret_p"].append(_pairs_to_state(st_p))
        outs["ckv_s"].append(ckv[Tp:].reshape(Bs, L, KV_LORA))
        outs["kro_s"].append(kro[Tp:].reshape(Bs, L, MLA_ROPE))
        outs["ret_s"].append(_pairs_to_state(st_s))

    return (y_p.reshape(Bp, S, D), y_s.reshape(Bs, L, D),
            jnp.stack(outs["ckv_p"]), jnp.stack(outs["kro_p"]), jnp.stack(outs["ret_p"]),
            jnp.stack(outs["ckv_s"]), jnp.stack(outs["kro_s"]), jnp.stack(outs["ret_s"]))
```

```python
import functools

import numpy as np
import jax
import jax.numpy as jnp
from jax import lax
from jax.experimental import pallas as pl
from jax.experimental.pallas import tpu as pltpu

F32 = jnp.float32
BF16 = jnp.bfloat16

CHUNK = 64
CHUNK_SHIFT = 6
RMS_EPS = 1e-6
ROPE_THETA = 10000.0
RET_HEADS = 8
RET_DK = 64
RET_DV = 64
RET_W = RET_HEADS * RET_DK
MLA_HEADS = 8
MLA_NOPE = 64
MLA_ROPE = 32
MLA_V = 64
Q_LORA = 256
KV_LORA = 128
MLA_W = MLA_HEADS * MLA_V
TOP_K = 2
N_PAIRS = RET_HEADS // 2
LANES = 128
SUBLANES = 8
QK_PAD = 256
OFF_Q = RET_W
OFF_K = OFF_Q + RET_W
OFF_V = OFF_K + RET_W
OFF_G = OFF_V + RET_W
OFF_CQ = OFF_G + Q_LORA
OFF_CKV = OFF_CQ + KV_LORA
IN_COLS = OFF_CKV + MLA_ROPE
IN_COLS_PAD = OFF_CKV + LANES
NEG = -1e30
VMEM_LIMIT = 56 * 1024 * 1024


def _pick(n, cands):
    for c in cands:
        if n % c == 0:
            return c
    return n


def _cparams(sem, flags=None):
    return pltpu.CompilerParams(dimension_semantics=sem, vmem_limit_bytes=VMEM_LIMIT, flags=flags)


def _rms(x, g):
    return x * lax.rsqrt(jnp.mean(x * x, axis=-1, keepdims=True) + RMS_EPS) * g


def _dot(a, b):
    return jnp.dot(a, b, preferred_element_type=F32)


def _dot_nt(a, b):
    return lax.dot_general(a, b, (((1,), (1,)), ((), ())), preferred_element_type=F32)


def _dot_tn(a, b):
    return lax.dot_general(a, b, (((0,), (0,)), ((), ())), preferred_element_type=F32)


def _proj_kernel(h_ref, *refs):
    _proj_body(h_ref[...], *refs)


def _proj_streams_kernel(hp_ref, hs_ref, *refs, n_prompt_tiles):
    x = jnp.where(pl.program_id(0) < n_prompt_tiles, hp_ref[...], hs_ref[...])
    refs[-1][...] = x
    _proj_body(x, *refs[:-1])


def _proj_body(x, an_ref, win_ref, qn_ref, wuq_ref, kvn_ref, wbig_ref,
               c64_ref, sm64_ref, sp64_ref, c16_ref, s16_ref, ck_ref, smk_ref, spk_ref,
               qr_ref, kr_ref, vr_ref, gr_ref, qx_ref, ckv_ref, kro_ref, kx_ref):
    xn = _rms(x, an_ref[...]).astype(BF16)
    c64, sm64, sp64 = c64_ref[...], sm64_ref[...], sp64_ref[...]

    def rope64(z):
        return (z * c64 + pltpu.roll(z, LANES - 32, 1) * sm64 + pltpu.roll(z, 32, 1) * sp64)

    for lo in range(0, RET_W, QK_PAD):
        zq = _dot(xn, win_ref[:, lo:lo + QK_PAD])
        zk = _dot(xn, win_ref[:, OFF_Q + lo:OFF_Q + lo + QK_PAD])
        for half in range(0, QK_PAD, LANES):
            dst = slice(lo + half, lo + half + LANES)
            qr_ref[:, dst] = rope64(zq[:, half:half + LANES])
            kr_ref[:, dst] = rope64(zk[:, half:half + LANES]) * (RET_DK ** -0.5)

    zkv = _dot(xn, win_ref[:, OFF_CQ:IN_COLS_PAD])
    ckv = _rms(zkv[:, :KV_LORA], kvn_ref[...])
    ckv_ref[...] = ckv
    zk = zkv[:, KV_LORA:]
    kro = (zk * ck_ref[...] + pltpu.roll(zk, LANES - 16, 1) * smk_ref[...]
           + pltpu.roll(zk, 16, 1) * spk_ref[...])
    kro_ref[...] = kro[:, :MLA_ROPE]
    kx_ref[:, :KV_LORA] = ckv.astype(BF16)
    kx_ref[:, KV_LORA:] = kro.astype(BF16)

    cq = _dot(xn, win_ref[:, OFF_G:OFF_CQ])
    q = _dot(_rms(cq, qn_ref[...]).astype(BF16), wuq_ref[...])
    nq = MLA_HEADS * MLA_NOPE
    x1, x2 = q[:, nq:nq + LANES], q[:, nq + LANES:nq + 2 * LANES]
    c16, s16 = c16_ref[...], s16_ref[...]
    scale = (MLA_NOPE + MLA_ROPE) ** -0.5 * float(np.log2(np.e))
    qcat = jnp.concatenate([q[:, :nq], x1 * c16 - x2 * s16, x2 * c16 + x1 * s16], axis=-1)
    qcat = (qcat * scale).astype(BF16)
    half_w = (MLA_HEADS // 2) * QK_PAD
    for g in range(2):
        rows = slice(g * (nq // 2), (g + 1) * (nq // 2))
        cols = slice(g * half_w, (g + 1) * half_w)
        qx_ref[:, cols] = (_dot(qcat[:, rows], wbig_ref[rows, cols])
                           + _dot(qcat[:, nq:], wbig_ref[nq:, cols])).astype(BF16)

    vr_ref[...] = _dot(xn, win_ref[:, OFF_K:OFF_V]).astype(BF16)
    gr_ref[...] = _dot(xn, win_ref[:, OFF_V:OFF_G])


def _stream_specs(tm, width, n_prompt_tiles):
    return (pl.BlockSpec((tm, width), lambda i: (jnp.minimum(i, n_prompt_tiles - 1), 0)),
            pl.BlockSpec((tm, width), lambda i: (jnp.maximum(i - n_prompt_tiles, 0), 0)))


def _proj(h, an, win, qn, wuq, kvn, wbig, tabs, tab_tile, tm):
    streams = isinstance(h, tuple)
    T = sum(a.shape[0] for a in h) if streams else h.shape[0]
    D = h[0].shape[1] if streams else h.shape[1]
    row = lambda w: pl.BlockSpec((tm, w), lambda i: (i, 0))
    tab = pl.BlockSpec((tm, LANES), lambda i: (tab_tile(i), 0))
    full = lambda a: pl.BlockSpec(a.shape, lambda i: (0,) * a.ndim)
    out_shapes = (
        jax.ShapeDtypeStruct((T, RET_W), F32), jax.ShapeDtypeStruct((T, RET_W), F32),
        jax.ShapeDtypeStruct((T, RET_W), BF16), jax.ShapeDtypeStruct((T, RET_W), F32),
        jax.ShapeDtypeStruct((T, MLA_HEADS * QK_PAD), BF16),
        jax.ShapeDtypeStruct((T, KV_LORA), F32), jax.ShapeDtypeStruct((T, MLA_ROPE), F32),
        jax.ShapeDtypeStruct((T, QK_PAD), BF16))
    out_specs = (row(RET_W), row(RET_W), row(RET_W), row(RET_W), row(MLA_HEADS * QK_PAD),
                 row(KV_LORA), row(MLA_ROPE), row(QK_PAD))
    if streams:
        npt = h[0].shape[0] // tm
        kern = functools.partial(_proj_streams_kernel, n_prompt_tiles=npt)
        h_specs, h_args = list(_stream_specs(tm, D, npt)), h
        out_specs += (row(D),)
        out_shapes += (jax.ShapeDtypeStruct((T, D), F32),)
    else:
        kern, h_specs, h_args = _proj_kernel, [row(D)], (h,)
    return pl.pallas_call(
        kern, grid=(T // tm,),
        in_specs=h_specs + [full(an), full(win), full(qn), full(wuq), full(kvn), full(wbig)]
                 + [tab] * 8,
        out_specs=out_specs,
        out_shape=out_shapes, compiler_params=_cparams(("parallel",)), name="proj",
    )(*h_args, an, win, qn, wuq, kvn, wbig, *tabs)


def _ret_kernel(q_ref, k_ref, v_ref, g_ref, s0_ref, d_ref, qd_ref, kd_ref, gs_ref, bm_ref, rn_ref,
                o_ref, so_ref, st_ref, *, n_chunks):
    i = pl.program_id(1)

    @pl.when(i == 0)
    def _():
        st_ref[...] = s0_ref[0]

    lane = lax.broadcasted_iota(jnp.int32, (CHUNK, LANES), 1)
    first = lane < RET_DK

    def chunk(c, carry):
        rows = pl.ds(pl.multiple_of(c * CHUNK, CHUNK), CHUNK)
        for p in range(N_PAIRS):
            cols = slice(p * LANES, (p + 1) * LANES)
            qp, kp, vp = q_ref[rows, cols], k_ref[rows, cols], v_ref[rows, cols]
            q2 = jnp.concatenate([jnp.where(first, qp, 0.0), jnp.where(first, 0.0, qp)], axis=0)
            sd = (_dot_nt(q2.astype(BF16), kp.astype(BF16)) * d_ref[p]).astype(BF16)
            qq = (qp * qd_ref[p]).astype(BF16)
            st = st_ref[p]
            lhs = jnp.concatenate([jnp.concatenate([qq, qq], axis=0), sd], axis=1)
            rhs = jnp.concatenate([st.astype(BF16), vp], axis=0)
            o2 = _dot(lhs, rhs)
            o = jnp.where(first, o2[:CHUNK], o2[CHUNK:])
            upd = _dot_tn((kp * kd_ref[p]).astype(BF16), vp)
            st_ref[p] = st * gs_ref[p] + upd * bm_ref[...]
            oo = o * o
            ss_a = jnp.sum(jnp.where(first, oo, 0.0), axis=-1, keepdims=True)
            ss_b = jnp.sum(jnp.where(first, 0.0, oo), axis=-1, keepdims=True)
            rs = jnp.where(first, lax.rsqrt(ss_a * (1.0 / RET_DV) + RMS_EPS),
                           lax.rsqrt(ss_b * (1.0 / RET_DV) + RMS_EPS))
            g = g_ref[rows, cols]
            o_ref[rows, cols] = (o * rs * rn_ref[:, cols] * (g * jax.nn.sigmoid(g))).astype(BF16)
        return carry

    lax.fori_loop(0, n_chunks, chunk, 0, unroll=min(n_chunks, 4))

    @pl.when(i == pl.num_programs(1) - 1)
    def _():
        so_ref[0] = st_ref[...]


def _retention(qr, kr, vr, gr, s0, tabs, rn, nb, seq, row0, rb):
    nblk = seq // rb
    off = row0 // rb
    tok = pl.BlockSpec((rb, RET_W), lambda b, i: (off + b * nblk + i, 0))
    full = lambda a: pl.BlockSpec(a.shape, lambda b, i: (0,) * a.ndim)
    st_spec = pl.BlockSpec((1, N_PAIRS, LANES, LANES), lambda b, i: (b, 0, 0, 0))
    d, qd, kd, gs, bm = tabs
    return pl.pallas_call(
        functools.partial(_ret_kernel, n_chunks=rb // CHUNK), grid=(nb, nblk),
        in_specs=[tok, tok, tok, tok, st_spec, full(d), full(qd), full(kd), full(gs), full(bm), full(rn)],
        out_specs=(pl.BlockSpec((rb, RET_W), lambda b, i: (b * nblk + i, 0)), st_spec),
        out_shape=(jax.ShapeDtypeStruct((nb * seq, RET_W), BF16),
                   jax.ShapeDtypeStruct((nb, N_PAIRS, LANES, LANES), F32)),
        scratch_shapes=[pltpu.VMEM((N_PAIRS, LANES, LANES), F32)],
        compiler_params=_cparams(("parallel", "arbitrary")), name="retention",
    )(qr, kr, vr, gr, s0, d, qd, kd, gs, bm, rn)


def _attn_kernel(q_ref, k_ref, wuv_ref, o_ref, q_s, s0_s, s1_s, m_s, l_s, acc_s,
                 *, tq, kb, q_pos0, sk_valid):
    i = pl.program_id(1)
    R = MLA_HEADS * tq
    for h in range(MLA_HEADS):
        q_s[h * tq:(h + 1) * tq, :] = q_ref[:, h * QK_PAD:(h + 1) * QK_PAD]
    m_s[...] = jnp.full(m_s.shape, NEG, F32)
    l_s[...] = jnp.zeros(l_s.shape, F32)
    acc_s[...] = jnp.zeros(acc_s.shape, F32)

    chunk_end = lambda t: ((t >> CHUNK_SHIFT) + 1) << CHUNK_SHIFT
    kb_shift = kb.bit_length() - 1
    qstart = q_pos0 + i * tq
    lim_first = jnp.minimum(chunk_end(qstart), sk_valid)
    lim_last = jnp.minimum(chunk_end(qstart + tq - 1), sk_valid)
    n_full = lim_first >> kb_shift
    n_blk = (lim_last + kb - 1) >> kb_shift

    last = n_blk - 1

    def key_block(j):
        return k_ref[pl.ds(pl.multiple_of(j * kb, kb), kb), :]

    def scores(j, s_ref):
        s_ref[...] = _dot_nt(q_s[...], key_block(j))

    def update(j, s_ref, masked):
        s = s_ref[...]
        if masked:
            tok = lax.broadcasted_iota(jnp.int32, (R, 1), 0) & (tq - 1)
            row_lim = jnp.minimum(chunk_end(qstart + tok), sk_valid)
            kidx = j * kb + lax.broadcasted_iota(jnp.int32, (1, kb), 1)
            s = jnp.where(kidx < row_lim, s, NEG)
        m_prev = m_s[...]
        m_next = jnp.maximum(m_prev, jnp.max(s, axis=-1, keepdims=True))
        p = jnp.exp2(s - jnp.tile(m_next, (1, kb // LANES)))
        alpha = jnp.exp2(m_prev - m_next)
        p_lanes = p[:, :LANES]
        for c in range(LANES, kb, LANES):
            p_lanes = p_lanes + p[:, c:c + LANES]
        l_s[...] = alpha * l_s[...] + p_lanes
        acc_s[...] = alpha * acc_s[...] + _dot(p.astype(BF16), key_block(j)[:, :KV_LORA])
        m_s[...] = m_next

    n_pipe = jnp.minimum(n_full, last)
    odd = n_pipe & 1

    @pl.when(odd == 1)
    def _():
        scores(0, s0_s)
        update(0, s0_s, False)

    scores(odd, s0_s)

    def pair(jj, c):
        j = odd + 2 * jj
        scores(j + 1, s1_s)
        update(j, s0_s, False)
        scores(j + 2, s0_s)
        update(j + 1, s1_s, False)
        return c

    lax.fori_loop(0, n_pipe >> 1, pair, 0)
    update(n_pipe, s0_s, True)

    def tail(j, c):
        scores(j, s0_s)
        update(j, s0_s, True)
        return c

    lax.fori_loop(n_pipe + 1, n_blk, tail, 0)

    o_lat = (acc_s[...] / jnp.sum(l_s[...], axis=-1, keepdims=True)).astype(BF16)
    out = None
    for p in range(MLA_HEADS // 2):
        pair = jnp.concatenate([o_lat[(2 * p) * tq:(2 * p + 1) * tq],
                                o_lat[(2 * p + 1) * tq:(2 * p + 2) * tq]], axis=1)
        term = _dot(pair, wuv_ref[p])
        out = term if out is None else out + term
    o_ref[...] = out.astype(BF16)


def _attention(qx, kx, wuv, nb, seq_q, row0, tq, kb, q_pos0, sk, sk_valid):
    nq = seq_q // tq
    off = row0 // tq
    R = MLA_HEADS * tq
    return pl.pallas_call(
        functools.partial(_attn_kernel, tq=tq, kb=kb, q_pos0=q_pos0, sk_valid=sk_valid),
        grid=(nb, nq),
        in_specs=[pl.BlockSpec((tq, MLA_HEADS * QK_PAD), lambda b, i: (off + b * nq + i, 0)),
                  pl.BlockSpec((sk, QK_PAD), lambda b, i: (b, 0)),
                  pl.BlockSpec(wuv.shape, lambda b, i: (0, 0, 0))],
        out_specs=pl.BlockSpec((tq, MLA_W), lambda b, i: (b * nq + i, 0)),
        out_shape=jax.ShapeDtypeStruct((nb * seq_q, MLA_W), BF16),
        scratch_shapes=[pltpu.VMEM((R, QK_PAD), BF16), pltpu.VMEM((R, kb), F32),
                        pltpu.VMEM((R, kb), F32), pltpu.VMEM((R, LANES), F32),
                        pltpu.VMEM((R, LANES), F32), pltpu.VMEM((R, KV_LORA), F32)],
        compiler_params=_cparams(("parallel", "arbitrary")),
        name="attention",
    )(qx, kx, wuv)


def _merge_kernel(h_ref, ap_ref, bp_ref, as_ref, bs_ref, wo_ref, o_ref, *, n_prompt_tiles):
    def project(a_ref, b_ref):
        o_ref[...] = (h_ref[...] + _dot(a_ref[...], wo_ref[:RET_W, :])
                      + _dot(b_ref[...], wo_ref[RET_W:, :]))

    @pl.when(pl.program_id(0) < n_prompt_tiles)
    def _():
        project(ap_ref, bp_ref)

    @pl.when(pl.program_id(0) >= n_prompt_tiles)
    def _():
        project(as_ref, bs_ref)


def _merge(h, a_p, b_p, a_s, b_s, wo, tm):
    T, D = h.shape
    npt = a_p.shape[0] // tm
    row = lambda w: pl.BlockSpec((tm, w), lambda i: (i, 0))
    (a_prm, a_smp), (b_prm, b_smp) = _stream_specs(tm, RET_W, npt), _stream_specs(tm, MLA_W, npt)
    return pl.pallas_call(
        functools.partial(_merge_kernel, n_prompt_tiles=npt), grid=(T // tm,),
        in_specs=[row(D), a_prm, b_prm, a_smp, b_smp, pl.BlockSpec(wo.shape, lambda i: (0, 0))],
        out_specs=row(D), out_shape=jax.ShapeDtypeStruct((T, D), F32),
        compiler_params=_cparams(("parallel",)), name="merge",
    )(h, a_p, b_p, a_s, b_s, wo)


def _ffn_kernel(h_ref, fn_ref, wg_ref, wu_ref, wd_ref, o_ref):
    hh = h_ref[...]
    u = _rms(hh, fn_ref[...]).astype(BF16)
    a = _dot(u, wg_ref[...])
    mid = (a * jax.nn.sigmoid(a) * _dot(u, wu_ref[...])).astype(BF16)
    o_ref[...] = hh + _dot(mid, wd_ref[...])


def _ffn(h, fn, wg, wu, wd, tm):
    T, D = h.shape
    resident = lambda a: pl.BlockSpec(a.shape, lambda i: (0, 0), pipeline_mode=pl.Buffered(1))
    return pl.pallas_call(
        _ffn_kernel, grid=(T // tm,),
        in_specs=[pl.BlockSpec((tm, D), lambda i: (i, 0)), pl.BlockSpec(fn.shape, lambda i: (0, 0)),
                  resident(wg), resident(wu), resident(wd)],
        out_specs=pl.BlockSpec((tm, D), lambda i: (i, 0)),
        out_shape=jax.ShapeDtypeStruct((T, D), F32),
        compiler_params=_cparams(("parallel",)), name="ffn",
    )(h, fn, wg, wu, wd)


def _dense_layer_kernel(h_ref, ap_ref, bp_ref, as_ref, bs_ref, pp_ref, ps_ref, wo_ref, fn_ref,
                        wg_ref, wu_ref, wd_ref, pn_ref, wpg_ref, wpp_ref, o_ref, *, n_prompt_tiles):
    prompt = pl.program_id(0) < n_prompt_tiles
    a = jnp.where(prompt, ap_ref[...], as_ref[...])
    b = jnp.where(prompt, bp_ref[...], bs_ref[...])
    h1 = h_ref[...] + _dot(a, wo_ref[:RET_W, :]) + _dot(b, wo_ref[RET_W:, :])
    u = _rms(h1, fn_ref[...]).astype(BF16)
    g = _dot(u, wg_ref[...])
    mid = (g * jax.nn.sigmoid(g) * _dot(u, wu_ref[...])).astype(BF16)
    h2 = h1 + _dot(mid, wd_ref[...])
    gate = jax.nn.sigmoid(_dot(_rms(h2, pn_ref[...]).astype(BF16), wpg_ref[...]))
    p = jnp.where(prompt, pp_ref[...], ps_ref[...]).astype(BF16)
    o_ref[...] = h2 + gate * _dot(p, wpp_ref[...])


def _dense_layer(h, a_p, b_p, a_s, b_s, p_p, p_s, wo, fn, wg, wu, wd, pn, wpg, wpp, tm):
    T, D = h.shape
    npt = a_p.shape[0] // tm
    row = pl.BlockSpec((tm, D), lambda i: (i, 0))
    resident = lambda w: pl.BlockSpec(w.shape, lambda i: (0, 0), pipeline_mode=pl.Buffered(1))
    (a_prm, a_smp), (b_prm, b_smp) = _stream_specs(tm, RET_W, npt), _stream_specs(tm, MLA_W, npt)
    p_prm, p_smp = _stream_specs(tm, p_p.shape[1], npt)
    return pl.pallas_call(
        functools.partial(_dense_layer_kernel, n_prompt_tiles=npt), grid=(T // tm,),
        in_specs=[row, a_prm, b_prm, a_smp, b_smp, p_prm, p_smp] + [
            resident(w) for w in (wo, fn, wg, wu, wd, pn, wpg, wpp)],
        out_specs=row, out_shape=jax.ShapeDtypeStruct((T, D), F32),
        compiler_params=_cparams(("parallel",)), name="dense_layer",
    )(h, a_p, b_p, a_s, b_s, p_p, p_s, wo, fn, wg, wu, wd, pn, wpg, wpp)


L_E1, L_E2, L_W1, L_W2, L_R1, L_R2 = range(6)


def _router_kernel(h_ref, fn_ref, wr_ref, meta_ref, cnt_ref, run_s, *, n_experts):
    tm = h_ref.shape[0]

    @pl.when(pl.program_id(0) == 0)
    def _():
        run_s[...] = jnp.zeros(run_s.shape, F32)

    u = _rms(h_ref[...], fn_ref[...])
    logits = jnp.dot(u, wr_ref[...], precision=lax.Precision.HIGHEST, preferred_element_type=F32)
    lane = lax.broadcasted_iota(jnp.int32, logits.shape, 1).astype(F32)
    lg = jnp.where(lane < n_experts, logits, NEG)
    m1 = jnp.max(lg, axis=-1, keepdims=True)
    i1 = jnp.min(jnp.where(lg == m1, lane, float(LANES)), axis=-1, keepdims=True)
    lg2 = jnp.where(lane == i1, NEG, lg)
    m2 = jnp.max(lg2, axis=-1, keepdims=True)
    i2 = jnp.min(jnp.where(lg2 == m2, lane, float(LANES)), axis=-1, keepdims=True)
    e2 = jnp.exp(m2 - m1)
    den = 1.0 + e2

    hit = jnp.where((lane == i1) | (lane == i2), 1.0, 0.0)
    r_i = lax.broadcasted_iota(jnp.int32, (tm, tm), 0)
    c_i = lax.broadcasted_iota(jnp.int32, (tm, tm), 1)
    before = jnp.where(c_i < r_i, 1.0, 0.0).astype(BF16)
    prefix = _dot(before, hit.astype(BF16)) + run_s[0:1, :]
    r1 = jnp.sum(jnp.where(lane == i1, prefix, 0.0), axis=-1, keepdims=True)
    r2 = jnp.sum(jnp.where(lane == i2, prefix, 0.0), axis=-1, keepdims=True)
    run_s[...] = run_s[...] + jnp.sum(hit, axis=0, keepdims=True)
    cnt_ref[...] = run_s[...]

    meta = jnp.where(lane == L_E1, i1, 0.0)
    for ln, val in ((L_E2, i2), (L_W1, 1.0 / den), (L_W2, e2 / den), (L_R1, r1), (L_R2, r2)):
        meta = jnp.where(lane == ln, val, meta)
    meta_ref[...] = meta


def _router(h, fn, wr, n_experts, tm):
    T, D = h.shape
    return pl.pallas_call(
        functools.partial(_router_kernel, n_experts=n_experts), grid=(T // tm,),
        in_specs=[pl.BlockSpec((tm, D), lambda i: (i, 0)), pl.BlockSpec(fn.shape, lambda i: (0, 0)),
                  pl.BlockSpec(wr.shape, lambda i: (0, 0))],
        out_specs=(pl.BlockSpec((tm, LANES), lambda i: (i, 0)),
                   pl.BlockSpec((SUBLANES, LANES), lambda i: (0, 0))),
        out_shape=(jax.ShapeDtypeStruct((T, LANES), F32), jax.ShapeDtypeStruct((SUBLANES, LANES), F32)),
        scratch_shapes=[pltpu.VMEM((SUBLANES, LANES), F32)],
        compiler_params=_cparams(("arbitrary",)), name="router",
    )(h, fn, wr)


def _tile_index_copy(d_hbm, idx_s, sem, tile, slot):
    n = d_hbm.shape[1]
    return pltpu.make_async_copy(d_hbm.at[tile], idx_s.at[pl.ds(pl.multiple_of(slot * n, n), n)],
                                 sem.at[slot])


def _dispatch_kernel(fill_ref, d_hbm, h_ref, xs_out, idx_s, zero_s, isem, rsem, fsem):
    tm = h_ref.shape[0]
    tmg = zero_s.shape[0]
    i, n = pl.program_id(0), pl.num_programs(0)
    slot = i % 2

    def fill_copy(k):
        start = pl.multiple_of(fill_ref[k], tmg)
        return pltpu.make_async_copy(zero_s, xs_out.at[pl.ds(start, tmg)], fsem)

    @pl.when(i == 0)
    def _():
        _tile_index_copy(d_hbm, idx_s, isem, 0, 0).start()
        zero_s[...] = jnp.zeros(zero_s.shape, F32)
        for k in range(fill_ref.shape[0]):
            pl.when(fill_ref[k] >= 0)(lambda k=k: fill_copy(k).start(priority=k % 2))
        for k in range(fill_ref.shape[0]):
            pl.when(fill_ref[k] >= 0)(lambda k=k: fill_copy(k).wait())

    @pl.when(i + 1 < n)
    def _():
        _tile_index_copy(d_hbm, idx_s, isem, i + 1, 1 - slot).start()

    _tile_index_copy(d_hbm, idx_s, isem, i, slot).wait()

    def row_copy(r, dst):
        return pltpu.make_async_copy(h_ref.at[pl.ds(r, 1)], xs_out.at[pl.ds(dst, 1)], rsem)

    def body(g, c):
        r0 = pl.multiple_of(g * SUBLANES, SUBLANES)
        for u in range(SUBLANES):
            for k in range(TOP_K):
                row_copy(r0 + u, idx_s[slot * (TOP_K * tm) + k * tm + r0 + u]).start(priority=k % 2)
        return c

    lax.fori_loop(0, tm // SUBLANES, body, 0)
    for k in range(TOP_K):
        pltpu.make_async_copy(h_ref, xs_out.at[pl.ds(0, tm)], rsem).wait()


def _dispatch(fill_starts, d_tiles, h, n_rows, tm, tmg):
    T, D = h.shape
    grid_spec = pltpu.PrefetchScalarGridSpec(
        num_scalar_prefetch=1, grid=(T // tm,),
        in_specs=[pl.BlockSpec(memory_space=pl.ANY), pl.BlockSpec((tm, D), lambda i, fs: (i, 0))],
        out_specs=pl.BlockSpec(memory_space=pl.ANY),
        scratch_shapes=[pltpu.SMEM((2 * TOP_K * tm,), jnp.int32), pltpu.VMEM((tmg, D), F32),
                        pltpu.SemaphoreType.DMA((2,)), pltpu.SemaphoreType.DMA(()),
                        pltpu.SemaphoreType.DMA(())])
    return pl.pallas_call(
        _dispatch_kernel, grid_spec=grid_spec, out_shape=jax.ShapeDtypeStruct((n_rows, D), F32),
        compiler_params=_cparams(("arbitrary",)), name="dispatch",
    )(fill_starts, d_tiles, h)


def _gmm_kernel(te_ref, nu_ref, x_ref, fn_ref, wg_ref, wu_ref, wd_ref, o_ref, u_s):
    i, f = pl.program_id(0), pl.program_id(1)
    used = i < nu_ref[0]

    @pl.when(f == 0)
    def _():
        o_ref[...] = jnp.zeros(o_ref.shape, F32)

    @pl.when(used & (f == 0))
    def _():
        u_s[...] = _rms(x_ref[...], fn_ref[...]).astype(BF16)

    @pl.when(used)
    def _():
        u = u_s[...]
        a = _dot(u, wg_ref[0].astype(BF16))
        mid = (a * jax.nn.sigmoid(a) * _dot(u, wu_ref[0].astype(BF16))).astype(BF16)
        o_ref[...] += _dot(mid, wd_ref[0].astype(BF16))


def _gmm(tile_expert, n_used, xs, fn, wg, wu, wd, tmg, tf):
    R, D = xs.shape
    F = wg.shape[-1]
    nf = F // tf
    f_eff = lambda i, f, nu: jnp.where(i < nu[0], f, nf - 1)
    grid_spec = pltpu.PrefetchScalarGridSpec(
        num_scalar_prefetch=2, grid=(R // tmg, nf),
        in_specs=[pl.BlockSpec((tmg, D), lambda i, f, te, nu: (i, 0)),
                  pl.BlockSpec(fn.shape, lambda i, f, te, nu: (0, 0)),
                  pl.BlockSpec((1, D, tf), lambda i, f, te, nu: (te[i], 0, f_eff(i, f, nu))),
                  pl.BlockSpec((1, D, tf), lambda i, f, te, nu: (te[i], 0, f_eff(i, f, nu))),
                  pl.BlockSpec((1, tf, D), lambda i, f, te, nu: (te[i], f_eff(i, f, nu), 0))],
        out_specs=pl.BlockSpec((tmg, D), lambda i, f, te, nu: (i, 0)),
        scratch_shapes=[pltpu.VMEM((tmg, D), BF16)])
    return pl.pallas_call(
        _gmm_kernel, grid_spec=grid_spec, out_shape=jax.ShapeDtypeStruct((R, D), F32),
        compiler_params=_cparams(("arbitrary", "arbitrary")), name="experts",
    )(tile_expert, n_used, xs, fn, wg, wu, wd)


def _combine_kernel(d_hbm, h_ref, meta_ref, ys_hbm, o_ref, idx_s, ya_s, yb_s, isem, rsem):
    tm = h_ref.shape[0]
    i, n = pl.program_id(0), pl.num_programs(0)
    slot = i % 2

    def fetch(tile, s):
        cp = _tile_index_copy(d_hbm, idx_s, isem, tile, s)
        cp.start()
        cp.wait()

        def body(g, c):
            r0 = pl.multiple_of(g * SUBLANES, SUBLANES)
            base = s * (TOP_K * tm) + r0
            for u in range(SUBLANES):
                pltpu.make_async_copy(ys_hbm.at[pl.ds(idx_s[base + u], 1)],
                                      ya_s.at[s, pl.ds(r0 + u, 1)], rsem.at[s]).start(priority=0)
                pltpu.make_async_copy(ys_hbm.at[pl.ds(idx_s[base + tm + u], 1)],
                                      yb_s.at[s, pl.ds(r0 + u, 1)], rsem.at[s]).start(priority=1)
            return c

        lax.fori_loop(0, tm // SUBLANES, body, 0)

    @pl.when(i == 0)
    def _():
        fetch(0, 0)

    @pl.when(i + 1 < n)
    def _():
        fetch(i + 1, 1 - slot)

    pltpu.make_async_copy(ys_hbm.at[pl.ds(0, tm)], ya_s.at[slot], rsem.at[slot]).wait()
    pltpu.make_async_copy(ys_hbm.at[pl.ds(0, tm)], yb_s.at[slot], rsem.at[slot]).wait()
    meta = meta_ref[...]
    o_ref[...] = (h_ref[...] + meta[:, L_W1:L_W1 + 1] * ya_s[slot]
                  + meta[:, L_W2:L_W2 + 1] * yb_s[slot])


def _combine(d_tiles, h, meta, ys, tm):
    T, D = h.shape
    return pl.pallas_call(
        _combine_kernel, grid=(T // tm,),
        in_specs=[pl.BlockSpec(memory_space=pl.ANY), pl.BlockSpec((tm, D), lambda i: (i, 0)),
                  pl.BlockSpec((tm, LANES), lambda i: (i, 0)), pl.BlockSpec(memory_space=pl.ANY)],
        out_specs=pl.BlockSpec((tm, D), lambda i: (i, 0)),
        out_shape=jax.ShapeDtypeStruct((T, D), F32),
        scratch_shapes=[pltpu.SMEM((2 * TOP_K * tm,), jnp.int32), pltpu.VMEM((2, tm, D), F32),
                        pltpu.VMEM((2, tm, D), F32), pltpu.SemaphoreType.DMA((2,)),
                        pltpu.SemaphoreType.DMA((2,))],
        compiler_params=_cparams(("arbitrary",)), name="combine",
    )(d_tiles, h, meta, ys)


def _moe(h, fn, w_router, wg, wu, wd, tm, tmg, tf):
    T, D = h.shape
    E = w_router.shape[-1]
    meta, counts = _router(h, fn, jnp.pad(w_router, ((0, 0), (0, LANES - E))), E, tm)
    cnt = counts[0, :E].astype(jnp.int32)
    padded = ((cnt + tmg - 1) // tmg) * tmg
    ends = jnp.cumsum(padded)
    off = ends - padded
    n_tiles = -(-TOP_K * T // tmg) + E
    tile_expert = jnp.minimum(
        jnp.sum(ends[None, :] <= (jnp.arange(n_tiles, dtype=jnp.int32) * tmg)[:, None], axis=1), E - 1
    ).astype(jnp.int32)
    n_used = (ends[-1] // tmg).astype(jnp.int32).reshape(1)
    e12 = meta[:, L_E1:L_E2 + 1].astype(jnp.int32)
    off12 = jnp.sum(jnp.where(e12[..., None] == jnp.arange(E, dtype=jnp.int32), off, 0), axis=-1)
    dest = off12 + meta[:, L_R1:L_R2 + 1].astype(jnp.int32)
    d_tiles = dest.reshape(T // tm, tm, TOP_K).transpose(0, 2, 1).reshape(T // tm, TOP_K * tm)

    trailing = ends[-1] + jnp.arange(E + 1, dtype=jnp.int32) * tmg
    fill_starts = jnp.concatenate([
        jnp.where(padded > 0, ends - tmg, -1),
        jnp.where(trailing < n_tiles * tmg, trailing, -1)]).astype(jnp.int32)
    xs = _dispatch(fill_starts, d_tiles, h, n_tiles * tmg, tm, tmg)
    ys = _gmm(tile_expert, n_used, xs, fn, wg, wu, wd, tmg, tf)
    return _combine(d_tiles, h, meta, ys, tm)


def _ple_kernel(h_ref, pn_ref, wg_ref, wp_ref, fin_ref, *refs, final, n_prompt_tiles):
    *p_refs, o_ref = refs
    if len(p_refs) == 2:
        p = jnp.where(pl.program_id(0) < n_prompt_tiles, p_refs[0][...], p_refs[1][...])
    else:
        p = p_refs[0][...]
    hh = h_ref[...]
    gate = jax.nn.sigmoid(_dot(_rms(hh, pn_ref[...]).astype(BF16), wg_ref[...]))
    out = hh + gate * _dot(p.astype(BF16), wp_ref[...])
    if final:
        out = _rms(out, fin_ref[...])
    o_ref[...] = out


def _ple(h, pn, wg, p, wp, fin, tm, final, row0=0, rows=None):
    D = h.shape[1]
    rows = h.shape[0] if rows is None else rows
    off = row0 // tm
    full = lambda a: pl.BlockSpec(a.shape, lambda i: (0,) * a.ndim)
    if isinstance(p, tuple):
        npt = p[0].shape[0] // tm
        p_specs = list(_stream_specs(tm, p[0].shape[1], npt))
    else:
        npt, p_specs, p = 0, [pl.BlockSpec((tm, p.shape[1]), lambda i: (i, 0))], (p,)
    return pl.pallas_call(
        functools.partial(_ple_kernel, final=final, n_prompt_tiles=npt), grid=(rows // tm,),
        in_specs=[pl.BlockSpec((tm, D), lambda i: (off + i, 0)), full(pn), full(wg), full(wp),
                  full(fin)] + p_specs,
        out_specs=pl.BlockSpec((tm, D), lambda i: (i, 0)),
        out_shape=jax.ShapeDtypeStruct((rows, D), F32),
        compiler_params=_cparams(("parallel",)), name="ple",
    )(h, pn, wg, wp, fin, *p)


def _rope_tables(pos):
    pos = pos.astype(F32)[:, None]
    lane = np.arange(LANES)
    inv32 = ROPE_THETA ** (-jnp.arange(32, dtype=F32) / 32)
    inv16 = ROPE_THETA ** (-jnp.arange(16, dtype=F32) / 16)
    a64 = (pos * inv32[None, :])[:, lane % 32]
    a16 = (pos * inv16[None, :])[:, lane % 16]
    lo64 = jnp.asarray((lane % 64) < 32)
    c64, s64 = jnp.cos(a64), jnp.sin(a64)
    c16, s16 = jnp.cos(a16), jnp.sin(a16)
    in_k = jnp.asarray(lane < MLA_ROPE)
    lo_k = jnp.asarray(lane < 16)
    hi_k = jnp.asarray((lane >= 16) & (lane < MLA_ROPE))
    return (c64, jnp.where(lo64, -s64, 0.0), jnp.where(lo64, 0.0, s64), c16, s16,
            jnp.where(in_k, c16, 0.0), jnp.where(lo_k, -s16, 0.0), jnp.where(hi_k, s16, 0.0))


def _ret_tables(L):
    log_g = jnp.log(1.0 - jnp.exp2(-5.0 - jnp.arange(RET_HEADS, dtype=F32)))
    idx = jnp.arange(L, dtype=F32)
    dist = jnp.abs(idx[:, None] - idx[None, :])
    d = jnp.exp(dist[None] * log_g[:, None, None])
    qdec = jnp.exp((idx[:, None] + 1.0) * log_g[None, :])
    kdec = jnp.exp((L - 1.0 - idx)[:, None] * log_g[None, :])
    sdec = jnp.exp(L * log_g)
    wide = lambda t: jnp.repeat(t, RET_DK, axis=1).reshape(L, N_PAIRS, LANES).transpose(1, 0, 2)
    bm = jnp.asarray(np.kron(np.eye(2, dtype=np.float32), np.ones((RET_DK, RET_DV), np.float32)))
    gs = jnp.repeat(sdec, RET_DK).reshape(N_PAIRS, LANES, 1) * bm[None]
    return d.reshape(N_PAIRS, 2 * L, L), wide(qdec), wide(kdec), gs, bm


def _state_to_pairs(s):
    B = s.shape[0]
    s = s.reshape(B, N_PAIRS, 2, RET_DK, RET_DV)
    eye = jnp.eye(2, dtype=s.dtype)
    out = s[:, :, :, :, None, :] * eye[None, None, :, None, :, None]
    return out.reshape(B, N_PAIRS, LANES, LANES)


def _pairs_to_state(sp):
    B = sp.shape[0]
    s = sp.reshape(B, N_PAIRS, 2, RET_DK, 2, RET_DV)
    return jnp.stack([s[:, :, 0, :, 0, :], s[:, :, 1, :, 1, :]], axis=2).reshape(B, RET_HEADS, RET_DK, RET_DV)


def _uq_perm():
    per = MLA_NOPE + MLA_ROPE
    half = MLA_ROPE // 2
    nope = [h * per + d for h in range(MLA_HEADS) for d in range(MLA_NOPE)]
    x1 = [h * per + MLA_NOPE + f for h in range(MLA_HEADS) for f in range(half)]
    x2 = [h * per + MLA_NOPE + half + f for h in range(MLA_HEADS) for f in range(half)]
    return np.array(nope + x1 + x2, np.int32)


def _big_query_weight(w_uk):
    H, half = MLA_HEADS, MLA_ROPE // 2
    eye = jnp.eye(H, dtype=w_uk.dtype)
    wpad = jnp.pad(w_uk, ((0, 0), (0, 0), (0, QK_PAD - KV_LORA)))
    top = (eye[:, None, :, None] * wpad[:, :, None, :]).reshape(H * MLA_NOPE, H * QK_PAD)
    sel = np.zeros((2 * H * half, H * QK_PAD), np.float32)
    for h in range(H):
        for f in range(half):
            sel[h * half + f, h * QK_PAD + KV_LORA + f] = 1.0
            sel[H * half + h * half + f, h * QK_PAD + KV_LORA + half + f] = 1.0
    return jnp.concatenate([top, jnp.asarray(sel, w_uk.dtype)], axis=0)


def kernel(x_prompt, x_sample, p_prompt, p_sample, cache_ckv, cache_krope, state_ret, attn_norm, w_in, q_norm, w_uq, kv_norm, w_uk, w_uv, ret_norm, w_o, ffn_norm, w_gate_d, w_up_d, w_down_d, w_router, w_gate_e, w_up_e, w_down_e, ple_norm, w_ple_gate, w_ple_proj, final_norm):
    Bp, S, D = x_prompt.shape
    Bs, L, _ = x_sample.shape
    depth = w_in.shape[0]
    P = cache_ckv.shape[2]
    Tp, Ts = Bp * S, Bs * L
    T = Tp + Ts
    assert S % CHUNK == 0 and P % CHUNK == 0 and L == CHUNK and Tp % CHUNK == 0
    assert w_router.shape[-1] >= TOP_K

    tm_proj = _pick(int(np.gcd(S, Ts)), (512, 256, 128, 64))
    tm_gmm = 1024
    rb = _pick(S, (512, 256, 128, 64))
    tq = _pick(S, (256, 128, 64))
    kb = _pick(S, (512, 256, 128))
    kb_s = 512
    sk_s = P + L
    sk_pad = -(-sk_s // kb_s) * kb_s

    pos = jnp.concatenate([jnp.arange(S, dtype=jnp.int32),
                           P + jnp.tile(jnp.arange(L, dtype=jnp.int32), Bs)])
    rope_tabs = _rope_tables(pos)
    n_pt, pt_per_seq = Tp // tm_proj, S // tm_proj
    tab_tile = lambda i: jnp.where(i < n_pt, i % pt_per_seq, pt_per_seq + i - n_pt)
    ret_tabs_p = _ret_tables(CHUNK)
    ret_tabs_s = _ret_tables(L)
    perm = _uq_perm()
    row2 = lambda v: v.reshape(1, -1)

    h = (x_prompt.reshape(Tp, D), x_sample.reshape(Ts, D))
    outs = {k: [] for k in ("ckv_p", "kro_p", "ret_p", "ckv_s", "kro_s", "ret_s")}
    for l in range(depth):
        win = jnp.pad(w_in[l], ((0, 0), (0, IN_COLS_PAD - IN_COLS))).astype(BF16)
        wuq = w_uq[l][:, perm].astype(BF16)
        wbig = _big_query_weight(w_uk[l]).astype(BF16)
        qr, kr, vr, gr, qx, ckv, kro, kx, *h_cat = _proj(
            h, row2(attn_norm[l]), win, row2(q_norm[l]), wuq, row2(kv_norm[l]), wbig, rope_tabs,
            tab_tile, tm_proj)
        if h_cat:
            h, = h_cat

        rn = row2(ret_norm[l])
        zero_state = jnp.zeros((Bp, N_PAIRS, LANES, LANES), F32)
        o_ret_p, st_p = _retention(qr, kr, vr, gr, zero_state, ret_tabs_p, rn, Bp, S, 0, rb)
        o_ret_s, st_s = _retention(qr, kr, vr, gr, _state_to_pairs(state_ret[l].astype(F32)),
                                   ret_tabs_s, rn, Bs, L, Tp, L)

        wuv = w_uv[l].astype(BF16)
        wuv_big = (jnp.eye(MLA_HEADS, dtype=BF16)[:, None, :, None] * wuv[:, :, None, :]
                   ).reshape(MLA_HEADS // 2, 2 * KV_LORA, MLA_W)
        o_mla_p = _attention(qx, kx, wuv_big, Bp, S, 0, tq, kb, 0, S, S)
        cache_kx = jnp.concatenate(
            [cache_ckv[l], cache_krope[l], jnp.zeros((Bs, P, QK_PAD - KV_LORA - MLA_ROPE), F32)],
            axis=-1).astype(BF16)
        kx_s = jnp.concatenate([cache_kx, kx[Tp:].reshape(Bs, L, QK_PAD),
                                jnp.zeros((Bs, sk_pad - sk_s, QK_PAD), BF16)], axis=1)
        o_mla_s = _attention(qx, kx_s.reshape(Bs * sk_pad, QK_PAD), wuv_big, Bs, L, Tp, L, kb_s, P,
                             sk_pad, sk_s)

        fn = row2(ffn_norm[l])
        j = l // 2
        wo = w_o[l].astype(BF16)
        ple_w = (row2(ple_norm[l]), w_ple_gate[l].astype(BF16))
        wp = w_ple_proj[l].astype(BF16)
        p_l = (p_prompt[l].reshape(Tp, -1), p_sample[l].reshape(Ts, -1))
        last = l == depth - 1
        if l % 2 == 0 and not last:
            h = _dense_layer(h, o_ret_p, o_mla_p, o_ret_s, o_mla_s, *p_l, wo, fn,
                             w_gate_d[j].astype(BF16), w_up_d[j].astype(BF16),
                             w_down_d[j].astype(BF16), *ple_w, wp, tm_proj)
        else:
            h = _merge(h, o_ret_p, o_mla_p, o_ret_s, o_mla_s, wo, tm_proj)
            if l % 2 == 0:
                h = _ffn(h, fn, w_gate_d[j].astype(BF16), w_up_d[j].astype(BF16),
                         w_down_d[j].astype(BF16), tm_proj)
            else:
                tf = _pick(w_gate_e.shape[-1], (512, 256, 128))
                h = _moe(h, fn, w_router[j], w_gate_e[j], w_up_e[j], w_down_e[j], tm_proj, tm_gmm, tf)
            if not last:
                h = _ple(h, *ple_w, p_l, wp, row2(final_norm), tm_proj, False)
            else:
                y_p = _ple(h, *ple_w, p_l[0], wp, row2(final_norm), tm_proj, True, 0, Tp)
                y_s = _ple(h, *ple_w, p_l[1], wp, row2(final_norm), tm_proj, True, Tp, Ts)

        outs["ckv_p"].append(ckv[:Tp].reshape(Bp, S, KV_LORA))
        outs["kro_p"].append(kro[:Tp].reshape(Bp, S, MLA_ROPE))
        outs["ret_p"].append(_pairs_to_state(st_p))
        outs["ckv_s"].append(ckv[Tp:].reshape(Bs, L, KV_LORA))
        outs["kro_s"].append(kro[Tp:].reshape(Bs, L, MLA_ROPE))
        outs["ret_s"].append(_pairs_to_state(st_s))

    return (y_p.reshape(Bp, S, D), y_s.reshape(Bs, L, D),
            jnp.stack(outs["ckv_p"]), jnp.stack(outs["kro_p"]), jnp.stack(outs["ret_p"]),
            jnp.stack(outs["ckv_s"]), jnp.stack(outs["kro_s"]), jnp.stack(outs["ret_s"]))
```

```python
import functools

import numpy as np
import jax
import jax.numpy as jnp
from jax import lax
from jax.experimental import pallas as pl
from jax.experimental.pallas import tpu as pltpu

F32 = jnp.float32
BF16 = jnp.bfloat16

CHUNK = 64
CHUNK_SHIFT = 6
RMS_EPS = 1e-6
ROPE_THETA = 10000.0
RET_HEADS = 8
RET_DK = 64
RET_DV = 64
RET_W = RET_HEADS * RET_DK
MLA_HEADS = 8
MLA_NOPE = 64
MLA_ROPE = 32
MLA_V = 64
Q_LORA = 256
KV_LORA = 128
MLA_W = MLA_HEADS * MLA_V
TOP_K = 2
N_PAIRS = RET_HEADS // 2
LANES = 128
SUBLANES = 8
QK_PAD = 256
OFF_Q = RET_W
OFF_K = OFF_Q + RET_W
OFF_V = OFF_K + RET_W
OFF_G = OFF_V + RET_W
OFF_CQ = OFF_G + Q_LORA
OFF_CKV = OFF_CQ + KV_LORA
IN_COLS = OFF_CKV + MLA_ROPE
IN_COLS_PAD = OFF_CKV + LANES
NEG = -1e30
VMEM_LIMIT = 56 * 1024 * 1024


def _pick(n, cands):
    for c in cands:
        if n % c == 0:
            return c
    return n


def _cparams(sem, flags=None):
    return pltpu.CompilerParams(dimension_semantics=sem, vmem_limit_bytes=VMEM_LIMIT, flags=flags)


def _rms(x, g):
    return x * lax.rsqrt(jnp.mean(x * x, axis=-1, keepdims=True) + RMS_EPS) * g


def _dot(a, b):
    return jnp.dot(a, b, preferred_element_type=F32)


def _dot_nt(a, b):
    return lax.dot_general(a, b, (((1,), (1,)), ((), ())), preferred_element_type=F32)


def _dot_tn(a, b):
    return lax.dot_general(a, b, (((0,), (0,)), ((), ())), preferred_element_type=F32)


def _proj_kernel(h_ref, *refs):
    _proj_body(h_ref[...], *refs)


def _proj_streams_kernel(hp_ref, hs_ref, *refs, n_prompt_tiles):
    x = jnp.where(pl.program_id(0) < n_prompt_tiles, hp_ref[...], hs_ref[...])
    refs[-1][...] = x
    _proj_body(x, *refs[:-1])


def _proj_body(x, an_ref, win_ref, qn_ref, wuq_ref, kvn_ref, wbig_ref,
               c64_ref, sm64_ref, sp64_ref, c16_ref, s16_ref, ck_ref, smk_ref, spk_ref,
               qr_ref, kr_ref, vr_ref, gr_ref, qx_ref, ckv_ref, kro_ref, kx_ref):
    xn = _rms(x, an_ref[...]).astype(BF16)
    c64, sm64, sp64 = c64_ref[...], sm64_ref[...], sp64_ref[...]

    def rope64(z):
        return (z * c64 + pltpu.roll(z, LANES - 32, 1) * sm64 + pltpu.roll(z, 32, 1) * sp64)

    for lo in range(0, RET_W, QK_PAD):
        zq = _dot(xn, win_ref[:, lo:lo + QK_PAD])
        zk = _dot(xn, win_ref[:, OFF_Q + lo:OFF_Q + lo + QK_PAD])
        for half in range(0, QK_PAD, LANES):
            dst = slice(lo + half, lo + half + LANES)
            qr_ref[:, dst] = rope64(zq[:, half:half + LANES])
            kr_ref[:, dst] = rope64(zk[:, half:half + LANES]) * (RET_DK ** -0.5)

    zkv = _dot(xn, win_ref[:, OFF_CQ:IN_COLS_PAD])
    ckv = _rms(zkv[:, :KV_LORA], kvn_ref[...])
    ckv_ref[...] = ckv
    zk = zkv[:, KV_LORA:]
    kro = (zk * ck_ref[...] + pltpu.roll(zk, LANES - 16, 1) * smk_ref[...]
           + pltpu.roll(zk, 16, 1) * spk_ref[...])
    kro_ref[...] = kro[:, :MLA_ROPE]
    kx_ref[:, :KV_LORA] = ckv.astype(BF16)
    kx_ref[:, KV_LORA:] = kro.astype(BF16)

    cq = _dot(xn, win_ref[:, OFF_G:OFF_CQ])
    q = _dot(_rms(cq, qn_ref[...]).astype(BF16), wuq_ref[...])
    nq = MLA_HEADS * MLA_NOPE
    x1, x2 = q[:, nq:nq + LANES], q[:, nq + LANES:nq + 2 * LANES]
    c16, s16 = c16_ref[...], s16_ref[...]
    scale = (MLA_NOPE + MLA_ROPE) ** -0.5 * float(np.log2(np.e))
    qcat = jnp.concatenate([q[:, :nq], x1 * c16 - x2 * s16, x2 * c16 + x1 * s16], axis=-1)
    qcat = (qcat * scale).astype(BF16)
    half_w = (MLA_HEADS // 2) * QK_PAD
    for g in range(2):
        rows = slice(g * (nq // 2), (g + 1) * (nq // 2))
        cols = slice(g * half_w, (g + 1) * half_w)
        qx_ref[:, cols] = (_dot(qcat[:, rows], wbig_ref[rows, cols])
                           + _dot(qcat[:, nq:], wbig_ref[nq:, cols])).astype(BF16)

    vr_ref[...] = _dot(xn, win_ref[:, OFF_K:OFF_V]).astype(BF16)
    gr_ref[...] = _dot(xn, win_ref[:, OFF_V:OFF_G])


def _stream_specs(tm, width, n_prompt_tiles):
    return (pl.BlockSpec((tm, width), lambda i: (jnp.minimum(i, n_prompt_tiles - 1), 0)),
            pl.BlockSpec((tm, width), lambda i: (jnp.maximum(i - n_prompt_tiles, 0), 0)))


def _proj(h, an, win, qn, wuq, kvn, wbig, tabs, tab_tile, tm):
    streams = isinstance(h, tuple)
    T = sum(a.shape[0] for a in h) if streams else h.shape[0]
    D = h[0].shape[1] if streams else h.shape[1]
    row = lambda w: pl.BlockSpec((tm, w), lambda i: (i, 0))
    tab = pl.BlockSpec((tm, LANES), lambda i: (tab_tile(i), 0))
    full = lambda a: pl.BlockSpec(a.shape, lambda i: (0,) * a.ndim)
    out_shapes = (
        jax.ShapeDtypeStruct((T, RET_W), F32), jax.ShapeDtypeStruct((T, RET_W), F32),
        jax.ShapeDtypeStruct((T, RET_W), BF16), jax.ShapeDtypeStruct((T, RET_W), F32),
        jax.ShapeDtypeStruct((T, MLA_HEADS * QK_PAD), BF16),
        jax.ShapeDtypeStruct((T, KV_LORA), F32), jax.ShapeDtypeStruct((T, MLA_ROPE), F32),
        jax.ShapeDtypeStruct((T, QK_PAD), BF16))
    out_specs = (row(RET_W), row(RET_W), row(RET_W), row(RET_W), row(MLA_HEADS * QK_PAD),
                 row(KV_LORA), row(MLA_ROPE), row(QK_PAD))
    if streams:
        npt = h[0].shape[0] // tm
        kern = functools.partial(_proj_streams_kernel, n_prompt_tiles=npt)
        h_specs, h_args = list(_stream_specs(tm, D, npt)), h
        out_specs += (row(D),)
        out_shapes += (jax.ShapeDtypeStruct((T, D), F32),)
    else:
        kern, h_specs, h_args = _proj_kernel, [row(D)], (h,)
    return pl.pallas_call(
        kern, grid=(T // tm,),
        in_specs=h_specs + [full(an), full(win), full(qn), full(wuq), full(kvn), full(wbig)]
                 + [tab] * 8,
        out_specs=out_specs,
        out_shape=out_shapes, compiler_params=_cparams(("parallel",)), name="proj",
    )(*h_args, an, win, qn, wuq, kvn, wbig, *tabs)


def _ret_kernel(q_ref, k_ref, v_ref, g_ref, s0_ref, d_ref, qd_ref, kd_ref, gs_ref, bm_ref, rn_ref,
                o_ref, so_ref, st_ref, *, n_chunks):
    i = pl.program_id(1)

    @pl.when(i == 0)
    def _():
        st_ref[...] = s0_ref[0]

    lane = lax.broadcasted_iota(jnp.int32, (CHUNK, LANES), 1)
    first = lane < RET_DK

    def chunk(c, carry):
        rows = pl.ds(pl.multiple_of(c * CHUNK, CHUNK), CHUNK)
        for p in range(N_PAIRS):
            cols = slice(p * LANES, (p + 1) * LANES)
            qp, kp, vp = q_ref[rows, cols], k_ref[rows, cols], v_ref[rows, cols]
            q2 = jnp.concatenate([jnp.where(first, qp, 0.0), jnp.where(first, 0.0, qp)], axis=0)
            sd = (_dot_nt(q2.astype(BF16), kp.astype(BF16)) * d_ref[p]).astype(BF16)
            qq = (qp * qd_ref[p]).astype(BF16)
            st = st_ref[p]
            lhs = jnp.concatenate([jnp.concatenate([qq, qq], axis=0), sd], axis=1)
            rhs = jnp.concatenate([st.astype(BF16), vp], axis=0)
            o2 = _dot(lhs, rhs)
            o = jnp.where(first, o2[:CHUNK], o2[CHUNK:])
            upd = _dot_tn((kp * kd_ref[p]).astype(BF16), vp)
            st_ref[p] = st * gs_ref[p] + upd * bm_ref[...]
            oo = o * o
            ss_a = jnp.sum(jnp.where(first, oo, 0.0), axis=-1, keepdims=True)
            ss_b = jnp.sum(jnp.where(first, 0.0, oo), axis=-1, keepdims=True)
            rs = jnp.where(first, lax.rsqrt(ss_a * (1.0 / RET_DV) + RMS_EPS),
                           lax.rsqrt(ss_b * (1.0 / RET_DV) + RMS_EPS))
            g = g_ref[rows, cols]
            o_ref[rows, cols] = (o * rs * rn_ref[:, cols] * (g * jax.nn.sigmoid(g))).astype(BF16)
        return carry

    lax.fori_loop(0, n_chunks, chunk, 0, unroll=min(n_chunks, 4))

    @pl.when(i == pl.num_programs(1) - 1)
    def _():
        so_ref[0] = st_ref[...]


def _retention(qr, kr, vr, gr, s0, tabs, rn, nb, seq, row0, rb):
    nblk = seq // rb
    off = row0 // rb
    tok = pl.BlockSpec((rb, RET_W), lambda b, i: (off + b * nblk + i, 0))
    full = lambda a: pl.BlockSpec(a.shape, lambda b, i: (0,) * a.ndim)
    st_spec = pl.BlockSpec((1, N_PAIRS, LANES, LANES), lambda b, i: (b, 0, 0, 0))
    d, qd, kd, gs, bm = tabs
    return pl.pallas_call(
        functools.partial(_ret_kernel, n_chunks=rb // CHUNK), grid=(nb, nblk),
        in_specs=[tok, tok, tok, tok, st_spec, full(d), full(qd), full(kd), full(gs), full(bm), full(rn)],
        out_specs=(pl.BlockSpec((rb, RET_W), lambda b, i: (b * nblk + i, 0)), st_spec),
        out_shape=(jax.ShapeDtypeStruct((nb * seq, RET_W), BF16),
                   jax.ShapeDtypeStruct((nb, N_PAIRS, LANES, LANES), F32)),
        scratch_shapes=[pltpu.VMEM((N_PAIRS, LANES, LANES), F32)],
        compiler_params=_cparams(("parallel", "arbitrary")), name="retention",
    )(qr, kr, vr, gr, s0, d, qd, kd, gs, bm, rn)


def _attn_kernel(q_ref, k_ref, wuv_ref, o_ref, q_s, s0_s, s1_s, m_s, l_s, acc_s,
                 *, tq, kb, q_pos0, sk_valid):
    i = pl.program_id(1)
    R = MLA_HEADS * tq
    for h in range(MLA_HEADS):
        q_s[h * tq:(h + 1) * tq, :] = q_ref[:, h * QK_PAD:(h + 1) * QK_PAD]
    m_s[...] = jnp.full(m_s.shape, NEG, F32)
    l_s[...] = jnp.zeros(l_s.shape, F32)
    acc_s[...] = jnp.zeros(acc_s.shape, F32)

    chunk_end = lambda t: ((t >> CHUNK_SHIFT) + 1) << CHUNK_SHIFT
    kb_shift = kb.bit_length() - 1
    qstart = q_pos0 + i * tq
    lim_first = jnp.minimum(chunk_end(qstart), sk_valid)
    lim_last = jnp.minimum(chunk_end(qstart + tq - 1), sk_valid)
    n_full = lim_first >> kb_shift
    n_blk = (lim_last + kb - 1) >> kb_shift

    last = n_blk - 1

    def key_block(j):
        return k_ref[pl.ds(pl.multiple_of(j * kb, kb), kb), :]

    def scores(j, s_ref):
        s_ref[...] = _dot_nt(q_s[...], key_block(j))

    def update(j, s_ref, masked):
        s = s_ref[...]
        if masked:
            tok = lax.broadcasted_iota(jnp.int32, (R, 1), 0) & (tq - 1)
            row_lim = jnp.minimum(chunk_end(qstart + tok), sk_valid)
            kidx = j * kb + lax.broadcasted_iota(jnp.int32, (1, kb), 1)
            s = jnp.where(kidx < row_lim, s, NEG)
        m_prev = m_s[...]
        m_next = jnp.maximum(m_prev, jnp.max(s, axis=-1, keepdims=True))
        p = jnp.exp2(s - jnp.tile(m_next, (1, kb // LANES)))
        alpha = jnp.exp2(m_prev - m_next)
        p_lanes = p[:, :LANES]
        for c in range(LANES, kb, LANES):
            p_lanes = p_lanes + p[:, c:c + LANES]
        l_s[...] = alpha * l_s[...] + p_lanes
        acc_s[...] = alpha * acc_s[...] + _dot(p.astype(BF16), key_block(j)[:, :KV_LORA])
        m_s[...] = m_next

    n_pipe = jnp.minimum(n_full, last)
    odd = n_pipe & 1

    @pl.when(odd == 1)
    def _():
        scores(0, s0_s)
        update(0, s0_s, False)

    scores(odd, s0_s)

    def pair(jj, c):
        j = odd + 2 * jj
        scores(j + 1, s1_s)
        update(j, s0_s, False)
        scores(j + 2, s0_s)
        update(j + 1, s1_s, False)
        return c

    lax.fori_loop(0, n_pipe >> 1, pair, 0)
    update(n_pipe, s0_s, True)

    def tail(j, c):
        scores(j, s0_s)
        update(j, s0_s, True)
        return c

    lax.fori_loop(n_pipe + 1, n_blk, tail, 0)

    o_lat = (acc_s[...] / jnp.sum(l_s[...], axis=-1, keepdims=True)).astype(BF16)
    out = None
    for p in range(MLA_HEADS // 2):
        pair = jnp.concatenate([o_lat[(2 * p) * tq:(2 * p + 1) * tq],
                                o_lat[(2 * p + 1) * tq:(2 * p + 2) * tq]], axis=1)
        term = _dot(pair, wuv_ref[p])
        out = term if out is None else out + term
    o_ref[...] = out.astype(BF16)


def _attention(qx, kx, wuv, nb, seq_q, row0, tq, kb, q_pos0, sk, sk_valid):
    nq = seq_q // tq
    off = row0 // tq
    R = MLA_HEADS * tq
    return pl.pallas_call(
        functools.partial(_attn_kernel, tq=tq, kb=kb, q_pos0=q_pos0, sk_valid=sk_valid),
        grid=(nb, nq),
        in_specs=[pl.BlockSpec((tq, MLA_HEADS * QK_PAD), lambda b, i: (off + b * nq + i, 0)),
                  pl.BlockSpec((sk, QK_PAD), lambda b, i: (b, 0)),
                  pl.BlockSpec(wuv.shape, lambda b, i: (0, 0, 0))],
        out_specs=pl.BlockSpec((tq, MLA_W), lambda b, i: (b * nq + i, 0)),
        out_shape=jax.ShapeDtypeStruct((nb * seq_q, MLA_W), BF16),
        scratch_shapes=[pltpu.VMEM((R, QK_PAD), BF16), pltpu.VMEM((R, kb), F32),
                        pltpu.VMEM((R, kb), F32), pltpu.VMEM((R, LANES), F32),
                        pltpu.VMEM((R, LANES), F32), pltpu.VMEM((R, KV_LORA), F32)],
        compiler_params=_cparams(("parallel", "arbitrary")),
        name="attention",
    )(qx, kx, wuv)


def _merge_kernel(h_ref, ap_ref, bp_ref, as_ref, bs_ref, wo_ref, o_ref, *, n_prompt_tiles):
    def project(a_ref, b_ref):
        o_ref[...] = (h_ref[...] + _dot(a_ref[...], wo_ref[:RET_W, :])
                      + _dot(b_ref[...], wo_ref[RET_W:, :]))

    @pl.when(pl.program_id(0) < n_prompt_tiles)
    def _():
        project(ap_ref, bp_ref)

    @pl.when(pl.program_id(0) >= n_prompt_tiles)
    def _():
        project(as_ref, bs_ref)


def _merge(h, a_p, b_p, a_s, b_s, wo, tm):
    T, D = h.shape
    npt = a_p.shape[0] // tm
    row = lambda w: pl.BlockSpec((tm, w), lambda i: (i, 0))
    (a_prm, a_smp), (b_prm, b_smp) = _stream_specs(tm, RET_W, npt), _stream_specs(tm, MLA_W, npt)
    return pl.pallas_call(
        functools.partial(_merge_kernel, n_prompt_tiles=npt), grid=(T // tm,),
        in_specs=[row(D), a_prm, b_prm, a_smp, b_smp, pl.BlockSpec(wo.shape, lambda i: (0, 0))],
        out_specs=row(D), out_shape=jax.ShapeDtypeStruct((T, D), F32),
        compiler_params=_cparams(("parallel",)), name="merge",
    )(h, a_p, b_p, a_s, b_s, wo)


def _ffn_kernel(h_ref, fn_ref, wg_ref, wu_ref, wd_ref, o_ref):
    hh = h_ref[...]
    u = _rms(hh, fn_ref[...]).astype(BF16)
    a = _dot(u, wg_ref[...])
    mid = (a * jax.nn.sigmoid(a) * _dot(u, wu_ref[...])).astype(BF16)
    o_ref[...] = hh + _dot(mid, wd_ref[...])


def _ffn(h, fn, wg, wu, wd, tm):
    T, D = h.shape
    resident = lambda a: pl.BlockSpec(a.shape, lambda i: (0, 0), pipeline_mode=pl.Buffered(1))
    return pl.pallas_call(
        _ffn_kernel, grid=(T // tm,),
        in_specs=[pl.BlockSpec((tm, D), lambda i: (i, 0)), pl.BlockSpec(fn.shape, lambda i: (0, 0)),
                  resident(wg), resident(wu), resident(wd)],
        out_specs=pl.BlockSpec((tm, D), lambda i: (i, 0)),
        out_shape=jax.ShapeDtypeStruct((T, D), F32),
        compiler_params=_cparams(("parallel",)), name="ffn",
    )(h, fn, wg, wu, wd)


def _dense_layer_kernel(h_ref, ap_ref, bp_ref, as_ref, bs_ref, pp_ref, ps_ref, wo_ref, fn_ref,
                        wg_ref, wu_ref, wd_ref, pn_ref, wpg_ref, wpp_ref, o_ref, *, n_prompt_tiles):
    prompt = pl.program_id(0) < n_prompt_tiles
    a = jnp.where(prompt, ap_ref[...], as_ref[...])
    b = jnp.where(prompt, bp_ref[...], bs_ref[...])
    h1 = h_ref[...] + _dot(a, wo_ref[:RET_W, :]) + _dot(b, wo_ref[RET_W:, :])
    u = _rms(h1, fn_ref[...]).astype(BF16)
    g = _dot(u, wg_ref[...])
    mid = (g * jax.nn.sigmoid(g) * _dot(u, wu_ref[...])).astype(BF16)
    h2 = h1 + _dot(mid, wd_ref[...])
    gate = jax.nn.sigmoid(_dot(_rms(h2, pn_ref[...]).astype(BF16), wpg_ref[...]))
    p = jnp.where(prompt, pp_ref[...], ps_ref[...]).astype(BF16)
    o_ref[...] = h2 + gate * _dot(p, wpp_ref[...])


def _dense_layer(h, a_p, b_p, a_s, b_s, p_p, p_s, wo, fn, wg, wu, wd, pn, wpg, wpp, tm):
    T, D = h.shape
    npt = a_p.shape[0] // tm
    row = pl.BlockSpec((tm, D), lambda i: (i, 0))
    resident = lambda w: pl.BlockSpec(w.shape, lambda i: (0, 0), pipeline_mode=pl.Buffered(1))
    (a_prm, a_smp), (b_prm, b_smp) = _stream_specs(tm, RET_W, npt), _stream_specs(tm, MLA_W, npt)
    p_prm, p_smp = _stream_specs(tm, p_p.shape[1], npt)
    return pl.pallas_call(
        functools.partial(_dense_layer_kernel, n_prompt_tiles=npt), grid=(T // tm,),
        in_specs=[row, a_prm, b_prm, a_smp, b_smp, p_prm, p_smp] + [
            resident(w) for w in (wo, fn, wg, wu, wd, pn, wpg, wpp)],
        out_specs=row, out_shape=jax.ShapeDtypeStruct((T, D), F32),
        compiler_params=_cparams(("parallel",)), name="dense_layer",
    )(h, a_p, b_p, a_s, b_s, p_p, p_s, wo, fn, wg, wu, wd, pn, wpg, wpp)


L_E1, L_E2, L_W1, L_W2, L_R1, L_R2 = range(6)


def _router_kernel(h_ref, fn_ref, wr_ref, meta_ref, cnt_ref, run_s, *, n_experts):
    tm = h_ref.shape[0]

    @pl.when(pl.program_id(0) == 0)
    def _():
        run_s[...] = jnp.zeros(run_s.shape, F32)

    u = _rms(h_ref[...], fn_ref[...])
    logits = jnp.dot(u, wr_ref[...], precision=lax.Precision.HIGHEST, preferred_element_type=F32)
    lane = lax.broadcasted_iota(jnp.int32, logits.shape, 1).astype(F32)
    lg = jnp.where(lane < n_experts, logits, NEG)
    m1 = jnp.max(lg, axis=-1, keepdims=True)
    i1 = jnp.min(jnp.where(lg == m1, lane, float(LANES)), axis=-1, keepdims=True)
    lg2 = jnp.where(lane == i1, NEG, lg)
    m2 = jnp.max(lg2, axis=-1, keepdims=True)
    i2 = jnp.min(jnp.where(lg2 == m2, lane, float(LANES)), axis=-1, keepdims=True)
    e2 = jnp.exp(m2 - m1)
    den = 1.0 + e2

    hit = jnp.where((lane == i1) | (lane == i2), 1.0, 0.0)
    r_i = lax.broadcasted_iota(jnp.int32, (tm, tm), 0)
    c_i = lax.broadcasted_iota(jnp.int32, (tm, tm), 1)
    before = jnp.where(c_i < r_i, 1.0, 0.0).astype(BF16)
    prefix = _dot(before, hit.astype(BF16)) + run_s[0:1, :]
    r1 = jnp.sum(jnp.where(lane == i1, prefix, 0.0), axis=-1, keepdims=True)
    r2 = jnp.sum(jnp.where(lane == i2, prefix, 0.0), axis=-1, keepdims=True)
    run_s[...] = run_s[...] + jnp.sum(hit, axis=0, keepdims=True)
    cnt_ref[...] = run_s[...]

    meta = jnp.where(lane == L_E1, i1, 0.0)
    for ln, val in ((L_E2, i2), (L_W1, 1.0 / den), (L_W2, e2 / den), (L_R1, r1), (L_R2, r2)):
        meta = jnp.where(lane == ln, val, meta)
    meta_ref[...] = meta


def _router(h, fn, wr, n_experts, tm):
    T, D = h.shape
    return pl.pallas_call(
        functools.partial(_router_kernel, n_experts=n_experts), grid=(T // tm,),
        in_specs=[pl.BlockSpec((tm, D), lambda i: (i, 0)), pl.BlockSpec(fn.shape, lambda i: (0, 0)),
                  pl.BlockSpec(wr.shape, lambda i: (0, 0))],
        out_specs=(pl.BlockSpec((tm, LANES), lambda i: (i, 0)),
                   pl.BlockSpec((SUBLANES, LANES), lambda i: (0, 0))),
        out_shape=(jax.ShapeDtypeStruct((T, LANES), F32), jax.ShapeDtypeStruct((SUBLANES, LANES), F32)),
        scratch_shapes=[pltpu.VMEM((SUBLANES, LANES), F32)],
        compiler_params=_cparams(("arbitrary",)), name="router",
    )(h, fn, wr)


def _tile_index_copy(d_hbm, idx_s, sem, tile, slot):
    n = d_hbm.shape[1]
    return pltpu.make_async_copy(d_hbm.at[tile], idx_s.at[pl.ds(pl.multiple_of(slot * n, n), n)],
                                 sem.at[slot])


def _dispatch_kernel(fill_ref, d_hbm, h_ref, xs_out, idx_s, zero_s, isem, rsem, fsem):
    tm = h_ref.shape[0]
    tmg = zero_s.shape[0]
    i, n = pl.program_id(0), pl.num_programs(0)
    slot = i % 2

    def fill_copy(k):
        start = pl.multiple_of(fill_ref[k], tmg)
        return pltpu.make_async_copy(zero_s, xs_out.at[pl.ds(start, tmg)], fsem)

    @pl.when(i == 0)
    def _():
        _tile_index_copy(d_hbm, idx_s, isem, 0, 0).start()
        zero_s[...] = jnp.zeros(zero_s.shape, F32)
        for k in range(fill_ref.shape[0]):
            pl.when(fill_ref[k] >= 0)(lambda k=k: fill_copy(k).start(priority=k % 2))
        for k in range(fill_ref.shape[0]):
            pl.when(fill_ref[k] >= 0)(lambda k=k: fill_copy(k).wait())

    @pl.when(i + 1 < n)
    def _():
        _tile_index_copy(d_hbm, idx_s, isem, i + 1, 1 - slot).start()

    _tile_index_copy(d_hbm, idx_s, isem, i, slot).wait()

    def row_copy(r, dst):
        return pltpu.make_async_copy(h_ref.at[pl.ds(r, 1)], xs_out.at[pl.ds(dst, 1)], rsem)

    def body(g, c):
        r0 = pl.multiple_of(g * SUBLANES, SUBLANES)
        for u in range(SUBLANES):
            for k in range(TOP_K):
                row_copy(r0 + u, idx_s[slot * (TOP_K * tm) + k * tm + r0 + u]).start(priority=k % 2)
        return c

    lax.fori_loop(0, tm // SUBLANES, body, 0)
    for k in range(TOP_K):
        pltpu.make_async_copy(h_ref, xs_out.at[pl.ds(0, tm)], rsem).wait()


def _dispatch(fill_starts, d_tiles, h, n_rows, tm, tmg):
    T, D = h.shape
    grid_spec = pltpu.PrefetchScalarGridSpec(
        num_scalar_prefetch=1, grid=(T // tm,),
        in_specs=[pl.BlockSpec(memory_space=pl.ANY), pl.BlockSpec((tm, D), lambda i, fs: (i, 0))],
        out_specs=pl.BlockSpec(memory_space=pl.ANY),
        scratch_shapes=[pltpu.SMEM((2 * TOP_K * tm,), jnp.int32), pltpu.VMEM((tmg, D), F32),
                        pltpu.SemaphoreType.DMA((2,)), pltpu.SemaphoreType.DMA(()),
                        pltpu.SemaphoreType.DMA(())])
    return pl.pallas_call(
        _dispatch_kernel, grid_spec=grid_spec, out_shape=jax.ShapeDtypeStruct((n_rows, D), F32),
        compiler_params=_cparams(("arbitrary",)), name="dispatch",
    )(fill_starts, d_tiles, h)


def _gmm_kernel(te_ref, nu_ref, x_ref, fn_ref, wg_ref, wu_ref, wd_ref, o_ref, u_s):
    i, f = pl.program_id(0), pl.program_id(1)
    used = i < nu_ref[0]

    @pl.when(f == 0)
    def _():
        o_ref[...] = jnp.zeros(o_ref.shape, F32)

    @pl.when(used & (f == 0))
    def _():
        u_s[...] = _rms(x_ref[...], fn_ref[...]).astype(BF16)

    @pl.when(used)
    def _():
        u = u_s[...]
        a = _dot(u, wg_ref[0].astype(BF16))
        mid = (a * jax.nn.sigmoid(a) * _dot(u, wu_ref[0].astype(BF16))).astype(BF16)
        o_ref[...] += _dot(mid, wd_ref[0].astype(BF16))


def _gmm(tile_expert, n_used, xs, fn, wg, wu, wd, tmg, tf):
    R, D = xs.shape
    F = wg.shape[-1]
    nf = F // tf
    f_eff = lambda i, f, nu: jnp.where(i < nu[0], f, nf - 1)
    grid_spec = pltpu.PrefetchScalarGridSpec(
        num_scalar_prefetch=2, grid=(R // tmg, nf),
        in_specs=[pl.BlockSpec((tmg, D), lambda i, f, te, nu: (i, 0)),
                  pl.BlockSpec(fn.shape, lambda i, f, te, nu: (0, 0)),
                  pl.BlockSpec((1, D, tf), lambda i, f, te, nu: (te[i], 0, f_eff(i, f, nu))),
                  pl.BlockSpec((1, D, tf), lambda i, f, te, nu: (te[i], 0, f_eff(i, f, nu))),
                  pl.BlockSpec((1, tf, D), lambda i, f, te, nu: (te[i], f_eff(i, f, nu), 0))],
        out_specs=pl.BlockSpec((tmg, D), lambda i, f, te, nu: (i, 0)),
        scratch_shapes=[pltpu.VMEM((tmg, D), BF16)])
    return pl.pallas_call(
        _gmm_kernel, grid_spec=grid_spec, out_shape=jax.ShapeDtypeStruct((R, D), F32),
        compiler_params=_cparams(("arbitrary", "arbitrary")), name="experts",
    )(tile_expert, n_used, xs, fn, wg, wu, wd)


def _combine_kernel(d_hbm, h_ref, meta_ref, ys_hbm, o_ref, idx_s, ya_s, yb_s, isem, rsem):
    tm = h_ref.shape[0]
    i, n = pl.program_id(0), pl.num_programs(0)
    slot = i % 2

    def fetch(tile, s):
        _tile_index_copy(d_hbm, idx_s, isem, tile, s).wait()

        def body(g, c):
            r0 = pl.multiple_of(g * SUBLANES, SUBLANES)
            base = s * (TOP_K * tm) + r0
            for u in range(SUBLANES):
                pltpu.make_async_copy(ys_hbm.at[pl.ds(idx_s[base + u], 1)],
                                      ya_s.at[s, pl.ds(r0 + u, 1)], rsem.at[s]).start(priority=0)
                pltpu.make_async_copy(ys_hbm.at[pl.ds(idx_s[base + tm + u], 1)],
                                      yb_s.at[s, pl.ds(r0 + u, 1)], rsem.at[s]).start(priority=1)
            return c

        lax.fori_loop(0, tm // SUBLANES, body, 0)

    @pl.when(i == 0)
    def _():
        _tile_index_copy(d_hbm, idx_s, isem, 0, 0).start()
        fetch(0, 0)

        @pl.when(n > 1)
        def _():
            _tile_index_copy(d_hbm, idx_s, isem, 1, 1).start()

    @pl.when(i + 1 < n)
    def _():
        fetch(i + 1, 1 - slot)

    @pl.when(i + 2 < n)
    def _():
        _tile_index_copy(d_hbm, idx_s, isem, i + 2, slot).start()

    pltpu.make_async_copy(ys_hbm.at[pl.ds(0, tm)], ya_s.at[slot], rsem.at[slot]).wait()
    pltpu.make_async_copy(ys_hbm.at[pl.ds(0, tm)], yb_s.at[slot], rsem.at[slot]).wait()
    meta = meta_ref[...]
    o_ref[...] = (h_ref[...] + meta[:, L_W1:L_W1 + 1] * ya_s[slot]
                  + meta[:, L_W2:L_W2 + 1] * yb_s[slot])


def _combine(d_tiles, h, meta, ys, tm):
    T, D = h.shape
    return pl.pallas_call(
        _combine_kernel, grid=(T // tm,),
        in_specs=[pl.BlockSpec(memory_space=pl.ANY), pl.BlockSpec((tm, D), lambda i: (i, 0)),
                  pl.BlockSpec((tm, LANES), lambda i: (i, 0)), pl.BlockSpec(memory_space=pl.ANY)],
        out_specs=pl.BlockSpec((tm, D), lambda i: (i, 0)),
        out_shape=jax.ShapeDtypeStruct((T, D), F32),
        scratch_shapes=[pltpu.SMEM((2 * TOP_K * tm,), jnp.int32), pltpu.VMEM((2, tm, D), F32),
                        pltpu.VMEM((2, tm, D), F32), pltpu.SemaphoreType.DMA((2,)),
                        pltpu.SemaphoreType.DMA((2,))],
        compiler_params=_cparams(("arbitrary",)), name="combine",
    )(d_tiles, h, meta, ys)


def _moe(h, fn, w_router, wg, wu, wd, tm, tmg, tf):
    T, D = h.shape
    E = w_router.shape[-1]
    meta, counts = _router(h, fn, jnp.pad(w_router, ((0, 0), (0, LANES - E))), E, tm)
    cnt = counts[0, :E].astype(jnp.int32)
    padded = ((cnt + tmg - 1) // tmg) * tmg
    ends = jnp.cumsum(padded)
    off = ends - padded
    n_tiles = -(-TOP_K * T // tmg) + E
    tile_expert = jnp.minimum(
        jnp.sum(ends[None, :] <= (jnp.arange(n_tiles, dtype=jnp.int32) * tmg)[:, None], axis=1), E - 1
    ).astype(jnp.int32)
    n_used = (ends[-1] // tmg).astype(jnp.int32).reshape(1)
    e12 = meta[:, L_E1:L_E2 + 1].astype(jnp.int32)
    off12 = jnp.sum(jnp.where(e12[..., None] == jnp.arange(E, dtype=jnp.int32), off, 0), axis=-1)
    dest = off12 + meta[:, L_R1:L_R2 + 1].astype(jnp.int32)
    d_tiles = dest.reshape(T // tm, tm, TOP_K).transpose(0, 2, 1).reshape(T // tm, TOP_K * tm)

    trailing = ends[-1] + jnp.arange(E + 1, dtype=jnp.int32) * tmg
    fill_starts = jnp.concatenate([
        jnp.where(padded > 0, ends - tmg, -1),
        jnp.where(trailing < n_tiles * tmg, trailing, -1)]).astype(jnp.int32)
    xs = _dispatch(fill_starts, d_tiles, h, n_tiles * tmg, tm, tmg)
    ys = _gmm(tile_expert, n_used, xs, fn, wg, wu, wd, tmg, tf)
    return _combine(d_tiles, h, meta, ys, tm)


def _ple_kernel(h_ref, pn_ref, wg_ref, wp_ref, fin_ref, *refs, final, n_prompt_tiles):
    *p_refs, o_ref = refs
    if len(p_refs) == 2:
        p = jnp.where(pl.program_id(0) < n_prompt_tiles, p_refs[0][...], p_refs[1][...])
    else:
        p = p_refs[0][...]
    hh = h_ref[...]
    gate = jax.nn.sigmoid(_dot(_rms(hh, pn_ref[...]).astype(BF16), wg_ref[...]))
    out = hh + gate * _dot(p.astype(BF16), wp_ref[...])
    if final:
        out = _rms(out, fin_ref[...])
    o_ref[...] = out


def _ple(h, pn, wg, p, wp, fin, tm, final, row0=0, rows=None):
    D = h.shape[1]
    rows = h.shape[0] if rows is None else rows
    off = row0 // tm
    full = lambda a: pl.BlockSpec(a.shape, lambda i: (0,) * a.ndim)
    if isinstance(p, tuple):
        npt = p[0].shape[0] // tm
        p_specs = list(_stream_specs(tm, p[0].shape[1], npt))
    else:
        npt, p_specs, p = 0, [pl.BlockSpec((tm, p.shape[1]), lambda i: (i, 0))], (p,)
    return pl.pallas_call(
        functools.partial(_ple_kernel, final=final, n_prompt_tiles=npt), grid=(rows // tm,),
        in_specs=[pl.BlockSpec((tm, D), lambda i: (off + i, 0)), full(pn), full(wg), full(wp),
                  full(fin)] + p_specs,
        out_specs=pl.BlockSpec((tm, D), lambda i: (i, 0)),
        out_shape=jax.ShapeDtypeStruct((rows, D), F32),
        compiler_params=_cparams(("parallel",)), name="ple",
    )(h, pn, wg, wp, fin, *p)


def _rope_tables(pos):
    pos = pos.astype(F32)[:, None]
    lane = np.arange(LANES)
    inv32 = ROPE_THETA ** (-jnp.arange(32, dtype=F32) / 32)
    inv16 = ROPE_THETA ** (-jnp.arange(16, dtype=F32) / 16)
    a32, a16 = pos * inv32[None, :], pos * inv16[None, :]
    wide = lambda t: jnp.tile(t, (1, LANES // t.shape[1]))
    lo64 = jnp.asarray((lane % 64) < 32)
    c64, s64 = wide(jnp.cos(a32)), wide(jnp.sin(a32))
    c16, s16 = wide(jnp.cos(a16)), wide(jnp.sin(a16))
    in_k = jnp.asarray(lane < MLA_ROPE)
    lo_k = jnp.asarray(lane < 16)
    hi_k = jnp.asarray((lane >= 16) & (lane < MLA_ROPE))
    return (c64, jnp.where(lo64, -s64, 0.0), jnp.where(lo64, 0.0, s64), c16, s16,
            jnp.where(in_k, c16, 0.0), jnp.where(lo_k, -s16, 0.0), jnp.where(hi_k, s16, 0.0))


def _ret_tables(L):
    f = np.float32
    log_g = np.log(f(1.0) - np.exp2(f(-5.0) - np.arange(RET_HEADS, dtype=f)))
    idx = np.arange(L, dtype=f)
    dist = np.abs(idx[:, None] - idx[None, :])
    d = np.exp(dist[None] * log_g[:, None, None])
    qdec = np.exp((idx[:, None] + f(1.0)) * log_g[None, :])
    kdec = np.exp((f(L) - f(1.0) - idx)[:, None] * log_g[None, :])
    sdec = np.exp(f(L) * log_g)
    wide = lambda t: np.repeat(t, RET_DK, axis=1).reshape(L, N_PAIRS, LANES).transpose(1, 0, 2)
    bm = np.kron(np.eye(2, dtype=f), np.ones((RET_DK, RET_DV), f))
    gs = np.repeat(sdec, RET_DK).reshape(N_PAIRS, LANES, 1) * bm[None]
    tabs = (d.reshape(N_PAIRS, 2 * L, L), wide(qdec), wide(kdec), gs, bm)
    return tuple(jnp.asarray(t, F32) for t in tabs)


def _state_to_pairs(s):
    B = s.shape[0]
    s = s.reshape(B, N_PAIRS, 2, RET_DK, RET_DV)
    eye = jnp.eye(2, dtype=s.dtype)
    out = s[:, :, :, :, None, :] * eye[None, None, :, None, :, None]
    return out.reshape(B, N_PAIRS, LANES, LANES)


def _pairs_to_state(sp):
    B = sp.shape[0]
    s = sp.reshape(B, N_PAIRS, 2, RET_DK, 2, RET_DV)
    return jnp.stack([s[:, :, 0, :, 0, :], s[:, :, 1, :, 1, :]], axis=2).reshape(B, RET_HEADS, RET_DK, RET_DV)


def _group_uq_columns(w_uq):
    half = MLA_ROPE // 2
    w = w_uq.reshape(w_uq.shape[0], MLA_HEADS, MLA_NOPE + MLA_ROPE)
    parts = (w[:, :, :MLA_NOPE], w[:, :, MLA_NOPE:MLA_NOPE + half], w[:, :, MLA_NOPE + half:])
    return jnp.concatenate([p.reshape(w_uq.shape[0], -1) for p in parts], axis=1)


def _big_query_weight(w_uk):
    H, half = MLA_HEADS, MLA_ROPE // 2
    eye = jnp.eye(H, dtype=w_uk.dtype)
    wpad = jnp.pad(w_uk, ((0, 0), (0, 0), (0, QK_PAD - KV_LORA)))
    top = (eye[:, None, :, None] * wpad[:, :, None, :]).reshape(H * MLA_NOPE, H * QK_PAD)
    sel = np.zeros((2 * H * half, H * QK_PAD), np.float32)
    for h in range(H):
        for f in range(half):
            sel[h * half + f, h * QK_PAD + KV_LORA + f] = 1.0
            sel[H * half + h * half + f, h * QK_PAD + KV_LORA + half + f] = 1.0
    return jnp.concatenate([top, jnp.asarray(sel, w_uk.dtype)], axis=0)


def kernel(x_prompt, x_sample, p_prompt, p_sample, cache_ckv, cache_krope, state_ret, attn_norm, w_in, q_norm, w_uq, kv_norm, w_uk, w_uv, ret_norm, w_o, ffn_norm, w_gate_d, w_up_d, w_down_d, w_router, w_gate_e, w_up_e, w_down_e, ple_norm, w_ple_gate, w_ple_proj, final_norm):
    Bp, S, D = x_prompt.shape
    Bs, L, _ = x_sample.shape
    depth = w_in.shape[0]
    P = cache_ckv.shape[2]
    Tp, Ts = Bp * S, Bs * L
    T = Tp + Ts
    assert S % CHUNK == 0 and P % CHUNK == 0 and L == CHUNK and Tp % CHUNK == 0
    assert w_router.shape[-1] >= TOP_K

    tm_proj = _pick(int(np.gcd(S, Ts)), (512, 256, 128, 64))
    tm_gmm = 1024
    rb = _pick(S, (512, 256, 128, 64))
    tq = _pick(S, (256, 128, 64))
    kb = _pick(S, (512, 256, 128))
    kb_s = 512
    sk_s = P + L
    sk_pad = -(-sk_s // kb_s) * kb_s

    pos = jnp.concatenate([jnp.arange(S, dtype=jnp.int32),
                           P + jnp.tile(jnp.arange(L, dtype=jnp.int32), Bs)])
    rope_tabs = _rope_tables(pos)
    n_pt, pt_per_seq = Tp // tm_proj, S // tm_proj
    tab_tile = lambda i: jnp.where(i < n_pt, i % pt_per_seq, pt_per_seq + i - n_pt)
    ret_tabs = _ret_tables(CHUNK)
    row2 = lambda v: v.reshape(1, -1)

    h = (x_prompt.reshape(Tp, D), x_sample.reshape(Ts, D))
    outs = {k: [] for k in ("ckv_p", "kro_p", "ret_p", "ckv_s", "kro_s", "ret_s")}
    for l in range(depth):
        win = jnp.pad(w_in[l], ((0, 0), (0, IN_COLS_PAD - IN_COLS))).astype(BF16)
        wuq = _group_uq_columns(w_uq[l]).astype(BF16)
        wbig = _big_query_weight(w_uk[l]).astype(BF16)
        qr, kr, vr, gr, qx, ckv, kro, kx, *h_cat = _proj(
            h, row2(attn_norm[l]), win, row2(q_norm[l]), wuq, row2(kv_norm[l]), wbig, rope_tabs,
            tab_tile, tm_proj)
        if h_cat:
            h, = h_cat

        rn = row2(ret_norm[l])
        zero_state = jnp.zeros((Bp, N_PAIRS, LANES, LANES), F32)
        o_ret_p, st_p = _retention(qr, kr, vr, gr, zero_state, ret_tabs, rn, Bp, S, 0, rb)
        o_ret_s, st_s = _retention(qr, kr, vr, gr, _state_to_pairs(state_ret[l].astype(F32)),
                                   ret_tabs, rn, Bs, L, Tp, L)

        wuv = w_uv[l].astype(BF16)
        wuv_big = (jnp.eye(MLA_HEADS, dtype=BF16)[:, None, :, None] * wuv[:, :, None, :]
                   ).reshape(MLA_HEADS // 2, 2 * KV_LORA, MLA_W)
        o_mla_p = _attention(qx, kx, wuv_big, Bp, S, 0, tq, kb, 0, S, S)
        cache_kx = jnp.concatenate(
            [cache_ckv[l], cache_krope[l], jnp.zeros((Bs, P, QK_PAD - KV_LORA - MLA_ROPE), F32)],
            axis=-1).astype(BF16)
        kx_s = jnp.concatenate([cache_kx, kx[Tp:].reshape(Bs, L, QK_PAD),
                                jnp.zeros((Bs, sk_pad - sk_s, QK_PAD), BF16)], axis=1)
        o_mla_s = _attention(qx, kx_s.reshape(Bs * sk_pad, QK_PAD), wuv_big, Bs, L, Tp, L, kb_s, P,
                             sk_pad, sk_s)

        fn = row2(ffn_norm[l])
        j = l // 2
        wo = w_o[l].astype(BF16)
        ple_w = (row2(ple_norm[l]), w_ple_gate[l].astype(BF16))
        wp = w_ple_proj[l].astype(BF16)
        p_l = (p_prompt[l].reshape(Tp, -1), p_sample[l].reshape(Ts, -1))
        last = l == depth - 1
        if l % 2 == 0 and not last:
            h = _dense_layer(h, o_ret_p, o_mla_p, o_ret_s, o_mla_s, *p_l, wo, fn,
                             w_gate_d[j].astype(BF16), w_up_d[j].astype(BF16),
                             w_down_d[j].astype(BF16), *ple_w, wp, tm_proj)
        else:
            h = _merge(h, o_ret_p, o_mla_p, o_ret_s, o_mla_s, wo, tm_proj)
            if l % 2 == 0:
                h = _ffn(h, fn, w_gate_d[j].astype(BF16), w_up_d[j].astype(BF16),
                         w_down_d[j].astype(BF16), tm_proj)
            else:
                tf = _pick(w_gate_e.shape[-1], (512, 256, 128))
                h = _moe(h, fn, w_router[j], w_gate_e[j], w_up_e[j], w_down_e[j], tm_proj, tm_gmm, tf)
            if not last:
                h = _ple(h, *ple_w, p_l, wp, row2(final_norm), tm_proj, False)
            else:
                y_p = _ple(h, *ple_w, p_l[0], wp, row2(final_norm), tm_proj, True, 0, Tp)
                y_s = _ple(h, *ple_w, p_l[1], wp, row2(final_norm), tm_proj, True, Tp, Ts)

        outs["ckv_p"].append(ckv[:Tp].reshape(Bp, S, KV_LORA))
        outs["kro_p"].append(kro[:Tp].reshape(Bp, S, MLA_ROPE))
        outs["ret_p"].append(_pairs_to_state(st_p))
        outs["ckv_s"].append(ckv[Tp:].reshape(Bs, L, KV_LORA))
        outs["kro_s"].append(kro[Tp:].reshape(Bs, L, MLA_ROPE))
        outs["ret_s"].append(_pairs_to_state(st_s))

    return (y_p.reshape(Bp, S, D), y_s.reshape(Bs, L, D),
            jnp.stack(outs["ckv_p"]), jnp.stack(outs["kro_p"]), jnp.stack(outs["ret_p"]),
            jnp.stack(outs["ckv_s"]), jnp.stack(outs["kro_s"]), jnp.stack(outs["ret_s"]))
```

```python
import functools

import numpy as np
import jax
import jax.numpy as jnp
from jax import lax
from jax.experimental import pallas as pl
from jax.experimental.pallas import tpu as pltpu

F32 = jnp.float32
BF16 = jnp.bfloat16

CHUNK = 64
CHUNK_SHIFT = 6
RMS_EPS = 1e-6
ROPE_THETA = 10000.0
RET_HEADS = 8
RET_DK = 64
RET_DV = 64
RET_W = RET_HEADS * RET_DK
MLA_HEADS = 8
MLA_NOPE = 64
MLA_ROPE = 32
MLA_V = 64
Q_LORA = 256
KV_LORA = 128
MLA_W = MLA_HEADS * MLA_V
RET_HALF = RET_DK // 2
ROPE_HALF = MLA_ROPE // 2
TOP_K = 2
N_PAIRS = RET_HEADS // 2
LANES = 128
SUBLANES = 8
QK_PAD = 256
OFF_Q = RET_W
OFF_K = OFF_Q + RET_W
OFF_V = OFF_K + RET_W
OFF_G = OFF_V + RET_W
OFF_CQ = OFF_G + Q_LORA
OFF_CKV = OFF_CQ + KV_LORA
IN_COLS = OFF_CKV + MLA_ROPE
IN_COLS_PAD = OFF_CKV + LANES
NEG = -1e30
VMEM_LIMIT = 56 * 1024 * 1024


def _pick(n, cands):
    for c in cands:
        if n % c == 0:
            return c
    return n


def _cparams(sem, flags=None):
    return pltpu.CompilerParams(dimension_semantics=sem, vmem_limit_bytes=VMEM_LIMIT, flags=flags)


def _rms(x, g):
    return x * lax.rsqrt(jnp.mean(x * x, axis=-1, keepdims=True) + RMS_EPS) * g


def _dot(a, b):
    return jnp.dot(a, b, preferred_element_type=F32)


def _dot_nt(a, b):
    return lax.dot_general(a, b, (((1,), (1,)), ((), ())), preferred_element_type=F32)


def _dot_tn(a, b):
    return lax.dot_general(a, b, (((0,), (0,)), ((), ())), preferred_element_type=F32)


def _proj_kernel(h_ref, *refs):
    _proj_body(h_ref[...], *refs)


def _proj_streams_kernel(hp_ref, hs_ref, *refs, n_prompt_tiles):
    x = jnp.where(pl.program_id(0) < n_prompt_tiles, hp_ref[...], hs_ref[...])
    refs[-1][...] = x
    _proj_body(x, *refs[:-1])


def _proj_body(x, an_ref, win_ref, qn_ref, wuq_ref, kvn_ref, wbig_ref,
               c64_ref, sm64_ref, sp64_ref, c16_ref, s16_ref, ck_ref, smk_ref, spk_ref,
               qr_ref, kr_ref, vr_ref, gr_ref, qx_ref, ckv_ref, kro_ref, kx_ref):
    xn = _rms(x, an_ref[...]).astype(BF16)
    c64, sm64, sp64 = c64_ref[...], sm64_ref[...], sp64_ref[...]

    def rope64(z):
        return (z * c64 + pltpu.roll(z, LANES - RET_HALF, 1) * sm64
                + pltpu.roll(z, RET_HALF, 1) * sp64)

    for lo in range(0, RET_W, QK_PAD):
        zq = _dot(xn, win_ref[:, lo:lo + QK_PAD])
        zk = _dot(xn, win_ref[:, OFF_Q + lo:OFF_Q + lo + QK_PAD])
        for half in range(0, QK_PAD, LANES):
            dst = slice(lo + half, lo + half + LANES)
            qr_ref[:, dst] = rope64(zq[:, half:half + LANES])
            kr_ref[:, dst] = rope64(zk[:, half:half + LANES]) * (RET_DK ** -0.5)

    zkv = _dot(xn, win_ref[:, OFF_CQ:IN_COLS_PAD])
    ckv = _rms(zkv[:, :KV_LORA], kvn_ref[...])
    ckv_ref[...] = ckv
    zk = zkv[:, KV_LORA:]
    kro = (zk * ck_ref[...] + pltpu.roll(zk, LANES - ROPE_HALF, 1) * smk_ref[...]
           + pltpu.roll(zk, ROPE_HALF, 1) * spk_ref[...])
    kro_ref[...] = kro[:, :MLA_ROPE]
    kx_ref[:, :KV_LORA] = ckv.astype(BF16)
    kx_ref[:, KV_LORA:] = kro.astype(BF16)

    cq = _dot(xn, win_ref[:, OFF_G:OFF_CQ])
    q = _dot(_rms(cq, qn_ref[...]).astype(BF16), wuq_ref[...])
    nq = MLA_HEADS * MLA_NOPE
    x1, x2 = q[:, nq:nq + LANES], q[:, nq + LANES:nq + 2 * LANES]
    c16, s16 = c16_ref[...], s16_ref[...]
    scale = (MLA_NOPE + MLA_ROPE) ** -0.5 * float(np.log2(np.e))
    qcat = jnp.concatenate([q[:, :nq], x1 * c16 - x2 * s16, x2 * c16 + x1 * s16], axis=-1)
    qcat = (qcat * scale).astype(BF16)
    half_w = (MLA_HEADS // 2) * QK_PAD
    for g in range(2):
        rows = slice(g * (nq // 2), (g + 1) * (nq // 2))
        cols = slice(g * half_w, (g + 1) * half_w)
        qx_ref[:, cols] = (_dot(qcat[:, rows], wbig_ref[rows, cols])
                           + _dot(qcat[:, nq:], wbig_ref[nq:, cols])).astype(BF16)

    vr_ref[...] = _dot(xn, win_ref[:, OFF_K:OFF_V]).astype(BF16)
    gr_ref[...] = _dot(xn, win_ref[:, OFF_V:OFF_G])


def _stream_specs(tm, width, n_prompt_tiles):
    return (pl.BlockSpec((tm, width), lambda i: (jnp.minimum(i, n_prompt_tiles - 1), 0)),
            pl.BlockSpec((tm, width), lambda i: (jnp.maximum(i - n_prompt_tiles, 0), 0)))


def _proj(h, an, win, qn, wuq, kvn, wbig, tabs, tab_tile, tm):
    streams = isinstance(h, tuple)
    T = sum(a.shape[0] for a in h) if streams else h.shape[0]
    D = h[0].shape[1] if streams else h.shape[1]
    row = lambda w: pl.BlockSpec((tm, w), lambda i: (i, 0))
    tab = pl.BlockSpec((tm, LANES), lambda i: (tab_tile(i), 0))
    full = lambda a: pl.BlockSpec(a.shape, lambda i: (0,) * a.ndim)
    out_shapes = (
        jax.ShapeDtypeStruct((T, RET_W), F32), jax.ShapeDtypeStruct((T, RET_W), F32),
        jax.ShapeDtypeStruct((T, RET_W), BF16), jax.ShapeDtypeStruct((T, RET_W), F32),
        jax.ShapeDtypeStruct((T, MLA_HEADS * QK_PAD), BF16),
        jax.ShapeDtypeStruct((T, KV_LORA), F32), jax.ShapeDtypeStruct((T, MLA_ROPE), F32),
        jax.ShapeDtypeStruct((T, QK_PAD), BF16))
    out_specs = (row(RET_W), row(RET_W), row(RET_W), row(RET_W), row(MLA_HEADS * QK_PAD),
                 row(KV_LORA), row(MLA_ROPE), row(QK_PAD))
    if streams:
        npt = h[0].shape[0] // tm
        kern = functools.partial(_proj_streams_kernel, n_prompt_tiles=npt)
        h_specs, h_args = list(_stream_specs(tm, D, npt)), h
        out_specs += (row(D),)
        out_shapes += (jax.ShapeDtypeStruct((T, D), F32),)
    else:
        kern, h_specs, h_args = _proj_kernel, [row(D)], (h,)
    return pl.pallas_call(
        kern, grid=(T // tm,),
        in_specs=h_specs + [full(an), full(win), full(qn), full(wuq), full(kvn), full(wbig)]
                 + [tab] * 8,
        out_specs=out_specs,
        out_shape=out_shapes, compiler_params=_cparams(("parallel",)), name="proj",
    )(*h_args, an, win, qn, wuq, kvn, wbig, *tabs)


def _ret_kernel(q_ref, k_ref, v_ref, g_ref, s0_ref, d_ref, qd_ref, kd_ref, gs_ref, bm_ref, rn_ref,
                o_ref, so_ref, st_ref, *, n_chunks):
    i = pl.program_id(1)

    @pl.when(i == 0)
    def _():
        st_ref[...] = s0_ref[0]

    lane = lax.broadcasted_iota(jnp.int32, (CHUNK, LANES), 1)
    first = lane < RET_DK

    def chunk(c, carry):
        rows = pl.ds(pl.multiple_of(c * CHUNK, CHUNK), CHUNK)
        for p in range(N_PAIRS):
            cols = slice(p * LANES, (p + 1) * LANES)
            qp, kp, vp = q_ref[rows, cols], k_ref[rows, cols], v_ref[rows, cols]
            q2 = jnp.concatenate([jnp.where(first, qp, 0.0), jnp.where(first, 0.0, qp)], axis=0)
            sd = (_dot_nt(q2.astype(BF16), kp.astype(BF16)) * d_ref[p]).astype(BF16)
            qq = (qp * qd_ref[p]).astype(BF16)
            st = st_ref[p]
            lhs = jnp.concatenate([jnp.concatenate([qq, qq], axis=0), sd], axis=1)
            rhs = jnp.concatenate([st.astype(BF16), vp], axis=0)
            o2 = _dot(lhs, rhs)
            o = jnp.where(first, o2[:CHUNK], o2[CHUNK:])
            upd = _dot_tn((kp * kd_ref[p]).astype(BF16), vp)
            st_ref[p] = st * gs_ref[p] + upd * bm_ref[...]
            oo = o * o
            ss_a = jnp.sum(jnp.where(first, oo, 0.0), axis=-1, keepdims=True)
            ss_b = jnp.sum(jnp.where(first, 0.0, oo), axis=-1, keepdims=True)
            rs = jnp.where(first, lax.rsqrt(ss_a * (1.0 / RET_DV) + RMS_EPS),
                           lax.rsqrt(ss_b * (1.0 / RET_DV) + RMS_EPS))
            g = g_ref[rows, cols]
            o_ref[rows, cols] = (o * rs * rn_ref[:, cols] * (g * jax.nn.sigmoid(g))).astype(BF16)
        return carry

    lax.fori_loop(0, n_chunks, chunk, 0, unroll=min(n_chunks, 4))

    @pl.when(i == pl.num_programs(1) - 1)
    def _():
        so_ref[0] = st_ref[...]


def _retention(qr, kr, vr, gr, s0, tabs, rn, nb, seq, row0, rb):
    nblk = seq // rb
    off = row0 // rb
    tok = pl.BlockSpec((rb, RET_W), lambda b, i: (off + b * nblk + i, 0))
    full = lambda a: pl.BlockSpec(a.shape, lambda b, i: (0,) * a.ndim)
    st_spec = pl.BlockSpec((1, N_PAIRS, LANES, LANES), lambda b, i: (b, 0, 0, 0))
    d, qd, kd, gs, bm = tabs
    return pl.pallas_call(
        functools.partial(_ret_kernel, n_chunks=rb // CHUNK), grid=(nb, nblk),
        in_specs=[tok, tok, tok, tok, st_spec, full(d), full(qd), full(kd), full(gs), full(bm), full(rn)],
        out_specs=(pl.BlockSpec((rb, RET_W), lambda b, i: (b * nblk + i, 0)), st_spec),
        out_shape=(jax.ShapeDtypeStruct((nb * seq, RET_W), BF16),
                   jax.ShapeDtypeStruct((nb, N_PAIRS, LANES, LANES), F32)),
        scratch_shapes=[pltpu.VMEM((N_PAIRS, LANES, LANES), F32)],
        compiler_params=_cparams(("parallel", "arbitrary")), name="retention",
    )(qr, kr, vr, gr, s0, d, qd, kd, gs, bm, rn)


def _attn_kernel(q_ref, k_ref, wuv_ref, o_ref, q_s, s0_s, s1_s, m_s, l_s, acc_s,
                 *, tq, kb, q_pos0, sk_valid):
    i = pl.program_id(1)
    R = MLA_HEADS * tq
    for h in range(MLA_HEADS):
        q_s[h * tq:(h + 1) * tq, :] = q_ref[:, h * QK_PAD:(h + 1) * QK_PAD]
    m_s[...] = jnp.full(m_s.shape, NEG, F32)
    l_s[...] = jnp.zeros(l_s.shape, F32)
    acc_s[...] = jnp.zeros(acc_s.shape, F32)

    chunk_end = lambda t: ((t >> CHUNK_SHIFT) + 1) << CHUNK_SHIFT
    kb_shift = kb.bit_length() - 1
    qstart = q_pos0 + i * tq
    lim_first = jnp.minimum(chunk_end(qstart), sk_valid)
    lim_last = jnp.minimum(chunk_end(qstart + tq - 1), sk_valid)
    n_full = lim_first >> kb_shift
    n_blk = (lim_last + kb - 1) >> kb_shift

    last = n_blk - 1

    def key_block(j):
        return k_ref[pl.ds(pl.multiple_of(j * kb, kb), kb), :]

    def scores(j, s_ref):
        s_ref[...] = _dot_nt(q_s[...], key_block(j))

    def update(j, s_ref, masked):
        s = s_ref[...]
        if masked:
            tok = lax.broadcasted_iota(jnp.int32, (R, 1), 0) & (tq - 1)
            row_lim = jnp.minimum(chunk_end(qstart + tok), sk_valid)
            kidx = j * kb + lax.broadcasted_iota(jnp.int32, (1, kb), 1)
            s = jnp.where(kidx < row_lim, s, NEG)
        m_prev = m_s[...]
        m_next = jnp.maximum(m_prev, jnp.max(s, axis=-1, keepdims=True))
        p = jnp.exp2(s - jnp.tile(m_next, (1, kb // LANES)))
        alpha = jnp.exp2(m_prev - m_next)
        p_lanes = p[:, :LANES]
        for c in range(LANES, kb, LANES):
            p_lanes = p_lanes + p[:, c:c + LANES]
        l_s[...] = alpha * l_s[...] + p_lanes
        acc_s[...] = alpha * acc_s[...] + _dot(p.astype(BF16), key_block(j)[:, :KV_LORA])
        m_s[...] = m_next

    n_pipe = jnp.minimum(n_full, last)
    odd = n_pipe & 1

    @pl.when(odd == 1)
    def _():
        scores(0, s0_s)
        update(0, s0_s, False)

    scores(odd, s0_s)

    def pair(jj, c):
        j = odd + 2 * jj
        scores(j + 1, s1_s)
        update(j, s0_s, False)
        scores(j + 2, s0_s)
        update(j + 1, s1_s, False)
        return c

    lax.fori_loop(0, n_pipe >> 1, pair, 0)
    update(n_pipe, s0_s, True)

    def tail(j, c):
        scores(j, s0_s)
        update(j, s0_s, True)
        return c

    lax.fori_loop(n_pipe + 1, n_blk, tail, 0)

    o_lat = (acc_s[...] / jnp.sum(l_s[...], axis=-1, keepdims=True)).astype(BF16)
    out = None
    for p in range(MLA_HEADS // 2):
        pair = jnp.concatenate([o_lat[(2 * p) * tq:(2 * p + 1) * tq],
                                o_lat[(2 * p + 1) * tq:(2 * p + 2) * tq]], axis=1)
        term = _dot(pair, wuv_ref[p])
        out = term if out is None else out + term
    o_ref[...] = out.astype(BF16)


def _attention(qx, kx, wuv, nb, seq_q, row0, tq, kb, q_pos0, sk, sk_valid):
    nq = seq_q // tq
    off = row0 // tq
    R = MLA_HEADS * tq
    return pl.pallas_call(
        functools.partial(_attn_kernel, tq=tq, kb=kb, q_pos0=q_pos0, sk_valid=sk_valid),
        grid=(nb, nq),
        in_specs=[pl.BlockSpec((tq, MLA_HEADS * QK_PAD), lambda b, i: (off + b * nq + i, 0)),
                  pl.BlockSpec((sk, QK_PAD), lambda b, i: (b, 0)),
                  pl.BlockSpec(wuv.shape, lambda b, i: (0, 0, 0))],
        out_specs=pl.BlockSpec((tq, MLA_W), lambda b, i: (b * nq + i, 0)),
        out_shape=jax.ShapeDtypeStruct((nb * seq_q, MLA_W), BF16),
        scratch_shapes=[pltpu.VMEM((R, QK_PAD), BF16), pltpu.VMEM((R, kb), F32),
                        pltpu.VMEM((R, kb), F32), pltpu.VMEM((R, LANES), F32),
                        pltpu.VMEM((R, LANES), F32), pltpu.VMEM((R, KV_LORA), F32)],
        compiler_params=_cparams(("parallel", "arbitrary")),
        name="attention",
    )(qx, kx, wuv)


def _merge_kernel(h_ref, ap_ref, bp_ref, as_ref, bs_ref, wo_ref, o_ref, *, n_prompt_tiles):
    def project(a_ref, b_ref):
        o_ref[...] = (h_ref[...] + _dot(a_ref[...], wo_ref[:RET_W, :])
                      + _dot(b_ref[...], wo_ref[RET_W:, :]))

    @pl.when(pl.program_id(0) < n_prompt_tiles)
    def _():
        project(ap_ref, bp_ref)

    @pl.when(pl.program_id(0) >= n_prompt_tiles)
    def _():
        project(as_ref, bs_ref)


def _merge(h, a_p, b_p, a_s, b_s, wo, tm):
    T, D = h.shape
    npt = a_p.shape[0] // tm
    row = lambda w: pl.BlockSpec((tm, w), lambda i: (i, 0))
    (a_prm, a_smp), (b_prm, b_smp) = _stream_specs(tm, RET_W, npt), _stream_specs(tm, MLA_W, npt)
    return pl.pallas_call(
        functools.partial(_merge_kernel, n_prompt_tiles=npt), grid=(T // tm,),
        in_specs=[row(D), a_prm, b_prm, a_smp, b_smp, pl.BlockSpec(wo.shape, lambda i: (0, 0))],
        out_specs=row(D), out_shape=jax.ShapeDtypeStruct((T, D), F32),
        compiler_params=_cparams(("parallel",)), name="merge",
    )(h, a_p, b_p, a_s, b_s, wo)


def _ffn_kernel(h_ref, fn_ref, wg_ref, wu_ref, wd_ref, o_ref):
    hh = h_ref[...]
    u = _rms(hh, fn_ref[...]).astype(BF16)
    a = _dot(u, wg_ref[...])
    mid = (a * jax.nn.sigmoid(a) * _dot(u, wu_ref[...])).astype(BF16)
    o_ref[...] = hh + _dot(mid, wd_ref[...])


def _ffn(h, fn, wg, wu, wd, tm):
    T, D = h.shape
    resident = lambda a: pl.BlockSpec(a.shape, lambda i: (0, 0), pipeline_mode=pl.Buffered(1))
    return pl.pallas_call(
        _ffn_kernel, grid=(T // tm,),
        in_specs=[pl.BlockSpec((tm, D), lambda i: (i, 0)), pl.BlockSpec(fn.shape, lambda i: (0, 0)),
                  resident(wg), resident(wu), resident(wd)],
        out_specs=pl.BlockSpec((tm, D), lambda i: (i, 0)),
        out_shape=jax.ShapeDtypeStruct((T, D), F32),
        compiler_params=_cparams(("parallel",)), name="ffn",
    )(h, fn, wg, wu, wd)


def _dense_layer_kernel(h_ref, ap_ref, bp_ref, as_ref, bs_ref, pp_ref, ps_ref, wo_ref, fn_ref,
                        wg_ref, wu_ref, wd_ref, pn_ref, wpg_ref, wpp_ref, o_ref, *, n_prompt_tiles):
    prompt = pl.program_id(0) < n_prompt_tiles
    a = jnp.where(prompt, ap_ref[...], as_ref[...])
    b = jnp.where(prompt, bp_ref[...], bs_ref[...])
    h1 = h_ref[...] + _dot(a, wo_ref[:RET_W, :]) + _dot(b, wo_ref[RET_W:, :])
    u = _rms(h1, fn_ref[...]).astype(BF16)
    g = _dot(u, wg_ref[...])
    mid = (g * jax.nn.sigmoid(g) * _dot(u, wu_ref[...])).astype(BF16)
    h2 = h1 + _dot(mid, wd_ref[...])
    gate = jax.nn.sigmoid(_dot(_rms(h2, pn_ref[...]).astype(BF16), wpg_ref[...]))
    p = jnp.where(prompt, pp_ref[...], ps_ref[...]).astype(BF16)
    o_ref[...] = h2 + gate * _dot(p, wpp_ref[...])


def _dense_layer(h, a_p, b_p, a_s, b_s, p_p, p_s, wo, fn, wg, wu, wd, pn, wpg, wpp, tm):
    T, D = h.shape
    npt = a_p.shape[0] // tm
    row = pl.BlockSpec((tm, D), lambda i: (i, 0))
    resident = lambda w: pl.BlockSpec(w.shape, lambda i: (0, 0), pipeline_mode=pl.Buffered(1))
    (a_prm, a_smp), (b_prm, b_smp) = _stream_specs(tm, RET_W, npt), _stream_specs(tm, MLA_W, npt)
    p_prm, p_smp = _stream_specs(tm, p_p.shape[1], npt)
    return pl.pallas_call(
        functools.partial(_dense_layer_kernel, n_prompt_tiles=npt), grid=(T // tm,),
        in_specs=[row, a_prm, b_prm, a_smp, b_smp, p_prm, p_smp] + [
            resident(w) for w in (wo, fn, wg, wu, wd, pn, wpg, wpp)],
        out_specs=row, out_shape=jax.ShapeDtypeStruct((T, D), F32),
        compiler_params=_cparams(("parallel",)), name="dense_layer",
    )(h, a_p, b_p, a_s, b_s, p_p, p_s, wo, fn, wg, wu, wd, pn, wpg, wpp)


L_E1, L_E2, L_W1, L_W2, L_R1, L_R2 = range(6)


def _router_kernel(h_ref, fn_ref, wr_ref, meta_ref, cnt_ref, run_s, *, n_experts):
    tm = h_ref.shape[0]

    @pl.when(pl.program_id(0) == 0)
    def _():
        run_s[...] = jnp.zeros(run_s.shape, F32)

    u = _rms(h_ref[...], fn_ref[...])
    u_hi = u.astype(BF16)
    u_lo = (u - u_hi.astype(F32)).astype(BF16)
    hi_terms = _dot(u_hi, wr_ref[...])
    logits = hi_terms[:, :LANES] + hi_terms[:, LANES:] + _dot(u_lo, wr_ref[:, :LANES])
    lane = lax.broadcasted_iota(jnp.int32, logits.shape, 1).astype(F32)
    lg = jnp.where(lane < n_experts, logits, NEG)
    m1 = jnp.max(lg, axis=-1, keepdims=True)
    i1 = jnp.min(jnp.where(lg == m1, lane, float(LANES)), axis=-1, keepdims=True)
    lg2 = jnp.where(lane == i1, NEG, lg)
    m2 = jnp.max(lg2, axis=-1, keepdims=True)
    i2 = jnp.min(jnp.where(lg2 == m2, lane, float(LANES)), axis=-1, keepdims=True)
    e2 = jnp.exp(m2 - m1)
    den = 1.0 + e2

    hit = jnp.where((lane == i1) | (lane == i2), 1.0, 0.0)
    r_i = lax.broadcasted_iota(jnp.int32, (tm, tm), 0)
    c_i = lax.broadcasted_iota(jnp.int32, (tm, tm), 1)
    before = jnp.where(c_i < r_i, 1.0, 0.0).astype(BF16)
    prefix = _dot(before, hit.astype(BF16)) + run_s[0:1, :]
    r1 = jnp.sum(jnp.where(lane == i1, prefix, 0.0), axis=-1, keepdims=True)
    r2 = jnp.sum(jnp.where(lane == i2, prefix, 0.0), axis=-1, keepdims=True)
    run_s[...] = run_s[...] + jnp.sum(hit, axis=0, keepdims=True)
    cnt_ref[...] = run_s[...]

    meta = jnp.where(lane == L_E1, i1, 0.0)
    for ln, val in ((L_E2, i2), (L_W1, 1.0 / den), (L_W2, e2 / den), (L_R1, r1), (L_R2, r2)):
        meta = jnp.where(lane == ln, val, meta)
    meta_ref[...] = meta


def _router(h, fn, wr, n_experts, tm):
    T, D = h.shape
    return pl.pallas_call(
        functools.partial(_router_kernel, n_experts=n_experts), grid=(T // tm,),
        in_specs=[pl.BlockSpec((tm, D), lambda i: (i, 0)), pl.BlockSpec(fn.shape, lambda i: (0, 0)),
                  pl.BlockSpec(wr.shape, lambda i: (0, 0))],
        out_specs=(pl.BlockSpec((tm, LANES), lambda i: (i, 0)),
                   pl.BlockSpec((SUBLANES, LANES), lambda i: (0, 0))),
        out_shape=(jax.ShapeDtypeStruct((T, LANES), F32), jax.ShapeDtypeStruct((SUBLANES, LANES), F32)),
        scratch_shapes=[pltpu.VMEM((SUBLANES, LANES), F32)],
        compiler_params=_cparams(("arbitrary",)), name="router",
    )(h, fn, wr)


def _tile_index_copy(d_hbm, idx_s, sem, tile, slot):
    n = d_hbm.shape[1]
    return pltpu.make_async_copy(d_hbm.at[tile], idx_s.at[pl.ds(pl.multiple_of(slot * n, n), n)],
                                 sem.at[slot])


def _dispatch_kernel(fill_ref, d_hbm, h_ref, xs_out, idx_s, zero_s, isem, rsem, fsem):
    tm = h_ref.shape[0]
    tmg = zero_s.shape[0]
    i, n = pl.program_id(0), pl.num_programs(0)
    slot = i % 2

    def fill_copy(k):
        start = pl.multiple_of(fill_ref[k], tmg)
        return pltpu.make_async_copy(zero_s, xs_out.at[pl.ds(start, tmg)], fsem)

    @pl.when(i == 0)
    def _():
        _tile_index_copy(d_hbm, idx_s, isem, 0, 0).start()
        zero_s[...] = jnp.zeros(zero_s.shape, F32)
        for k in range(fill_ref.shape[0]):
            pl.when(fill_ref[k] >= 0)(lambda k=k: fill_copy(k).start(priority=k % 2))
        for k in range(fill_ref.shape[0]):
            pl.when(fill_ref[k] >= 0)(lambda k=k: fill_copy(k).wait())

    @pl.when(i + 1 < n)
    def _():
        _tile_index_copy(d_hbm, idx_s, isem, i + 1, 1 - slot).start()

    _tile_index_copy(d_hbm, idx_s, isem, i, slot).wait()

    def row_copy(r, dst):
        return pltpu.make_async_copy(h_ref.at[pl.ds(r, 1)], xs_out.at[pl.ds(dst, 1)], rsem)

    def body(g, c):
        r0 = pl.multiple_of(g * SUBLANES, SUBLANES)
        for u in range(SUBLANES):
            for k in range(TOP_K):
                row_copy(r0 + u, idx_s[slot * (TOP_K * tm) + k * tm + r0 + u]).start(priority=k % 2)
        return c

    lax.fori_loop(0, tm // SUBLANES, body, 0)
    for k in range(TOP_K):
        pltpu.make_async_copy(h_ref, xs_out.at[pl.ds(0, tm)], rsem).wait()


def _dispatch(fill_starts, d_tiles, h, n_rows, tm, tmg):
    T, D = h.shape
    grid_spec = pltpu.PrefetchScalarGridSpec(
        num_scalar_prefetch=1, grid=(T // tm,),
        in_specs=[pl.BlockSpec(memory_space=pl.ANY), pl.BlockSpec((tm, D), lambda i, fs: (i, 0))],
        out_specs=pl.BlockSpec(memory_space=pl.ANY),
        scratch_shapes=[pltpu.SMEM((2 * TOP_K * tm,), jnp.int32), pltpu.VMEM((tmg, D), F32),
                        pltpu.SemaphoreType.DMA((2,)), pltpu.SemaphoreType.DMA(()),
                        pltpu.SemaphoreType.DMA(())])
    return pl.pallas_call(
        _dispatch_kernel, grid_spec=grid_spec, out_shape=jax.ShapeDtypeStruct((n_rows, D), F32),
        compiler_params=_cparams(("arbitrary",)), name="dispatch",
    )(fill_starts, d_tiles, h)


def _gmm_kernel(te_ref, nu_ref, x_ref, fn_ref, wg_ref, wu_ref, wd_ref, o_ref, u_s):
    i, f = pl.program_id(0), pl.program_id(1)
    used = i < nu_ref[0]

    @pl.when(f == 0)
    def _():
        o_ref[...] = jnp.zeros(o_ref.shape, F32)

    @pl.when(used & (f == 0))
    def _():
        u_s[...] = _rms(x_ref[...], fn_ref[...]).astype(BF16)

    @pl.when(used)
    def _():
        u = u_s[...]
        a = _dot(u, wg_ref[0].astype(BF16))
        mid = (a * jax.nn.sigmoid(a) * _dot(u, wu_ref[0].astype(BF16))).astype(BF16)
        o_ref[...] += _dot(mid, wd_ref[0].astype(BF16))


def _gmm(tile_expert, n_used, xs, fn, wg, wu, wd, tmg, tf):
    R, D = xs.shape
    F = wg.shape[-1]
    nf = F // tf
    f_eff = lambda i, f, nu: jnp.where(i < nu[0], f, nf - 1)
    grid_spec = pltpu.PrefetchScalarGridSpec(
        num_scalar_prefetch=2, grid=(R // tmg, nf),
        in_specs=[pl.BlockSpec((tmg, D), lambda i, f, te, nu: (i, 0)),
                  pl.BlockSpec(fn.shape, lambda i, f, te, nu: (0, 0)),
                  pl.BlockSpec((1, D, tf), lambda i, f, te, nu: (te[i], 0, f_eff(i, f, nu))),
                  pl.BlockSpec((1, D, tf), lambda i, f, te, nu: (te[i], 0, f_eff(i, f, nu))),
                  pl.BlockSpec((1, tf, D), lambda i, f, te, nu: (te[i], f_eff(i, f, nu), 0))],
        out_specs=pl.BlockSpec((tmg, D), lambda i, f, te, nu: (i, 0)),
        scratch_shapes=[pltpu.VMEM((tmg, D), BF16)])
    return pl.pallas_call(
        _gmm_kernel, grid_spec=grid_spec, out_shape=jax.ShapeDtypeStruct((R, D), F32),
        compiler_params=_cparams(("arbitrary", "arbitrary")), name="experts",
    )(tile_expert, n_used, xs, fn, wg, wu, wd)


def _combine_kernel(d_hbm, h_ref, meta_ref, ys_hbm, o_ref, idx_s, ya_s, yb_s, isem, rsem):
    tm = h_ref.shape[0]
    i, n = pl.program_id(0), pl.num_programs(0)
    slot = i % 2

    def fetch(tile, s):
        _tile_index_copy(d_hbm, idx_s, isem, tile, s).wait()

        def body(g, c):
            r0 = pl.multiple_of(g * SUBLANES, SUBLANES)
            base = s * (TOP_K * tm) + r0
            for u in range(SUBLANES):
                pltpu.make_async_copy(ys_hbm.at[pl.ds(idx_s[base + u], 1)],
                                      ya_s.at[s, pl.ds(r0 + u, 1)], rsem.at[s]).start(priority=0)
                pltpu.make_async_copy(ys_hbm.at[pl.ds(idx_s[base + tm + u], 1)],
                                      yb_s.at[s, pl.ds(r0 + u, 1)], rsem.at[s]).start(priority=1)
            return c

        lax.fori_loop(0, tm // SUBLANES, body, 0)

    @pl.when(i == 0)
    def _():
        _tile_index_copy(d_hbm, idx_s, isem, 0, 0).start()
        fetch(0, 0)

        @pl.when(n > 1)
        def _():
            _tile_index_copy(d_hbm, idx_s, isem, 1, 1).start()

    @pl.when(i + 1 < n)
    def _():
        fetch(i + 1, 1 - slot)

    @pl.when(i + 2 < n)
    def _():
        _tile_index_copy(d_hbm, idx_s, isem, i + 2, slot).start()

    pltpu.make_async_copy(ys_hbm.at[pl.ds(0, tm)], ya_s.at[slot], rsem.at[slot]).wait()
    pltpu.make_async_copy(ys_hbm.at[pl.ds(0, tm)], yb_s.at[slot], rsem.at[slot]).wait()
    meta = meta_ref[...]
    o_ref[...] = (h_ref[...] + meta[:, L_W1:L_W1 + 1] * ya_s[slot]
                  + meta[:, L_W2:L_W2 + 1] * yb_s[slot])


def _combine(d_tiles, h, meta, ys, tm):
    T, D = h.shape
    return pl.pallas_call(
        _combine_kernel, grid=(T // tm,),
        in_specs=[pl.BlockSpec(memory_space=pl.ANY), pl.BlockSpec((tm, D), lambda i: (i, 0)),
                  pl.BlockSpec((tm, LANES), lambda i: (i, 0)), pl.BlockSpec(memory_space=pl.ANY)],
        out_specs=pl.BlockSpec((tm, D), lambda i: (i, 0)),
        out_shape=jax.ShapeDtypeStruct((T, D), F32),
        scratch_shapes=[pltpu.SMEM((2 * TOP_K * tm,), jnp.int32), pltpu.VMEM((2, tm, D), F32),
                        pltpu.VMEM((2, tm, D), F32), pltpu.SemaphoreType.DMA((2,)),
                        pltpu.SemaphoreType.DMA((2,))],
        compiler_params=_cparams(("arbitrary",)), name="combine",
    )(d_tiles, h, meta, ys)


def _moe(h, fn, w_router, wg, wu, wd, tm, tmg, tf):
    T, D = h.shape
    E = w_router.shape[-1]
    wr = jnp.pad(w_router, ((0, 0), (0, LANES - E)))
    wr_hi = wr.astype(BF16)
    wr_lo = (wr - wr_hi.astype(F32)).astype(BF16)
    meta, counts = _router(h, fn, jnp.concatenate([wr_hi, wr_lo], axis=1), E, tm)
    cnt = counts[0, :E].astype(jnp.int32)
    padded = ((cnt + tmg - 1) // tmg) * tmg
    ends = jnp.cumsum(padded)
    off = ends - padded
    n_tiles = -(-TOP_K * T // tmg) + E
    tile_expert = jnp.minimum(
        jnp.sum(ends[None, :] <= (jnp.arange(n_tiles, dtype=jnp.int32) * tmg)[:, None], axis=1), E - 1
    ).astype(jnp.int32)
    n_used = (ends[-1] // tmg).astype(jnp.int32).reshape(1)
    e12 = meta[:, L_E1:L_E2 + 1].astype(jnp.int32)
    off12 = jnp.sum(jnp.where(e12[..., None] == jnp.arange(E, dtype=jnp.int32), off, 0), axis=-1)
    dest = off12 + meta[:, L_R1:L_R2 + 1].astype(jnp.int32)
    d_tiles = dest.reshape(T // tm, tm, TOP_K).transpose(0, 2, 1).reshape(T // tm, TOP_K * tm)

    trailing = ends[-1] + jnp.arange(E + 1, dtype=jnp.int32) * tmg
    fill_starts = jnp.concatenate([
        jnp.where(padded > 0, ends - tmg, -1),
        jnp.where(trailing < n_tiles * tmg, trailing, -1)]).astype(jnp.int32)
    xs = _dispatch(fill_starts, d_tiles, h, n_tiles * tmg, tm, tmg)
    ys = _gmm(tile_expert, n_used, xs, fn, wg, wu, wd, tmg, tf)
    return _combine(d_tiles, h, meta, ys, tm)


def _ple_kernel(h_ref, pn_ref, wg_ref, wp_ref, fin_ref, *refs, final, n_prompt_tiles):
    *p_refs, o_ref = refs
    if len(p_refs) == 2:
        p = jnp.where(pl.program_id(0) < n_prompt_tiles, p_refs[0][...], p_refs[1][...])
    else:
        p = p_refs[0][...]
    hh = h_ref[...]
    gate = jax.nn.sigmoid(_dot(_rms(hh, pn_ref[...]).astype(BF16), wg_ref[...]))
    out = hh + gate * _dot(p.astype(BF16), wp_ref[...])
    if final:
        out = _rms(out, fin_ref[...])
    o_ref[...] = out


def _ple(h, pn, wg, p, wp, fin, tm, final, row0=0, rows=None):
    D = h.shape[1]
    rows = h.shape[0] if rows is None else rows
    off = row0 // tm
    full = lambda a: pl.BlockSpec(a.shape, lambda i: (0,) * a.ndim)
    if isinstance(p, tuple):
        npt = p[0].shape[0] // tm
        p_specs = list(_stream_specs(tm, p[0].shape[1], npt))
    else:
        npt, p_specs, p = 0, [pl.BlockSpec((tm, p.shape[1]), lambda i: (i, 0))], (p,)
    return pl.pallas_call(
        functools.partial(_ple_kernel, final=final, n_prompt_tiles=npt), grid=(rows // tm,),
        in_specs=[pl.BlockSpec((tm, D), lambda i: (off + i, 0)), full(pn), full(wg), full(wp),
                  full(fin)] + p_specs,
        out_specs=pl.BlockSpec((tm, D), lambda i: (i, 0)),
        out_shape=jax.ShapeDtypeStruct((rows, D), F32),
        compiler_params=_cparams(("parallel",)), name="ple",
    )(h, pn, wg, wp, fin, *p)


def _rope_tables(pos):
    pos = pos.astype(F32)[:, None]
    lane = np.arange(LANES)
    freqs = lambda half: ROPE_THETA ** (-jnp.arange(half, dtype=F32) / half)
    a_ret, a_mla = pos * freqs(RET_HALF)[None, :], pos * freqs(ROPE_HALF)[None, :]
    wide = lambda t: jnp.tile(t, (1, LANES // t.shape[1]))
    lo64 = jnp.asarray((lane % RET_DK) < RET_HALF)
    c64, s64 = wide(jnp.cos(a_ret)), wide(jnp.sin(a_ret))
    c16, s16 = wide(jnp.cos(a_mla)), wide(jnp.sin(a_mla))
    in_k = jnp.asarray(lane < MLA_ROPE)
    lo_k = jnp.asarray(lane < ROPE_HALF)
    hi_k = jnp.asarray((lane >= ROPE_HALF) & (lane < MLA_ROPE))
    return (c64, jnp.where(lo64, -s64, 0.0), jnp.where(lo64, 0.0, s64), c16, s16,
            jnp.where(in_k, c16, 0.0), jnp.where(lo_k, -s16, 0.0), jnp.where(hi_k, s16, 0.0))


def _ret_tables(L):
    f = np.float32
    log_g = np.log(f(1.0) - np.exp2(f(-5.0) - np.arange(RET_HEADS, dtype=f)))
    idx = np.arange(L, dtype=f)
    dist = np.abs(idx[:, None] - idx[None, :])
    d = np.exp(dist[None] * log_g[:, None, None])
    qdec = np.exp((idx[:, None] + f(1.0)) * log_g[None, :])
    kdec = np.exp((f(L) - f(1.0) - idx)[:, None] * log_g[None, :])
    sdec = np.exp(f(L) * log_g)
    wide = lambda t: np.repeat(t, RET_DK, axis=1).reshape(L, N_PAIRS, LANES).transpose(1, 0, 2)
    bm = np.kron(np.eye(2, dtype=f), np.ones((RET_DK, RET_DV), f))
    gs = np.repeat(sdec, RET_DK).reshape(N_PAIRS, LANES, 1) * bm[None]
    tabs = (d.reshape(N_PAIRS, 2 * L, L), wide(qdec), wide(kdec), gs, bm)
    return tuple(jnp.asarray(t, F32) for t in tabs)


def _state_to_pairs(s):
    B = s.shape[0]
    s = s.reshape(B, N_PAIRS, 2, RET_DK, RET_DV)
    eye = jnp.eye(2, dtype=s.dtype)
    out = s[:, :, :, :, None, :] * eye[None, None, :, None, :, None]
    return out.reshape(B, N_PAIRS, LANES, LANES)


def _pairs_to_state(sp):
    B = sp.shape[0]
    s = sp.reshape(B, N_PAIRS, 2, RET_DK, 2, RET_DV)
    return jnp.stack([s[:, :, 0, :, 0, :], s[:, :, 1, :, 1, :]], axis=2).reshape(B, RET_HEADS, RET_DK, RET_DV)


def _group_uq_columns(w_uq):
    half = MLA_ROPE // 2
    w = w_uq.reshape(w_uq.shape[0], MLA_HEADS, MLA_NOPE + MLA_ROPE)
    parts = (w[:, :, :MLA_NOPE], w[:, :, MLA_NOPE:MLA_NOPE + half], w[:, :, MLA_NOPE + half:])
    return jnp.concatenate([p.reshape(w_uq.shape[0], -1) for p in parts], axis=1)


def _big_query_weight(w_uk):
    H, half = MLA_HEADS, MLA_ROPE // 2
    eye = jnp.eye(H, dtype=w_uk.dtype)
    wpad = jnp.pad(w_uk, ((0, 0), (0, 0), (0, QK_PAD - KV_LORA)))
    top = (eye[:, None, :, None] * wpad[:, :, None, :]).reshape(H * MLA_NOPE, H * QK_PAD)
    sel = np.zeros((2 * H * half, H * QK_PAD), np.float32)
    for h in range(H):
        for f in range(half):
            sel[h * half + f, h * QK_PAD + KV_LORA + f] = 1.0
            sel[H * half + h * half + f, h * QK_PAD + KV_LORA + half + f] = 1.0
    return jnp.concatenate([top, jnp.asarray(sel, w_uk.dtype)], axis=0)


def kernel(x_prompt, x_sample, p_prompt, p_sample, cache_ckv, cache_krope, state_ret, attn_norm, w_in, q_norm, w_uq, kv_norm, w_uk, w_uv, ret_norm, w_o, ffn_norm, w_gate_d, w_up_d, w_down_d, w_router, w_gate_e, w_up_e, w_down_e, ple_norm, w_ple_gate, w_ple_proj, final_norm):
    Bp, S, D = x_prompt.shape
    Bs, L, _ = x_sample.shape
    depth = w_in.shape[0]
    P = cache_ckv.shape[2]
    Tp, Ts = Bp * S, Bs * L
    T = Tp + Ts
    assert S % CHUNK == 0 and P % CHUNK == 0 and L == CHUNK and Tp % CHUNK == 0
    assert w_router.shape[-1] >= TOP_K

    tm_proj = _pick(int(np.gcd(S, Ts)), (512, 256, 128, 64))
    tm_gmm = 1024
    rb = _pick(S, (512, 256, 128, 64))
    tq = _pick(S, (256, 128, 64))
    kb = _pick(S, (512, 256, 128))
    kb_s = 512
    sk_s = P + L
    sk_pad = -(-sk_s // kb_s) * kb_s

    pos = jnp.concatenate([jnp.arange(S, dtype=jnp.int32),
                           P + jnp.tile(jnp.arange(L, dtype=jnp.int32), Bs)])
    rope_tabs = _rope_tables(pos)
    n_pt, pt_per_seq = Tp // tm_proj, S // tm_proj
    tab_tile = lambda i: jnp.where(i < n_pt, i % pt_per_seq, pt_per_seq + i - n_pt)
    ret_tabs = _ret_tables(CHUNK)
    row2 = lambda v: v.reshape(1, -1)

    h = (x_prompt.reshape(Tp, D), x_sample.reshape(Ts, D))
    outs = {k: [] for k in ("ckv_p", "kro_p", "ret_p", "ckv_s", "kro_s", "ret_s")}
    for l in range(depth):
        win = jnp.pad(w_in[l], ((0, 0), (0, IN_COLS_PAD - IN_COLS))).astype(BF16)
        wuq = _group_uq_columns(w_uq[l]).astype(BF16)
        wbig = _big_query_weight(w_uk[l]).astype(BF16)
        qr, kr, vr, gr, qx, ckv, kro, kx, *h_cat = _proj(
            h, row2(attn_norm[l]), win, row2(q_norm[l]), wuq, row2(kv_norm[l]), wbig, rope_tabs,
            tab_tile, tm_proj)
        if h_cat:
            h, = h_cat

        rn = row2(ret_norm[l])
        zero_state = jnp.zeros((Bp, N_PAIRS, LANES, LANES), F32)
        o_ret_p, st_p = _retention(qr, kr, vr, gr, zero_state, ret_tabs, rn, Bp, S, 0, rb)
        o_ret_s, st_s = _retention(qr, kr, vr, gr, _state_to_pairs(state_ret[l].astype(F32)),
                                   ret_tabs, rn, Bs, L, Tp, L)

        wuv = w_uv[l].astype(BF16)
        wuv_big = (jnp.eye(MLA_HEADS, dtype=BF16)[:, None, :, None] * wuv[:, :, None, :]
                   ).reshape(MLA_HEADS // 2, 2 * KV_LORA, MLA_W)
        o_mla_p = _attention(qx, kx, wuv_big, Bp, S, 0, tq, kb, 0, S, S)
        cache_kx = jnp.concatenate(
            [cache_ckv[l], cache_krope[l], jnp.zeros((Bs, P, QK_PAD - KV_LORA - MLA_ROPE), F32)],
            axis=-1).astype(BF16)
        kx_s = jnp.concatenate([cache_kx, kx[Tp:].reshape(Bs, L, QK_PAD),
                                jnp.zeros((Bs, sk_pad - sk_s, QK_PAD), BF16)], axis=1)
        o_mla_s = _attention(qx, kx_s.reshape(Bs * sk_pad, QK_PAD), wuv_big, Bs, L, Tp, L, kb_s, P,
                             sk_pad, sk_s)

        fn = row2(ffn_norm[l])
        j = l // 2
        wo = w_o[l].astype(BF16)
        ple_w = (row2(ple_norm[l]), w_ple_gate[l].astype(BF16))
        wp = w_ple_proj[l].astype(BF16)
        p_l = (p_prompt[l].reshape(Tp, -1), p_sample[l].reshape(Ts, -1))
        last = l == depth - 1
        if l % 2 == 0 and not last:
            h = _dense_layer(h, o_ret_p, o_mla_p, o_ret_s, o_mla_s, *p_l, wo, fn,
                             w_gate_d[j].astype(BF16), w_up_d[j].astype(BF16),
                             w_down_d[j].astype(BF16), *ple_w, wp, tm_proj)
        else:
            h = _merge(h, o_ret_p, o_mla_p, o_ret_s, o_mla_s, wo, tm_proj)
            if l % 2 == 0:
                h = _ffn(h, fn, w_gate_d[j].astype(BF16), w_up_d[j].astype(BF16),
                         w_down_d[j].astype(BF16), tm_proj)
            else:
                tf = _pick(w_gate_e.shape[-1], (512, 256, 128))
                h = _moe(h, fn, w_router[j], w_gate_e[j], w_up_e[j], w_down_e[j], tm_proj, tm_gmm, tf)
            if not last:
                h = _ple(h, *ple_w, p_l, wp, row2(final_norm), tm_proj, False)
            else:
                y_p = _ple(h, *ple_w, p_l[0], wp, row2(final_norm), tm_proj, True, 0, Tp)
                y_s = _ple(h, *ple_w, p_l[1], wp, row2(final_norm), tm_proj, True, Tp, Ts)

        outs["ckv_p"].append(ckv[:Tp].reshape(Bp, S, KV_LORA))
        outs["kro_p"].append(kro[:Tp].reshape(Bp, S, MLA_ROPE))
        outs["ret_p"].append(_pairs_to_state(st_p))
        outs["ckv_s"].append(ckv[Tp:].reshape(Bs, L, KV_LORA))
        outs["kro_s"].append(kro[Tp:].reshape(Bs, L, MLA_ROPE))
        outs["ret_s"].append(_pairs_to_state(st_s))

    return (y_p.reshape(Bp, S, D), y_s.reshape(Bs, L, D),
            jnp.stack(outs["ckv_p"]), jnp.stack(outs["kro_p"]), jnp.stack(outs["ret_p"]),
            jnp.stack(outs["ckv_s"]), jnp.stack(outs["kro_s"]), jnp.stack(outs["ret_s"]))
```

```python
import functools

import numpy as np
import jax
import jax.numpy as jnp
from jax import lax
from jax.experimental import pallas as pl
from jax.experimental.pallas import tpu as pltpu

F32 = jnp.float32
BF16 = jnp.bfloat16

CHUNK = 64
CHUNK_SHIFT = 6
RMS_EPS = 1e-6
ROPE_THETA = 10000.0
RET_HEADS = 8
RET_DK = 64
RET_DV = 64
RET_W = RET_HEADS * RET_DK
MLA_HEADS = 8
MLA_NOPE = 64
MLA_ROPE = 32
MLA_V = 64
Q_LORA = 256
KV_LORA = 128
MLA_W = MLA_HEADS * MLA_V
RET_HALF = RET_DK // 2
ROPE_HALF = MLA_ROPE // 2
TOP_K = 2
N_PAIRS = RET_HEADS // 2
LANES = 128
SUBLANES = 8
QK_PAD = 256
OFF_Q = RET_W
OFF_K = OFF_Q + RET_W
OFF_V = OFF_K + RET_W
OFF_G = OFF_V + RET_W
OFF_CQ = OFF_G + Q_LORA
OFF_CKV = OFF_CQ + KV_LORA
IN_COLS = OFF_CKV + MLA_ROPE
IN_COLS_PAD = OFF_CKV + LANES
NEG = -1e30
VMEM_LIMIT = 56 * 1024 * 1024


def _pick(n, cands):
    for c in cands:
        if n % c == 0:
            return c
    return n


def _cparams(sem, flags=None):
    return pltpu.CompilerParams(dimension_semantics=sem, vmem_limit_bytes=VMEM_LIMIT, flags=flags)


def _rms(x, g):
    return x * lax.rsqrt(jnp.mean(x * x, axis=-1, keepdims=True) + RMS_EPS) * g


def _dot(a, b):
    return jnp.dot(a, b, preferred_element_type=F32)


def _dot_nt(a, b):
    return lax.dot_general(a, b, (((1,), (1,)), ((), ())), preferred_element_type=F32)


def _dot_tn(a, b):
    return lax.dot_general(a, b, (((0,), (0,)), ((), ())), preferred_element_type=F32)


def _proj_kernel(h_ref, *refs):
    _proj_body(h_ref[...], *refs)


def _proj_streams_kernel(hp_ref, hs_ref, *refs, n_prompt_tiles):
    x = jnp.where(pl.program_id(0) < n_prompt_tiles, hp_ref[...], hs_ref[...])
    refs[-1][...] = x
    _proj_body(x, *refs[:-1])


def _proj_body(x, an_ref, win_ref, qn_ref, wuq_ref, kvn_ref, wbig_ref,
               c64_ref, sm64_ref, sp64_ref, c16_ref, s16_ref, ck_ref, smk_ref, spk_ref,
               qr_ref, kr_ref, vr_ref, gr_ref, qx_ref, ckv_ref, kro_ref, kx_ref):
    xn = _rms(x, an_ref[...]).astype(BF16)
    c64, sm64, sp64 = c64_ref[...], sm64_ref[...], sp64_ref[...]

    def rope64(z):
        return (z * c64 + pltpu.roll(z, LANES - RET_HALF, 1) * sm64
                + pltpu.roll(z, RET_HALF, 1) * sp64)

    for lo in range(0, RET_W, QK_PAD):
        zq = _dot(xn, win_ref[:, lo:lo + QK_PAD])
        zk = _dot(xn, win_ref[:, OFF_Q + lo:OFF_Q + lo + QK_PAD])
        for half in range(0, QK_PAD, LANES):
            dst = slice(lo + half, lo + half + LANES)
            qr_ref[:, dst] = rope64(zq[:, half:half + LANES])
            kr_ref[:, dst] = rope64(zk[:, half:half + LANES]) * (RET_DK ** -0.5)

    zkv = _dot(xn, win_ref[:, OFF_CQ:IN_COLS_PAD])
    ckv = _rms(zkv[:, :KV_LORA], kvn_ref[...])
    ckv_ref[...] = ckv
    zk = zkv[:, KV_LORA:]
    kro = (zk * ck_ref[...] + pltpu.roll(zk, LANES - ROPE_HALF, 1) * smk_ref[...]
           + pltpu.roll(zk, ROPE_HALF, 1) * spk_ref[...])
    kro_ref[...] = kro[:, :MLA_ROPE]
    kx_ref[:, :KV_LORA] = ckv.astype(BF16)
    kx_ref[:, KV_LORA:] = kro.astype(BF16)

    cq = _dot(xn, win_ref[:, OFF_G:OFF_CQ])
    q = _dot(_rms(cq, qn_ref[...]).astype(BF16), wuq_ref[...])
    nq = MLA_HEADS * MLA_NOPE
    x1, x2 = q[:, nq:nq + LANES], q[:, nq + LANES:nq + 2 * LANES]
    c16, s16 = c16_ref[...], s16_ref[...]
    scale = (MLA_NOPE + MLA_ROPE) ** -0.5 * float(np.log2(np.e))
    qcat = jnp.concatenate([q[:, :nq], x1 * c16 - x2 * s16, x2 * c16 + x1 * s16], axis=-1)
    qcat = (qcat * scale).astype(BF16)
    half_w = (MLA_HEADS // 2) * QK_PAD
    for g in range(2):
        rows = slice(g * (nq // 2), (g + 1) * (nq // 2))
        cols = slice(g * half_w, (g + 1) * half_w)
        qx_ref[:, cols] = (_dot(qcat[:, rows], wbig_ref[rows, cols])
                           + _dot(qcat[:, nq:], wbig_ref[nq:, cols])).astype(BF16)

    vr_ref[...] = _dot(xn, win_ref[:, OFF_K:OFF_V]).astype(BF16)
    gr_ref[...] = _dot(xn, win_ref[:, OFF_V:OFF_G])


def _stream_specs(tm, width, n_prompt_tiles):
    return (pl.BlockSpec((tm, width), lambda i: (jnp.minimum(i, n_prompt_tiles - 1), 0)),
            pl.BlockSpec((tm, width), lambda i: (jnp.maximum(i - n_prompt_tiles, 0), 0)))


def _proj(h, an, win, qn, wuq, kvn, wbig, tabs, tab_tile, tm):
    streams = isinstance(h, tuple)
    T = sum(a.shape[0] for a in h) if streams else h.shape[0]
    D = h[0].shape[1] if streams else h.shape[1]
    row = lambda w: pl.BlockSpec((tm, w), lambda i: (i, 0))
    tab = pl.BlockSpec((tm, LANES), lambda i: (tab_tile(i), 0))
    full = lambda a: pl.BlockSpec(a.shape, lambda i: (0,) * a.ndim)
    out_shapes = (
        jax.ShapeDtypeStruct((T, RET_W), F32), jax.ShapeDtypeStruct((T, RET_W), F32),
        jax.ShapeDtypeStruct((T, RET_W), BF16), jax.ShapeDtypeStruct((T, RET_W), F32),
        jax.ShapeDtypeStruct((T, MLA_HEADS * QK_PAD), BF16),
        jax.ShapeDtypeStruct((T, KV_LORA), F32), jax.ShapeDtypeStruct((T, MLA_ROPE), F32),
        jax.ShapeDtypeStruct((T, QK_PAD), BF16))
    out_specs = (row(RET_W), row(RET_W), row(RET_W), row(RET_W), row(MLA_HEADS * QK_PAD),
                 row(KV_LORA), row(MLA_ROPE), row(QK_PAD))
    if streams:
        npt = h[0].shape[0] // tm
        kern = functools.partial(_proj_streams_kernel, n_prompt_tiles=npt)
        h_specs, h_args = list(_stream_specs(tm, D, npt)), h
        out_specs += (row(D),)
        out_shapes += (jax.ShapeDtypeStruct((T, D), F32),)
    else:
        kern, h_specs, h_args = _proj_kernel, [row(D)], (h,)
    return pl.pallas_call(
        kern, grid=(T // tm,),
        in_specs=h_specs + [full(an), full(win), full(qn), full(wuq), full(kvn), full(wbig)]
                 + [tab] * 8,
        out_specs=out_specs,
        out_shape=out_shapes, compiler_params=_cparams(("parallel",)), name="proj",
    )(*h_args, an, win, qn, wuq, kvn, wbig, *tabs)


def _ret_kernel(q_ref, k_ref, v_ref, g_ref, s0_ref, d_ref, qd_ref, kd_ref, gs_ref, bm_ref, rn_ref,
                o_ref, so_ref, st_ref, *, n_chunks):
    i = pl.program_id(1)

    @pl.when(i == 0)
    def _():
        st_ref[...] = s0_ref[0]

    lane = lax.broadcasted_iota(jnp.int32, (CHUNK, LANES), 1)
    first = lane < RET_DK

    def chunk(c, carry):
        rows = pl.ds(pl.multiple_of(c * CHUNK, CHUNK), CHUNK)
        for p in range(N_PAIRS):
            cols = slice(p * LANES, (p + 1) * LANES)
            qp, kp, vp = q_ref[rows, cols], k_ref[rows, cols], v_ref[rows, cols]
            q2 = jnp.concatenate([jnp.where(first, qp, 0.0), jnp.where(first, 0.0, qp)], axis=0)
            sd = (_dot_nt(q2.astype(BF16), kp.astype(BF16)) * d_ref[p]).astype(BF16)
            qq = (qp * qd_ref[p]).astype(BF16)
            st = st_ref[p]
            lhs = jnp.concatenate([jnp.concatenate([qq, qq], axis=0), sd], axis=1)
            rhs = jnp.concatenate([st.astype(BF16), vp], axis=0)
            o2 = _dot(lhs, rhs)
            o = jnp.where(first, o2[:CHUNK], o2[CHUNK:])
            upd = _dot_tn((kp * kd_ref[p]).astype(BF16), vp)
            st_ref[p] = st * gs_ref[p] + upd * bm_ref[...]
            oo = o * o
            ss_a = jnp.sum(jnp.where(first, oo, 0.0), axis=-1, keepdims=True)
            ss_b = jnp.sum(jnp.where(first, 0.0, oo), axis=-1, keepdims=True)
            rs = jnp.where(first, lax.rsqrt(ss_a * (1.0 / RET_DV) + RMS_EPS),
                           lax.rsqrt(ss_b * (1.0 / RET_DV) + RMS_EPS))
            g = g_ref[rows, cols]
            o_ref[rows, cols] = (o * rs * rn_ref[:, cols] * (g * jax.nn.sigmoid(g))).astype(BF16)
        return carry

    lax.fori_loop(0, n_chunks, chunk, 0, unroll=min(n_chunks, 4))

    @pl.when(i == pl.num_programs(1) - 1)
    def _():
        so_ref[0] = st_ref[...]


def _retention(qr, kr, vr, gr, s0, tabs, rn, nb, seq, row0, rb):
    nblk = seq // rb
    off = row0 // rb
    tok = pl.BlockSpec((rb, RET_W), lambda b, i: (off + b * nblk + i, 0))
    full = lambda a: pl.BlockSpec(a.shape, lambda b, i: (0,) * a.ndim)
    st_spec = pl.BlockSpec((1, N_PAIRS, LANES, LANES), lambda b, i: (b, 0, 0, 0))
    d, qd, kd, gs, bm = tabs
    return pl.pallas_call(
        functools.partial(_ret_kernel, n_chunks=rb // CHUNK), grid=(nb, nblk),
        in_specs=[tok, tok, tok, tok, st_spec, full(d), full(qd), full(kd), full(gs), full(bm), full(rn)],
        out_specs=(pl.BlockSpec((rb, RET_W), lambda b, i: (b * nblk + i, 0)), st_spec),
        out_shape=(jax.ShapeDtypeStruct((nb * seq, RET_W), BF16),
                   jax.ShapeDtypeStruct((nb, N_PAIRS, LANES, LANES), F32)),
        scratch_shapes=[pltpu.VMEM((N_PAIRS, LANES, LANES), F32)],
        compiler_params=_cparams(("parallel", "arbitrary")), name="retention",
    )(qr, kr, vr, gr, s0, d, qd, kd, gs, bm, rn)


def _attn_kernel(q_ref, k_ref, wuv_ref, o_ref, q_s, s0_s, s1_s, m_s, l_s, acc_s,
                 *, tq, kb, q_pos0, sk_valid):
    i = pl.program_id(1)
    R = MLA_HEADS * tq
    for h in range(MLA_HEADS):
        q_s[h * tq:(h + 1) * tq, :] = q_ref[:, h * QK_PAD:(h + 1) * QK_PAD]
    m_s[...] = jnp.full(m_s.shape, NEG, F32)
    l_s[...] = jnp.zeros(l_s.shape, F32)
    acc_s[...] = jnp.zeros(acc_s.shape, F32)

    chunk_end = lambda t: ((t >> CHUNK_SHIFT) + 1) << CHUNK_SHIFT
    kb_shift = kb.bit_length() - 1
    qstart = q_pos0 + i * tq
    lim_first = jnp.minimum(chunk_end(qstart), sk_valid)
    lim_last = jnp.minimum(chunk_end(qstart + tq - 1), sk_valid)
    n_full = lim_first >> kb_shift
    n_blk = (lim_last + kb - 1) >> kb_shift

    last = n_blk - 1

    def key_block(j):
        return k_ref[pl.ds(pl.multiple_of(j * kb, kb), kb), :]

    def scores(j, s_ref):
        s_ref[...] = _dot_nt(q_s[...], key_block(j))

    def update(j, s_ref, masked):
        s = s_ref[...]
        if masked:
            tok = lax.broadcasted_iota(jnp.int32, (R, 1), 0) & (tq - 1)
            row_lim = jnp.minimum(chunk_end(qstart + tok), sk_valid)
            kidx = j * kb + lax.broadcasted_iota(jnp.int32, (1, kb), 1)
            s = jnp.where(kidx < row_lim, s, NEG)
        m_prev = m_s[...]
        m_next = jnp.maximum(m_prev, jnp.max(s, axis=-1, keepdims=True))
        p = jnp.exp2(s - jnp.tile(m_next, (1, kb // LANES)))
        alpha = jnp.exp2(m_prev - m_next)
        p_lanes = p[:, :LANES]
        for c in range(LANES, kb, LANES):
            p_lanes = p_lanes + p[:, c:c + LANES]
        l_s[...] = alpha * l_s[...] + p_lanes
        acc_s[...] = alpha * acc_s[...] + _dot(p.astype(BF16), key_block(j)[:, :KV_LORA])
        m_s[...] = m_next

    n_pipe = jnp.minimum(n_full, last)
    odd = n_pipe & 1

    @pl.when(odd == 1)
    def _():
        scores(0, s0_s)
        update(0, s0_s, False)

    scores(odd, s0_s)

    def pair(jj, c):
        j = odd + 2 * jj
        scores(j + 1, s1_s)
        update(j, s0_s, False)
        scores(j + 2, s0_s)
        update(j + 1, s1_s, False)
        return c

    lax.fori_loop(0, n_pipe >> 1, pair, 0)
    update(n_pipe, s0_s, True)

    def tail(j, c):
        scores(j, s0_s)
        update(j, s0_s, True)
        return c

    lax.fori_loop(n_pipe + 1, n_blk, tail, 0)

    o_lat = (acc_s[...] / jnp.sum(l_s[...], axis=-1, keepdims=True)).astype(BF16)
    out = None
    for p in range(MLA_HEADS // 2):
        pair = jnp.concatenate([o_lat[(2 * p) * tq:(2 * p + 1) * tq],
                                o_lat[(2 * p + 1) * tq:(2 * p + 2) * tq]], axis=1)
        term = _dot(pair, wuv_ref[p])
        out = term if out is None else out + term
    o_ref[...] = out.astype(BF16)


def _attn_t_kernel(q_ref, k_ref, kt_ref, wuv_ref, o_ref, qt_s, s0_s, s1_s, m_s, l_s, acc_s,
                   *, tq, kb, q_pos0, sk_valid):
    i = pl.program_id(1)
    R = MLA_HEADS * tq
    for h in range(MLA_HEADS):
        qh = q_ref[:, h * QK_PAD:(h + 1) * QK_PAD].astype(F32)
        qt_s[:, h * tq:(h + 1) * tq] = qh.T.astype(BF16)
    m_s[...] = jnp.full(m_s.shape, NEG, F32)
    l_s[...] = jnp.zeros(l_s.shape, F32)
    acc_s[...] = jnp.zeros(acc_s.shape, F32)

    chunk_end = lambda t: ((t >> CHUNK_SHIFT) + 1) << CHUNK_SHIFT
    kb_shift = kb.bit_length() - 1
    qstart = q_pos0 + i * tq
    lim_first = jnp.minimum(chunk_end(qstart), sk_valid)
    lim_last = jnp.minimum(chunk_end(qstart + tq - 1), sk_valid)
    n_full = lim_first >> kb_shift
    n_blk = (lim_last + kb - 1) >> kb_shift
    last = n_blk - 1

    def scores(j, s_ref):
        kblk = k_ref[pl.ds(pl.multiple_of(j * kb, kb), kb), :]
        s_ref[...] = _dot(kblk, qt_s[...])

    def update(j, s_ref, masked):
        s = s_ref[...]
        if masked:
            tok = lax.broadcasted_iota(jnp.int32, (1, R), 1) & (tq - 1)
            col_lim = jnp.minimum(chunk_end(qstart + tok), sk_valid)
            kidx = j * kb + lax.broadcasted_iota(jnp.int32, (kb, 1), 0)
            s = jnp.where(kidx < col_lim, s, NEG)
        m_prev = m_s[...]
        m_next = jnp.maximum(m_prev, jnp.max(s, axis=0, keepdims=True))
        p = jnp.exp2(s - m_next[0:1, :])
        alpha = jnp.exp2(m_prev - m_next)
        l_s[...] = alpha * l_s[...] + jnp.sum(p.reshape(kb // SUBLANES, SUBLANES, R), axis=0)
        vt = kt_ref[:KV_LORA, pl.ds(pl.multiple_of(j * kb, kb), kb)]
        acc_s[...] = alpha[0:1, :] * acc_s[...] + _dot(vt, p.astype(BF16))
        m_s[...] = m_next

    n_pipe = jnp.minimum(n_full, last)
    odd = n_pipe & 1

    @pl.when(odd == 1)
    def _():
        scores(0, s0_s)
        update(0, s0_s, False)

    scores(odd, s0_s)

    def pair(jj, c):
        j = odd + 2 * jj
        scores(j + 1, s1_s)
        update(j, s0_s, False)
        scores(j + 2, s0_s)
        update(j + 1, s1_s, False)
        return c

    lax.fori_loop(0, n_pipe >> 1, pair, 0)
    update(n_pipe, s0_s, True)

    def tail(j, c):
        scores(j, s0_s)
        update(j, s0_s, True)
        return c

    lax.fori_loop(n_pipe + 1, n_blk, tail, 0)

    o_lat_t = acc_s[...] / jnp.sum(l_s[...], axis=0, keepdims=True)
    o_lat = o_lat_t.T.astype(BF16)
    out = None
    for p in range(MLA_HEADS // 2):
        pair_rows = jnp.concatenate([o_lat[(2 * p) * tq:(2 * p + 1) * tq],
                                     o_lat[(2 * p + 1) * tq:(2 * p + 2) * tq]], axis=1)
        term = _dot(pair_rows, wuv_ref[p])
        out = term if out is None else out + term
    o_ref[...] = out.astype(BF16)


def _attention_t(qx, kx, kxt, wuv, nb, seq_q, row0, tq, kb, q_pos0, sk, sk_valid):
    nq = seq_q // tq
    off = row0 // tq
    R = MLA_HEADS * tq
    return pl.pallas_call(
        functools.partial(_attn_t_kernel, tq=tq, kb=kb, q_pos0=q_pos0, sk_valid=sk_valid),
        grid=(nb, nq),
        in_specs=[pl.BlockSpec((tq, MLA_HEADS * QK_PAD), lambda b, i: (off + b * nq + i, 0)),
                  pl.BlockSpec((sk, QK_PAD), lambda b, i: (b, 0)),
                  pl.BlockSpec((QK_PAD, sk), lambda b, i: (b, 0)),
                  pl.BlockSpec(wuv.shape, lambda b, i: (0, 0, 0))],
        out_specs=pl.BlockSpec((tq, MLA_W), lambda b, i: (b * nq + i, 0)),
        out_shape=jax.ShapeDtypeStruct((nb * seq_q, MLA_W), BF16),
        scratch_shapes=[pltpu.VMEM((QK_PAD, R), BF16), pltpu.VMEM((kb, R), F32),
                        pltpu.VMEM((kb, R), F32), pltpu.VMEM((SUBLANES, R), F32),
                        pltpu.VMEM((SUBLANES, R), F32), pltpu.VMEM((KV_LORA, R), F32)],
        compiler_params=_cparams(("parallel", "arbitrary")),
        name="attention_t",
    )(qx, kx, kxt, wuv)


def _attention(qx, kx, wuv, nb, seq_q, row0, tq, kb, q_pos0, sk, sk_valid):
    nq = seq_q // tq
    off = row0 // tq
    R = MLA_HEADS * tq
    return pl.pallas_call(
        functools.partial(_attn_kernel, tq=tq, kb=kb, q_pos0=q_pos0, sk_valid=sk_valid),
        grid=(nb, nq),
        in_specs=[pl.BlockSpec((tq, MLA_HEADS * QK_PAD), lambda b, i: (off + b * nq + i, 0)),
                  pl.BlockSpec((sk, QK_PAD), lambda b, i: (b, 0)),
                  pl.BlockSpec(wuv.shape, lambda b, i: (0, 0, 0))],
        out_specs=pl.BlockSpec((tq, MLA_W), lambda b, i: (b * nq + i, 0)),
        out_shape=jax.ShapeDtypeStruct((nb * seq_q, MLA_W), BF16),
        scratch_shapes=[pltpu.VMEM((R, QK_PAD), BF16), pltpu.VMEM((R, kb), F32),
                        pltpu.VMEM((R, kb), F32), pltpu.VMEM((R, LANES), F32),
                        pltpu.VMEM((R, LANES), F32), pltpu.VMEM((R, KV_LORA), F32)],
        compiler_params=_cparams(("parallel", "arbitrary")),
        name="attention",
    )(qx, kx, wuv)


def _merge_kernel(h_ref, ap_ref, bp_ref, as_ref, bs_ref, wo_ref, o_ref, *, n_prompt_tiles):
    def project(a_ref, b_ref):
        o_ref[...] = (h_ref[...] + _dot(a_ref[...], wo_ref[:RET_W, :])
                      + _dot(b_ref[...], wo_ref[RET_W:, :]))

    @pl.when(pl.program_id(0) < n_prompt_tiles)
    def _():
        project(ap_ref, bp_ref)

    @pl.when(pl.program_id(0) >= n_prompt_tiles)
    def _():
        project(as_ref, bs_ref)


def _merge(h, a_p, b_p, a_s, b_s, wo, tm):
    T, D = h.shape
    npt = a_p.shape[0] // tm
    row = lambda w: pl.BlockSpec((tm, w), lambda i: (i, 0))
    (a_prm, a_smp), (b_prm, b_smp) = _stream_specs(tm, RET_W, npt), _stream_specs(tm, MLA_W, npt)
    return pl.pallas_call(
        functools.partial(_merge_kernel, n_prompt_tiles=npt), grid=(T // tm,),
        in_specs=[row(D), a_prm, b_prm, a_smp, b_smp, pl.BlockSpec(wo.shape, lambda i: (0, 0))],
        out_specs=row(D), out_shape=jax.ShapeDtypeStruct((T, D), F32),
        compiler_params=_cparams(("parallel",)), name="merge",
    )(h, a_p, b_p, a_s, b_s, wo)


def _ffn_kernel(h_ref, fn_ref, wg_ref, wu_ref, wd_ref, o_ref):
    hh = h_ref[...]
    u = _rms(hh, fn_ref[...]).astype(BF16)
    a = _dot(u, wg_ref[...])
    mid = (a * jax.nn.sigmoid(a) * _dot(u, wu_ref[...])).astype(BF16)
    o_ref[...] = hh + _dot(mid, wd_ref[...])


def _ffn(h, fn, wg, wu, wd, tm):
    T, D = h.shape
    resident = lambda a: pl.BlockSpec(a.shape, lambda i: (0, 0), pipeline_mode=pl.Buffered(1))
    return pl.pallas_call(
        _ffn_kernel, grid=(T // tm,),
        in_specs=[pl.BlockSpec((tm, D), lambda i: (i, 0)), pl.BlockSpec(fn.shape, lambda i: (0, 0)),
                  resident(wg), resident(wu), resident(wd)],
        out_specs=pl.BlockSpec((tm, D), lambda i: (i, 0)),
        out_shape=jax.ShapeDtypeStruct((T, D), F32),
        compiler_params=_cparams(("parallel",)), name="ffn",
    )(h, fn, wg, wu, wd)


def _dense_layer_kernel(h_ref, ap_ref, bp_ref, as_ref, bs_ref, pp_ref, ps_ref, wo_ref, fn_ref,
                        wg_ref, wu_ref, wd_ref, pn_ref, wpg_ref, wpp_ref, o_ref, *, n_prompt_tiles):
    prompt = pl.program_id(0) < n_prompt_tiles
    a = jnp.where(prompt, ap_ref[...], as_ref[...])
    b = jnp.where(prompt, bp_ref[...], bs_ref[...])
    h1 = h_ref[...] + _dot(a, wo_ref[:RET_W, :]) + _dot(b, wo_ref[RET_W:, :])
    u = _rms(h1, fn_ref[...]).astype(BF16)
    g = _dot(u, wg_ref[...])
    mid = (g * jax.nn.sigmoid(g) * _dot(u, wu_ref[...])).astype(BF16)
    h2 = h1 + _dot(mid, wd_ref[...])
    gate = jax.nn.sigmoid(_dot(_rms(h2, pn_ref[...]).astype(BF16), wpg_ref[...]))
    p = jnp.where(prompt, pp_ref[...], ps_ref[...]).astype(BF16)
    o_ref[...] = h2 + gate * _dot(p, wpp_ref[...])


def _dense_layer(h, a_p, b_p, a_s, b_s, p_p, p_s, wo, fn, wg, wu, wd, pn, wpg, wpp, tm):
    T, D = h.shape
    npt = a_p.shape[0] // tm
    row = pl.BlockSpec((tm, D), lambda i: (i, 0))
    resident = lambda w: pl.BlockSpec(w.shape, lambda i: (0, 0), pipeline_mode=pl.Buffered(1))
    (a_prm, a_smp), (b_prm, b_smp) = _stream_specs(tm, RET_W, npt), _stream_specs(tm, MLA_W, npt)
    p_prm, p_smp = _stream_specs(tm, p_p.shape[1], npt)
    return pl.pallas_call(
        functools.partial(_dense_layer_kernel, n_prompt_tiles=npt), grid=(T // tm,),
        in_specs=[row, a_prm, b_prm, a_smp, b_smp, p_prm, p_smp] + [
            resident(w) for w in (wo, fn, wg, wu, wd, pn, wpg, wpp)],
        out_specs=row, out_shape=jax.ShapeDtypeStruct((T, D), F32),
        compiler_params=_cparams(("parallel",)), name="dense_layer",
    )(h, a_p, b_p, a_s, b_s, p_p, p_s, wo, fn, wg, wu, wd, pn, wpg, wpp)


L_E1, L_E2, L_W1, L_W2, L_R1, L_R2 = range(6)


def _router_kernel(h_ref, fn_ref, wr_ref, meta_ref, cnt_ref, run_s, *, n_experts):
    tm = h_ref.shape[0]

    @pl.when(pl.program_id(0) == 0)
    def _():
        run_s[...] = jnp.zeros(run_s.shape, F32)

    u = _rms(h_ref[...], fn_ref[...])
    u_hi = u.astype(BF16)
    u_lo = (u - u_hi.astype(F32)).astype(BF16)
    hi_terms = _dot(u_hi, wr_ref[...])
    logits = hi_terms[:, :LANES] + hi_terms[:, LANES:] + _dot(u_lo, wr_ref[:, :LANES])
    lane = lax.broadcasted_iota(jnp.int32, logits.shape, 1).astype(F32)
    lg = jnp.where(lane < n_experts, logits, NEG)
    m1 = jnp.max(lg, axis=-1, keepdims=True)
    i1 = jnp.min(jnp.where(lg == m1, lane, float(LANES)), axis=-1, keepdims=True)
    lg2 = jnp.where(lane == i1, NEG, lg)
    m2 = jnp.max(lg2, axis=-1, keepdims=True)
    i2 = jnp.min(jnp.where(lg2 == m2, lane, float(LANES)), axis=-1, keepdims=True)
    e2 = jnp.exp(m2 - m1)
    den = 1.0 + e2

    hit = jnp.where((lane == i1) | (lane == i2), 1.0, 0.0)
    r_i = lax.broadcasted_iota(jnp.int32, (tm, tm), 0)
    c_i = lax.broadcasted_iota(jnp.int32, (tm, tm), 1)
    before = jnp.where(c_i < r_i, 1.0, 0.0).astype(BF16)
    prefix = _dot(before, hit.astype(BF16)) + run_s[0:1, :]
    r1 = jnp.sum(jnp.where(lane == i1, prefix, 0.0), axis=-1, keepdims=True)
    r2 = jnp.sum(jnp.where(lane == i2, prefix, 0.0), axis=-1, keepdims=True)
    run_s[...] = run_s[...] + jnp.sum(hit, axis=0, keepdims=True)
    cnt_ref[...] = run_s[...]

    meta = jnp.where(lane == L_E1, i1, 0.0)
    for ln, val in ((L_E2, i2), (L_W1, 1.0 / den), (L_W2, e2 / den), (L_R1, r1), (L_R2, r2)):
        meta = jnp.where(lane == ln, val, meta)
    meta_ref[...] = meta


def _router(h, fn, wr, n_experts, tm):
    T, D = h.shape
    return pl.pallas_call(
        functools.partial(_router_kernel, n_experts=n_experts), grid=(T // tm,),
        in_specs=[pl.BlockSpec((tm, D), lambda i: (i, 0)), pl.BlockSpec(fn.shape, lambda i: (0, 0)),
                  pl.BlockSpec(wr.shape, lambda i: (0, 0))],
        out_specs=(pl.BlockSpec((tm, LANES), lambda i: (i, 0)),
                   pl.BlockSpec((SUBLANES, LANES), lambda i: (0, 0))),
        out_shape=(jax.ShapeDtypeStruct((T, LANES), F32), jax.ShapeDtypeStruct((SUBLANES, LANES), F32)),
        scratch_shapes=[pltpu.VMEM((SUBLANES, LANES), F32)],
        compiler_params=_cparams(("arbitrary",)), name="router",
    )(h, fn, wr)


def _tile_index_copy(d_hbm, idx_s, sem, tile, slot):
    n = d_hbm.shape[1]
    return pltpu.make_async_copy(d_hbm.at[tile], idx_s.at[pl.ds(pl.multiple_of(slot * n, n), n)],
                                 sem.at[slot])


def _dispatch_kernel(fill_ref, d_hbm, h_ref, xs_out, idx_s, zero_s, isem, rsem, fsem):
    tm = h_ref.shape[0]
    tmg = zero_s.shape[0]
    i, n = pl.program_id(0), pl.num_programs(0)
    slot = i % 2

    def fill_copy(k):
        start = pl.multiple_of(fill_ref[k], tmg)
        return pltpu.make_async_copy(zero_s, xs_out.at[pl.ds(start, tmg)], fsem)

    @pl.when(i == 0)
    def _():
        _tile_index_copy(d_hbm, idx_s, isem, 0, 0).start()
        zero_s[...] = jnp.zeros(zero_s.shape, F32)
        for k in range(fill_ref.shape[0]):
            pl.when(fill_ref[k] >= 0)(lambda k=k: fill_copy(k).start(priority=k % 2))
        for k in range(fill_ref.shape[0]):
            pl.when(fill_ref[k] >= 0)(lambda k=k: fill_copy(k).wait())

    @pl.when(i + 1 < n)
    def _():
        _tile_index_copy(d_hbm, idx_s, isem, i + 1, 1 - slot).start()

    _tile_index_copy(d_hbm, idx_s, isem, i, slot).wait()

    def row_copy(r, dst):
        return pltpu.make_async_copy(h_ref.at[pl.ds(r, 1)], xs_out.at[pl.ds(dst, 1)], rsem)

    def body(g, c):
        r0 = pl.multiple_of(g * SUBLANES, SUBLANES)
        for u in range(SUBLANES):
            for k in range(TOP_K):
                row_copy(r0 + u, idx_s[slot * (TOP_K * tm) + k * tm + r0 + u]).start(priority=k % 2)
        return c

    lax.fori_loop(0, tm // SUBLANES, body, 0)
    for k in range(TOP_K):
        pltpu.make_async_copy(h_ref, xs_out.at[pl.ds(0, tm)], rsem).wait()


def _dispatch(fill_starts, d_tiles, h, n_rows, tm, tmg):
    T, D = h.shape
    grid_spec = pltpu.PrefetchScalarGridSpec(
        num_scalar_prefetch=1, grid=(T // tm,),
        in_specs=[pl.BlockSpec(memory_space=pl.ANY), pl.BlockSpec((tm, D), lambda i, fs: (i, 0))],
        out_specs=pl.BlockSpec(memory_space=pl.ANY),
        scratch_shapes=[pltpu.SMEM((2 * TOP_K * tm,), jnp.int32), pltpu.VMEM((tmg, D), F32),
                        pltpu.SemaphoreType.DMA((2,)), pltpu.SemaphoreType.DMA(()),
                        pltpu.SemaphoreType.DMA(())])
    return pl.pallas_call(
        _dispatch_kernel, grid_spec=grid_spec, out_shape=jax.ShapeDtypeStruct((n_rows, D), F32),
        compiler_params=_cparams(("arbitrary",)), name="dispatch",
    )(fill_starts, d_tiles, h)


def _gmm_kernel(te_ref, nu_ref, x_ref, fn_ref, wg_ref, wu_ref, wd_ref, o_ref, u_s):
    i, f = pl.program_id(0), pl.program_id(1)
    used = i < nu_ref[0]

    @pl.when(f == 0)
    def _():
        o_ref[...] = jnp.zeros(o_ref.shape, F32)

    @pl.when(used & (f == 0))
    def _():
        u_s[...] = _rms(x_ref[...], fn_ref[...]).astype(BF16)

    @pl.when(used)
    def _():
        u = u_s[...]
        a = _dot(u, wg_ref[0].astype(BF16))
        mid = (a * jax.nn.sigmoid(a) * _dot(u, wu_ref[0].astype(BF16))).astype(BF16)
        o_ref[...] += _dot(mid, wd_ref[0].astype(BF16))


def _gmm(tile_expert, n_used, xs, fn, wg, wu, wd, tmg, tf):
    R, D = xs.shape
    F = wg.shape[-1]
    nf = F // tf
    f_eff = lambda i, f, nu: jnp.where(i < nu[0], f, nf - 1)
    grid_spec = pltpu.PrefetchScalarGridSpec(
        num_scalar_prefetch=2, grid=(R // tmg, nf),
        in_specs=[pl.BlockSpec((tmg, D), lambda i, f, te, nu: (i, 0)),
                  pl.BlockSpec(fn.shape, lambda i, f, te, nu: (0, 0)),
                  pl.BlockSpec((1, D, tf), lambda i, f, te, nu: (te[i], 0, f_eff(i, f, nu))),
                  pl.BlockSpec((1, D, tf), lambda i, f, te, nu: (te[i], 0, f_eff(i, f, nu))),
                  pl.BlockSpec((1, tf, D), lambda i, f, te, nu: (te[i], f_eff(i, f, nu), 0))],
        out_specs=pl.BlockSpec((tmg, D), lambda i, f, te, nu: (i, 0)),
        scratch_shapes=[pltpu.VMEM((tmg, D), BF16)])
    return pl.pallas_call(
        _gmm_kernel, grid_spec=grid_spec, out_shape=jax.ShapeDtypeStruct((R, D), F32),
        compiler_params=_cparams(("arbitrary", "arbitrary")), name="experts",
    )(tile_expert, n_used, xs, fn, wg, wu, wd)


def _combine_kernel(d_hbm, h_ref, meta_ref, ys_hbm, o_ref, idx_s, ya_s, yb_s, isem, rsem):
    tm = h_ref.shape[0]
    i, n = pl.program_id(0), pl.num_programs(0)
    slot = i % 2

    def fetch(tile, s):
        _tile_index_copy(d_hbm, idx_s, isem, tile, s).wait()

        def body(g, c):
            r0 = pl.multiple_of(g * SUBLANES, SUBLANES)
            base = s * (TOP_K * tm) + r0
            for u in range(SUBLANES):
                pltpu.make_async_copy(ys_hbm.at[pl.ds(idx_s[base + u], 1)],
                                      ya_s.at[s, pl.ds(r0 + u, 1)], rsem.at[s]).start(priority=0)
                pltpu.make_async_copy(ys_hbm.at[pl.ds(idx_s[base + tm + u], 1)],
                                      yb_s.at[s, pl.ds(r0 + u, 1)], rsem.at[s]).start(priority=1)
            return c

        lax.fori_loop(0, tm // SUBLANES, body, 0)

    @pl.when(i == 0)
    def _():
        _tile_index_copy(d_hbm, idx_s, isem, 0, 0).start()
        fetch(0, 0)

        @pl.when(n > 1)
        def _():
            _tile_index_copy(d_hbm, idx_s, isem, 1, 1).start()

    @pl.when(i + 1 < n)
    def _():
        fetch(i + 1, 1 - slot)

    @pl.when(i + 2 < n)
    def _():
        _tile_index_copy(d_hbm, idx_s, isem, i + 2, slot).start()

    pltpu.make_async_copy(ys_hbm.at[pl.ds(0, tm)], ya_s.at[slot], rsem.at[slot]).wait()
    pltpu.make_async_copy(ys_hbm.at[pl.ds(0, tm)], yb_s.at[slot], rsem.at[slot]).wait()
    meta = meta_ref[...]
    o_ref[...] = (h_ref[...] + meta[:, L_W1:L_W1 + 1] * ya_s[slot]
                  + meta[:, L_W2:L_W2 + 1] * yb_s[slot])


def _combine(d_tiles, h, meta, ys, tm):
    T, D = h.shape
    return pl.pallas_call(
        _combine_kernel, grid=(T // tm,),
        in_specs=[pl.BlockSpec(memory_space=pl.ANY), pl.BlockSpec((tm, D), lambda i: (i, 0)),
                  pl.BlockSpec((tm, LANES), lambda i: (i, 0)), pl.BlockSpec(memory_space=pl.ANY)],
        out_specs=pl.BlockSpec((tm, D), lambda i: (i, 0)),
        out_shape=jax.ShapeDtypeStruct((T, D), F32),
        scratch_shapes=[pltpu.SMEM((2 * TOP_K * tm,), jnp.int32), pltpu.VMEM((2, tm, D), F32),
                        pltpu.VMEM((2, tm, D), F32), pltpu.SemaphoreType.DMA((2,)),
                        pltpu.SemaphoreType.DMA((2,))],
        compiler_params=_cparams(("arbitrary",)), name="combine",
    )(d_tiles, h, meta, ys)


def _moe(h, fn, w_router, wg, wu, wd, tm, tmg, tf):
    T, D = h.shape
    E = w_router.shape[-1]
    wr = jnp.pad(w_router, ((0, 0), (0, LANES - E)))
    wr_hi = wr.astype(BF16)
    wr_lo = (wr - wr_hi.astype(F32)).astype(BF16)
    meta, counts = _router(h, fn, jnp.concatenate([wr_hi, wr_lo], axis=1), E, tm)
    cnt = counts[0, :E].astype(jnp.int32)
    padded = ((cnt + tmg - 1) // tmg) * tmg
    ends = jnp.cumsum(padded)
    off = ends - padded
    n_tiles = -(-TOP_K * T // tmg) + E
    tile_expert = jnp.minimum(
        jnp.sum(ends[None, :] <= (jnp.arange(n_tiles, dtype=jnp.int32) * tmg)[:, None], axis=1), E - 1
    ).astype(jnp.int32)
    n_used = (ends[-1] // tmg).astype(jnp.int32).reshape(1)
    e12 = meta[:, L_E1:L_E2 + 1].astype(jnp.int32)
    off12 = jnp.sum(jnp.where(e12[..., None] == jnp.arange(E, dtype=jnp.int32), off, 0), axis=-1)
    dest = off12 + meta[:, L_R1:L_R2 + 1].astype(jnp.int32)
    d_tiles = dest.reshape(T // tm, tm, TOP_K).transpose(0, 2, 1).reshape(T // tm, TOP_K * tm)

    trailing = ends[-1] + jnp.arange(E + 1, dtype=jnp.int32) * tmg
    fill_starts = jnp.concatenate([
        jnp.where(padded > 0, ends - tmg, -1),
        jnp.where(trailing < n_tiles * tmg, trailing, -1)]).astype(jnp.int32)
    xs = _dispatch(fill_starts, d_tiles, h, n_tiles * tmg, tm, tmg)
    ys = _gmm(tile_expert, n_used, xs, fn, wg, wu, wd, tmg, tf)
    return _combine(d_tiles, h, meta, ys, tm)


def _ple_kernel(h_ref, pn_ref, wg_ref, wp_ref, fin_ref, *refs, final, n_prompt_tiles):
    *p_refs, o_ref = refs
    if len(p_refs) == 2:
        p = jnp.where(pl.program_id(0) < n_prompt_tiles, p_refs[0][...], p_refs[1][...])
    else:
        p = p_refs[0][...]
    hh = h_ref[...]
    gate = jax.nn.sigmoid(_dot(_rms(hh, pn_ref[...]).astype(BF16), wg_ref[...]))
    out = hh + gate * _dot(p.astype(BF16), wp_ref[...])
    if final:
        out = _rms(out, fin_ref[...])
    o_ref[...] = out


def _ple(h, pn, wg, p, wp, fin, tm, final, row0=0, rows=None):
    D = h.shape[1]
    rows = h.shape[0] if rows is None else rows
    off = row0 // tm
    full = lambda a: pl.BlockSpec(a.shape, lambda i: (0,) * a.ndim)
    if isinstance(p, tuple):
        npt = p[0].shape[0] // tm
        p_specs = list(_stream_specs(tm, p[0].shape[1], npt))
    else:
        npt, p_specs, p = 0, [pl.BlockSpec((tm, p.shape[1]), lambda i: (i, 0))], (p,)
    return pl.pallas_call(
        functools.partial(_ple_kernel, final=final, n_prompt_tiles=npt), grid=(rows // tm,),
        in_specs=[pl.BlockSpec((tm, D), lambda i: (off + i, 0)), full(pn), full(wg), full(wp),
                  full(fin)] + p_specs,
        out_specs=pl.BlockSpec((tm, D), lambda i: (i, 0)),
        out_shape=jax.ShapeDtypeStruct((rows, D), F32),
        compiler_params=_cparams(("parallel",)), name="ple",
    )(h, pn, wg, wp, fin, *p)


def _rope_tables(pos):
    pos = pos.astype(F32)[:, None]
    lane = np.arange(LANES)
    freqs = lambda half: ROPE_THETA ** (-jnp.arange(half, dtype=F32) / half)
    a_ret, a_mla = pos * freqs(RET_HALF)[None, :], pos * freqs(ROPE_HALF)[None, :]
    wide = lambda t: jnp.tile(t, (1, LANES // t.shape[1]))
    lo64 = jnp.asarray((lane % RET_DK) < RET_HALF)
    c64, s64 = wide(jnp.cos(a_ret)), wide(jnp.sin(a_ret))
    c16, s16 = wide(jnp.cos(a_mla)), wide(jnp.sin(a_mla))
    in_k = jnp.asarray(lane < MLA_ROPE)
    lo_k = jnp.asarray(lane < ROPE_HALF)
    hi_k = jnp.asarray((lane >= ROPE_HALF) & (lane < MLA_ROPE))
    return (c64, jnp.where(lo64, -s64, 0.0), jnp.where(lo64, 0.0, s64), c16, s16,
            jnp.where(in_k, c16, 0.0), jnp.where(lo_k, -s16, 0.0), jnp.where(hi_k, s16, 0.0))


def _ret_tables(L):
    f = np.float32
    log_g = np.log(f(1.0) - np.exp2(f(-5.0) - np.arange(RET_HEADS, dtype=f)))
    idx = np.arange(L, dtype=f)
    dist = np.abs(idx[:, None] - idx[None, :])
    d = np.exp(dist[None] * log_g[:, None, None])
    qdec = np.exp((idx[:, None] + f(1.0)) * log_g[None, :])
    kdec = np.exp((f(L) - f(1.0) - idx)[:, None] * log_g[None, :])
    sdec = np.exp(f(L) * log_g)
    wide = lambda t: np.repeat(t, RET_DK, axis=1).reshape(L, N_PAIRS, LANES).transpose(1, 0, 2)
    bm = np.kron(np.eye(2, dtype=f), np.ones((RET_DK, RET_DV), f))
    gs = np.repeat(sdec, RET_DK).reshape(N_PAIRS, LANES, 1) * bm[None]
    tabs = (d.reshape(N_PAIRS, 2 * L, L), wide(qdec), wide(kdec), gs, bm)
    return tuple(jnp.asarray(t, F32) for t in tabs)


def _state_to_pairs(s):
    B = s.shape[0]
    s = s.reshape(B, N_PAIRS, 2, RET_DK, RET_DV)
    eye = jnp.eye(2, dtype=s.dtype)
    out = s[:, :, :, :, None, :] * eye[None, None, :, None, :, None]
    return out.reshape(B, N_PAIRS, LANES, LANES)


def _pairs_to_state(sp):
    B = sp.shape[0]
    s = sp.reshape(B, N_PAIRS, 2, RET_DK, 2, RET_DV)
    return jnp.stack([s[:, :, 0, :, 0, :], s[:, :, 1, :, 1, :]], axis=2).reshape(B, RET_HEADS, RET_DK, RET_DV)


def _group_uq_columns(w_uq):
    half = MLA_ROPE // 2
    w = w_uq.reshape(w_uq.shape[0], MLA_HEADS, MLA_NOPE + MLA_ROPE)
    parts = (w[:, :, :MLA_NOPE], w[:, :, MLA_NOPE:MLA_NOPE + half], w[:, :, MLA_NOPE + half:])
    return jnp.concatenate([p.reshape(w_uq.shape[0], -1) for p in parts], axis=1)


def _big_query_weight(w_uk):
    H, half = MLA_HEADS, MLA_ROPE // 2
    eye = jnp.eye(H, dtype=w_uk.dtype)
    wpad = jnp.pad(w_uk, ((0, 0), (0, 0), (0, QK_PAD - KV_LORA)))
    top = (eye[:, None, :, None] * wpad[:, :, None, :]).reshape(H * MLA_NOPE, H * QK_PAD)
    sel = np.zeros((2 * H * half, H * QK_PAD), np.float32)
    for h in range(H):
        for f in range(half):
            sel[h * half + f, h * QK_PAD + KV_LORA + f] = 1.0
            sel[H * half + h * half + f, h * QK_PAD + KV_LORA + half + f] = 1.0
    return jnp.concatenate([top, jnp.asarray(sel, w_uk.dtype)], axis=0)


def kernel(x_prompt, x_sample, p_prompt, p_sample, cache_ckv, cache_krope, state_ret, attn_norm, w_in, q_norm, w_uq, kv_norm, w_uk, w_uv, ret_norm, w_o, ffn_norm, w_gate_d, w_up_d, w_down_d, w_router, w_gate_e, w_up_e, w_down_e, ple_norm, w_ple_gate, w_ple_proj, final_norm):
    Bp, S, D = x_prompt.shape
    Bs, L, _ = x_sample.shape
    depth = w_in.shape[0]
    P = cache_ckv.shape[2]
    Tp, Ts = Bp * S, Bs * L
    T = Tp + Ts
    assert S % CHUNK == 0 and P % CHUNK == 0 and L == CHUNK and Tp % CHUNK == 0
    assert w_router.shape[-1] >= TOP_K

    tm_proj = _pick(int(np.gcd(S, Ts)), (512, 256, 128, 64))
    tm_gmm = 1024
    rb = _pick(S, (512, 256, 128, 64))
    tq = _pick(S, (256, 128, 64))
    kb = _pick(S, (512, 256, 128))
    kb_s = 512
    sk_s = P + L
    sk_pad = -(-sk_s // kb_s) * kb_s

    pos = jnp.concatenate([jnp.arange(S, dtype=jnp.int32),
                           P + jnp.tile(jnp.arange(L, dtype=jnp.int32), Bs)])
    rope_tabs = _rope_tables(pos)
    n_pt, pt_per_seq = Tp // tm_proj, S // tm_proj
    tab_tile = lambda i: jnp.where(i < n_pt, i % pt_per_seq, pt_per_seq + i - n_pt)
    ret_tabs = _ret_tables(CHUNK)
    row2 = lambda v: v.reshape(1, -1)

    h = (x_prompt.reshape(Tp, D), x_sample.reshape(Ts, D))
    outs = {k: [] for k in ("ckv_p", "kro_p", "ret_p", "ckv_s", "kro_s", "ret_s")}
    for l in range(depth):
        win = jnp.pad(w_in[l], ((0, 0), (0, IN_COLS_PAD - IN_COLS))).astype(BF16)
        wuq = _group_uq_columns(w_uq[l]).astype(BF16)
        wbig = _big_query_weight(w_uk[l]).astype(BF16)
        qr, kr, vr, gr, qx, ckv, kro, kx, *h_cat = _proj(
            h, row2(attn_norm[l]), win, row2(q_norm[l]), wuq, row2(kv_norm[l]), wbig, rope_tabs,
            tab_tile, tm_proj)
        if h_cat:
            h, = h_cat

        rn = row2(ret_norm[l])
        zero_state = jnp.zeros((Bp, N_PAIRS, LANES, LANES), F32)
        o_ret_p, st_p = _retention(qr, kr, vr, gr, zero_state, ret_tabs, rn, Bp, S, 0, rb)
        o_ret_s, st_s = _retention(qr, kr, vr, gr, _state_to_pairs(state_ret[l].astype(F32)),
                                   ret_tabs, rn, Bs, L, Tp, L)

        wuv = w_uv[l].astype(BF16)
        wuv_big = (jnp.eye(MLA_HEADS, dtype=BF16)[:, None, :, None] * wuv[:, :, None, :]
                   ).reshape(MLA_HEADS // 2, 2 * KV_LORA, MLA_W)
        kxt = kx[:Tp].reshape(Bp, S, QK_PAD).transpose(0, 2, 1).reshape(Bp * QK_PAD, S)
        o_mla_p = _attention_t(qx, kx, kxt, wuv_big, Bp, S, 0, tq, kb, 0, S, S)
        cache_kx = jnp.concatenate(
            [cache_ckv[l], cache_krope[l], jnp.zeros((Bs, P, QK_PAD - KV_LORA - MLA_ROPE), F32)],
            axis=-1).astype(BF16)
        kx_s = jnp.concatenate([cache_kx, kx[Tp:].reshape(Bs, L, QK_PAD),
                                jnp.zeros((Bs, sk_pad - sk_s, QK_PAD), BF16)], axis=1)
        o_mla_s = _attention(qx, kx_s.reshape(Bs * sk_pad, QK_PAD), wuv_big, Bs, L, Tp, L, kb_s, P,
                             sk_pad, sk_s)

        fn = row2(ffn_norm[l])
        j = l // 2
        wo = w_o[l].astype(BF16)
        ple_w = (row2(ple_norm[l]), w_ple_gate[l].astype(BF16))
        wp = w_ple_proj[l].astype(BF16)
        p_l = (p_prompt[l].reshape(Tp, -1), p_sample[l].reshape(Ts, -1))
        last = l == depth - 1
        if l % 2 == 0 and not last:
            h = _dense_layer(h, o_ret_p, o_mla_p, o_ret_s, o_mla_s, *p_l, wo, fn,
                             w_gate_d[j].astype(BF16), w_up_d[j].astype(BF16),
                             w_down_d[j].astype(BF16), *ple_w, wp, tm_proj)
        else:
            h = _merge(h, o_ret_p, o_mla_p, o_ret_s, o_mla_s, wo, tm_proj)
            if l % 2 == 0:
                h = _ffn(h, fn, w_gate_d[j].astype(BF16), w_up_d[j].astype(BF16),
                         w_down_d[j].astype(BF16), tm_proj)
            else:
                tf = _pick(w_gate_e.shape[-1], (512, 256, 128))
                h = _moe(h, fn, w_router[j], w_gate_e[j], w_up_e[j], w_down_e[j], tm_proj, tm_gmm, tf)
            if not last:
                h = _ple(h, *ple_w, p_l, wp, row2(final_norm), tm_proj, False)
            else:
                y_p = _ple(h, *ple_w, p_l[0], wp, row2(final_norm), tm_proj, True, 0, Tp)
                y_s = _ple(h, *ple_w, p_l[1], wp, row2(final_norm), tm_proj, True, Tp, Ts)

        outs["ckv_p"].append(ckv[:Tp].reshape(Bp, S, KV_LORA))
        outs["kro_p"].append(kro[:Tp].reshape(Bp, S, MLA_ROPE))
        outs["ret_p"].append(_pairs_to_state(st_p))
        outs["ckv_s"].append(ckv[Tp:].reshape(Bs, L, KV_LORA))
        outs["kro_s"].append(kro[Tp:].reshape(Bs, L, MLA_ROPE))
        outs["ret_s"].append(_pairs_to_state(st_s))

    return (y_p.reshape(Bp, S, D), y_s.reshape(Bs, L, D),
            jnp.stack(outs["ckv_p"]), jnp.stack(outs["kro_p"]), jnp.stack(outs["ret_p"]),
            jnp.stack(outs["ckv_s"]), jnp.stack(outs["kro_s"]), jnp.stack(outs["ret_s"]))
```

```python
import functools

import numpy as np
import jax
import jax.numpy as jnp
from jax import lax
from jax.experimental import pallas as pl
from jax.experimental.pallas import tpu as pltpu

F32 = jnp.float32
BF16 = jnp.bfloat16

CHUNK = 64
CHUNK_SHIFT = 6
RMS_EPS = 1e-6
ROPE_THETA = 10000.0
RET_HEADS = 8
RET_DK = 64
RET_DV = 64
RET_W = RET_HEADS * RET_DK
MLA_HEADS = 8
MLA_NOPE = 64
MLA_ROPE = 32
MLA_V = 64
Q_LORA = 256
KV_LORA = 128
MLA_W = MLA_HEADS * MLA_V
RET_HALF = RET_DK // 2
ROPE_HALF = MLA_ROPE // 2
TOP_K = 2
N_PAIRS = RET_HEADS // 2
LANES = 128
SUBLANES = 8
QK_PAD = 256
OFF_Q = RET_W
OFF_K = OFF_Q + RET_W
OFF_V = OFF_K + RET_W
OFF_G = OFF_V + RET_W
OFF_CQ = OFF_G + Q_LORA
OFF_CKV = OFF_CQ + KV_LORA
IN_COLS = OFF_CKV + MLA_ROPE
IN_COLS_PAD = OFF_CKV + LANES
NEG = -1e30
VMEM_LIMIT = 56 * 1024 * 1024


def _pick(n, cands):
    for c in cands:
        if n % c == 0:
            return c
    return n


def _cparams(sem, flags=None):
    return pltpu.CompilerParams(dimension_semantics=sem, vmem_limit_bytes=VMEM_LIMIT, flags=flags)


def _rms(x, g):
    return x * lax.rsqrt(jnp.mean(x * x, axis=-1, keepdims=True) + RMS_EPS) * g


def _dot(a, b):
    return jnp.dot(a, b, preferred_element_type=F32)


def _dot_nt(a, b):
    return lax.dot_general(a, b, (((1,), (1,)), ((), ())), preferred_element_type=F32)


def _dot_tn(a, b):
    return lax.dot_general(a, b, (((0,), (0,)), ((), ())), preferred_element_type=F32)


def _proj_kernel(h_ref, *refs):
    _proj_body(h_ref[...], *refs)


def _proj_streams_kernel(hp_ref, hs_ref, *refs, n_prompt_tiles):
    x = jnp.where(pl.program_id(0) < n_prompt_tiles, hp_ref[...], hs_ref[...])
    refs[-1][...] = x
    _proj_body(x, *refs[:-1])


def _proj_body(x, an_ref, win_ref, qn_ref, wuq_ref, kvn_ref, wbig_ref,
               c64_ref, sm64_ref, sp64_ref, c16_ref, s16_ref, ck_ref, smk_ref, spk_ref,
               qr_ref, kr_ref, vr_ref, gr_ref, qx_ref, ckv_ref, kro_ref, kx_ref):
    xn = _rms(x, an_ref[...]).astype(BF16)
    c64, sm64, sp64 = c64_ref[...], sm64_ref[...], sp64_ref[...]

    def rope64(z):
        return (z * c64 + pltpu.roll(z, LANES - RET_HALF, 1) * sm64
                + pltpu.roll(z, RET_HALF, 1) * sp64)

    for lo in range(0, RET_W, QK_PAD):
        zq = _dot(xn, win_ref[:, lo:lo + QK_PAD])
        zk = _dot(xn, win_ref[:, OFF_Q + lo:OFF_Q + lo + QK_PAD])
        for half in range(0, QK_PAD, LANES):
            dst = slice(lo + half, lo + half + LANES)
            qr_ref[:, dst] = rope64(zq[:, half:half + LANES])
            kr_ref[:, dst] = rope64(zk[:, half:half + LANES]) * (RET_DK ** -0.5)

    zkv = _dot(xn, win_ref[:, OFF_CQ:IN_COLS_PAD])
    ckv = _rms(zkv[:, :KV_LORA], kvn_ref[...])
    ckv_ref[...] = ckv
    zk = zkv[:, KV_LORA:]
    kro = (zk * ck_ref[...] + pltpu.roll(zk, LANES - ROPE_HALF, 1) * smk_ref[...]
           + pltpu.roll(zk, ROPE_HALF, 1) * spk_ref[...])
    kro_ref[...] = kro[:, :MLA_ROPE]
    kx_ref[:, :KV_LORA] = ckv.astype(BF16)
    kx_ref[:, KV_LORA:] = kro.astype(BF16)

    cq = _dot(xn, win_ref[:, OFF_G:OFF_CQ])
    q = _dot(_rms(cq, qn_ref[...]).astype(BF16), wuq_ref[...])
    nq = MLA_HEADS * MLA_NOPE
    x1, x2 = q[:, nq:nq + LANES], q[:, nq + LANES:nq + 2 * LANES]
    c16, s16 = c16_ref[...], s16_ref[...]
    scale = (MLA_NOPE + MLA_ROPE) ** -0.5 * float(np.log2(np.e))
    qcat = jnp.concatenate([q[:, :nq], x1 * c16 - x2 * s16, x2 * c16 + x1 * s16], axis=-1)
    qcat = (qcat * scale).astype(BF16)
    half_w = (MLA_HEADS // 2) * QK_PAD
    for g in range(2):
        rows = slice(g * (nq // 2), (g + 1) * (nq // 2))
        cols = slice(g * half_w, (g + 1) * half_w)
        qx_ref[:, cols] = (_dot(qcat[:, rows], wbig_ref[rows, cols])
                           + _dot(qcat[:, nq:], wbig_ref[nq:, cols])).astype(BF16)

    vr_ref[...] = _dot(xn, win_ref[:, OFF_K:OFF_V]).astype(BF16)
    gr_ref[...] = _dot(xn, win_ref[:, OFF_V:OFF_G])


def _stream_specs(tm, width, n_prompt_tiles):
    return (pl.BlockSpec((tm, width), lambda i: (jnp.minimum(i, n_prompt_tiles - 1), 0)),
            pl.BlockSpec((tm, width), lambda i: (jnp.maximum(i - n_prompt_tiles, 0), 0)))


def _proj(h, an, win, qn, wuq, kvn, wbig, tabs, tab_tile, tm):
    streams = isinstance(h, tuple)
    T = sum(a.shape[0] for a in h) if streams else h.shape[0]
    D = h[0].shape[1] if streams else h.shape[1]
    row = lambda w: pl.BlockSpec((tm, w), lambda i: (i, 0))
    tab = pl.BlockSpec((tm, LANES), lambda i: (tab_tile(i), 0))
    full = lambda a: pl.BlockSpec(a.shape, lambda i: (0,) * a.ndim)
    out_shapes = (
        jax.ShapeDtypeStruct((T, RET_W), F32), jax.ShapeDtypeStruct((T, RET_W), F32),
        jax.ShapeDtypeStruct((T, RET_W), BF16), jax.ShapeDtypeStruct((T, RET_W), F32),
        jax.ShapeDtypeStruct((T, MLA_HEADS * QK_PAD), BF16),
        jax.ShapeDtypeStruct((T, KV_LORA), F32), jax.ShapeDtypeStruct((T, MLA_ROPE), F32),
        jax.ShapeDtypeStruct((T, QK_PAD), BF16))
    out_specs = (row(RET_W), row(RET_W), row(RET_W), row(RET_W), row(MLA_HEADS * QK_PAD),
                 row(KV_LORA), row(MLA_ROPE), row(QK_PAD))
    if streams:
        npt = h[0].shape[0] // tm
        kern = functools.partial(_proj_streams_kernel, n_prompt_tiles=npt)
        h_specs, h_args = list(_stream_specs(tm, D, npt)), h
        out_specs += (row(D),)
        out_shapes += (jax.ShapeDtypeStruct((T, D), F32),)
    else:
        kern, h_specs, h_args = _proj_kernel, [row(D)], (h,)
    return pl.pallas_call(
        kern, grid=(T // tm,),
        in_specs=h_specs + [full(an), full(win), full(qn), full(wuq), full(kvn), full(wbig)]
                 + [tab] * 8,
        out_specs=out_specs,
        out_shape=out_shapes, compiler_params=_cparams(("parallel",)), name="proj",
    )(*h_args, an, win, qn, wuq, kvn, wbig, *tabs)


def _ret_kernel(q_ref, k_ref, v_ref, g_ref, s0_ref, d_ref, qd_ref, kd_ref, gs_ref, bm_ref, rn_ref,
                o_ref, so_ref, st_ref, *, n_chunks):
    i = pl.program_id(1)

    @pl.when(i == 0)
    def _():
        st_ref[...] = s0_ref[0]

    lane = lax.broadcasted_iota(jnp.int32, (CHUNK, LANES), 1)
    first = lane < RET_DK

    def chunk(c, carry):
        rows = pl.ds(pl.multiple_of(c * CHUNK, CHUNK), CHUNK)
        for p in range(N_PAIRS):
            cols = slice(p * LANES, (p + 1) * LANES)
            qp, kp, vp = q_ref[rows, cols], k_ref[rows, cols], v_ref[rows, cols]
            q2 = jnp.concatenate([jnp.where(first, qp, 0.0), jnp.where(first, 0.0, qp)], axis=0)
            sd = (_dot_nt(q2.astype(BF16), kp.astype(BF16)) * d_ref[p]).astype(BF16)
            qq = (qp * qd_ref[p]).astype(BF16)
            st = st_ref[p]
            lhs = jnp.concatenate([jnp.concatenate([qq, qq], axis=0), sd], axis=1)
            rhs = jnp.concatenate([st.astype(BF16), vp], axis=0)
            o2 = _dot(lhs, rhs)
            o = jnp.where(first, o2[:CHUNK], o2[CHUNK:])
            upd = _dot_tn((kp * kd_ref[p]).astype(BF16), vp)
            st_ref[p] = st * gs_ref[p] + upd * bm_ref[...]
            oo = o * o
            ss_a = jnp.sum(jnp.where(first, oo, 0.0), axis=-1, keepdims=True)
            ss_b = jnp.sum(jnp.where(first, 0.0, oo), axis=-1, keepdims=True)
            rs = jnp.where(first, lax.rsqrt(ss_a * (1.0 / RET_DV) + RMS_EPS),
                           lax.rsqrt(ss_b * (1.0 / RET_DV) + RMS_EPS))
            g = g_ref[rows, cols]
            o_ref[rows, cols] = (o * rs * rn_ref[:, cols] * (g * jax.nn.sigmoid(g))).astype(BF16)
        return carry

    lax.fori_loop(0, n_chunks, chunk, 0, unroll=min(n_chunks, 4))

    @pl.when(i == pl.num_programs(1) - 1)
    def _():
        so_ref[0] = st_ref[...]


def _retention(qr, kr, vr, gr, s0, tabs, rn, nb, seq, row0, rb):
    nblk = seq // rb
    off = row0 // rb
    tok = pl.BlockSpec((rb, RET_W), lambda b, i: (off + b * nblk + i, 0))
    full = lambda a: pl.BlockSpec(a.shape, lambda b, i: (0,) * a.ndim)
    st_spec = pl.BlockSpec((1, N_PAIRS, LANES, LANES), lambda b, i: (b, 0, 0, 0))
    d, qd, kd, gs, bm = tabs
    return pl.pallas_call(
        functools.partial(_ret_kernel, n_chunks=rb // CHUNK), grid=(nb, nblk),
        in_specs=[tok, tok, tok, tok, st_spec, full(d), full(qd), full(kd), full(gs), full(bm), full(rn)],
        out_specs=(pl.BlockSpec((rb, RET_W), lambda b, i: (b * nblk + i, 0)), st_spec),
        out_shape=(jax.ShapeDtypeStruct((nb * seq, RET_W), BF16),
                   jax.ShapeDtypeStruct((nb, N_PAIRS, LANES, LANES), F32)),
        scratch_shapes=[pltpu.VMEM((N_PAIRS, LANES, LANES), F32)],
        compiler_params=_cparams(("parallel", "arbitrary")), name="retention",
    )(qr, kr, vr, gr, s0, d, qd, kd, gs, bm, rn)


def _attn_kernel(q_ref, k_ref, wuv_ref, o_ref, q_s, s0_s, s1_s, m_s, l_s, acc_s,
                 *, tq, kb, q_pos0, sk_valid):
    i = pl.program_id(1)
    R = MLA_HEADS * tq
    for h in range(MLA_HEADS):
        q_s[h * tq:(h + 1) * tq, :] = q_ref[:, h * QK_PAD:(h + 1) * QK_PAD]
    m_s[...] = jnp.full(m_s.shape, NEG, F32)
    l_s[...] = jnp.zeros(l_s.shape, F32)
    acc_s[...] = jnp.zeros(acc_s.shape, F32)

    chunk_end = lambda t: ((t >> CHUNK_SHIFT) + 1) << CHUNK_SHIFT
    kb_shift = kb.bit_length() - 1
    qstart = q_pos0 + i * tq
    lim_first = jnp.minimum(chunk_end(qstart), sk_valid)
    lim_last = jnp.minimum(chunk_end(qstart + tq - 1), sk_valid)
    n_full = lim_first >> kb_shift
    n_blk = (lim_last + kb - 1) >> kb_shift

    last = n_blk - 1

    def key_block(j):
        return k_ref[pl.ds(pl.multiple_of(j * kb, kb), kb), :]

    def scores(j, s_ref):
        s_ref[...] = _dot_nt(q_s[...], key_block(j))

    def update(j, s_ref, masked):
        s = s_ref[...]
        if masked:
            tok = lax.broadcasted_iota(jnp.int32, (R, 1), 0) & (tq - 1)
            row_lim = jnp.minimum(chunk_end(qstart + tok), sk_valid)
            kidx = j * kb + lax.broadcasted_iota(jnp.int32, (1, kb), 1)
            s = jnp.where(kidx < row_lim, s, NEG)
        m_prev = m_s[...]
        m_next = jnp.maximum(m_prev, jnp.max(s, axis=-1, keepdims=True))
        p = jnp.exp2(s - jnp.tile(m_next, (1, kb // LANES)))
        alpha = jnp.exp2(m_prev - m_next)
        p_lanes = p[:, :LANES]
        for c in range(LANES, kb, LANES):
            p_lanes = p_lanes + p[:, c:c + LANES]
        l_s[...] = alpha * l_s[...] + p_lanes
        acc_s[...] = alpha * acc_s[...] + _dot(p.astype(BF16), key_block(j)[:, :KV_LORA])
        m_s[...] = m_next

    n_pipe = jnp.minimum(n_full, last)
    odd = n_pipe & 1

    @pl.when(odd == 1)
    def _():
        scores(0, s0_s)
        update(0, s0_s, False)

    scores(odd, s0_s)

    def pair(jj, c):
        j = odd + 2 * jj
        scores(j + 1, s1_s)
        update(j, s0_s, False)
        scores(j + 2, s0_s)
        update(j + 1, s1_s, False)
        return c

    lax.fori_loop(0, n_pipe >> 1, pair, 0)
    update(n_pipe, s0_s, True)

    def tail(j, c):
        scores(j, s0_s)
        update(j, s0_s, True)
        return c

    lax.fori_loop(n_pipe + 1, n_blk, tail, 0)

    o_lat = (acc_s[...] / jnp.sum(l_s[...], axis=-1, keepdims=True)).astype(BF16)
    out = None
    for p in range(MLA_HEADS // 2):
        pair = jnp.concatenate([o_lat[(2 * p) * tq:(2 * p + 1) * tq],
                                o_lat[(2 * p + 1) * tq:(2 * p + 2) * tq]], axis=1)
        term = _dot(pair, wuv_ref[p])
        out = term if out is None else out + term
    o_ref[...] = out.astype(BF16)


def _attn_cached_kernel(q_ref, ckv_ref, kro_ref, knew_ref, wuv_ref, o_ref, k_s, *scratch,
                        past, **static):
    n_new = knew_ref.shape[0]
    k_s[0:past, :KV_LORA] = ckv_ref[0].astype(BF16)
    k_s[0:past, KV_LORA:] = jnp.zeros((past, QK_PAD - KV_LORA), BF16)
    k_s[0:past, KV_LORA:KV_LORA + MLA_ROPE] = kro_ref[0].astype(BF16)
    k_s[past:past + n_new, :] = knew_ref[...]
    k_s[past + n_new:, :] = jnp.zeros((k_s.shape[0] - past - n_new, QK_PAD), BF16)
    _attn_kernel(q_ref, k_s, wuv_ref, o_ref, *scratch, **static)


def _attention_cached(qx, cache_ckv, cache_krope, kx, wuv, row0, kb):
    nb, past, _ = cache_ckv.shape
    L = (qx.shape[0] - row0) // nb
    sk_valid = past + L
    sk = -(-sk_valid // kb) * kb
    R = MLA_HEADS * L
    off = row0 // L
    return pl.pallas_call(
        functools.partial(_attn_cached_kernel, past=past, tq=L, kb=kb, q_pos0=past,
                          sk_valid=sk_valid),
        grid=(nb, 1),
        in_specs=[pl.BlockSpec((L, MLA_HEADS * QK_PAD), lambda b, i: (off + b, 0)),
                  pl.BlockSpec((1, past, KV_LORA), lambda b, i: (b, 0, 0)),
                  pl.BlockSpec((1, past, MLA_ROPE), lambda b, i: (b, 0, 0)),
                  pl.BlockSpec((L, QK_PAD), lambda b, i: (off + b, 0)),
                  pl.BlockSpec(wuv.shape, lambda b, i: (0, 0, 0))],
        out_specs=pl.BlockSpec((L, MLA_W), lambda b, i: (b, 0)),
        out_shape=jax.ShapeDtypeStruct((nb * L, MLA_W), BF16),
        scratch_shapes=[pltpu.VMEM((sk, QK_PAD), BF16), pltpu.VMEM((R, QK_PAD), BF16),
                        pltpu.VMEM((R, kb), F32), pltpu.VMEM((R, kb), F32),
                        pltpu.VMEM((R, LANES), F32), pltpu.VMEM((R, LANES), F32),
                        pltpu.VMEM((R, KV_LORA), F32)],
        compiler_params=_cparams(("parallel", "arbitrary")),
        name="attention_cached",
    )(qx, cache_ckv, cache_krope, kx, wuv)


def _attention(qx, kx, wuv, nb, seq_q, row0, tq, kb, q_pos0, sk, sk_valid):
    nq = seq_q // tq
    off = row0 // tq
    R = MLA_HEADS * tq
    return pl.pallas_call(
        functools.partial(_attn_kernel, tq=tq, kb=kb, q_pos0=q_pos0, sk_valid=sk_valid),
        grid=(nb, nq),
        in_specs=[pl.BlockSpec((tq, MLA_HEADS * QK_PAD), lambda b, i: (off + b * nq + i, 0)),
                  pl.BlockSpec((sk, QK_PAD), lambda b, i: (b, 0)),
                  pl.BlockSpec(wuv.shape, lambda b, i: (0, 0, 0))],
        out_specs=pl.BlockSpec((tq, MLA_W), lambda b, i: (b * nq + i, 0)),
        out_shape=jax.ShapeDtypeStruct((nb * seq_q, MLA_W), BF16),
        scratch_shapes=[pltpu.VMEM((R, QK_PAD), BF16), pltpu.VMEM((R, kb), F32),
                        pltpu.VMEM((R, kb), F32), pltpu.VMEM((R, LANES), F32),
                        pltpu.VMEM((R, LANES), F32), pltpu.VMEM((R, KV_LORA), F32)],
        compiler_params=_cparams(("parallel", "arbitrary")),
        name="attention",
    )(qx, kx, wuv)


def _merge_kernel(h_ref, ap_ref, bp_ref, as_ref, bs_ref, wo_ref, o_ref, *, n_prompt_tiles):
    def project(a_ref, b_ref):
        o_ref[...] = (h_ref[...] + _dot(a_ref[...], wo_ref[:RET_W, :])
                      + _dot(b_ref[...], wo_ref[RET_W:, :]))

    @pl.when(pl.program_id(0) < n_prompt_tiles)
    def _():
        project(ap_ref, bp_ref)

    @pl.when(pl.program_id(0) >= n_prompt_tiles)
    def _():
        project(as_ref, bs_ref)


def _merge(h, a_p, b_p, a_s, b_s, wo, tm):
    T, D = h.shape
    npt = a_p.shape[0] // tm
    row = lambda w: pl.BlockSpec((tm, w), lambda i: (i, 0))
    (a_prm, a_smp), (b_prm, b_smp) = _stream_specs(tm, RET_W, npt), _stream_specs(tm, MLA_W, npt)
    return pl.pallas_call(
        functools.partial(_merge_kernel, n_prompt_tiles=npt), grid=(T // tm,),
        in_specs=[row(D), a_prm, b_prm, a_smp, b_smp, pl.BlockSpec(wo.shape, lambda i: (0, 0))],
        out_specs=row(D), out_shape=jax.ShapeDtypeStruct((T, D), F32),
        compiler_params=_cparams(("parallel",)), name="merge",
    )(h, a_p, b_p, a_s, b_s, wo)


def _ffn_kernel(h_ref, fn_ref, wg_ref, wu_ref, wd_ref, o_ref):
    hh = h_ref[...]
    u = _rms(hh, fn_ref[...]).astype(BF16)
    a = _dot(u, wg_ref[...])
    mid = (a * jax.nn.sigmoid(a) * _dot(u, wu_ref[...])).astype(BF16)
    o_ref[...] = hh + _dot(mid, wd_ref[...])


def _ffn(h, fn, wg, wu, wd, tm):
    T, D = h.shape
    resident = lambda a: pl.BlockSpec(a.shape, lambda i: (0, 0), pipeline_mode=pl.Buffered(1))
    return pl.pallas_call(
        _ffn_kernel, grid=(T // tm,),
        in_specs=[pl.BlockSpec((tm, D), lambda i: (i, 0)), pl.BlockSpec(fn.shape, lambda i: (0, 0)),
                  resident(wg), resident(wu), resident(wd)],
        out_specs=pl.BlockSpec((tm, D), lambda i: (i, 0)),
        out_shape=jax.ShapeDtypeStruct((T, D), F32),
        compiler_params=_cparams(("parallel",)), name="ffn",
    )(h, fn, wg, wu, wd)


def _dense_layer_kernel(h_ref, ap_ref, bp_ref, as_ref, bs_ref, pp_ref, ps_ref, wo_ref, fn_ref,
                        wg_ref, wu_ref, wd_ref, pn_ref, wpg_ref, wpp_ref, o_ref, *, n_prompt_tiles):
    prompt = pl.program_id(0) < n_prompt_tiles
    a = jnp.where(prompt, ap_ref[...], as_ref[...])
    b = jnp.where(prompt, bp_ref[...], bs_ref[...])
    h1 = h_ref[...] + _dot(a, wo_ref[:RET_W, :]) + _dot(b, wo_ref[RET_W:, :])
    u = _rms(h1, fn_ref[...]).astype(BF16)
    g = _dot(u, wg_ref[...])
    mid = (g * jax.nn.sigmoid(g) * _dot(u, wu_ref[...])).astype(BF16)
    h2 = h1 + _dot(mid, wd_ref[...])
    gate = jax.nn.sigmoid(_dot(_rms(h2, pn_ref[...]).astype(BF16), wpg_ref[...]))
    p = jnp.where(prompt, pp_ref[...], ps_ref[...]).astype(BF16)
    o_ref[...] = h2 + gate * _dot(p, wpp_ref[...])


def _dense_layer(h, a_p, b_p, a_s, b_s, p_p, p_s, wo, fn, wg, wu, wd, pn, wpg, wpp, tm):
    T, D = h.shape
    npt = a_p.shape[0] // tm
    row = pl.BlockSpec((tm, D), lambda i: (i, 0))
    resident = lambda w: pl.BlockSpec(w.shape, lambda i: (0, 0), pipeline_mode=pl.Buffered(1))
    (a_prm, a_smp), (b_prm, b_smp) = _stream_specs(tm, RET_W, npt), _stream_specs(tm, MLA_W, npt)
    p_prm, p_smp = _stream_specs(tm, p_p.shape[1], npt)
    return pl.pallas_call(
        functools.partial(_dense_layer_kernel, n_prompt_tiles=npt), grid=(T // tm,),
        in_specs=[row, a_prm, b_prm, a_smp, b_smp, p_prm, p_smp] + [
            resident(w) for w in (wo, fn, wg, wu, wd, pn, wpg, wpp)],
        out_specs=row, out_shape=jax.ShapeDtypeStruct((T, D), F32),
        compiler_params=_cparams(("parallel",)), name="dense_layer",
    )(h, a_p, b_p, a_s, b_s, p_p, p_s, wo, fn, wg, wu, wd, pn, wpg, wpp)


L_E1, L_E2, L_W1, L_W2, L_R1, L_R2 = range(6)


def _merge_router_kernel(h_ref, ap_ref, bp_ref, as_ref, bs_ref, wo_ref, fn_ref, wr_ref,
                         h1_ref, meta_ref, cnt_ref, run_s, *, n_prompt_tiles, n_experts):
    tm = h_ref.shape[0]
    prompt = pl.program_id(0) < n_prompt_tiles
    a = jnp.where(prompt, ap_ref[...], as_ref[...])
    b = jnp.where(prompt, bp_ref[...], bs_ref[...])
    h1 = h_ref[...] + _dot(a, wo_ref[:RET_W, :]) + _dot(b, wo_ref[RET_W:, :])
    h1_ref[...] = h1

    @pl.when(pl.program_id(0) == 0)
    def _():
        run_s[...] = jnp.zeros(run_s.shape, F32)

    u = _rms(h1, fn_ref[...])
    u_hi = u.astype(BF16)
    u_lo = (u - u_hi.astype(F32)).astype(BF16)
    hi_terms = _dot(u_hi, wr_ref[...])
    logits = hi_terms[:, :LANES] + hi_terms[:, LANES:] + _dot(u_lo, wr_ref[:, :LANES])
    lane = lax.broadcasted_iota(jnp.int32, logits.shape, 1).astype(F32)
    lg = jnp.where(lane < n_experts, logits, NEG)
    m1 = jnp.max(lg, axis=-1, keepdims=True)
    i1 = jnp.min(jnp.where(lg == m1, lane, float(LANES)), axis=-1, keepdims=True)
    lg2 = jnp.where(lane == i1, NEG, lg)
    m2 = jnp.max(lg2, axis=-1, keepdims=True)
    i2 = jnp.min(jnp.where(lg2 == m2, lane, float(LANES)), axis=-1, keepdims=True)
    e2 = jnp.exp(m2 - m1)
    den = 1.0 + e2

    hit = jnp.where((lane == i1) | (lane == i2), 1.0, 0.0)
    r_i = lax.broadcasted_iota(jnp.int32, (tm, tm), 0)
    c_i = lax.broadcasted_iota(jnp.int32, (tm, tm), 1)
    before = jnp.where(c_i < r_i, 1.0, 0.0).astype(BF16)
    prefix = _dot(before, hit.astype(BF16)) + run_s[0:1, :]
    r1 = jnp.sum(jnp.where(lane == i1, prefix, 0.0), axis=-1, keepdims=True)
    r2 = jnp.sum(jnp.where(lane == i2, prefix, 0.0), axis=-1, keepdims=True)
    run_s[...] = run_s[...] + jnp.sum(hit, axis=0, keepdims=True)
    cnt_ref[...] = run_s[...]

    meta = jnp.where(lane == L_E1, i1, 0.0)
    for ln, val in ((L_E2, i2), (L_W1, 1.0 / den), (L_W2, e2 / den), (L_R1, r1), (L_R2, r2)):
        meta = jnp.where(lane == ln, val, meta)
    meta_ref[...] = meta


def _merge_router(h, a_p, b_p, a_s, b_s, wo, fn, wr, n_experts, tm):
    T, D = h.shape
    npt = a_p.shape[0] // tm
    row = lambda w: pl.BlockSpec((tm, w), lambda i: (i, 0))
    full = lambda a: pl.BlockSpec(a.shape, lambda i: (0, 0))
    (a_prm, a_smp), (b_prm, b_smp) = _stream_specs(tm, RET_W, npt), _stream_specs(tm, MLA_W, npt)
    return pl.pallas_call(
        functools.partial(_merge_router_kernel, n_prompt_tiles=npt, n_experts=n_experts),
        grid=(T // tm,),
        in_specs=[row(D), a_prm, b_prm, a_smp, b_smp, full(wo), full(fn), full(wr)],
        out_specs=(row(D), row(LANES), pl.BlockSpec((SUBLANES, LANES), lambda i: (0, 0))),
        out_shape=(jax.ShapeDtypeStruct((T, D), F32), jax.ShapeDtypeStruct((T, LANES), F32),
                   jax.ShapeDtypeStruct((SUBLANES, LANES), F32)),
        scratch_shapes=[pltpu.VMEM((SUBLANES, LANES), F32)],
        compiler_params=_cparams(("arbitrary",)), name="merge_router",
    )(h, a_p, b_p, a_s, b_s, wo, fn, wr)


def _tile_index_copy(d_hbm, idx_s, sem, tile, slot):
    n = d_hbm.shape[1]
    return pltpu.make_async_copy(d_hbm.at[tile], idx_s.at[pl.ds(pl.multiple_of(slot * n, n), n)],
                                 sem.at[slot])


def _dispatch_kernel(fill_ref, d_hbm, h_ref, xs_out, idx_s, zero_s, isem, rsem, fsem):
    tm = h_ref.shape[0]
    tmg = zero_s.shape[0]
    i, n = pl.program_id(0), pl.num_programs(0)
    slot = i % 2

    def fill_copy(k):
        start = pl.multiple_of(fill_ref[k], tmg)
        return pltpu.make_async_copy(zero_s, xs_out.at[pl.ds(start, tmg)], fsem)

    @pl.when(i == 0)
    def _():
        _tile_index_copy(d_hbm, idx_s, isem, 0, 0).start()
        zero_s[...] = jnp.zeros(zero_s.shape, F32)
        for k in range(fill_ref.shape[0]):
            pl.when(fill_ref[k] >= 0)(lambda k=k: fill_copy(k).start(priority=k % 2))
        for k in range(fill_ref.shape[0]):
            pl.when(fill_ref[k] >= 0)(lambda k=k: fill_copy(k).wait())

    @pl.when(i + 1 < n)
    def _():
        _tile_index_copy(d_hbm, idx_s, isem, i + 1, 1 - slot).start()

    _tile_index_copy(d_hbm, idx_s, isem, i, slot).wait()

    def row_copy(r, dst):
        return pltpu.make_async_copy(h_ref.at[pl.ds(r, 1)], xs_out.at[pl.ds(dst, 1)], rsem)

    def body(g, c):
        r0 = pl.multiple_of(g * SUBLANES, SUBLANES)
        for u in range(SUBLANES):
            for k in range(TOP_K):
                row_copy(r0 + u, idx_s[slot * (TOP_K * tm) + k * tm + r0 + u]).start(priority=k % 2)
        return c

    lax.fori_loop(0, tm // SUBLANES, body, 0)
    for k in range(TOP_K):
        pltpu.make_async_copy(h_ref, xs_out.at[pl.ds(0, tm)], rsem).wait()


def _dispatch(fill_starts, d_tiles, h, n_rows, tm, tmg):
    T, D = h.shape
    grid_spec = pltpu.PrefetchScalarGridSpec(
        num_scalar_prefetch=1, grid=(T // tm,),
        in_specs=[pl.BlockSpec(memory_space=pl.ANY), pl.BlockSpec((tm, D), lambda i, fs: (i, 0))],
        out_specs=pl.BlockSpec(memory_space=pl.ANY),
        scratch_shapes=[pltpu.SMEM((2 * TOP_K * tm,), jnp.int32), pltpu.VMEM((tmg, D), F32),
                        pltpu.SemaphoreType.DMA((2,)), pltpu.SemaphoreType.DMA(()),
                        pltpu.SemaphoreType.DMA(())])
    return pl.pallas_call(
        _dispatch_kernel, grid_spec=grid_spec, out_shape=jax.ShapeDtypeStruct((n_rows, D), F32),
        compiler_params=_cparams(("arbitrary",)), name="dispatch",
    )(fill_starts, d_tiles, h)


def _gmm_kernel(te_ref, nu_ref, x_ref, fn_ref, wg_ref, wu_ref, wd_ref, o_ref, u_s):
    i, f = pl.program_id(0), pl.program_id(1)
    used = i < nu_ref[0]

    @pl.when(f == 0)
    def _():
        o_ref[...] = jnp.zeros(o_ref.shape, F32)

    @pl.when(used & (f == 0))
    def _():
        u_s[...] = _rms(x_ref[...], fn_ref[...]).astype(BF16)

    @pl.when(used)
    def _():
        u = u_s[...]
        a = _dot(u, wg_ref[0].astype(BF16))
        mid = (a * jax.nn.sigmoid(a) * _dot(u, wu_ref[0].astype(BF16))).astype(BF16)
        o_ref[...] += _dot(mid, wd_ref[0].astype(BF16))


def _gmm(tile_expert, n_used, xs, fn, wg, wu, wd, tmg, tf):
    R, D = xs.shape
    F = wg.shape[-1]
    nf = F // tf
    f_eff = lambda i, f, nu: jnp.where(i < nu[0], f, nf - 1)
    grid_spec = pltpu.PrefetchScalarGridSpec(
        num_scalar_prefetch=2, grid=(R // tmg, nf),
        in_specs=[pl.BlockSpec((tmg, D), lambda i, f, te, nu: (i, 0)),
                  pl.BlockSpec(fn.shape, lambda i, f, te, nu: (0, 0)),
                  pl.BlockSpec((1, D, tf), lambda i, f, te, nu: (te[i], 0, f_eff(i, f, nu))),
                  pl.BlockSpec((1, D, tf), lambda i, f, te, nu: (te[i], 0, f_eff(i, f, nu))),
                  pl.BlockSpec((1, tf, D), lambda i, f, te, nu: (te[i], f_eff(i, f, nu), 0))],
        out_specs=pl.BlockSpec((tmg, D), lambda i, f, te, nu: (i, 0)),
        scratch_shapes=[pltpu.VMEM((tmg, D), BF16)])
    return pl.pallas_call(
        _gmm_kernel, grid_spec=grid_spec, out_shape=jax.ShapeDtypeStruct((R, D), F32),
        compiler_params=_cparams(("arbitrary", "arbitrary")), name="experts",
    )(tile_expert, n_used, xs, fn, wg, wu, wd)


def _combine_kernel(d_hbm, h_ref, meta_ref, ys_hbm, o_ref, idx_s, ya_s, yb_s, isem, rsem):
    tm = h_ref.shape[0]
    i, n = pl.program_id(0), pl.num_programs(0)
    slot = i % 2

    def fetch(tile, s):
        _tile_index_copy(d_hbm, idx_s, isem, tile, s).wait()

        def body(g, c):
            r0 = pl.multiple_of(g * SUBLANES, SUBLANES)
            base = s * (TOP_K * tm) + r0
            for u in range(SUBLANES):
                pltpu.make_async_copy(ys_hbm.at[pl.ds(idx_s[base + u], 1)],
                                      ya_s.at[s, pl.ds(r0 + u, 1)], rsem.at[s]).start(priority=0)
                pltpu.make_async_copy(ys_hbm.at[pl.ds(idx_s[base + tm + u], 1)],
                                      yb_s.at[s, pl.ds(r0 + u, 1)], rsem.at[s]).start(priority=1)
            return c

        lax.fori_loop(0, tm // SUBLANES, body, 0)

    @pl.when(i == 0)
    def _():
        _tile_index_copy(d_hbm, idx_s, isem, 0, 0).start()
        fetch(0, 0)

        @pl.when(n > 1)
        def _():
            _tile_index_copy(d_hbm, idx_s, isem, 1, 1).start()

    @pl.when(i + 1 < n)
    def _():
        fetch(i + 1, 1 - slot)

    @pl.when(i + 2 < n)
    def _():
        _tile_index_copy(d_hbm, idx_s, isem, i + 2, slot).start()

    pltpu.make_async_copy(ys_hbm.at[pl.ds(0, tm)], ya_s.at[slot], rsem.at[slot]).wait()
    pltpu.make_async_copy(ys_hbm.at[pl.ds(0, tm)], yb_s.at[slot], rsem.at[slot]).wait()
    meta = meta_ref[...]
    o_ref[...] = (h_ref[...] + meta[:, L_W1:L_W1 + 1] * ya_s[slot]
                  + meta[:, L_W2:L_W2 + 1] * yb_s[slot])


def _combine(d_tiles, h, meta, ys, tm):
    T, D = h.shape
    return pl.pallas_call(
        _combine_kernel, grid=(T // tm,),
        in_specs=[pl.BlockSpec(memory_space=pl.ANY), pl.BlockSpec((tm, D), lambda i: (i, 0)),
                  pl.BlockSpec((tm, LANES), lambda i: (i, 0)), pl.BlockSpec(memory_space=pl.ANY)],
        out_specs=pl.BlockSpec((tm, D), lambda i: (i, 0)),
        out_shape=jax.ShapeDtypeStruct((T, D), F32),
        scratch_shapes=[pltpu.SMEM((2 * TOP_K * tm,), jnp.int32), pltpu.VMEM((2, tm, D), F32),
                        pltpu.VMEM((2, tm, D), F32), pltpu.SemaphoreType.DMA((2,)),
                        pltpu.SemaphoreType.DMA((2,))],
        compiler_params=_cparams(("arbitrary",)), name="combine",
    )(d_tiles, h, meta, ys)


def _moe(h, heads, wo, fn, w_router, wg, wu, wd, tm, tmg, tf):
    T, D = h.shape
    E = w_router.shape[-1]
    wr = jnp.pad(w_router, ((0, 0), (0, LANES - E)))
    wr_hi = wr.astype(BF16)
    wr_lo = (wr - wr_hi.astype(F32)).astype(BF16)
    h, meta, counts = _merge_router(h, *heads, wo, fn, jnp.concatenate([wr_hi, wr_lo], axis=1), E, tm)
    cnt = counts[0, :E].astype(jnp.int32)
    padded = ((cnt + tmg - 1) // tmg) * tmg
    ends = jnp.cumsum(padded)
    off = ends - padded
    n_tiles = -(-TOP_K * T // tmg) + E
    tile_expert = jnp.minimum(
        jnp.sum(ends[None, :] <= (jnp.arange(n_tiles, dtype=jnp.int32) * tmg)[:, None], axis=1), E - 1
    ).astype(jnp.int32)
    n_used = (ends[-1] // tmg).astype(jnp.int32).reshape(1)
    e12 = meta[:, L_E1:L_E2 + 1].astype(jnp.int32)
    off12 = jnp.sum(jnp.where(e12[..., None] == jnp.arange(E, dtype=jnp.int32), off, 0), axis=-1)
    dest = off12 + meta[:, L_R1:L_R2 + 1].astype(jnp.int32)
    d_tiles = dest.reshape(T // tm, tm, TOP_K).transpose(0, 2, 1).reshape(T // tm, TOP_K * tm)

    trailing = ends[-1] + jnp.arange(E + 1, dtype=jnp.int32) * tmg
    fill_starts = jnp.concatenate([
        jnp.where(padded > 0, ends - tmg, -1),
        jnp.where(trailing < n_tiles * tmg, trailing, -1)]).astype(jnp.int32)
    xs = _dispatch(fill_starts, d_tiles, h, n_tiles * tmg, tm, tmg)
    ys = _gmm(tile_expert, n_used, xs, fn, wg, wu, wd, tmg, tf)
    return _combine(d_tiles, h, meta, ys, tm)


def _ple_kernel(h_ref, pn_ref, wg_ref, wp_ref, fin_ref, *refs, final, n_prompt_tiles):
    *p_refs, o_ref = refs
    if len(p_refs) == 2:
        p = jnp.where(pl.program_id(0) < n_prompt_tiles, p_refs[0][...], p_refs[1][...])
    else:
        p = p_refs[0][...]
    hh = h_ref[...]
    gate = jax.nn.sigmoid(_dot(_rms(hh, pn_ref[...]).astype(BF16), wg_ref[...]))
    out = hh + gate * _dot(p.astype(BF16), wp_ref[...])
    if final:
        out = _rms(out, fin_ref[...])
    o_ref[...] = out


def _ple(h, pn, wg, p, wp, fin, tm, final, row0=0, rows=None):
    D = h.shape[1]
    rows = h.shape[0] if rows is None else rows
    off = row0 // tm
    full = lambda a: pl.BlockSpec(a.shape, lambda i: (0,) * a.ndim)
    if isinstance(p, tuple):
        npt = p[0].shape[0] // tm
        p_specs = list(_stream_specs(tm, p[0].shape[1], npt))
    else:
        npt, p_specs, p = 0, [pl.BlockSpec((tm, p.shape[1]), lambda i: (i, 0))], (p,)
    return pl.pallas_call(
        functools.partial(_ple_kernel, final=final, n_prompt_tiles=npt), grid=(rows // tm,),
        in_specs=[pl.BlockSpec((tm, D), lambda i: (off + i, 0)), full(pn), full(wg), full(wp),
                  full(fin)] + p_specs,
        out_specs=pl.BlockSpec((tm, D), lambda i: (i, 0)),
        out_shape=jax.ShapeDtypeStruct((rows, D), F32),
        compiler_params=_cparams(("parallel",)), name="ple",
    )(h, pn, wg, wp, fin, *p)


def _rope_tables(pos):
    pos = pos.astype(F32)[:, None]
    lane = np.arange(LANES)
    freqs = lambda half: ROPE_THETA ** (-jnp.arange(half, dtype=F32) / half)
    a_ret, a_mla = pos * freqs(RET_HALF)[None, :], pos * freqs(ROPE_HALF)[None, :]
    wide = lambda t: jnp.tile(t, (1, LANES // t.shape[1]))
    lo64 = jnp.asarray((lane % RET_DK) < RET_HALF)
    c64, s64 = wide(jnp.cos(a_ret)), wide(jnp.sin(a_ret))
    c16, s16 = wide(jnp.cos(a_mla)), wide(jnp.sin(a_mla))
    in_k = jnp.asarray(lane < MLA_ROPE)
    lo_k = jnp.asarray(lane < ROPE_HALF)
    hi_k = jnp.asarray((lane >= ROPE_HALF) & (lane < MLA_ROPE))
    return (c64, jnp.where(lo64, -s64, 0.0), jnp.where(lo64, 0.0, s64), c16, s16,
            jnp.where(in_k, c16, 0.0), jnp.where(lo_k, -s16, 0.0), jnp.where(hi_k, s16, 0.0))


def _ret_tables(L):
    f = np.float32
    log_g = np.log(f(1.0) - np.exp2(f(-5.0) - np.arange(RET_HEADS, dtype=f)))
    idx = np.arange(L, dtype=f)
    dist = np.abs(idx[:, None] - idx[None, :])
    d = np.exp(dist[None] * log_g[:, None, None])
    qdec = np.exp((idx[:, None] + f(1.0)) * log_g[None, :])
    kdec = np.exp((f(L) - f(1.0) - idx)[:, None] * log_g[None, :])
    sdec = np.exp(f(L) * log_g)
    wide = lambda t: np.repeat(t, RET_DK, axis=1).reshape(L, N_PAIRS, LANES).transpose(1, 0, 2)
    bm = np.kron(np.eye(2, dtype=f), np.ones((RET_DK, RET_DV), f))
    gs = np.repeat(sdec, RET_DK).reshape(N_PAIRS, LANES, 1) * bm[None]
    tabs = (d.reshape(N_PAIRS, 2 * L, L), wide(qdec), wide(kdec), gs, bm)
    return tuple(jnp.asarray(t, F32) for t in tabs)


def _state_to_pairs(s):
    B = s.shape[0]
    s = s.reshape(B, N_PAIRS, 2, RET_DK, RET_DV)
    eye = jnp.eye(2, dtype=s.dtype)
    out = s[:, :, :, :, None, :] * eye[None, None, :, None, :, None]
    return out.reshape(B, N_PAIRS, LANES, LANES)


def _pairs_to_state(sp):
    B = sp.shape[0]
    s = sp.reshape(B, N_PAIRS, 2, RET_DK, 2, RET_DV)
    return jnp.stack([s[:, :, 0, :, 0, :], s[:, :, 1, :, 1, :]], axis=2).reshape(B, RET_HEADS, RET_DK, RET_DV)


def _group_uq_columns(w_uq):
    half = MLA_ROPE // 2
    w = w_uq.reshape(w_uq.shape[0], MLA_HEADS, MLA_NOPE + MLA_ROPE)
    parts = (w[:, :, :MLA_NOPE], w[:, :, MLA_NOPE:MLA_NOPE + half], w[:, :, MLA_NOPE + half:])
    return jnp.concatenate([p.reshape(w_uq.shape[0], -1) for p in parts], axis=1)


def _big_query_weight(w_uk):
    H, half = MLA_HEADS, MLA_ROPE // 2
    eye = jnp.eye(H, dtype=w_uk.dtype)
    wpad = jnp.pad(w_uk, ((0, 0), (0, 0), (0, QK_PAD - KV_LORA)))
    top = (eye[:, None, :, None] * wpad[:, :, None, :]).reshape(H * MLA_NOPE, H * QK_PAD)
    sel = np.zeros((2 * H * half, H * QK_PAD), np.float32)
    for h in range(H):
        for f in range(half):
            sel[h * half + f, h * QK_PAD + KV_LORA + f] = 1.0
            sel[H * half + h * half + f, h * QK_PAD + KV_LORA + half + f] = 1.0
    return jnp.concatenate([top, jnp.asarray(sel, w_uk.dtype)], axis=0)


def kernel(x_prompt, x_sample, p_prompt, p_sample, cache_ckv, cache_krope, state_ret, attn_norm, w_in, q_norm, w_uq, kv_norm, w_uk, w_uv, ret_norm, w_o, ffn_norm, w_gate_d, w_up_d, w_down_d, w_router, w_gate_e, w_up_e, w_down_e, ple_norm, w_ple_gate, w_ple_proj, final_norm):
    Bp, S, D = x_prompt.shape
    Bs, L, _ = x_sample.shape
    depth = w_in.shape[0]
    P = cache_ckv.shape[2]
    Tp, Ts = Bp * S, Bs * L
    T = Tp + Ts
    assert S % CHUNK == 0 and P % CHUNK == 0 and L == CHUNK and Tp % CHUNK == 0
    assert w_router.shape[-1] >= TOP_K

    tm_proj = _pick(int(np.gcd(S, Ts)), (512, 256, 128, 64))
    tm_gmm = 1024
    rb = _pick(S, (512, 256, 128, 64))
    tq = _pick(S, (256, 128, 64))
    kb = _pick(S, (512, 256, 128))
    kb_s = 512

    pos = jnp.concatenate([jnp.arange(S, dtype=jnp.int32),
                           P + jnp.tile(jnp.arange(L, dtype=jnp.int32), Bs)])
    rope_tabs = _rope_tables(pos)
    n_pt, pt_per_seq = Tp // tm_proj, S // tm_proj
    tab_tile = lambda i: jnp.where(i < n_pt, i % pt_per_seq, pt_per_seq + i - n_pt)
    ret_tabs = _ret_tables(CHUNK)
    row2 = lambda v: v.reshape(1, -1)

    h = (x_prompt.reshape(Tp, D), x_sample.reshape(Ts, D))
    outs = {k: [] for k in ("ckv_p", "kro_p", "ret_p", "ckv_s", "kro_s", "ret_s")}
    for l in range(depth):
        win = jnp.pad(w_in[l], ((0, 0), (0, IN_COLS_PAD - IN_COLS))).astype(BF16)
        wuq = _group_uq_columns(w_uq[l]).astype(BF16)
        wbig = _big_query_weight(w_uk[l]).astype(BF16)
        qr, kr, vr, gr, qx, ckv, kro, kx, *h_cat = _proj(
            h, row2(attn_norm[l]), win, row2(q_norm[l]), wuq, row2(kv_norm[l]), wbig, rope_tabs,
            tab_tile, tm_proj)
        if h_cat:
            h, = h_cat

        rn = row2(ret_norm[l])
        zero_state = jnp.zeros((Bp, N_PAIRS, LANES, LANES), F32)
        o_ret_p, st_p = _retention(qr, kr, vr, gr, zero_state, ret_tabs, rn, Bp, S, 0, rb)
        o_ret_s, st_s = _retention(qr, kr, vr, gr, _state_to_pairs(state_ret[l].astype(F32)),
                                   ret_tabs, rn, Bs, L, Tp, L)

        wuv = w_uv[l].astype(BF16)
        wuv_big = (jnp.eye(MLA_HEADS, dtype=BF16)[:, None, :, None] * wuv[:, :, None, :]
                   ).reshape(MLA_HEADS // 2, 2 * KV_LORA, MLA_W)
        o_mla_p = _attention(qx, kx, wuv_big, Bp, S, 0, tq, kb, 0, S, S)
        o_mla_s = _attention_cached(qx, cache_ckv[l], cache_krope[l], kx, wuv_big, Tp, kb_s)

        fn = row2(ffn_norm[l])
        j = l // 2
        wo = w_o[l].astype(BF16)
        ple_w = (row2(ple_norm[l]), w_ple_gate[l].astype(BF16))
        wp = w_ple_proj[l].astype(BF16)
        p_l = (p_prompt[l].reshape(Tp, -1), p_sample[l].reshape(Ts, -1))
        last = l == depth - 1
        if l % 2 == 0 and not last:
            h = _dense_layer(h, o_ret_p, o_mla_p, o_ret_s, o_mla_s, *p_l, wo, fn,
                             w_gate_d[j].astype(BF16), w_up_d[j].astype(BF16),
                             w_down_d[j].astype(BF16), *ple_w, wp, tm_proj)
        else:
            heads = (o_ret_p, o_mla_p, o_ret_s, o_mla_s)
            if l % 2 == 0:
                h = _merge(h, *heads, wo, tm_proj)
                h = _ffn(h, fn, w_gate_d[j].astype(BF16), w_up_d[j].astype(BF16),
                         w_down_d[j].astype(BF16), tm_proj)
            else:
                tf = _pick(w_gate_e.shape[-1], (512, 256, 128))
                h = _moe(h, heads, wo, fn, w_router[j], w_gate_e[j], w_up_e[j], w_down_e[j],
                         tm_proj, tm_gmm, tf)
            if not last:
                h = _ple(h, *ple_w, p_l, wp, row2(final_norm), tm_proj, False)
            else:
                y_p = _ple(h, *ple_w, p_l[0], wp, row2(final_norm), tm_proj, True, 0, Tp)
                y_s = _ple(h, *ple_w, p_l[1], wp, row2(final_norm), tm_proj, True, Tp, Ts)

        outs["ckv_p"].append(ckv[:Tp].reshape(Bp, S, KV_LORA))
        outs["kro_p"].append(kro[:Tp].reshape(Bp, S, MLA_ROPE))
        outs["ret_p"].append(_pairs_to_state(st_p))
        outs["ckv_s"].append(ckv[Tp:].reshape(Bs, L, KV_LORA))
        outs["kro_s"].append(kro[Tp:].reshape(Bs, L, MLA_ROPE))
        outs["ret_s"].append(_pairs_to_state(st_s))

    return (y_p.reshape(Bp, S, D), y_s.reshape(Bs, L, D),
            jnp.stack(outs["ckv_p"]), jnp.stack(outs["kro_p"]), jnp.stack(outs["ret_p"]),
            jnp.stack(outs["ckv_s"]), jnp.stack(outs["kro_s"]), jnp.stack(outs["ret_s"]))
```

```python
import functools

import numpy as np
import jax
import jax.numpy as jnp
from jax import lax
from jax.experimental import pallas as pl
from jax.experimental.pallas import tpu as pltpu

F32 = jnp.float32
BF16 = jnp.bfloat16

CHUNK = 64
CHUNK_SHIFT = 6
RMS_EPS = 1e-6
ROPE_THETA = 10000.0
RET_HEADS = 8
RET_DK = 64
RET_DV = 64
RET_W = RET_HEADS * RET_DK
MLA_HEADS = 8
MLA_NOPE = 64
MLA_ROPE = 32
MLA_V = 64
Q_LORA = 256
KV_LORA = 128
MLA_W = MLA_HEADS * MLA_V
RET_HALF = RET_DK // 2
ROPE_HALF = MLA_ROPE // 2
TOP_K = 2
N_PAIRS = RET_HEADS // 2
LANES = 128
SUBLANES = 8
QK_PAD = 256
OFF_Q = RET_W
OFF_K = OFF_Q + RET_W
OFF_V = OFF_K + RET_W
OFF_G = OFF_V + RET_W
OFF_CQ = OFF_G + Q_LORA
OFF_CKV = OFF_CQ + KV_LORA
IN_COLS = OFF_CKV + MLA_ROPE
IN_COLS_PAD = OFF_CKV + LANES
NEG = -1e30
VMEM_LIMIT = 56 * 1024 * 1024


def _pick(n, cands):
    for c in cands:
        if n % c == 0:
            return c
    return n


def _cparams(sem, flags=None):
    return pltpu.CompilerParams(dimension_semantics=sem, vmem_limit_bytes=VMEM_LIMIT, flags=flags)


def _rms(x, g):
    return x * lax.rsqrt(jnp.mean(x * x, axis=-1, keepdims=True) + RMS_EPS) * g


def _dot(a, b):
    return jnp.dot(a, b, preferred_element_type=F32)


def _dot_nt(a, b):
    return lax.dot_general(a, b, (((1,), (1,)), ((), ())), preferred_element_type=F32)


def _dot_tn(a, b):
    return lax.dot_general(a, b, (((0,), (0,)), ((), ())), preferred_element_type=F32)


def _proj_kernel(h_ref, *refs):
    _proj_body(h_ref[...], *refs)


def _proj_streams_kernel(hp_ref, hs_ref, *refs, n_prompt_tiles):
    x = jnp.where(pl.program_id(0) < n_prompt_tiles, hp_ref[...], hs_ref[...])
    refs[-1][...] = x
    _proj_body(x, *refs[:-1])


def _proj_body(x, an_ref, win_ref, qn_ref, wuq_ref, kvn_ref, wbig_ref,
               c64_ref, sm64_ref, sp64_ref, c16_ref, s16_ref, ck_ref, smk_ref, spk_ref,
               qr_ref, kr_ref, vr_ref, gr_ref, qx_ref, ckv_ref, kro_ref, kx_ref):
    xn = _rms(x, an_ref[...]).astype(BF16)
    c64, sm64, sp64 = c64_ref[...], sm64_ref[...], sp64_ref[...]

    def rope64(z):
        return (z * c64 + pltpu.roll(z, LANES - RET_HALF, 1) * sm64
                + pltpu.roll(z, RET_HALF, 1) * sp64)

    for lo in range(0, RET_W, QK_PAD):
        zq = _dot(xn, win_ref[:, lo:lo + QK_PAD])
        zk = _dot(xn, win_ref[:, OFF_Q + lo:OFF_Q + lo + QK_PAD])
        for half in range(0, QK_PAD, LANES):
            dst = slice(lo + half, lo + half + LANES)
            qr_ref[:, dst] = rope64(zq[:, half:half + LANES])
            kr_ref[:, dst] = rope64(zk[:, half:half + LANES]) * (RET_DK ** -0.5)

    zkv = _dot(xn, win_ref[:, OFF_CQ:IN_COLS_PAD])
    ckv = _rms(zkv[:, :KV_LORA], kvn_ref[...])
    ckv_ref[...] = ckv
    zk = zkv[:, KV_LORA:]
    kro = (zk * ck_ref[...] + pltpu.roll(zk, LANES - ROPE_HALF, 1) * smk_ref[...]
           + pltpu.roll(zk, ROPE_HALF, 1) * spk_ref[...])
    kro_ref[...] = kro[:, :MLA_ROPE]
    kx_ref[:, :KV_LORA] = ckv.astype(BF16)
    kx_ref[:, KV_LORA:] = kro.astype(BF16)

    cq = _dot(xn, win_ref[:, OFF_G:OFF_CQ])
    q = _dot(_rms(cq, qn_ref[...]).astype(BF16), wuq_ref[...])
    nq = MLA_HEADS * MLA_NOPE
    x1, x2 = q[:, nq:nq + LANES], q[:, nq + LANES:nq + 2 * LANES]
    c16, s16 = c16_ref[...], s16_ref[...]
    scale = (MLA_NOPE + MLA_ROPE) ** -0.5 * float(np.log2(np.e))
    qcat = jnp.concatenate([q[:, :nq], x1 * c16 - x2 * s16, x2 * c16 + x1 * s16], axis=-1)
    qcat = (qcat * scale).astype(BF16)
    half_w = (MLA_HEADS // 2) * QK_PAD
    for g in range(2):
        rows = slice(g * (nq // 2), (g + 1) * (nq // 2))
        cols = slice(g * half_w, (g + 1) * half_w)
        qx_ref[:, cols] = (_dot(qcat[:, rows], wbig_ref[rows, cols])
                           + _dot(qcat[:, nq:], wbig_ref[nq:, cols])).astype(BF16)

    vr_ref[...] = _dot(xn, win_ref[:, OFF_K:OFF_V]).astype(BF16)
    gr_ref[...] = _dot(xn, win_ref[:, OFF_V:OFF_G])


def _stream_specs(tm, width, n_prompt_tiles):
    return (pl.BlockSpec((tm, width), lambda i: (jnp.minimum(i, n_prompt_tiles - 1), 0)),
            pl.BlockSpec((tm, width), lambda i: (jnp.maximum(i - n_prompt_tiles, 0), 0)))


def _proj(h, an, win, qn, wuq, kvn, wbig, tabs, tab_tile, tm):
    streams = isinstance(h, tuple)
    T = sum(a.shape[0] for a in h) if streams else h.shape[0]
    D = h[0].shape[1] if streams else h.shape[1]
    row = lambda w: pl.BlockSpec((tm, w), lambda i: (i, 0))
    tab = pl.BlockSpec((tm, LANES), lambda i: (tab_tile(i), 0))
    full = lambda a: pl.BlockSpec(a.shape, lambda i: (0,) * a.ndim)
    out_shapes = (
        jax.ShapeDtypeStruct((T, RET_W), F32), jax.ShapeDtypeStruct((T, RET_W), F32),
        jax.ShapeDtypeStruct((T, RET_W), BF16), jax.ShapeDtypeStruct((T, RET_W), F32),
        jax.ShapeDtypeStruct((T, MLA_HEADS * QK_PAD), BF16),
        jax.ShapeDtypeStruct((T, KV_LORA), F32), jax.ShapeDtypeStruct((T, MLA_ROPE), F32),
        jax.ShapeDtypeStruct((T, QK_PAD), BF16))
    out_specs = (row(RET_W), row(RET_W), row(RET_W), row(RET_W), row(MLA_HEADS * QK_PAD),
                 row(KV_LORA), row(MLA_ROPE), row(QK_PAD))
    if streams:
        npt = h[0].shape[0] // tm
        kern = functools.partial(_proj_streams_kernel, n_prompt_tiles=npt)
        h_specs, h_args = list(_stream_specs(tm, D, npt)), h
        out_specs += (row(D),)
        out_shapes += (jax.ShapeDtypeStruct((T, D), F32),)
    else:
        kern, h_specs, h_args = _proj_kernel, [row(D)], (h,)
    return pl.pallas_call(
        kern, grid=(T // tm,),
        in_specs=h_specs + [full(an), full(win), full(qn), full(wuq), full(kvn), full(wbig)]
                 + [tab] * 8,
        out_specs=out_specs,
        out_shape=out_shapes, compiler_params=_cparams(("parallel",)), name="proj",
    )(*h_args, an, win, qn, wuq, kvn, wbig, *tabs)


def _ret_kernel(q_ref, k_ref, v_ref, g_ref, s0_ref, d_ref, qd_ref, kd_ref, gs_ref, bm_ref, rn_ref,
                o_ref, so_ref, st_ref, *, n_chunks):
    i = pl.program_id(1)

    @pl.when(i == 0)
    def _():
        st_ref[...] = s0_ref[0]

    lane = lax.broadcasted_iota(jnp.int32, (CHUNK, LANES), 1)
    first = lane < RET_DK

    def chunk(c, carry):
        rows = pl.ds(pl.multiple_of(c * CHUNK, CHUNK), CHUNK)
        for p in range(N_PAIRS):
            cols = slice(p * LANES, (p + 1) * LANES)
            qp, kp, vp = q_ref[rows, cols], k_ref[rows, cols], v_ref[rows, cols]
            q2 = jnp.concatenate([jnp.where(first, qp, 0.0), jnp.where(first, 0.0, qp)], axis=0)
            sd = (_dot_nt(q2.astype(BF16), kp.astype(BF16)) * d_ref[p]).astype(BF16)
            qq = (qp * qd_ref[p]).astype(BF16)
            st = st_ref[p]
            lhs = jnp.concatenate([jnp.concatenate([qq, qq], axis=0), sd], axis=1)
            rhs = jnp.concatenate([st.astype(BF16), vp], axis=0)
            o2 = _dot(lhs, rhs)
            o = jnp.where(first, o2[:CHUNK], o2[CHUNK:])
            upd = _dot_tn((kp * kd_ref[p]).astype(BF16), vp)
            st_ref[p] = st * gs_ref[p] + upd * bm_ref[...]
            oo = o * o
            ss_a = jnp.sum(jnp.where(first, oo, 0.0), axis=-1, keepdims=True)
            ss_b = jnp.sum(jnp.where(first, 0.0, oo), axis=-1, keepdims=True)
            rs = jnp.where(first, lax.rsqrt(ss_a * (1.0 / RET_DV) + RMS_EPS),
                           lax.rsqrt(ss_b * (1.0 / RET_DV) + RMS_EPS))
            g = g_ref[rows, cols]
            o_ref[rows, cols] = (o * rs * rn_ref[:, cols] * (g * jax.nn.sigmoid(g))).astype(BF16)
        return carry

    lax.fori_loop(0, n_chunks, chunk, 0, unroll=min(n_chunks, 4))

    @pl.when(i == pl.num_programs(1) - 1)
    def _():
        so_ref[0] = st_ref[...]


def _retention(qr, kr, vr, gr, s0, tabs, rn, nb, seq, row0, rb):
    nblk = seq // rb
    off = row0 // rb
    tok = pl.BlockSpec((rb, RET_W), lambda b, i: (off + b * nblk + i, 0))
    full = lambda a: pl.BlockSpec(a.shape, lambda b, i: (0,) * a.ndim)
    st_spec = pl.BlockSpec((1, N_PAIRS, LANES, LANES), lambda b, i: (b, 0, 0, 0))
    d, qd, kd, gs, bm = tabs
    return pl.pallas_call(
        functools.partial(_ret_kernel, n_chunks=rb // CHUNK), grid=(nb, nblk),
        in_specs=[tok, tok, tok, tok, st_spec, full(d), full(qd), full(kd), full(gs), full(bm), full(rn)],
        out_specs=(pl.BlockSpec((rb, RET_W), lambda b, i: (b * nblk + i, 0)), st_spec),
        out_shape=(jax.ShapeDtypeStruct((nb * seq, RET_W), BF16),
                   jax.ShapeDtypeStruct((nb, N_PAIRS, LANES, LANES), F32)),
        scratch_shapes=[pltpu.VMEM((N_PAIRS, LANES, LANES), F32)],
        compiler_params=_cparams(("parallel", "arbitrary")), name="retention",
    )(qr, kr, vr, gr, s0, d, qd, kd, gs, bm, rn)


def _attn_kernel(q_ref, k_ref, wuv_ref, o_ref, q_s, s0_s, s1_s, m_s, l_s, acc_s,
                 *, tq, kb, q_pos0, sk_valid):
    i = pl.program_id(1)
    R = MLA_HEADS * tq
    for h in range(MLA_HEADS):
        q_s[h * tq:(h + 1) * tq, :] = q_ref[:, h * QK_PAD:(h + 1) * QK_PAD]
    m_s[...] = jnp.full(m_s.shape, NEG, F32)
    l_s[...] = jnp.zeros(l_s.shape, F32)
    acc_s[...] = jnp.zeros(acc_s.shape, F32)

    chunk_end = lambda t: ((t >> CHUNK_SHIFT) + 1) << CHUNK_SHIFT
    kb_shift = kb.bit_length() - 1
    qstart = q_pos0 + i * tq
    lim_first = jnp.minimum(chunk_end(qstart), sk_valid)
    lim_last = jnp.minimum(chunk_end(qstart + tq - 1), sk_valid)
    n_full = lim_first >> kb_shift
    n_blk = (lim_last + kb - 1) >> kb_shift

    last = n_blk - 1

    def key_block(j):
        return k_ref[pl.ds(pl.multiple_of(j * kb, kb), kb), :]

    def scores(j, s_ref):
        s_ref[...] = _dot_nt(q_s[...], key_block(j))

    def update(j, s_ref, masked, width=kb):
        s = s_ref[:, :width]
        if masked:
            tok = lax.broadcasted_iota(jnp.int32, (R, 1), 0) & (tq - 1)
            row_lim = jnp.minimum(chunk_end(qstart + tok), sk_valid)
            kidx = j * kb + lax.broadcasted_iota(jnp.int32, (1, width), 1)
            s = jnp.where(kidx < row_lim, s, NEG)
        m_prev = m_s[...]
        m_next = jnp.maximum(m_prev, jnp.max(s, axis=-1, keepdims=True))
        p = jnp.exp2(s - jnp.tile(m_next, (1, width // LANES)))
        alpha = jnp.exp2(m_prev - m_next)
        p_lanes = p[:, :LANES]
        for c in range(LANES, width, LANES):
            p_lanes = p_lanes + p[:, c:c + LANES]
        l_s[...] = alpha * l_s[...] + p_lanes
        acc_s[...] = alpha * acc_s[...] + _dot(p.astype(BF16), key_block(j)[:width, :KV_LORA])
        m_s[...] = m_next

    n_pipe = jnp.minimum(n_full, last)
    odd = n_pipe & 1

    @pl.when(odd == 1)
    def _():
        scores(0, s0_s)
        update(0, s0_s, False)

    scores(odd, s0_s)

    def pair(jj, c):
        j = odd + 2 * jj
        scores(j + 1, s1_s)
        update(j, s0_s, False)
        scores(j + 2, s0_s)
        update(j + 1, s1_s, False)
        return c

    lax.fori_loop(0, n_pipe >> 1, pair, 0)
    if (kb // 2) % LANES == 0:
        half_only = (n_pipe == last) & (lim_last - n_pipe * kb <= kb // 2)
        pl.when(half_only)(lambda: update(n_pipe, s0_s, True, kb // 2))
        pl.when(jnp.logical_not(half_only))(lambda: update(n_pipe, s0_s, True))
    else:
        update(n_pipe, s0_s, True)

    def tail(j, c):
        scores(j, s0_s)
        update(j, s0_s, True)
        return c

    lax.fori_loop(n_pipe + 1, n_blk, tail, 0)

    o_lat = (acc_s[...] / jnp.sum(l_s[...], axis=-1, keepdims=True)).astype(BF16)
    out = None
    for p in range(MLA_HEADS // 2):
        pair = jnp.concatenate([o_lat[(2 * p) * tq:(2 * p + 1) * tq],
                                o_lat[(2 * p + 1) * tq:(2 * p + 2) * tq]], axis=1)
        term = _dot(pair, wuv_ref[p])
        out = term if out is None else out + term
    o_ref[...] = out.astype(BF16)


def _attn_cached_kernel(q_ref, ckv_ref, kro_ref, knew_ref, wuv_ref, o_ref, k_s, *scratch,
                        past, **static):
    n_new = knew_ref.shape[0]
    k_s[0:past, :KV_LORA] = ckv_ref[0].astype(BF16)
    k_s[0:past, KV_LORA:] = jnp.zeros((past, QK_PAD - KV_LORA), BF16)
    k_s[0:past, KV_LORA:KV_LORA + MLA_ROPE] = kro_ref[0].astype(BF16)
    k_s[past:past + n_new, :] = knew_ref[...]
    k_s[past + n_new:, :] = jnp.zeros((k_s.shape[0] - past - n_new, QK_PAD), BF16)
    _attn_kernel(q_ref, k_s, wuv_ref, o_ref, *scratch, **static)


def _attention_cached(qx, cache_ckv, cache_krope, kx, wuv, row0, kb):
    nb, past, _ = cache_ckv.shape
    L = (qx.shape[0] - row0) // nb
    sk_valid = past + L
    sk = -(-sk_valid // kb) * kb
    R = MLA_HEADS * L
    off = row0 // L
    return pl.pallas_call(
        functools.partial(_attn_cached_kernel, past=past, tq=L, kb=kb, q_pos0=past,
                          sk_valid=sk_valid),
        grid=(nb, 1),
        in_specs=[pl.BlockSpec((L, MLA_HEADS * QK_PAD), lambda b, i: (off + b, 0)),
                  pl.BlockSpec((1, past, KV_LORA), lambda b, i: (b, 0, 0)),
                  pl.BlockSpec((1, past, MLA_ROPE), lambda b, i: (b, 0, 0)),
                  pl.BlockSpec((L, QK_PAD), lambda b, i: (off + b, 0)),
                  pl.BlockSpec(wuv.shape, lambda b, i: (0, 0, 0))],
        out_specs=pl.BlockSpec((L, MLA_W), lambda b, i: (b, 0)),
        out_shape=jax.ShapeDtypeStruct((nb * L, MLA_W), BF16),
        scratch_shapes=[pltpu.VMEM((sk, QK_PAD), BF16), pltpu.VMEM((R, QK_PAD), BF16),
                        pltpu.VMEM((R, kb), F32), pltpu.VMEM((R, kb), F32),
                        pltpu.VMEM((R, LANES), F32), pltpu.VMEM((R, LANES), F32),
                        pltpu.VMEM((R, KV_LORA), F32)],
        compiler_params=_cparams(("parallel", "arbitrary")),
        name="attention_cached",
    )(qx, cache_ckv, cache_krope, kx, wuv)


def _attention(qx, kx, wuv, nb, seq_q, row0, tq, kb, q_pos0, sk, sk_valid):
    nq = seq_q // tq
    off = row0 // tq
    R = MLA_HEADS * tq
    return pl.pallas_call(
        functools.partial(_attn_kernel, tq=tq, kb=kb, q_pos0=q_pos0, sk_valid=sk_valid),
        grid=(nb, nq),
        in_specs=[pl.BlockSpec((tq, MLA_HEADS * QK_PAD), lambda b, i: (off + b * nq + i, 0)),
                  pl.BlockSpec((sk, QK_PAD), lambda b, i: (b, 0)),
                  pl.BlockSpec(wuv.shape, lambda b, i: (0, 0, 0))],
        out_specs=pl.BlockSpec((tq, MLA_W), lambda b, i: (b * nq + i, 0)),
        out_shape=jax.ShapeDtypeStruct((nb * seq_q, MLA_W), BF16),
        scratch_shapes=[pltpu.VMEM((R, QK_PAD), BF16), pltpu.VMEM((R, kb), F32),
                        pltpu.VMEM((R, kb), F32), pltpu.VMEM((R, LANES), F32),
                        pltpu.VMEM((R, LANES), F32), pltpu.VMEM((R, KV_LORA), F32)],
        compiler_params=_cparams(("parallel", "arbitrary")),
        name="attention",
    )(qx, kx, wuv)


def _merge_kernel(h_ref, ap_ref, bp_ref, as_ref, bs_ref, wo_ref, o_ref, *, n_prompt_tiles):
    def project(a_ref, b_ref):
        o_ref[...] = (h_ref[...] + _dot(a_ref[...], wo_ref[:RET_W, :])
                      + _dot(b_ref[...], wo_ref[RET_W:, :]))

    @pl.when(pl.program_id(0) < n_prompt_tiles)
    def _():
        project(ap_ref, bp_ref)

    @pl.when(pl.program_id(0) >= n_prompt_tiles)
    def _():
        project(as_ref, bs_ref)


def _merge(h, a_p, b_p, a_s, b_s, wo, tm):
    T, D = h.shape
    npt = a_p.shape[0] // tm
    row = lambda w: pl.BlockSpec((tm, w), lambda i: (i, 0))
    (a_prm, a_smp), (b_prm, b_smp) = _stream_specs(tm, RET_W, npt), _stream_specs(tm, MLA_W, npt)
    return pl.pallas_call(
        functools.partial(_merge_kernel, n_prompt_tiles=npt), grid=(T // tm,),
        in_specs=[row(D), a_prm, b_prm, a_smp, b_smp, pl.BlockSpec(wo.shape, lambda i: (0, 0))],
        out_specs=row(D), out_shape=jax.ShapeDtypeStruct((T, D), F32),
        compiler_params=_cparams(("parallel",)), name="merge",
    )(h, a_p, b_p, a_s, b_s, wo)


def _ffn_kernel(h_ref, fn_ref, wg_ref, wu_ref, wd_ref, o_ref):
    hh = h_ref[...]
    u = _rms(hh, fn_ref[...]).astype(BF16)
    a = _dot(u, wg_ref[...])
    mid = (a * jax.nn.sigmoid(a) * _dot(u, wu_ref[...])).astype(BF16)
    o_ref[...] = hh + _dot(mid, wd_ref[...])


def _ffn(h, fn, wg, wu, wd, tm):
    T, D = h.shape
    resident = lambda a: pl.BlockSpec(a.shape, lambda i: (0, 0), pipeline_mode=pl.Buffered(1))
    return pl.pallas_call(
        _ffn_kernel, grid=(T // tm,),
        in_specs=[pl.BlockSpec((tm, D), lambda i: (i, 0)), pl.BlockSpec(fn.shape, lambda i: (0, 0)),
                  resident(wg), resident(wu), resident(wd)],
        out_specs=pl.BlockSpec((tm, D), lambda i: (i, 0)),
        out_shape=jax.ShapeDtypeStruct((T, D), F32),
        compiler_params=_cparams(("parallel",)), name="ffn",
    )(h, fn, wg, wu, wd)


def _dense_layer_kernel(h_ref, ap_ref, bp_ref, as_ref, bs_ref, pp_ref, ps_ref, wo_ref, fn_ref,
                        wg_ref, wu_ref, wd_ref, pn_ref, wpg_ref, wpp_ref, o_ref, *, n_prompt_tiles):
    prompt = pl.program_id(0) < n_prompt_tiles
    a = jnp.where(prompt, ap_ref[...], as_ref[...])
    b = jnp.where(prompt, bp_ref[...], bs_ref[...])
    h1 = h_ref[...] + _dot(a, wo_ref[:RET_W, :]) + _dot(b, wo_ref[RET_W:, :])
    u = _rms(h1, fn_ref[...]).astype(BF16)
    g = _dot(u, wg_ref[...])
    mid = (g * jax.nn.sigmoid(g) * _dot(u, wu_ref[...])).astype(BF16)
    h2 = h1 + _dot(mid, wd_ref[...])
    gate = jax.nn.sigmoid(_dot(_rms(h2, pn_ref[...]).astype(BF16), wpg_ref[...]))
    p = jnp.where(prompt, pp_ref[...], ps_ref[...]).astype(BF16)
    o_ref[...] = h2 + gate * _dot(p, wpp_ref[...])


def _dense_layer(h, a_p, b_p, a_s, b_s, p_p, p_s, wo, fn, wg, wu, wd, pn, wpg, wpp, tm):
    T, D = h.shape
    npt = a_p.shape[0] // tm
    row = pl.BlockSpec((tm, D), lambda i: (i, 0))
    resident = lambda w: pl.BlockSpec(w.shape, lambda i: (0, 0), pipeline_mode=pl.Buffered(1))
    (a_prm, a_smp), (b_prm, b_smp) = _stream_specs(tm, RET_W, npt), _stream_specs(tm, MLA_W, npt)
    p_prm, p_smp = _stream_specs(tm, p_p.shape[1], npt)
    return pl.pallas_call(
        functools.partial(_dense_layer_kernel, n_prompt_tiles=npt), grid=(T // tm,),
        in_specs=[row, a_prm, b_prm, a_smp, b_smp, p_prm, p_smp] + [
            resident(w) for w in (wo, fn, wg, wu, wd, pn, wpg, wpp)],
        out_specs=row, out_shape=jax.ShapeDtypeStruct((T, D), F32),
        compiler_params=_cparams(("parallel",)), name="dense_layer",
    )(h, a_p, b_p, a_s, b_s, p_p, p_s, wo, fn, wg, wu, wd, pn, wpg, wpp)


L_E1, L_E2, L_W1, L_W2, L_R1, L_R2 = range(6)


def _merge_router_kernel(h_ref, ap_ref, bp_ref, as_ref, bs_ref, wo_ref, fn_ref, wr_ref,
                         h1_ref, meta_ref, cnt_ref, run_s, *, n_prompt_tiles, n_experts):
    tm = h_ref.shape[0]
    prompt = pl.program_id(0) < n_prompt_tiles
    a = jnp.where(prompt, ap_ref[...], as_ref[...])
    b = jnp.where(prompt, bp_ref[...], bs_ref[...])
    h1 = h_ref[...] + _dot(a, wo_ref[:RET_W, :]) + _dot(b, wo_ref[RET_W:, :])
    h1_ref[...] = h1

    @pl.when(pl.program_id(0) == 0)
    def _():
        run_s[...] = jnp.zeros(run_s.shape, F32)

    u = _rms(h1, fn_ref[...])
    u_hi = u.astype(BF16)
    u_lo = (u - u_hi.astype(F32)).astype(BF16)
    hi_terms = _dot(u_hi, wr_ref[...])
    logits = hi_terms[:, :LANES] + hi_terms[:, LANES:] + _dot(u_lo, wr_ref[:, :LANES])
    lane = lax.broadcasted_iota(jnp.int32, logits.shape, 1).astype(F32)
    lg = jnp.where(lane < n_experts, logits, NEG)
    m1 = jnp.max(lg, axis=-1, keepdims=True)
    i1 = jnp.min(jnp.where(lg == m1, lane, float(LANES)), axis=-1, keepdims=True)
    lg2 = jnp.where(lane == i1, NEG, lg)
    m2 = jnp.max(lg2, axis=-1, keepdims=True)
    i2 = jnp.min(jnp.where(lg2 == m2, lane, float(LANES)), axis=-1, keepdims=True)
    e2 = jnp.exp(m2 - m1)
    den = 1.0 + e2

    hit = jnp.where((lane == i1) | (lane == i2), 1.0, 0.0)
    r_i = lax.broadcasted_iota(jnp.int32, (tm, tm), 0)
    c_i = lax.broadcasted_iota(jnp.int32, (tm, tm), 1)
    before = jnp.where(c_i < r_i, 1.0, 0.0).astype(BF16)
    prefix = _dot(before, hit.astype(BF16)) + run_s[0:1, :]
    r1 = jnp.sum(jnp.where(lane == i1, prefix, 0.0), axis=-1, keepdims=True)
    r2 = jnp.sum(jnp.where(lane == i2, prefix, 0.0), axis=-1, keepdims=True)
    run_s[...] = run_s[...] + jnp.sum(hit, axis=0, keepdims=True)
    cnt_ref[...] = run_s[...]

    meta = jnp.where(lane == L_E1, i1, 0.0)
    for ln, val in ((L_E2, i2), (L_W1, 1.0 / den), (L_W2, e2 / den), (L_R1, r1), (L_R2, r2)):
        meta = jnp.where(lane == ln, val, meta)
    meta_ref[...] = meta


def _merge_router(h, a_p, b_p, a_s, b_s, wo, fn, wr, n_experts, tm):
    T, D = h.shape
    npt = a_p.shape[0] // tm
    row = lambda w: pl.BlockSpec((tm, w), lambda i: (i, 0))
    full = lambda a: pl.BlockSpec(a.shape, lambda i: (0, 0))
    (a_prm, a_smp), (b_prm, b_smp) = _stream_specs(tm, RET_W, npt), _stream_specs(tm, MLA_W, npt)
    return pl.pallas_call(
        functools.partial(_merge_router_kernel, n_prompt_tiles=npt, n_experts=n_experts),
        grid=(T // tm,),
        in_specs=[row(D), a_prm, b_prm, a_smp, b_smp, full(wo), full(fn), full(wr)],
        out_specs=(row(D), row(LANES), pl.BlockSpec((SUBLANES, LANES), lambda i: (0, 0))),
        out_shape=(jax.ShapeDtypeStruct((T, D), F32), jax.ShapeDtypeStruct((T, LANES), F32),
                   jax.ShapeDtypeStruct((SUBLANES, LANES), F32)),
        scratch_shapes=[pltpu.VMEM((SUBLANES, LANES), F32)],
        compiler_params=_cparams(("arbitrary",)), name="merge_router",
    )(h, a_p, b_p, a_s, b_s, wo, fn, wr)


def _tile_index_copy(d_hbm, idx_s, sem, tile, slot):
    n = d_hbm.shape[1]
    return pltpu.make_async_copy(d_hbm.at[tile], idx_s.at[pl.ds(pl.multiple_of(slot * n, n), n)],
                                 sem.at[slot])


def _dispatch_kernel(fill_ref, d_hbm, h_ref, xs_out, idx_s, zero_s, isem, rsem, fsem):
    tm = h_ref.shape[0]
    tmg = zero_s.shape[0]
    i, n = pl.program_id(0), pl.num_programs(0)
    slot = i % 2

    def fill_copy(k):
        start = pl.multiple_of(fill_ref[k], tmg)
        return pltpu.make_async_copy(zero_s, xs_out.at[pl.ds(start, tmg)], fsem)

    @pl.when(i == 0)
    def _():
        _tile_index_copy(d_hbm, idx_s, isem, 0, 0).start()
        zero_s[...] = jnp.zeros(zero_s.shape, F32)
        for k in range(fill_ref.shape[0]):
            pl.when(fill_ref[k] >= 0)(lambda k=k: fill_copy(k).start(priority=k % 2))
        for k in range(fill_ref.shape[0]):
            pl.when(fill_ref[k] >= 0)(lambda k=k: fill_copy(k).wait())

    @pl.when(i + 1 < n)
    def _():
        _tile_index_copy(d_hbm, idx_s, isem, i + 1, 1 - slot).start()

    _tile_index_copy(d_hbm, idx_s, isem, i, slot).wait()

    def row_copy(r, dst):
        return pltpu.make_async_copy(h_ref.at[pl.ds(r, 1)], xs_out.at[pl.ds(dst, 1)], rsem)

    def body(g, c):
        r0 = pl.multiple_of(g * SUBLANES, SUBLANES)
        for u in range(SUBLANES):
            for k in range(TOP_K):
                row_copy(r0 + u, idx_s[slot * (TOP_K * tm) + k * tm + r0 + u]).start(priority=k % 2)
        return c

    lax.fori_loop(0, tm // SUBLANES, body, 0)
    for k in range(TOP_K):
        pltpu.make_async_copy(h_ref, xs_out.at[pl.ds(0, tm)], rsem).wait()


def _dispatch(fill_starts, d_tiles, h, n_rows, tm, tmg):
    T, D = h.shape
    grid_spec = pltpu.PrefetchScalarGridSpec(
        num_scalar_prefetch=1, grid=(T // tm,),
        in_specs=[pl.BlockSpec(memory_space=pl.ANY), pl.BlockSpec((tm, D), lambda i, fs: (i, 0))],
        out_specs=pl.BlockSpec(memory_space=pl.ANY),
        scratch_shapes=[pltpu.SMEM((2 * TOP_K * tm,), jnp.int32), pltpu.VMEM((tmg, D), F32),
                        pltpu.SemaphoreType.DMA((2,)), pltpu.SemaphoreType.DMA(()),
                        pltpu.SemaphoreType.DMA(())])
    return pl.pallas_call(
        _dispatch_kernel, grid_spec=grid_spec, out_shape=jax.ShapeDtypeStruct((n_rows, D), F32),
        compiler_params=_cparams(("arbitrary",)), name="dispatch",
    )(fill_starts, d_tiles, h)


def _gmm_kernel(te_ref, nu_ref, x_ref, fn_ref, wg_ref, wu_ref, wd_ref, o_ref, u_s):
    i, f = pl.program_id(0), pl.program_id(1)
    used = i < nu_ref[0]

    @pl.when(f == 0)
    def _():
        o_ref[...] = jnp.zeros(o_ref.shape, F32)

    @pl.when(used & (f == 0))
    def _():
        u_s[...] = _rms(x_ref[...], fn_ref[...]).astype(BF16)

    @pl.when(used)
    def _():
        u = u_s[...]
        a = _dot(u, wg_ref[0].astype(BF16))
        mid = (a * jax.nn.sigmoid(a) * _dot(u, wu_ref[0].astype(BF16))).astype(BF16)
        o_ref[...] += _dot(mid, wd_ref[0].astype(BF16))


def _gmm(tile_expert, n_used, xs, fn, wg, wu, wd, tmg, tf):
    R, D = xs.shape
    F = wg.shape[-1]
    nf = F // tf
    f_eff = lambda i, f, nu: jnp.where(i < nu[0], f, nf - 1)
    grid_spec = pltpu.PrefetchScalarGridSpec(
        num_scalar_prefetch=2, grid=(R // tmg, nf),
        in_specs=[pl.BlockSpec((tmg, D), lambda i, f, te, nu: (i, 0)),
                  pl.BlockSpec(fn.shape, lambda i, f, te, nu: (0, 0)),
                  pl.BlockSpec((1, D, tf), lambda i, f, te, nu: (te[i], 0, f_eff(i, f, nu))),
                  pl.BlockSpec((1, D, tf), lambda i, f, te, nu: (te[i], 0, f_eff(i, f, nu))),
                  pl.BlockSpec((1, tf, D), lambda i, f, te, nu: (te[i], f_eff(i, f, nu), 0))],
        out_specs=pl.BlockSpec((tmg, D), lambda i, f, te, nu: (i, 0)),
        scratch_shapes=[pltpu.VMEM((tmg, D), BF16)])
    return pl.pallas_call(
        _gmm_kernel, grid_spec=grid_spec, out_shape=jax.ShapeDtypeStruct((R, D), F32),
        compiler_params=_cparams(("arbitrary", "arbitrary")), name="experts",
    )(tile_expert, n_used, xs, fn, wg, wu, wd)


def _combine_kernel(d_hbm, h_ref, meta_ref, ys_hbm, o_ref, idx_s, ya_s, yb_s, isem, rsem):
    tm = h_ref.shape[0]
    i, n = pl.program_id(0), pl.num_programs(0)
    slot = i % 2

    def fetch(tile, s):
        _tile_index_copy(d_hbm, idx_s, isem, tile, s).wait()

        def body(g, c):
            r0 = pl.multiple_of(g * SUBLANES, SUBLANES)
            base = s * (TOP_K * tm) + r0
            for u in range(SUBLANES):
                pltpu.make_async_copy(ys_hbm.at[pl.ds(idx_s[base + u], 1)],
                                      ya_s.at[s, pl.ds(r0 + u, 1)], rsem.at[s]).start(priority=0)
                pltpu.make_async_copy(ys_hbm.at[pl.ds(idx_s[base + tm + u], 1)],
                                      yb_s.at[s, pl.ds(r0 + u, 1)], rsem.at[s]).start(priority=1)
            return c

        lax.fori_loop(0, tm // SUBLANES, body, 0)

    @pl.when(i == 0)
    def _():
        _tile_index_copy(d_hbm, idx_s, isem, 0, 0).start()
        fetch(0, 0)

        @pl.when(n > 1)
        def _():
            _tile_index_copy(d_hbm, idx_s, isem, 1, 1).start()

    @pl.when(i + 1 < n)
    def _():
        fetch(i + 1, 1 - slot)

    @pl.when(i + 2 < n)
    def _():
        _tile_index_copy(d_hbm, idx_s, isem, i + 2, slot).start()

    pltpu.make_async_copy(ys_hbm.at[pl.ds(0, tm)], ya_s.at[slot], rsem.at[slot]).wait()
    pltpu.make_async_copy(ys_hbm.at[pl.ds(0, tm)], yb_s.at[slot], rsem.at[slot]).wait()
    meta = meta_ref[...]
    o_ref[...] = (h_ref[...] + meta[:, L_W1:L_W1 + 1] * ya_s[slot]
                  + meta[:, L_W2:L_W2 + 1] * yb_s[slot])


def _combine(d_tiles, h, meta, ys, tm):
    T, D = h.shape
    return pl.pallas_call(
        _combine_kernel, grid=(T // tm,),
        in_specs=[pl.BlockSpec(memory_space=pl.ANY), pl.BlockSpec((tm, D), lambda i: (i, 0)),
                  pl.BlockSpec((tm, LANES), lambda i: (i, 0)), pl.BlockSpec(memory_space=pl.ANY)],
        out_specs=pl.BlockSpec((tm, D), lambda i: (i, 0)),
        out_shape=jax.ShapeDtypeStruct((T, D), F32),
        scratch_shapes=[pltpu.SMEM((2 * TOP_K * tm,), jnp.int32), pltpu.VMEM((2, tm, D), F32),
                        pltpu.VMEM((2, tm, D), F32), pltpu.SemaphoreType.DMA((2,)),
                        pltpu.SemaphoreType.DMA((2,))],
        compiler_params=_cparams(("arbitrary",)), name="combine",
    )(d_tiles, h, meta, ys)


def _moe(h, heads, wo, fn, w_router, wg, wu, wd, tm, tmg, tf):
    T, D = h.shape
    E = w_router.shape[-1]
    wr = jnp.pad(w_router, ((0, 0), (0, LANES - E)))
    wr_hi = wr.astype(BF16)
    wr_lo = (wr - wr_hi.astype(F32)).astype(BF16)
    h, meta, counts = _merge_router(h, *heads, wo, fn, jnp.concatenate([wr_hi, wr_lo], axis=1), E, tm)
    cnt = counts[0, :E].astype(jnp.int32)
    padded = ((cnt + tmg - 1) // tmg) * tmg
    ends = jnp.cumsum(padded)
    off = ends - padded
    n_tiles = -(-TOP_K * T // tmg) + E
    tile_expert = jnp.minimum(
        jnp.sum(ends[None, :] <= (jnp.arange(n_tiles, dtype=jnp.int32) * tmg)[:, None], axis=1), E - 1
    ).astype(jnp.int32)
    n_used = (ends[-1] // tmg).astype(jnp.int32).reshape(1)
    e12 = meta[:, L_E1:L_E2 + 1].astype(jnp.int32)
    off12 = jnp.sum(jnp.where(e12[..., None] == jnp.arange(E, dtype=jnp.int32), off, 0), axis=-1)
    dest = off12 + meta[:, L_R1:L_R2 + 1].astype(jnp.int32)
    d_tiles = dest.reshape(T // tm, tm, TOP_K).transpose(0, 2, 1).reshape(T // tm, TOP_K * tm)

    trailing = ends[-1] + jnp.arange(E + 1, dtype=jnp.int32) * tmg
    fill_starts = jnp.concatenate([
        jnp.where(padded > 0, ends - tmg, -1),
        jnp.where(trailing < n_tiles * tmg, trailing, -1)]).astype(jnp.int32)
    xs = _dispatch(fill_starts, d_tiles, h, n_tiles * tmg, tm, tmg)
    ys = _gmm(tile_expert, n_used, xs, fn, wg, wu, wd, tmg, tf)
    return _combine(d_tiles, h, meta, ys, tm)


def _ple_kernel(h_ref, pn_ref, wg_ref, wp_ref, fin_ref, *refs, final, n_prompt_tiles):
    *p_refs, o_ref = refs
    if len(p_refs) == 2:
        p = jnp.where(pl.program_id(0) < n_prompt_tiles, p_refs[0][...], p_refs[1][...])
    else:
        p = p_refs[0][...]
    hh = h_ref[...]
    gate = jax.nn.sigmoid(_dot(_rms(hh, pn_ref[...]).astype(BF16), wg_ref[...]))
    out = hh + gate * _dot(p.astype(BF16), wp_ref[...])
    if final:
        out = _rms(out, fin_ref[...])
    o_ref[...] = out


def _ple(h, pn, wg, p, wp, fin, tm, final, row0=0, rows=None):
    D = h.shape[1]
    rows = h.shape[0] if rows is None else rows
    off = row0 // tm
    full = lambda a: pl.BlockSpec(a.shape, lambda i: (0,) * a.ndim)
    if isinstance(p, tuple):
        npt = p[0].shape[0] // tm
        p_specs = list(_stream_specs(tm, p[0].shape[1], npt))
    else:
        npt, p_specs, p = 0, [pl.BlockSpec((tm, p.shape[1]), lambda i: (i, 0))], (p,)
    return pl.pallas_call(
        functools.partial(_ple_kernel, final=final, n_prompt_tiles=npt), grid=(rows // tm,),
        in_specs=[pl.BlockSpec((tm, D), lambda i: (off + i, 0)), full(pn), full(wg), full(wp),
                  full(fin)] + p_specs,
        out_specs=pl.BlockSpec((tm, D), lambda i: (i, 0)),
        out_shape=jax.ShapeDtypeStruct((rows, D), F32),
        compiler_params=_cparams(("parallel",)), name="ple",
    )(h, pn, wg, wp, fin, *p)


def _rope_tables(pos):
    pos = pos.astype(F32)[:, None]
    lane = np.arange(LANES)
    freqs = lambda half: ROPE_THETA ** (-jnp.arange(half, dtype=F32) / half)
    a_ret, a_mla = pos * freqs(RET_HALF)[None, :], pos * freqs(ROPE_HALF)[None, :]
    wide = lambda t: jnp.tile(t, (1, LANES // t.shape[1]))
    lo64 = jnp.asarray((lane % RET_DK) < RET_HALF)
    c64, s64 = wide(jnp.cos(a_ret)), wide(jnp.sin(a_ret))
    c16, s16 = wide(jnp.cos(a_mla)), wide(jnp.sin(a_mla))
    in_k = jnp.asarray(lane < MLA_ROPE)
    lo_k = jnp.asarray(lane < ROPE_HALF)
    hi_k = jnp.asarray((lane >= ROPE_HALF) & (lane < MLA_ROPE))
    return (c64, jnp.where(lo64, -s64, 0.0), jnp.where(lo64, 0.0, s64), c16, s16,
            jnp.where(in_k, c16, 0.0), jnp.where(lo_k, -s16, 0.0), jnp.where(hi_k, s16, 0.0))


def _ret_tables(L):
    f = np.float32
    log_g = np.log(f(1.0) - np.exp2(f(-5.0) - np.arange(RET_HEADS, dtype=f)))
    idx = np.arange(L, dtype=f)
    dist = np.abs(idx[:, None] - idx[None, :])
    d = np.exp(dist[None] * log_g[:, None, None])
    qdec = np.exp((idx[:, None] + f(1.0)) * log_g[None, :])
    kdec = np.exp((f(L) - f(1.0) - idx)[:, None] * log_g[None, :])
    sdec = np.exp(f(L) * log_g)
    wide = lambda t: np.repeat(t, RET_DK, axis=1).reshape(L, N_PAIRS, LANES).transpose(1, 0, 2)
    bm = np.kron(np.eye(2, dtype=f), np.ones((RET_DK, RET_DV), f))
    gs = np.repeat(sdec, RET_DK).reshape(N_PAIRS, LANES, 1) * bm[None]
    tabs = (d.reshape(N_PAIRS, 2 * L, L), wide(qdec), wide(kdec), gs, bm)
    return tuple(jnp.asarray(t, F32) for t in tabs)


def _state_to_pairs(s):
    B = s.shape[0]
    s = s.reshape(B, N_PAIRS, 2, RET_DK, RET_DV)
    eye = jnp.eye(2, dtype=s.dtype)
    out = s[:, :, :, :, None, :] * eye[None, None, :, None, :, None]
    return out.reshape(B, N_PAIRS, LANES, LANES)


def _pairs_to_state(sp):
    B = sp.shape[0]
    s = sp.reshape(B, N_PAIRS, 2, RET_DK, 2, RET_DV)
    return jnp.stack([s[:, :, 0, :, 0, :], s[:, :, 1, :, 1, :]], axis=2).reshape(B, RET_HEADS, RET_DK, RET_DV)


def _group_uq_columns(w_uq):
    half = MLA_ROPE // 2
    w = w_uq.reshape(w_uq.shape[0], MLA_HEADS, MLA_NOPE + MLA_ROPE)
    parts = (w[:, :, :MLA_NOPE], w[:, :, MLA_NOPE:MLA_NOPE + half], w[:, :, MLA_NOPE + half:])
    return jnp.concatenate([p.reshape(w_uq.shape[0], -1) for p in parts], axis=1)


def _big_query_weight(w_uk):
    H, half = MLA_HEADS, MLA_ROPE // 2
    eye = jnp.eye(H, dtype=w_uk.dtype)
    wpad = jnp.pad(w_uk, ((0, 0), (0, 0), (0, QK_PAD - KV_LORA)))
    top = (eye[:, None, :, None] * wpad[:, :, None, :]).reshape(H * MLA_NOPE, H * QK_PAD)
    sel = np.zeros((2 * H * half, H * QK_PAD), np.float32)
    for h in range(H):
        for f in range(half):
            sel[h * half + f, h * QK_PAD + KV_LORA + f] = 1.0
            sel[H * half + h * half + f, h * QK_PAD + KV_LORA + half + f] = 1.0
    return jnp.concatenate([top, jnp.asarray(sel, w_uk.dtype)], axis=0)


def kernel(x_prompt, x_sample, p_prompt, p_sample, cache_ckv, cache_krope, state_ret, attn_norm, w_in, q_norm, w_uq, kv_norm, w_uk, w_uv, ret_norm, w_o, ffn_norm, w_gate_d, w_up_d, w_down_d, w_router, w_gate_e, w_up_e, w_down_e, ple_norm, w_ple_gate, w_ple_proj, final_norm):
    Bp, S, D = x_prompt.shape
    Bs, L, _ = x_sample.shape
    depth = w_in.shape[0]
    P = cache_ckv.shape[2]
    Tp, Ts = Bp * S, Bs * L
    T = Tp + Ts
    assert S % CHUNK == 0 and P % CHUNK == 0 and L == CHUNK and Tp % CHUNK == 0
    assert w_router.shape[-1] >= TOP_K

    tm_proj = _pick(int(np.gcd(S, Ts)), (512, 256, 128, 64))
    tm_gmm = 1024
    rb = _pick(S, (512, 256, 128, 64))
    tq = _pick(S, (256, 128, 64))
    kb = _pick(S, (512, 256, 128))
    kb_s = 512

    pos = jnp.concatenate([jnp.arange(S, dtype=jnp.int32),
                           P + jnp.tile(jnp.arange(L, dtype=jnp.int32), Bs)])
    rope_tabs = _rope_tables(pos)
    n_pt, pt_per_seq = Tp // tm_proj, S // tm_proj
    tab_tile = lambda i: jnp.where(i < n_pt, i % pt_per_seq, pt_per_seq + i - n_pt)
    ret_tabs = _ret_tables(CHUNK)
    row2 = lambda v: v.reshape(1, -1)

    h = (x_prompt.reshape(Tp, D), x_sample.reshape(Ts, D))
    outs = {k: [] for k in ("ckv_p", "kro_p", "ret_p", "ckv_s", "kro_s", "ret_s")}
    for l in range(depth):
        win = jnp.pad(w_in[l], ((0, 0), (0, IN_COLS_PAD - IN_COLS))).astype(BF16)
        wuq = _group_uq_columns(w_uq[l]).astype(BF16)
        wbig = _big_query_weight(w_uk[l]).astype(BF16)
        qr, kr, vr, gr, qx, ckv, kro, kx, *h_cat = _proj(
            h, row2(attn_norm[l]), win, row2(q_norm[l]), wuq, row2(kv_norm[l]), wbig, rope_tabs,
            tab_tile, tm_proj)
        if h_cat:
            h, = h_cat

        rn = row2(ret_norm[l])
        zero_state = jnp.zeros((Bp, N_PAIRS, LANES, LANES), F32)
        o_ret_p, st_p = _retention(qr, kr, vr, gr, zero_state, ret_tabs, rn, Bp, S, 0, rb)
        o_ret_s, st_s = _retention(qr, kr, vr, gr, _state_to_pairs(state_ret[l].astype(F32)),
                                   ret_tabs, rn, Bs, L, Tp, L)

        wuv = w_uv[l].astype(BF16)
        wuv_big = (jnp.eye(MLA_HEADS, dtype=BF16)[:, None, :, None] * wuv[:, :, None, :]
                   ).reshape(MLA_HEADS // 2, 2 * KV_LORA, MLA_W)
        o_mla_p = _attention(qx, kx, wuv_big, Bp, S, 0, tq, kb, 0, S, S)
        o_mla_s = _attention_cached(qx, cache_ckv[l], cache_krope[l], kx, wuv_big, Tp, kb_s)

        fn = row2(ffn_norm[l])
        j = l // 2
        wo = w_o[l].astype(BF16)
        ple_w = (row2(ple_norm[l]), w_ple_gate[l].astype(BF16))
        wp = w_ple_proj[l].astype(BF16)
        p_l = (p_prompt[l].reshape(Tp, -1), p_sample[l].reshape(Ts, -1))
        last = l == depth - 1
        if l % 2 == 0 and not last:
            h = _dense_layer(h, o_ret_p, o_mla_p, o_ret_s, o_mla_s, *p_l, wo, fn,
                             w_gate_d[j].astype(BF16), w_up_d[j].astype(BF16),
                             w_down_d[j].astype(BF16), *ple_w, wp, tm_proj)
        else:
            heads = (o_ret_p, o_mla_p, o_ret_s, o_mla_s)
            if l % 2 == 0:
                h = _merge(h, *heads, wo, tm_proj)
                h = _ffn(h, fn, w_gate_d[j].astype(BF16), w_up_d[j].astype(BF16),
                         w_down_d[j].astype(BF16), tm_proj)
            else:
                tf = _pick(w_gate_e.shape[-1], (512, 256, 128))
                h = _moe(h, heads, wo, fn, w_router[j], w_gate_e[j], w_up_e[j], w_down_e[j],
                         tm_proj, tm_gmm, tf)
            if not last:
                h = _ple(h, *ple_w, p_l, wp, row2(final_norm), tm_proj, False)
            else:
                y_p = _ple(h, *ple_w, p_l[0], wp, row2(final_norm), tm_proj, True, 0, Tp)
                y_s = _ple(h, *ple_w, p_l[1], wp, row2(final_norm), tm_proj, True, Tp, Ts)

        outs["ckv_p"].append(ckv[:Tp].reshape(Bp, S, KV_LORA))
        outs["kro_p"].append(kro[:Tp].reshape(Bp, S, MLA_ROPE))
        outs["ret_p"].append(_pairs_to_state(st_p))
        outs["ckv_s"].append(ckv[Tp:].reshape(Bs, L, KV_LORA))
        outs["kro_s"].append(kro[Tp:].reshape(Bs, L, MLA_ROPE))
        outs["ret_s"].append(_pairs_to_state(st_s))

    return (y_p.reshape(Bp, S, D), y_s.reshape(Bs, L, D),
            jnp.stack(outs["ckv_p"]), jnp.stack(outs["kro_p"]), jnp.stack(outs["ret_p"]),
            jnp.stack(outs["ckv_s"]), jnp.stack(outs["kro_s"]), jnp.stack(outs["ret_s"]))
```

```python
import functools

import numpy as np
import jax
import jax.numpy as jnp
from jax import lax
from jax.experimental import pallas as pl
from jax.experimental.pallas import tpu as pltpu

F32 = jnp.float32
BF16 = jnp.bfloat16

CHUNK = 64
CHUNK_SHIFT = 6
RMS_EPS = 1e-6
ROPE_THETA = 10000.0
RET_HEADS = 8
RET_DK = 64
RET_DV = 64
RET_W = RET_HEADS * RET_DK
MLA_HEADS = 8
MLA_NOPE = 64
MLA_ROPE = 32
MLA_V = 64
Q_LORA = 256
KV_LORA = 128
MLA_W = MLA_HEADS * MLA_V
RET_HALF = RET_DK // 2
ROPE_HALF = MLA_ROPE // 2
TOP_K = 2
N_PAIRS = RET_HEADS // 2
LANES = 128
SUBLANES = 8
QK_PAD = 256
OFF_Q = RET_W
OFF_K = OFF_Q + RET_W
OFF_V = OFF_K + RET_W
OFF_G = OFF_V + RET_W
OFF_CQ = OFF_G + Q_LORA
OFF_CKV = OFF_CQ + KV_LORA
IN_COLS = OFF_CKV + MLA_ROPE
IN_COLS_PAD = OFF_CKV + LANES
NEG = -1e30
VMEM_LIMIT = 56 * 1024 * 1024


def _pick(n, cands):
    for c in cands:
        if n % c == 0:
            return c
    return n


def _cparams(sem, flags=None):
    return pltpu.CompilerParams(dimension_semantics=sem, vmem_limit_bytes=VMEM_LIMIT, flags=flags)


def _rms(x, g):
    return x * lax.rsqrt(jnp.mean(x * x, axis=-1, keepdims=True) + RMS_EPS) * g


def _dot(a, b):
    return jnp.dot(a, b, preferred_element_type=F32)


def _dot_nt(a, b):
    return lax.dot_general(a, b, (((1,), (1,)), ((), ())), preferred_element_type=F32)


def _dot_tn(a, b):
    return lax.dot_general(a, b, (((0,), (0,)), ((), ())), preferred_element_type=F32)


def _proj_kernel(h_ref, *refs):
    _proj_body(h_ref[...], *refs)


def _proj_streams_kernel(hp_ref, hs_ref, *refs, n_prompt_tiles):
    x = jnp.where(pl.program_id(0) < n_prompt_tiles, hp_ref[...], hs_ref[...])
    refs[-1][...] = x
    _proj_body(x, *refs[:-1])


def _proj_body(x, an_ref, win_ref, qn_ref, wuq_ref, kvn_ref, wbig_ref,
               c64_ref, sm64_ref, sp64_ref, c16_ref, s16_ref, ck_ref, smk_ref, spk_ref,
               qr_ref, kr_ref, vr_ref, gr_ref, qx_ref, ckv_ref, kro_ref, kx_ref):
    xn = _rms(x, an_ref[...]).astype(BF16)
    c64, sm64, sp64 = c64_ref[...], sm64_ref[...], sp64_ref[...]

    def rope64(z):
        return (z * c64 + pltpu.roll(z, LANES - RET_HALF, 1) * sm64
                + pltpu.roll(z, RET_HALF, 1) * sp64)

    for lo in range(0, RET_W, QK_PAD):
        zq = _dot(xn, win_ref[:, lo:lo + QK_PAD])
        zk = _dot(xn, win_ref[:, OFF_Q + lo:OFF_Q + lo + QK_PAD])
        for half in range(0, QK_PAD, LANES):
            dst = slice(lo + half, lo + half + LANES)
            qr_ref[:, dst] = rope64(zq[:, half:half + LANES])
            kr_ref[:, dst] = rope64(zk[:, half:half + LANES]) * (RET_DK ** -0.5)

    zkv = _dot(xn, win_ref[:, OFF_CQ:IN_COLS_PAD])
    ckv = _rms(zkv[:, :KV_LORA], kvn_ref[...])
    ckv_ref[...] = ckv
    zk = zkv[:, KV_LORA:]
    kro = (zk * ck_ref[...] + pltpu.roll(zk, LANES - ROPE_HALF, 1) * smk_ref[...]
           + pltpu.roll(zk, ROPE_HALF, 1) * spk_ref[...])
    kro_ref[...] = kro[:, :MLA_ROPE]
    kx_ref[:, :KV_LORA] = ckv.astype(BF16)
    kx_ref[:, KV_LORA:] = kro.astype(BF16)

    cq = _dot(xn, win_ref[:, OFF_G:OFF_CQ])
    q = _dot(_rms(cq, qn_ref[...]).astype(BF16), wuq_ref[...])
    nq = MLA_HEADS * MLA_NOPE
    x1, x2 = q[:, nq:nq + LANES], q[:, nq + LANES:nq + 2 * LANES]
    c16, s16 = c16_ref[...], s16_ref[...]
    scale = (MLA_NOPE + MLA_ROPE) ** -0.5 * float(np.log2(np.e))
    qcat = jnp.concatenate([q[:, :nq], x1 * c16 - x2 * s16, x2 * c16 + x1 * s16], axis=-1)
    qcat = (qcat * scale).astype(BF16)
    half_w = (MLA_HEADS // 2) * QK_PAD
    for g in range(2):
        rows = slice(g * (nq // 2), (g + 1) * (nq // 2))
        cols = slice(g * half_w, (g + 1) * half_w)
        qx_ref[:, cols] = (_dot(qcat[:, rows], wbig_ref[rows, cols])
                           + _dot(qcat[:, nq:], wbig_ref[nq:, cols])).astype(BF16)

    vr_ref[...] = _dot(xn, win_ref[:, OFF_K:OFF_V]).astype(BF16)
    gr_ref[...] = _dot(xn, win_ref[:, OFF_V:OFF_G])


def _stream_specs(tm, width, n_prompt_tiles):
    return (pl.BlockSpec((tm, width), lambda i: (jnp.minimum(i, n_prompt_tiles - 1), 0)),
            pl.BlockSpec((tm, width), lambda i: (jnp.maximum(i - n_prompt_tiles, 0), 0)))


def _proj(h, an, win, qn, wuq, kvn, wbig, tabs, tab_tile, tm):
    streams = isinstance(h, tuple)
    T = sum(a.shape[0] for a in h) if streams else h.shape[0]
    D = h[0].shape[1] if streams else h.shape[1]
    row = lambda w: pl.BlockSpec((tm, w), lambda i: (i, 0))
    tab = pl.BlockSpec((tm, LANES), lambda i: (tab_tile(i), 0))
    full = lambda a: pl.BlockSpec(a.shape, lambda i: (0,) * a.ndim)
    out_shapes = (
        jax.ShapeDtypeStruct((T, RET_W), F32), jax.ShapeDtypeStruct((T, RET_W), F32),
        jax.ShapeDtypeStruct((T, RET_W), BF16), jax.ShapeDtypeStruct((T, RET_W), F32),
        jax.ShapeDtypeStruct((T, MLA_HEADS * QK_PAD), BF16),
        jax.ShapeDtypeStruct((T, KV_LORA), F32), jax.ShapeDtypeStruct((T, MLA_ROPE), F32),
        jax.ShapeDtypeStruct((T, QK_PAD), BF16))
    out_specs = (row(RET_W), row(RET_W), row(RET_W), row(RET_W), row(MLA_HEADS * QK_PAD),
                 row(KV_LORA), row(MLA_ROPE), row(QK_PAD))
    if streams:
        npt = h[0].shape[0] // tm
        kern = functools.partial(_proj_streams_kernel, n_prompt_tiles=npt)
        h_specs, h_args = list(_stream_specs(tm, D, npt)), h
        out_specs += (row(D),)
        out_shapes += (jax.ShapeDtypeStruct((T, D), F32),)
    else:
        kern, h_specs, h_args = _proj_kernel, [row(D)], (h,)
    return pl.pallas_call(
        kern, grid=(T // tm,),
        in_specs=h_specs + [full(an), full(win), full(qn), full(wuq), full(kvn), full(wbig)]
                 + [tab] * 8,
        out_specs=out_specs,
        out_shape=out_shapes, compiler_params=_cparams(("parallel",)), name="proj",
    )(*h_args, an, win, qn, wuq, kvn, wbig, *tabs)


def _ret_kernel(q_ref, k_ref, v_ref, g_ref, s0_ref, d_ref, qd_ref, kd_ref, gs_ref, bm_ref, rn_ref,
                o_ref, so_ref, st_ref, *, n_chunks):
    i = pl.program_id(1)

    @pl.when(i == 0)
    def _():
        st_ref[...] = s0_ref[0]

    lane = lax.broadcasted_iota(jnp.int32, (CHUNK, LANES), 1)
    first = lane < RET_DK

    def chunk(c, carry):
        rows = pl.ds(pl.multiple_of(c * CHUNK, CHUNK), CHUNK)
        for p in range(N_PAIRS):
            cols = slice(p * LANES, (p + 1) * LANES)
            qp, kp, vp = q_ref[rows, cols], k_ref[rows, cols], v_ref[rows, cols]
            q2 = jnp.concatenate([jnp.where(first, qp, 0.0), jnp.where(first, 0.0, qp)], axis=0)
            sd = (_dot_nt(q2.astype(BF16), kp.astype(BF16)) * d_ref[p]).astype(BF16)
            qq = (qp * qd_ref[p]).astype(BF16)
            st = st_ref[p]
            lhs = jnp.concatenate([jnp.concatenate([qq, qq], axis=0), sd], axis=1)
            rhs = jnp.concatenate([st.astype(BF16), vp], axis=0)
            o2 = _dot(lhs, rhs)
            o = jnp.where(first, o2[:CHUNK], o2[CHUNK:])
            upd = _dot_tn((kp * kd_ref[p]).astype(BF16), vp)
            st_ref[p] = st * gs_ref[p] + upd * bm_ref[...]
            oo = o * o
            ss_a = jnp.sum(jnp.where(first, oo, 0.0), axis=-1, keepdims=True)
            ss_b = jnp.sum(jnp.where(first, 0.0, oo), axis=-1, keepdims=True)
            rs = jnp.where(first, lax.rsqrt(ss_a * (1.0 / RET_DV) + RMS_EPS),
                           lax.rsqrt(ss_b * (1.0 / RET_DV) + RMS_EPS))
            g = g_ref[rows, cols]
            o_ref[rows, cols] = (o * rs * rn_ref[:, cols] * (g * jax.nn.sigmoid(g))).astype(BF16)
        return carry

    lax.fori_loop(0, n_chunks, chunk, 0, unroll=min(n_chunks, 8))

    @pl.when(i == pl.num_programs(1) - 1)
    def _():
        so_ref[0] = st_ref[...]


def _retention(qr, kr, vr, gr, s0, tabs, rn, nb, seq, row0, rb):
    nblk = seq // rb
    off = row0 // rb
    tok = pl.BlockSpec((rb, RET_W), lambda b, i: (off + b * nblk + i, 0))
    full = lambda a: pl.BlockSpec(a.shape, lambda b, i: (0,) * a.ndim)
    st_spec = pl.BlockSpec((1, N_PAIRS, LANES, LANES), lambda b, i: (b, 0, 0, 0))
    d, qd, kd, gs, bm = tabs
    return pl.pallas_call(
        functools.partial(_ret_kernel, n_chunks=rb // CHUNK), grid=(nb, nblk),
        in_specs=[tok, tok, tok, tok, st_spec, full(d), full(qd), full(kd), full(gs), full(bm), full(rn)],
        out_specs=(pl.BlockSpec((rb, RET_W), lambda b, i: (b * nblk + i, 0)), st_spec),
        out_shape=(jax.ShapeDtypeStruct((nb * seq, RET_W), BF16),
                   jax.ShapeDtypeStruct((nb, N_PAIRS, LANES, LANES), F32)),
        scratch_shapes=[pltpu.VMEM((N_PAIRS, LANES, LANES), F32)],
        compiler_params=_cparams(("parallel", "arbitrary")), name="retention",
    )(qr, kr, vr, gr, s0, d, qd, kd, gs, bm, rn)


def _attn_kernel(q_ref, k_ref, wuv_ref, o_ref, q_s, s0_s, s1_s, m_s, l_s, acc_s,
                 *, tq, kb, q_pos0, sk_valid):
    i = pl.program_id(1)
    R = MLA_HEADS * tq
    for h in range(MLA_HEADS):
        q_s[h * tq:(h + 1) * tq, :] = q_ref[:, h * QK_PAD:(h + 1) * QK_PAD]
    m_s[...] = jnp.full(m_s.shape, NEG, F32)
    l_s[...] = jnp.zeros(l_s.shape, F32)
    acc_s[...] = jnp.zeros(acc_s.shape, F32)

    chunk_end = lambda t: ((t >> CHUNK_SHIFT) + 1) << CHUNK_SHIFT
    kb_shift = kb.bit_length() - 1
    qstart = q_pos0 + i * tq
    lim_first = jnp.minimum(chunk_end(qstart), sk_valid)
    lim_last = jnp.minimum(chunk_end(qstart + tq - 1), sk_valid)
    n_full = lim_first >> kb_shift
    n_blk = (lim_last + kb - 1) >> kb_shift

    last = n_blk - 1

    def key_block(j):
        return k_ref[pl.ds(pl.multiple_of(j * kb, kb), kb), :]

    def scores(j, s_ref):
        s_ref[...] = _dot_nt(q_s[...], key_block(j))

    def update(j, s_ref, masked, width=kb):
        s = s_ref[:, :width]
        if masked:
            tok = lax.broadcasted_iota(jnp.int32, (R, 1), 0) & (tq - 1)
            row_lim = jnp.minimum(chunk_end(qstart + tok), sk_valid)
            kidx = j * kb + lax.broadcasted_iota(jnp.int32, (1, width), 1)
            s = jnp.where(kidx < row_lim, s, NEG)
        m_prev = m_s[...]
        m_next = jnp.maximum(m_prev, jnp.max(s, axis=-1, keepdims=True))
        p = jnp.exp2(s - jnp.tile(m_next, (1, width // LANES)))
        alpha = jnp.exp2(m_prev - m_next)
        p_lanes = p[:, :LANES]
        for c in range(LANES, width, LANES):
            p_lanes = p_lanes + p[:, c:c + LANES]
        l_s[...] = alpha * l_s[...] + p_lanes
        acc_s[...] = alpha * acc_s[...] + _dot(p.astype(BF16), key_block(j)[:width, :KV_LORA])
        m_s[...] = m_next

    n_pipe = jnp.minimum(n_full, last)
    odd = n_pipe & 1

    @pl.when(odd == 1)
    def _():
        scores(0, s0_s)
        update(0, s0_s, False)

    scores(odd, s0_s)

    def pair(jj, c):
        j = odd + 2 * jj
        scores(j + 1, s1_s)
        update(j, s0_s, False)
        scores(j + 2, s0_s)
        update(j + 1, s1_s, False)
        return c

    lax.fori_loop(0, n_pipe >> 1, pair, 0)
    if (kb // 2) % LANES == 0:
        half_only = (n_pipe == last) & (lim_last - n_pipe * kb <= kb // 2)
        pl.when(half_only)(lambda: update(n_pipe, s0_s, True, kb // 2))
        pl.when(jnp.logical_not(half_only))(lambda: update(n_pipe, s0_s, True))
    else:
        update(n_pipe, s0_s, True)

    def tail(j, c):
        scores(j, s0_s)
        update(j, s0_s, True)
        return c

    lax.fori_loop(n_pipe + 1, n_blk, tail, 0)

    o_lat = (acc_s[...] / jnp.sum(l_s[...], axis=-1, keepdims=True)).astype(BF16)
    out = None
    for p in range(MLA_HEADS // 2):
        pair = jnp.concatenate([o_lat[(2 * p) * tq:(2 * p + 1) * tq],
                                o_lat[(2 * p + 1) * tq:(2 * p + 2) * tq]], axis=1)
        term = _dot(pair, wuv_ref[p])
        out = term if out is None else out + term
    o_ref[...] = out.astype(BF16)


def _attn_cached_kernel(q_ref, ckv_ref, kro_ref, knew_ref, wuv_ref, o_ref, k_s, *scratch,
                        past, **static):
    n_new = knew_ref.shape[0]
    k_s[0:past, :KV_LORA] = ckv_ref[0].astype(BF16)
    k_s[0:past, KV_LORA:] = jnp.zeros((past, QK_PAD - KV_LORA), BF16)
    k_s[0:past, KV_LORA:KV_LORA + MLA_ROPE] = kro_ref[0].astype(BF16)
    k_s[past:past + n_new, :] = knew_ref[...]
    k_s[past + n_new:, :] = jnp.zeros((k_s.shape[0] - past - n_new, QK_PAD), BF16)
    _attn_kernel(q_ref, k_s, wuv_ref, o_ref, *scratch, **static)


def _attention_cached(qx, cache_ckv, cache_krope, kx, wuv, row0, kb):
    nb, past, _ = cache_ckv.shape
    L = (qx.shape[0] - row0) // nb
    sk_valid = past + L
    sk = -(-sk_valid // kb) * kb
    R = MLA_HEADS * L
    off = row0 // L
    return pl.pallas_call(
        functools.partial(_attn_cached_kernel, past=past, tq=L, kb=kb, q_pos0=past,
                          sk_valid=sk_valid),
        grid=(nb, 1),
        in_specs=[pl.BlockSpec((L, MLA_HEADS * QK_PAD), lambda b, i: (off + b, 0)),
                  pl.BlockSpec((1, past, KV_LORA), lambda b, i: (b, 0, 0)),
                  pl.BlockSpec((1, past, MLA_ROPE), lambda b, i: (b, 0, 0)),
                  pl.BlockSpec((L, QK_PAD), lambda b, i: (off + b, 0)),
                  pl.BlockSpec(wuv.shape, lambda b, i: (0, 0, 0))],
        out_specs=pl.BlockSpec((L, MLA_W), lambda b, i: (b, 0)),
        out_shape=jax.ShapeDtypeStruct((nb * L, MLA_W), BF16),
        scratch_shapes=[pltpu.VMEM((sk, QK_PAD), BF16), pltpu.VMEM((R, QK_PAD), BF16),
                        pltpu.VMEM((R, kb), F32), pltpu.VMEM((R, kb), F32),
                        pltpu.VMEM((R, LANES), F32), pltpu.VMEM((R, LANES), F32),
                        pltpu.VMEM((R, KV_LORA), F32)],
        compiler_params=_cparams(("parallel", "arbitrary")),
        name="attention_cached",
    )(qx, cache_ckv, cache_krope, kx, wuv)


def _attention(qx, kx, wuv, nb, seq_q, row0, tq, kb, q_pos0, sk, sk_valid):
    nq = seq_q // tq
    off = row0 // tq
    R = MLA_HEADS * tq
    return pl.pallas_call(
        functools.partial(_attn_kernel, tq=tq, kb=kb, q_pos0=q_pos0, sk_valid=sk_valid),
        grid=(nb, nq),
        in_specs=[pl.BlockSpec((tq, MLA_HEADS * QK_PAD), lambda b, i: (off + b * nq + i, 0)),
                  pl.BlockSpec((sk, QK_PAD), lambda b, i: (b, 0)),
                  pl.BlockSpec(wuv.shape, lambda b, i: (0, 0, 0))],
        out_specs=pl.BlockSpec((tq, MLA_W), lambda b, i: (b * nq + i, 0)),
        out_shape=jax.ShapeDtypeStruct((nb * seq_q, MLA_W), BF16),
        scratch_shapes=[pltpu.VMEM((R, QK_PAD), BF16), pltpu.VMEM((R, kb), F32),
                        pltpu.VMEM((R, kb), F32), pltpu.VMEM((R, LANES), F32),
                        pltpu.VMEM((R, LANES), F32), pltpu.VMEM((R, KV_LORA), F32)],
        compiler_params=_cparams(("parallel", "arbitrary")),
        name="attention",
    )(qx, kx, wuv)


def _merge_kernel(h_ref, ap_ref, bp_ref, as_ref, bs_ref, wo_ref, o_ref, *, n_prompt_tiles):
    def project(a_ref, b_ref):
        o_ref[...] = (h_ref[...] + _dot(a_ref[...], wo_ref[:RET_W, :])
                      + _dot(b_ref[...], wo_ref[RET_W:, :]))

    @pl.when(pl.program_id(0) < n_prompt_tiles)
    def _():
        project(ap_ref, bp_ref)

    @pl.when(pl.program_id(0) >= n_prompt_tiles)
    def _():
        project(as_ref, bs_ref)


def _merge(h, a_p, b_p, a_s, b_s, wo, tm):
    T, D = h.shape
    npt = a_p.shape[0] // tm
    row = lambda w: pl.BlockSpec((tm, w), lambda i: (i, 0))
    (a_prm, a_smp), (b_prm, b_smp) = _stream_specs(tm, RET_W, npt), _stream_specs(tm, MLA_W, npt)
    return pl.pallas_call(
        functools.partial(_merge_kernel, n_prompt_tiles=npt), grid=(T // tm,),
        in_specs=[row(D), a_prm, b_prm, a_smp, b_smp, pl.BlockSpec(wo.shape, lambda i: (0, 0))],
        out_specs=row(D), out_shape=jax.ShapeDtypeStruct((T, D), F32),
        compiler_params=_cparams(("parallel",)), name="merge",
    )(h, a_p, b_p, a_s, b_s, wo)


def _ffn_kernel(h_ref, fn_ref, wg_ref, wu_ref, wd_ref, o_ref):
    hh = h_ref[...]
    u = _rms(hh, fn_ref[...]).astype(BF16)
    a = _dot(u, wg_ref[...])
    mid = (a * jax.nn.sigmoid(a) * _dot(u, wu_ref[...])).astype(BF16)
    o_ref[...] = hh + _dot(mid, wd_ref[...])


def _ffn(h, fn, wg, wu, wd, tm):
    T, D = h.shape
    resident = lambda a: pl.BlockSpec(a.shape, lambda i: (0, 0), pipeline_mode=pl.Buffered(1))
    return pl.pallas_call(
        _ffn_kernel, grid=(T // tm,),
        in_specs=[pl.BlockSpec((tm, D), lambda i: (i, 0)), pl.BlockSpec(fn.shape, lambda i: (0, 0)),
                  resident(wg), resident(wu), resident(wd)],
        out_specs=pl.BlockSpec((tm, D), lambda i: (i, 0)),
        out_shape=jax.ShapeDtypeStruct((T, D), F32),
        compiler_params=_cparams(("parallel",)), name="ffn",
    )(h, fn, wg, wu, wd)


def _dense_layer_kernel(h_ref, ap_ref, bp_ref, as_ref, bs_ref, pp_ref, ps_ref, wo_ref, fn_ref,
                        wg_ref, wu_ref, wd_ref, pn_ref, wpg_ref, wpp_ref, o_ref, *, n_prompt_tiles):
    prompt = pl.program_id(0) < n_prompt_tiles
    a = jnp.where(prompt, ap_ref[...], as_ref[...])
    b = jnp.where(prompt, bp_ref[...], bs_ref[...])
    h1 = h_ref[...] + _dot(a, wo_ref[:RET_W, :]) + _dot(b, wo_ref[RET_W:, :])
    u = _rms(h1, fn_ref[...]).astype(BF16)
    g = _dot(u, wg_ref[...])
    mid = (g * jax.nn.sigmoid(g) * _dot(u, wu_ref[...])).astype(BF16)
    h2 = h1 + _dot(mid, wd_ref[...])
    gate = jax.nn.sigmoid(_dot(_rms(h2, pn_ref[...]).astype(BF16), wpg_ref[...]))
    p = jnp.where(prompt, pp_ref[...], ps_ref[...]).astype(BF16)
    o_ref[...] = h2 + gate * _dot(p, wpp_ref[...])


def _dense_layer(h, a_p, b_p, a_s, b_s, p_p, p_s, wo, fn, wg, wu, wd, pn, wpg, wpp, tm):
    T, D = h.shape
    npt = a_p.shape[0] // tm
    row = pl.BlockSpec((tm, D), lambda i: (i, 0))
    resident = lambda w: pl.BlockSpec(w.shape, lambda i: (0, 0), pipeline_mode=pl.Buffered(1))
    (a_prm, a_smp), (b_prm, b_smp) = _stream_specs(tm, RET_W, npt), _stream_specs(tm, MLA_W, npt)
    p_prm, p_smp = _stream_specs(tm, p_p.shape[1], npt)
    return pl.pallas_call(
        functools.partial(_dense_layer_kernel, n_prompt_tiles=npt), grid=(T // tm,),
        in_specs=[row, a_prm, b_prm, a_smp, b_smp, p_prm, p_smp] + [
            resident(w) for w in (wo, fn, wg, wu, wd, pn, wpg, wpp)],
        out_specs=row, out_shape=jax.ShapeDtypeStruct((T, D), F32),
        compiler_params=_cparams(("parallel",)), name="dense_layer",
    )(h, a_p, b_p, a_s, b_s, p_p, p_s, wo, fn, wg, wu, wd, pn, wpg, wpp)


L_E1, L_E2, L_W1, L_W2, L_R1, L_R2 = range(6)


def _merge_router_kernel(h_ref, ap_ref, bp_ref, as_ref, bs_ref, wo_ref, fn_ref, wr_ref,
                         h1_ref, meta_ref, cnt_ref, run_s, *, n_prompt_tiles, n_experts):
    tm = h_ref.shape[0]
    prompt = pl.program_id(0) < n_prompt_tiles
    a = jnp.where(prompt, ap_ref[...], as_ref[...])
    b = jnp.where(prompt, bp_ref[...], bs_ref[...])
    h1 = h_ref[...] + _dot(a, wo_ref[:RET_W, :]) + _dot(b, wo_ref[RET_W:, :])
    h1_ref[...] = h1

    @pl.when(pl.program_id(0) == 0)
    def _():
        run_s[...] = jnp.zeros(run_s.shape, F32)

    u = _rms(h1, fn_ref[...])
    u_hi = u.astype(BF16)
    u_lo = (u - u_hi.astype(F32)).astype(BF16)
    hi_terms = _dot(u_hi, wr_ref[...])
    logits = hi_terms[:, :LANES] + hi_terms[:, LANES:] + _dot(u_lo, wr_ref[:, :LANES])
    lane = lax.broadcasted_iota(jnp.int32, logits.shape, 1).astype(F32)
    lg = jnp.where(lane < n_experts, logits, NEG)
    m1 = jnp.max(lg, axis=-1, keepdims=True)
    i1 = jnp.min(jnp.where(lg == m1, lane, float(LANES)), axis=-1, keepdims=True)
    lg2 = jnp.where(lane == i1, NEG, lg)
    m2 = jnp.max(lg2, axis=-1, keepdims=True)
    i2 = jnp.min(jnp.where(lg2 == m2, lane, float(LANES)), axis=-1, keepdims=True)
    e2 = jnp.exp(m2 - m1)
    den = 1.0 + e2

    hit = jnp.where((lane == i1) | (lane == i2), 1.0, 0.0)
    r_i = lax.broadcasted_iota(jnp.int32, (tm, tm), 0)
    c_i = lax.broadcasted_iota(jnp.int32, (tm, tm), 1)
    before = jnp.where(c_i < r_i, 1.0, 0.0).astype(BF16)
    prefix = _dot(before, hit.astype(BF16)) + run_s[0:1, :]
    r1 = jnp.sum(jnp.where(lane == i1, prefix, 0.0), axis=-1, keepdims=True)
    r2 = jnp.sum(jnp.where(lane == i2, prefix, 0.0), axis=-1, keepdims=True)
    run_s[...] = run_s[...] + jnp.sum(hit, axis=0, keepdims=True)
    cnt_ref[...] = run_s[...]

    meta = jnp.where(lane == L_E1, i1, 0.0)
    for ln, val in ((L_E2, i2), (L_W1, 1.0 / den), (L_W2, e2 / den), (L_R1, r1), (L_R2, r2)):
        meta = jnp.where(lane == ln, val, meta)
    meta_ref[...] = meta


def _merge_router(h, a_p, b_p, a_s, b_s, wo, fn, wr, n_experts, tm):
    T, D = h.shape
    npt = a_p.shape[0] // tm
    row = lambda w: pl.BlockSpec((tm, w), lambda i: (i, 0))
    full = lambda a: pl.BlockSpec(a.shape, lambda i: (0, 0))
    (a_prm, a_smp), (b_prm, b_smp) = _stream_specs(tm, RET_W, npt), _stream_specs(tm, MLA_W, npt)
    return pl.pallas_call(
        functools.partial(_merge_router_kernel, n_prompt_tiles=npt, n_experts=n_experts),
        grid=(T // tm,),
        in_specs=[row(D), a_prm, b_prm, a_smp, b_smp, full(wo), full(fn), full(wr)],
        out_specs=(row(D), row(LANES), pl.BlockSpec((SUBLANES, LANES), lambda i: (0, 0))),
        out_shape=(jax.ShapeDtypeStruct((T, D), F32), jax.ShapeDtypeStruct((T, LANES), F32),
                   jax.ShapeDtypeStruct((SUBLANES, LANES), F32)),
        scratch_shapes=[pltpu.VMEM((SUBLANES, LANES), F32)],
        compiler_params=_cparams(("arbitrary",)), name="merge_router",
    )(h, a_p, b_p, a_s, b_s, wo, fn, wr)


def _tile_index_copy(d_hbm, idx_s, sem, tile, slot):
    n = d_hbm.shape[1]
    return pltpu.make_async_copy(d_hbm.at[tile], idx_s.at[pl.ds(pl.multiple_of(slot * n, n), n)],
                                 sem.at[slot])


def _dispatch_kernel(fill_ref, d_hbm, h_ref, xs_out, idx_s, zero_s, isem, rsem, fsem):
    tm = h_ref.shape[0]
    tmg = zero_s.shape[0]
    i, n = pl.program_id(0), pl.num_programs(0)
    slot = i % 2

    def fill_copy(k):
        start = pl.multiple_of(fill_ref[k], tmg)
        return pltpu.make_async_copy(zero_s, xs_out.at[pl.ds(start, tmg)], fsem)

    @pl.when(i == 0)
    def _():
        _tile_index_copy(d_hbm, idx_s, isem, 0, 0).start()
        zero_s[...] = jnp.zeros(zero_s.shape, F32)
        for k in range(fill_ref.shape[0]):
            pl.when(fill_ref[k] >= 0)(lambda k=k: fill_copy(k).start(priority=k % 2))
        for k in range(fill_ref.shape[0]):
            pl.when(fill_ref[k] >= 0)(lambda k=k: fill_copy(k).wait())

    @pl.when(i + 1 < n)
    def _():
        _tile_index_copy(d_hbm, idx_s, isem, i + 1, 1 - slot).start()

    _tile_index_copy(d_hbm, idx_s, isem, i, slot).wait()

    def row_copy(r, dst):
        return pltpu.make_async_copy(h_ref.at[pl.ds(r, 1)], xs_out.at[pl.ds(dst, 1)], rsem)

    def body(g, c):
        r0 = pl.multiple_of(g * SUBLANES, SUBLANES)
        for u in range(SUBLANES):
            for k in range(TOP_K):
                row_copy(r0 + u, idx_s[slot * (TOP_K * tm) + k * tm + r0 + u]).start(priority=k % 2)
        return c

    lax.fori_loop(0, tm // SUBLANES, body, 0)
    for k in range(TOP_K):
        pltpu.make_async_copy(h_ref, xs_out.at[pl.ds(0, tm)], rsem).wait()


def _dispatch(fill_starts, d_tiles, h, n_rows, tm, tmg):
    T, D = h.shape
    grid_spec = pltpu.PrefetchScalarGridSpec(
        num_scalar_prefetch=1, grid=(T // tm,),
        in_specs=[pl.BlockSpec(memory_space=pl.ANY), pl.BlockSpec((tm, D), lambda i, fs: (i, 0))],
        out_specs=pl.BlockSpec(memory_space=pl.ANY),
        scratch_shapes=[pltpu.SMEM((2 * TOP_K * tm,), jnp.int32), pltpu.VMEM((tmg, D), F32),
                        pltpu.SemaphoreType.DMA((2,)), pltpu.SemaphoreType.DMA(()),
                        pltpu.SemaphoreType.DMA(())])
    return pl.pallas_call(
        _dispatch_kernel, grid_spec=grid_spec, out_shape=jax.ShapeDtypeStruct((n_rows, D), F32),
        compiler_params=_cparams(("arbitrary",)), name="dispatch",
    )(fill_starts, d_tiles, h)


def _gmm_kernel(te_ref, nu_ref, x_ref, fn_ref, wg_ref, wu_ref, wd_ref, o_ref, u_s):
    i, f = pl.program_id(0), pl.program_id(1)
    used = i < nu_ref[0]

    @pl.when(f == 0)
    def _():
        o_ref[...] = jnp.zeros(o_ref.shape, F32)

    @pl.when(used & (f == 0))
    def _():
        u_s[...] = _rms(x_ref[...], fn_ref[...]).astype(BF16)

    @pl.when(used)
    def _():
        u = u_s[...]
        a = _dot(u, wg_ref[0].astype(BF16))
        mid = (a * jax.nn.sigmoid(a) * _dot(u, wu_ref[0].astype(BF16))).astype(BF16)
        o_ref[...] += _dot(mid, wd_ref[0].astype(BF16))


def _gmm(tile_expert, n_used, xs, fn, wg, wu, wd, tmg, tf):
    R, D = xs.shape
    F = wg.shape[-1]
    nf = F // tf
    f_eff = lambda i, f, nu: jnp.where(i < nu[0], f, nf - 1)
    grid_spec = pltpu.PrefetchScalarGridSpec(
        num_scalar_prefetch=2, grid=(R // tmg, nf),
        in_specs=[pl.BlockSpec((tmg, D), lambda i, f, te, nu: (i, 0)),
                  pl.BlockSpec(fn.shape, lambda i, f, te, nu: (0, 0)),
                  pl.BlockSpec((1, D, tf), lambda i, f, te, nu: (te[i], 0, f_eff(i, f, nu))),
                  pl.BlockSpec((1, D, tf), lambda i, f, te, nu: (te[i], 0, f_eff(i, f, nu))),
                  pl.BlockSpec((1, tf, D), lambda i, f, te, nu: (te[i], f_eff(i, f, nu), 0))],
        out_specs=pl.BlockSpec((tmg, D), lambda i, f, te, nu: (i, 0)),
        scratch_shapes=[pltpu.VMEM((tmg, D), BF16)])
    return pl.pallas_call(
        _gmm_kernel, grid_spec=grid_spec, out_shape=jax.ShapeDtypeStruct((R, D), F32),
        compiler_params=_cparams(("arbitrary", "arbitrary")), name="experts",
    )(tile_expert, n_used, xs, fn, wg, wu, wd)


def _combine_kernel(d_hbm, h_ref, meta_ref, ys_hbm, o_ref, idx_s, ya_s, yb_s, isem, rsem):
    tm = h_ref.shape[0]
    i, n = pl.program_id(0), pl.num_programs(0)
    slot = i % 2

    def fetch(tile, s):
        _tile_index_copy(d_hbm, idx_s, isem, tile, s).wait()

        def body(g, c):
            r0 = pl.multiple_of(g * SUBLANES, SUBLANES)
            base = s * (TOP_K * tm) + r0
            for u in range(SUBLANES):
                pltpu.make_async_copy(ys_hbm.at[pl.ds(idx_s[base + u], 1)],
                                      ya_s.at[s, pl.ds(r0 + u, 1)], rsem.at[s]).start(priority=0)
                pltpu.make_async_copy(ys_hbm.at[pl.ds(idx_s[base + tm + u], 1)],
                                      yb_s.at[s, pl.ds(r0 + u, 1)], rsem.at[s]).start(priority=1)
            return c

        lax.fori_loop(0, tm // SUBLANES, body, 0)

    @pl.when(i == 0)
    def _():
        _tile_index_copy(d_hbm, idx_s, isem, 0, 0).start()
        fetch(0, 0)

        @pl.when(n > 1)
        def _():
            _tile_index_copy(d_hbm, idx_s, isem, 1, 1).start()

    @pl.when(i + 1 < n)
    def _():
        fetch(i + 1, 1 - slot)

    @pl.when(i + 2 < n)
    def _():
        _tile_index_copy(d_hbm, idx_s, isem, i + 2, slot).start()

    pltpu.make_async_copy(ys_hbm.at[pl.ds(0, tm)], ya_s.at[slot], rsem.at[slot]).wait()
    pltpu.make_async_copy(ys_hbm.at[pl.ds(0, tm)], yb_s.at[slot], rsem.at[slot]).wait()
    meta = meta_ref[...]
    o_ref[...] = (h_ref[...] + meta[:, L_W1:L_W1 + 1] * ya_s[slot]
                  + meta[:, L_W2:L_W2 + 1] * yb_s[slot])


def _combine(d_tiles, h, meta, ys, tm):
    T, D = h.shape
    return pl.pallas_call(
        _combine_kernel, grid=(T // tm,),
        in_specs=[pl.BlockSpec(memory_space=pl.ANY), pl.BlockSpec((tm, D), lambda i: (i, 0)),
                  pl.BlockSpec((tm, LANES), lambda i: (i, 0)), pl.BlockSpec(memory_space=pl.ANY)],
        out_specs=pl.BlockSpec((tm, D), lambda i: (i, 0)),
        out_shape=jax.ShapeDtypeStruct((T, D), F32),
        scratch_shapes=[pltpu.SMEM((2 * TOP_K * tm,), jnp.int32), pltpu.VMEM((2, tm, D), F32),
                        pltpu.VMEM((2, tm, D), F32), pltpu.SemaphoreType.DMA((2,)),
                        pltpu.SemaphoreType.DMA((2,))],
        compiler_params=_cparams(("arbitrary",)), name="combine",
    )(d_tiles, h, meta, ys)


def _moe(h, heads, wo, fn, w_router, wg, wu, wd, tm, tmg, tf):
    T, D = h.shape
    E = w_router.shape[-1]
    wr = jnp.pad(w_router, ((0, 0), (0, LANES - E)))
    wr_hi = wr.astype(BF16)
    wr_lo = (wr - wr_hi.astype(F32)).astype(BF16)
    h, meta, counts = _merge_router(h, *heads, wo, fn, jnp.concatenate([wr_hi, wr_lo], axis=1), E, tm)
    cnt = counts[0, :E].astype(jnp.int32)
    padded = ((cnt + tmg - 1) // tmg) * tmg
    ends = jnp.cumsum(padded)
    off = ends - padded
    n_tiles = -(-TOP_K * T // tmg) + E
    tile_expert = jnp.minimum(
        jnp.sum(ends[None, :] <= (jnp.arange(n_tiles, dtype=jnp.int32) * tmg)[:, None], axis=1), E - 1
    ).astype(jnp.int32)
    n_used = (ends[-1] // tmg).astype(jnp.int32).reshape(1)
    e12 = meta[:, L_E1:L_E2 + 1].astype(jnp.int32)
    off12 = jnp.sum(jnp.where(e12[..., None] == jnp.arange(E, dtype=jnp.int32), off, 0), axis=-1)
    dest = off12 + meta[:, L_R1:L_R2 + 1].astype(jnp.int32)
    d_tiles = dest.reshape(T // tm, tm, TOP_K).transpose(0, 2, 1).reshape(T // tm, TOP_K * tm)

    trailing = ends[-1] + jnp.arange(E + 1, dtype=jnp.int32) * tmg
    fill_starts = jnp.concatenate([
        jnp.where(padded > 0, ends - tmg, -1),
        jnp.where(trailing < n_tiles * tmg, trailing, -1)]).astype(jnp.int32)
    xs = _dispatch(fill_starts, d_tiles, h, n_tiles * tmg, tm, tmg)
    ys = _gmm(tile_expert, n_used, xs, fn, wg, wu, wd, tmg, tf)
    return _combine(d_tiles, h, meta, ys, tm)


def _ple_kernel(h_ref, pn_ref, wg_ref, wp_ref, fin_ref, *refs, final, n_prompt_tiles):
    *p_refs, o_ref = refs
    if len(p_refs) == 2:
        p = jnp.where(pl.program_id(0) < n_prompt_tiles, p_refs[0][...], p_refs[1][...])
    else:
        p = p_refs[0][...]
    hh = h_ref[...]
    gate = jax.nn.sigmoid(_dot(_rms(hh, pn_ref[...]).astype(BF16), wg_ref[...]))
    out = hh + gate * _dot(p.astype(BF16), wp_ref[...])
    if final:
        out = _rms(out, fin_ref[...])
    o_ref[...] = out


def _ple(h, pn, wg, p, wp, fin, tm, final, row0=0, rows=None):
    D = h.shape[1]
    rows = h.shape[0] if rows is None else rows
    off = row0 // tm
    full = lambda a: pl.BlockSpec(a.shape, lambda i: (0,) * a.ndim)
    if isinstance(p, tuple):
        npt = p[0].shape[0] // tm
        p_specs = list(_stream_specs(tm, p[0].shape[1], npt))
    else:
        npt, p_specs, p = 0, [pl.BlockSpec((tm, p.shape[1]), lambda i: (i, 0))], (p,)
    return pl.pallas_call(
        functools.partial(_ple_kernel, final=final, n_prompt_tiles=npt), grid=(rows // tm,),
        in_specs=[pl.BlockSpec((tm, D), lambda i: (off + i, 0)), full(pn), full(wg), full(wp),
                  full(fin)] + p_specs,
        out_specs=pl.BlockSpec((tm, D), lambda i: (i, 0)),
        out_shape=jax.ShapeDtypeStruct((rows, D), F32),
        compiler_params=_cparams(("parallel",)), name="ple",
    )(h, pn, wg, wp, fin, *p)


def _rope_tables(pos):
    pos = pos.astype(F32)[:, None]
    lane = np.arange(LANES)
    freqs = lambda half: ROPE_THETA ** (-jnp.arange(half, dtype=F32) / half)
    a_ret, a_mla = pos * freqs(RET_HALF)[None, :], pos * freqs(ROPE_HALF)[None, :]
    wide = lambda t: jnp.tile(t, (1, LANES // t.shape[1]))
    lo64 = jnp.asarray((lane % RET_DK) < RET_HALF)
    c64, s64 = wide(jnp.cos(a_ret)), wide(jnp.sin(a_ret))
    c16, s16 = wide(jnp.cos(a_mla)), wide(jnp.sin(a_mla))
    in_k = jnp.asarray(lane < MLA_ROPE)
    lo_k = jnp.asarray(lane < ROPE_HALF)
    hi_k = jnp.asarray((lane >= ROPE_HALF) & (lane < MLA_ROPE))
    return (c64, jnp.where(lo64, -s64, 0.0), jnp.where(lo64, 0.0, s64), c16, s16,
            jnp.where(in_k, c16, 0.0), jnp.where(lo_k, -s16, 0.0), jnp.where(hi_k, s16, 0.0))


def _ret_tables(L):
    f = np.float32
    log_g = np.log(f(1.0) - np.exp2(f(-5.0) - np.arange(RET_HEADS, dtype=f)))
    idx = np.arange(L, dtype=f)
    dist = np.abs(idx[:, None] - idx[None, :])
    d = np.exp(dist[None] * log_g[:, None, None])
    qdec = np.exp((idx[:, None] + f(1.0)) * log_g[None, :])
    kdec = np.exp((f(L) - f(1.0) - idx)[:, None] * log_g[None, :])
    sdec = np.exp(f(L) * log_g)
    wide = lambda t: np.repeat(t, RET_DK, axis=1).reshape(L, N_PAIRS, LANES).transpose(1, 0, 2)
    bm = np.kron(np.eye(2, dtype=f), np.ones((RET_DK, RET_DV), f))
    gs = np.repeat(sdec, RET_DK).reshape(N_PAIRS, LANES, 1) * bm[None]
    tabs = (d.reshape(N_PAIRS, 2 * L, L), wide(qdec), wide(kdec), gs, bm)
    return tuple(jnp.asarray(t, F32) for t in tabs)


def _state_to_pairs(s):
    B = s.shape[0]
    s = s.reshape(B, N_PAIRS, 2, RET_DK, RET_DV)
    eye = jnp.eye(2, dtype=s.dtype)
    out = s[:, :, :, :, None, :] * eye[None, None, :, None, :, None]
    return out.reshape(B, N_PAIRS, LANES, LANES)


def _pairs_to_state(sp):
    B = sp.shape[0]
    s = sp.reshape(B, N_PAIRS, 2, RET_DK, 2, RET_DV)
    return jnp.stack([s[:, :, 0, :, 0, :], s[:, :, 1, :, 1, :]], axis=2).reshape(B, RET_HEADS, RET_DK, RET_DV)


def _group_uq_columns(w_uq):
    half = MLA_ROPE // 2
    w = w_uq.reshape(w_uq.shape[0], MLA_HEADS, MLA_NOPE + MLA_ROPE)
    parts = (w[:, :, :MLA_NOPE], w[:, :, MLA_NOPE:MLA_NOPE + half], w[:, :, MLA_NOPE + half:])
    return jnp.concatenate([p.reshape(w_uq.shape[0], -1) for p in parts], axis=1)


def _big_query_weight(w_uk):
    H, half = MLA_HEADS, MLA_ROPE // 2
    eye = jnp.eye(H, dtype=w_uk.dtype)
    wpad = jnp.pad(w_uk, ((0, 0), (0, 0), (0, QK_PAD - KV_LORA)))
    top = (eye[:, None, :, None] * wpad[:, :, None, :]).reshape(H * MLA_NOPE, H * QK_PAD)
    sel = np.zeros((2 * H * half, H * QK_PAD), np.float32)
    for h in range(H):
        for f in range(half):
            sel[h * half + f, h * QK_PAD + KV_LORA + f] = 1.0
            sel[H * half + h * half + f, h * QK_PAD + KV_LORA + half + f] = 1.0
    return jnp.concatenate([top, jnp.asarray(sel, w_uk.dtype)], axis=0)


def kernel(x_prompt, x_sample, p_prompt, p_sample, cache_ckv, cache_krope, state_ret, attn_norm, w_in, q_norm, w_uq, kv_norm, w_uk, w_uv, ret_norm, w_o, ffn_norm, w_gate_d, w_up_d, w_down_d, w_router, w_gate_e, w_up_e, w_down_e, ple_norm, w_ple_gate, w_ple_proj, final_norm):
    Bp, S, D = x_prompt.shape
    Bs, L, _ = x_sample.shape
    depth = w_in.shape[0]
    P = cache_ckv.shape[2]
    Tp, Ts = Bp * S, Bs * L
    T = Tp + Ts
    assert S % CHUNK == 0 and P % CHUNK == 0 and L == CHUNK and Tp % CHUNK == 0
    assert w_router.shape[-1] >= TOP_K

    tm_proj = _pick(int(np.gcd(S, Ts)), (512, 256, 128, 64))
    tm_gmm = 1024
    rb = _pick(S, (512, 256, 128, 64))
    tq = _pick(S, (256, 128, 64))
    kb = _pick(S, (512, 256, 128))
    kb_s = 512

    pos = jnp.concatenate([jnp.arange(S, dtype=jnp.int32),
                           P + jnp.tile(jnp.arange(L, dtype=jnp.int32), Bs)])
    rope_tabs = _rope_tables(pos)
    n_pt, pt_per_seq = Tp // tm_proj, S // tm_proj
    tab_tile = lambda i: jnp.where(i < n_pt, i % pt_per_seq, pt_per_seq + i - n_pt)
    ret_tabs = _ret_tables(CHUNK)
    row2 = lambda v: v.reshape(1, -1)

    h = (x_prompt.reshape(Tp, D), x_sample.reshape(Ts, D))
    outs = {k: [] for k in ("ckv_p", "kro_p", "ret_p", "ckv_s", "kro_s", "ret_s")}
    for l in range(depth):
        win = jnp.pad(w_in[l], ((0, 0), (0, IN_COLS_PAD - IN_COLS))).astype(BF16)
        wuq = _group_uq_columns(w_uq[l]).astype(BF16)
        wbig = _big_query_weight(w_uk[l]).astype(BF16)
        qr, kr, vr, gr, qx, ckv, kro, kx, *h_cat = _proj(
            h, row2(attn_norm[l]), win, row2(q_norm[l]), wuq, row2(kv_norm[l]), wbig, rope_tabs,
            tab_tile, tm_proj)
        if h_cat:
            h, = h_cat

        rn = row2(ret_norm[l])
        zero_state = jnp.zeros((Bp, N_PAIRS, LANES, LANES), F32)
        o_ret_p, st_p = _retention(qr, kr, vr, gr, zero_state, ret_tabs, rn, Bp, S, 0, rb)
        o_ret_s, st_s = _retention(qr, kr, vr, gr, _state_to_pairs(state_ret[l].astype(F32)),
                                   ret_tabs, rn, Bs, L, Tp, L)

        wuv = w_uv[l].astype(BF16)
        wuv_big = (jnp.eye(MLA_HEADS, dtype=BF16)[:, None, :, None] * wuv[:, :, None, :]
                   ).reshape(MLA_HEADS // 2, 2 * KV_LORA, MLA_W)
        o_mla_p = _attention(qx, kx, wuv_big, Bp, S, 0, tq, kb, 0, S, S)
        o_mla_s = _attention_cached(qx, cache_ckv[l], cache_krope[l], kx, wuv_big, Tp, kb_s)

        fn = row2(ffn_norm[l])
        j = l // 2
        wo = w_o[l].astype(BF16)
        ple_w = (row2(ple_norm[l]), w_ple_gate[l].astype(BF16))
        wp = w_ple_proj[l].astype(BF16)
        p_l = (p_prompt[l].reshape(Tp, -1), p_sample[l].reshape(Ts, -1))
        last = l == depth - 1
        if l % 2 == 0 and not last:
            h = _dense_layer(h, o_ret_p, o_mla_p, o_ret_s, o_mla_s, *p_l, wo, fn,
                             w_gate_d[j].astype(BF16), w_up_d[j].astype(BF16),
                             w_down_d[j].astype(BF16), *ple_w, wp, tm_proj)
        else:
            heads = (o_ret_p, o_mla_p, o_ret_s, o_mla_s)
            if l % 2 == 0:
                h = _merge(h, *heads, wo, tm_proj)
                h = _ffn(h, fn, w_gate_d[j].astype(BF16), w_up_d[j].astype(BF16),
                         w_down_d[j].astype(BF16), tm_proj)
            else:
                tf = _pick(w_gate_e.shape[-1], (512, 256, 128))
                h = _moe(h, heads, wo, fn, w_router[j], w_gate_e[j], w_up_e[j], w_down_e[j],
                         tm_proj, tm_gmm, tf)
            if not last:
                h = _ple(h, *ple_w, p_l, wp, row2(final_norm), tm_proj, False)
            else:
                y_p = _ple(h, *ple_w, p_l[0], wp, row2(final_norm), tm_proj, True, 0, Tp)
                y_s = _ple(h, *ple_w, p_l[1], wp, row2(final_norm), tm_proj, True, Tp, Ts)

        outs["ckv_p"].append(ckv[:Tp].reshape(Bp, S, KV_LORA))
        outs["kro_p"].append(kro[:Tp].reshape(Bp, S, MLA_ROPE))
        outs["ret_p"].append(_pairs_to_state(st_p))
        outs["ckv_s"].append(ckv[Tp:].reshape(Bs, L, KV_LORA))
        outs["kro_s"].append(kro[Tp:].reshape(Bs, L, MLA_ROPE))
        outs["ret_s"].append(_pairs_to_state(st_s))

    return (y_p.reshape(Bp, S, D), y_s.reshape(Bs, L, D),
            jnp.stack(outs["ckv_p"]), jnp.stack(outs["kro_p"]), jnp.stack(outs["ret_p"]),
            jnp.stack(outs["ckv_s"]), jnp.stack(outs["kro_s"]), jnp.stack(outs["ret_s"]))
```
